```python
import math
import jax, jax.numpy as jnp
from jax import lax
import numpy as np

D_MODEL = 1024
BATCH = 8
SEQ = 2048
DEPTH = 2
DEC_BATCH = 128
DEC_SEQ = 1
PAST_LEN = 16384
PAGE_SIZE = 128

N_EVEN = (DEPTH + 1) // 2
N_ODD = DEPTH // 2
D_PLE = 256
D_FF = 4 * D_MODEL
RMS_EPS = 1e-6
GN_EPS = 1e-5
CHUNK = 128

A_HEADS = 8
A_HD = 64
A_W = A_HEADS * A_HD
A_DECAY_LORA = 64
A_AAA_LORA = 64
A_GATE_LORA = 128
A_SHIFT_W = 3 * A_W + A_DECAY_LORA + A_AAA_LORA + A_GATE_LORA
A_SPLITS = [A_W, 2 * A_W, 3 * A_W, 3 * A_W + A_DECAY_LORA, 3 * A_W + A_DECAY_LORA + A_AAA_LORA]
A_LN_EPS = 64e-5

B_HEADS = 4
B_HD = 128
B_W = B_HEADS * B_HD
B_CONV = 4
B_IN_W = 4 * B_W + 2 * B_HEADS
B_SPLITS = [B_W, 2 * B_W, 3 * B_W, 4 * B_W, 4 * B_W + B_HEADS]

MIX0_IN = A_SHIFT_W + B_IN_W
MIX0_OUT = A_W + B_W

C_HEADS = 4
C_DK = 256
C_DV = 512
C_OUT_W = C_HEADS * C_DV
C_IN_W = 2 * C_HEADS * C_DK + 2 * C_OUT_W
ROPE_BASE = 10000.0

kernel_name = 'rwkv7_mlstm_retention_hybrid_step'


def rmsnorm(x, g):
    xf = x.astype(jnp.float32)
    y = xf * lax.rsqrt(jnp.mean(xf * xf, axis=-1, keepdims=True) + RMS_EPS)
    return (y * g.astype(jnp.float32)).astype(x.dtype)


def head_norm(x, g, eps):
    xf = x.astype(jnp.float32)
    mu = jnp.mean(xf, axis=-1, keepdims=True)
    var = jnp.mean(jnp.square(xf - mu), axis=-1, keepdims=True)
    y = ((xf - mu) * lax.rsqrt(var + eps)).reshape(x.shape[:-2] + (-1,))
    return y * g.astype(jnp.float32)


def _chunks(t, L):
    Bsz, T = t.shape[:2]
    return jnp.moveaxis(t.reshape((Bsz, T // L, L) + t.shape[2:]), 1, 0)


def _unchunk(t):
    nc, Bsz, L = t.shape[:3]
    return jnp.moveaxis(t, 0, 1).reshape((Bsz, nc * L) + t.shape[3:])


def rope(x, pos):
    half = x.shape[-1] // 2
    inv = ROPE_BASE ** (-jnp.arange(half, dtype=jnp.float32) / half)
    ang = pos.astype(jnp.float32)[:, None] * inv[None, :]
    cos = jnp.cos(ang)[None, :, None, :]
    sin = jnp.sin(ang)[None, :, None, :]
    x1, x2 = x[..., :half], x[..., half:]
    return jnp.concatenate([x1 * cos - x2 * sin, x1 * sin + x2 * cos], axis=-1)


def rwkv7_recurrence(S0, r, w, k, v, a_vec, b_vec):
    def step(S, inp):
        r_t, w_t, k_t, v_t, a_t, b_t = inp
        sa = jnp.einsum('bhvk,bhk->bhv', S, a_t)
        S = S * w_t[:, :, None, :] + sa[..., None] * b_t[:, :, None, :] + v_t[..., None] * k_t[:, :, None, :]
        return S, jnp.einsum('bhvk,bhk->bhv', S, r_t)
    xs = tuple(jnp.moveaxis(t, 1, 0) for t in (r, w, k, v, a_vec, b_vec))
    S, y = lax.scan(step, S0, xs)
    return jnp.moveaxis(y, 0, 1), S


def mlstm_chunkwise(C0, n0, m0, q, k, v, i_pre, log_f):
    T = q.shape[1]
    L = math.gcd(T, CHUNK)
    causal = jnp.tril(jnp.ones((L, L), dtype=bool))

    def step(carry, inp):
        C, n, m = carry
        qc, kc, vc, ic, fc = inp
        b = jnp.cumsum(fc, axis=1)
        dlog = b[:, :, None, :] - b[:, None, :, :] + ic[:, None, :, :]
        dlog = jnp.where(causal[None, :, :, None], dlog, -jnp.inf)
        inter = b + m[:, None, :]
        m_t = jnp.maximum(inter, jnp.max(dlog, axis=2))
        s = jnp.einsum('bjhd,bihd->bjih', qc, kc) * jnp.exp(dlog - m_t[:, :, None, :])
        iw = jnp.exp(inter - m_t)
        num = jnp.einsum('bjih,bihe->bjhe', s, vc) + iw[..., None] * jnp.einsum('bjhd,bhde->bjhe', qc, C)
        den = jnp.sum(s, axis=2) + iw * jnp.einsum('bjhd,bhd->bjh', qc, n)
        h = num / jnp.maximum(jnp.abs(den), jnp.exp(-m_t))[..., None]
        bL = b[:, -1]
        gi = bL[:, None, :] - b + ic
        m_new = jnp.maximum(bL + m, jnp.max(gi, axis=1))
        wi = jnp.exp(gi - m_new[:, None, :])
        wc = jnp.exp(bL + m - m_new)
        C = wc[..., None, None] * C + jnp.einsum('blh,blhd,blhe->bhde', wi, kc, vc)
        n = wc[..., None] * n + jnp.einsum('blh,blhd->bhd', wi, kc)
        return (C, n, m_new), h

    xs = tuple(_chunks(t, L) for t in (q, k, v, i_pre, log_f))
    (C, n, m), h = lax.scan(step, (C0, n0, m0), xs)
    return _unchunk(h), C, n, m


def retention_chunkwise(S0, q, k, v):
    T, H = q.shape[1], q.shape[2]
    L = math.gcd(T, CHUNK)
    lg = jnp.log1p(-jnp.exp2(-5.0 - jnp.arange(H, dtype=jnp.float32)))
    idx = jnp.arange(L, dtype=jnp.float32)
    diff = idx[:, None] - idx[None, :]
    mask = jnp.where(diff[..., None] >= 0, jnp.exp(jnp.maximum(diff, 0.0)[..., None] * lg), 0.0)
    q_dec = jnp.exp((idx[:, None] + 1.0) * lg)
    k_dec = jnp.exp((L - 1.0 - idx)[:, None] * lg)
    c_dec = jnp.exp(L * lg)

    def step(S, inp):
        qc, kc, vc = inp
        s = jnp.einsum('bjhd,bihd->bjih', qc, kc) * mask[None]
        o = jnp.einsum('bjih,bihe->bjhe', s, vc) + jnp.einsum('bjhd,bhde->bjhe', qc * q_dec[None, :, :, None], S)
        S = c_dec[None, :, None, None] * S + jnp.einsum('bihd,bihe->bhde', kc * k_dec[None, :, :, None], vc)
        return S, o

    xs = tuple(_chunks(t, L) for t in (q, k, v))
    S, o = lax.scan(step, S0, xs)
    return _unchunk(o), S


def mix_even(xn, st, W, e):
    f32 = jnp.float32
    shift_buf, S_a, conv_buf, C, n, m = st
    Bsz, T, _ = xn.shape
    u = xn @ W['mix0_w_in'][e]
    ua, ub = u[..., :A_SHIFT_W], u[..., A_SHIFT_W:]

    ua_prev = jnp.concatenate([shift_buf[:, None, :].astype(ua.dtype), ua[:, :-1]], axis=1)
    ua_s = ua + (ua_prev - ua) * W['rw_mu'][e]
    r, k, v, wd, ad, gd = jnp.split(ua_s, A_SPLITS, axis=-1)
    w_log = -jax.nn.softplus(-(W['rw_w0'][e] + jnp.tanh(wd) @ W['rw_w2'][e])) - 0.5
    decay = jnp.exp(-jnp.exp(w_log.astype(f32)))
    a = jax.nn.sigmoid(W['rw_a0'][e] + ad @ W['rw_a2'][e])
    g = jax.nn.sigmoid(gd) @ W['rw_g2'][e]
    heads = lambda t: t.astype(f32).reshape(Bsz, T, A_HEADS, A_HD)
    kk = heads(k * W['rw_kk'][e])
    kk = kk / jnp.maximum(jnp.sqrt(jnp.sum(kk * kk, axis=-1, keepdims=True)), 1e-12)
    k = k * (1.0 + (a - 1.0) * W['rw_ka'][e])
    rh, kh, vh, ah = heads(r), heads(k), heads(v), heads(a)
    y, S_a = rwkv7_recurrence(S_a.astype(f32), rh, heads(decay), kh, vh, -kk, kk * ah)
    bonus = jnp.sum(rh * kh * W['rw_rk'][e].astype(f32), axis=-1, keepdims=True) * vh
    ya = (head_norm(y, W['rw_ln'][e], A_LN_EPS) + bonus.reshape(Bsz, T, A_W)) * g.astype(f32)

    q, kb, vb, ob, ip, fp = jnp.split(ub, B_SPLITS, axis=-1)
    full = jnp.concatenate([conv_buf.astype(q.dtype), jnp.concatenate([q, kb], axis=-1)], axis=1)
    cw = W['ml_conv_w'][e]
    qk = jax.nn.silu(W['ml_conv_b'][e] + sum(full[:, j:j + T] * cw[j] for j in range(B_CONV)))
    qb, kb = jnp.split(qk, 2, axis=-1)
    mh = lambda t: t.astype(f32).reshape(Bsz, T, B_HEADS, B_HD)
    i_pre = (ip + W['ml_i_bias'][e]).astype(f32)
    log_f = jax.nn.log_sigmoid((fp + W['ml_f_bias'][e]).astype(f32))
    hb, C, n, m = mlstm_chunkwise(C.astype(f32), n.astype(f32), m.astype(f32),
                                  mh(qb), mh(kb) * (B_HD ** -0.5), mh(vb), i_pre, log_f)
    hb = jax.nn.sigmoid(ob.astype(f32)) * hb.reshape(Bsz, T, B_W)
    yb = head_norm(hb.reshape(Bsz, T, B_HEADS, B_HD), W['ml_gn'][e], GN_EPS)

    out = jnp.concatenate([ya, yb], axis=-1).astype(xn.dtype) @ W['mix0_w_out'][e]
    new_st = (ua[:, -1].astype(f32), S_a, full[:, T:].astype(f32), C, n, m)
    return out, new_st


def mix_odd(xn, S, pos, W, o):
    f32 = jnp.float32
    Bsz, T, _ = xn.shape
    u = xn @ W['ret_w_in'][o]
    q, k, v, g = jnp.split(u, [C_HEADS * C_DK, 2 * C_HEADS * C_DK, 2 * C_HEADS * C_DK + C_OUT_W], axis=-1)
    q = rope(q.astype(f32).reshape(Bsz, T, C_HEADS, C_DK), pos)
    k = rope(k.astype(f32).reshape(Bsz, T, C_HEADS, C_DK), pos) * (C_DK ** -0.5)
    v = v.astype(f32).reshape(Bsz, T, C_HEADS, C_DV)
    y, S = retention_chunkwise(S.astype(f32), q, k, v)
    y = head_norm(y, W['ret_gn'][o], GN_EPS) * jax.nn.silu(g.astype(f32))
    return y.astype(xn.dtype) @ W['ret_w_out'][o], S


def trunk(x, p, pos, ev_states, od_states, W):
    new_ev = [[] for _ in ev_states]
    new_od = []
    h = x
    for li in range(DEPTH):
        g = W['norm_g'][li]
        xn = rmsnorm(h, g[0])
        if li % 2 == 0:
            e = li // 2
            mix, nst = mix_even(xn, tuple(s[e] for s in ev_states), W, e)
            for lst, s in zip(new_ev, nst):
                lst.append(s)
        else:
            o = li // 2
            mix, S = mix_odd(xn, od_states[o], pos, W, o)
            new_od.append(S)
        h = h + rmsnorm(mix, g[1])
        f = jnp.square(jax.nn.relu(rmsnorm(h, g[2]) @ W['ffn_up'][li])) @ W['ffn_down'][li]
        h = h + rmsnorm(f, g[3])
        h = h + jax.nn.sigmoid(h @ W['ple_gate'][li]) * (p[li] @ W['ple_proj'][li])
    return h, [jnp.stack(l) for l in new_ev], jnp.stack(new_od)


def setup_inputs(seed: int = 0) -> dict:
    key = jax.random.key(seed)
    ks = jax.random.split(key, 40)
    f32 = jnp.float32
    nrm = lambda i, shape, scale: scale * jax.random.normal(ks[i], shape, f32)
    gain = lambda i, shape: 1.0 + nrm(i, shape, 0.05)
    d = {}
    d['x_prompt'] = nrm(0, (BATCH, SEQ, D_MODEL), 1.0)
    d['x_sample'] = nrm(1, (DEC_BATCH, DEC_SEQ, D_MODEL), 1.0)
    d['state_rwkv_shift'] = nrm(2, (N_EVEN, DEC_BATCH, A_SHIFT_W), 1.0)
    d['state_rwkv_S'] = nrm(3, (N_EVEN, DEC_BATCH, A_HEADS, A_HD, A_HD), 0.3)
    d['state_mlstm_conv'] = nrm(4, (N_EVEN, DEC_BATCH, B_CONV - 1, 2 * B_W), 1.0)
    d['state_mlstm_C'] = nrm(5, (N_EVEN, DEC_BATCH, B_HEADS, B_HD, B_HD), 0.1)
    d['state_mlstm_n'] = nrm(6, (N_EVEN, DEC_BATCH, B_HEADS, B_HD), 0.3)
    d['state_mlstm_m'] = 1.0 + nrm(7, (N_EVEN, DEC_BATCH, B_HEADS), 1.0)
    d['state_ret_S'] = nrm(8, (N_ODD, DEC_BATCH, C_HEADS, C_DK, C_DV), 0.05)
    d['p_prompt'] = nrm(9, (DEPTH, BATCH, SEQ, D_PLE), 1.0)
    d['p_sample'] = nrm(10, (DEPTH, DEC_BATCH, DEC_SEQ, D_PLE), 1.0)
    d['norm_g'] = gain(11, (DEPTH, 4, D_MODEL))
    d['ffn_up'] = nrm(12, (DEPTH, D_MODEL, D_FF), D_MODEL ** -0.5)
    d['ffn_down'] = nrm(13, (DEPTH, D_FF, D_MODEL), D_FF ** -0.5)
    d['ple_gate'] = nrm(14, (DEPTH, D_MODEL, D_MODEL), D_MODEL ** -0.5)
    d['ple_proj'] = nrm(15, (DEPTH, D_PLE, D_MODEL), 0.5 * D_PLE ** -0.5)
    d['mix0_w_in'] = nrm(16, (N_EVEN, D_MODEL, MIX0_IN), D_MODEL ** -0.5)
    d['rw_mu'] = jax.random.uniform(ks[17], (N_EVEN, A_SHIFT_W), f32)
    d['rw_w0'] = jnp.linspace(-6.5, -1.5, A_W, dtype=f32)[None, :] + nrm(18, (N_EVEN, A_W), 0.1)
    d['rw_w2'] = nrm(19, (N_EVEN, A_DECAY_LORA, A_W), 0.5 * A_DECAY_LORA ** -0.5)
    d['rw_a0'] = nrm(20, (N_EVEN, A_W), 0.1)
    d['rw_a2'] = nrm(21, (N_EVEN, A_AAA_LORA, A_W), 0.5 * A_AAA_LORA ** -0.5)
    d['rw_g2'] = nrm(22, (N_EVEN, A_GATE_LORA, A_W), A_GATE_LORA ** -0.5)
    d['rw_kk'] = 0.85 + nrm(23, (N_EVEN, A_W), 0.05)
    d['rw_ka'] = 1.0 + nrm(24, (N_EVEN, A_W), 0.05)
    d['rw_rk'] = nrm(25, (N_EVEN, A_HEADS, A_HD), 0.1)
    d['rw_ln'] = gain(26, (N_EVEN, A_W))
    d['ml_conv_w'] = nrm(27, (N_EVEN, B_CONV, 2 * B_W), 0.5)
    d['ml_conv_b'] = nrm(28, (N_EVEN, 2 * B_W), 0.02)
    d['ml_i_bias'] = nrm(29, (N_EVEN, B_HEADS), 0.1)
    d['ml_f_bias'] = jnp.linspace(3.0, 6.0, B_HEADS, dtype=f32)[None, :] + nrm(30, (N_EVEN, B_HEADS), 0.1)
    d['ml_gn'] = gain(31, (N_EVEN, B_W))
    d['mix0_w_out'] = nrm(32, (N_EVEN, MIX0_OUT, D_MODEL), MIX0_OUT ** -0.5)
    d['ret_w_in'] = nrm(33, (N_ODD, D_MODEL, C_IN_W), D_MODEL ** -0.5)
    d['ret_gn'] = gain(34, (N_ODD, C_OUT_W))
    d['ret_w_out'] = nrm(35, (N_ODD, C_OUT_W, D_MODEL), C_OUT_W ** -0.5)
    return d


def reference(x_prompt, x_sample, state_rwkv_shift, state_rwkv_S, state_mlstm_conv, state_mlstm_C,
              state_mlstm_n, state_mlstm_m, state_ret_S, p_prompt, p_sample, norm_g, ffn_up, ffn_down,
              ple_gate, ple_proj, mix0_w_in, rw_mu, rw_w0, rw_w2, rw_a0, rw_a2, rw_g2, rw_kk, rw_ka,
              rw_rk, rw_ln, ml_conv_w, ml_conv_b, ml_i_bias, ml_f_bias, ml_gn, mix0_w_out,
              ret_w_in, ret_gn, ret_w_out):
    W = dict(norm_g=norm_g, ffn_up=ffn_up, ffn_down=ffn_down, ple_gate=ple_gate, ple_proj=ple_proj,
             mix0_w_in=mix0_w_in, rw_mu=rw_mu, rw_w0=rw_w0, rw_w2=rw_w2, rw_a0=rw_a0, rw_a2=rw_a2,
             rw_g2=rw_g2, rw_kk=rw_kk, rw_ka=rw_ka, rw_rk=rw_rk, rw_ln=rw_ln, ml_conv_w=ml_conv_w,
             ml_conv_b=ml_conv_b, ml_i_bias=ml_i_bias, ml_f_bias=ml_f_bias, ml_gn=ml_gn,
             mix0_w_out=mix0_w_out, ret_w_in=ret_w_in, ret_gn=ret_gn, ret_w_out=ret_w_out)
    f32 = jnp.float32
    Bp, Tp = x_prompt.shape[:2]
    Ts = x_sample.shape[1]
    ev0 = (jnp.zeros((N_EVEN, Bp, A_SHIFT_W), f32),
           jnp.zeros((N_EVEN, Bp, A_HEADS, A_HD, A_HD), f32),
           jnp.zeros((N_EVEN, Bp, B_CONV - 1, 2 * B_W), f32),
           jnp.zeros((N_EVEN, Bp, B_HEADS, B_HD, B_HD), f32),
           jnp.zeros((N_EVEN, Bp, B_HEADS, B_HD), f32),
           jnp.zeros((N_EVEN, Bp, B_HEADS), f32))
    od0 = jnp.zeros((N_ODD, Bp, C_HEADS, C_DK, C_DV), f32)
    y_prompt, ev_p, ret_p = trunk(x_prompt, p_prompt, jnp.arange(Tp), ev0, od0, W)
    ev_s_in = (state_rwkv_shift, state_rwkv_S, state_mlstm_conv, state_mlstm_C, state_mlstm_n, state_mlstm_m)
    y_sample, ev_s, ret_s = trunk(x_sample, p_sample, PAST_LEN + jnp.arange(Ts), ev_s_in, state_ret_S, W)
    p_shift, p_Sa, p_conv, p_C, p_n, p_m = ev_p
    s_shift, s_Sa, s_conv, s_C, s_n, s_m = ev_s
    return (y_prompt, y_sample, p_shift, p_Sa, p_conv, p_C, p_n, p_m, ret_p,
            s_shift, s_Sa, s_conv, s_C, s_n, s_m, ret_s)
```

```python
import functools
import math

import jax
import jax.numpy as jnp
from jax import lax
from jax.experimental import pallas as pl
from jax.experimental.pallas import tpu as pltpu

F32 = jnp.float32
BF16 = jnp.bfloat16

D_MODEL = 1024
D_PLE = 256
D_FF = 4 * D_MODEL
RMS_EPS = 1e-6
GN_EPS = 1e-5
CHUNK = 128

A_HEADS = 8
A_HD = 64
A_W = A_HEADS * A_HD
A_DECAY_LORA = 64
A_AAA_LORA = 64
A_GATE_LORA = 128
A_LORA_W = A_DECAY_LORA + A_AAA_LORA + A_GATE_LORA
A_SHIFT_W = 3 * A_W + A_LORA_W
A_LN_EPS = 64e-5

B_HEADS = 4
B_HD = 128
B_W = B_HEADS * B_HD
B_CONV = 4

C_HEADS = 4
C_DK = 256
C_DV = 512
C_OUT_W = C_HEADS * C_DV
ROPE_BASE = 10000.0
PAST_LEN = 16384

LANES = 128
SUBLANES = 8
VMEM_LIMIT = 56 * 1024 * 1024


def _cparams(*sem):
    return pltpu.CompilerParams(dimension_semantics=sem, vmem_limit_bytes=VMEM_LIMIT)


def _const_spec(shape):
    nd = len(shape)
    return pl.BlockSpec(shape, lambda *_: (0,) * nd)


def _rms(x, g):
    return x * lax.rsqrt(jnp.mean(x * x, axis=-1, keepdims=True) + RMS_EPS) * g


def _dot(a, b):
    return jnp.dot(a, b, preferred_element_type=F32)


def _bdot(a, b):
    return jnp.dot(a.astype(BF16), b.astype(BF16), preferred_element_type=F32)


def _dot_nt(a, b):
    return lax.dot_general(a, b, (((1,), (1,)), ((), ())), preferred_element_type=F32)


def _dot_tn(a, b):
    return lax.dot_general(a, b, (((0,), (0,)), ((), ())), preferred_element_type=F32)


def _hilo(x):
    hi = x.astype(BF16)
    lo = (x - hi.astype(F32)).astype(BF16)
    return hi, lo


def _dot_sel(x, sel):
    hi, lo = _hilo(x)
    return _dot(hi, sel) + _dot(lo, sel)


def _dot_sel_left(sel, x):
    hi, lo = _hilo(x)
    return _dot(sel, hi) + _dot(sel, lo)


def _dot3(a, b, dot):
    ah, al = _hilo(a)
    bh, bl = _hilo(b)
    return dot(ah, bh) + dot(ah, bl) + dot(al, bh)


def _sigmoid(x):
    return 1.0 / (1.0 + jnp.exp(-x))


def _softplus(x):
    return jnp.maximum(x, 0.0) + jnp.log1p(jnp.exp(-jnp.abs(x)))


def _log_sigmoid(x):
    return -_softplus(-x)


def _silu(x):
    return x * _sigmoid(x)


def _lane_col(x, idx):
    lane = lax.broadcasted_iota(jnp.int32, x.shape, 1)
    return jnp.sum(jnp.where(lane == idx, x, 0.0), axis=1, keepdims=True)


def _row_pick(x, idx):
    row = lax.broadcasted_iota(jnp.int32, x.shape, 0)
    return jnp.sum(jnp.where(row == idx, x, 0.0), axis=0, keepdims=True)


def _seg_matrix(n, seg):
    sh = seg.bit_length() - 1
    r = jnp.right_shift(lax.broadcasted_iota(jnp.int32, (n, n), 0), sh)
    c = jnp.right_shift(lax.broadcasted_iota(jnp.int32, (n, n), 1), sh)
    return (r == c).astype(BF16)


def _inproj_kernel(x_ref, g_ref, *refs):
    n = len(refs) // 2
    xn = _rms(x_ref[...], g_ref[0:1, :]).astype(BF16)
    for w_ref, o_ref in zip(refs[:n], refs[n:]):
        o_ref[...] = _dot(xn, w_ref[...]).astype(o_ref.dtype)


def _inproj(h, g, ws, dtypes, tm):
    n = h.shape[0]
    return pl.pallas_call(
        _inproj_kernel,
        grid=(n // tm,),
        in_specs=[pl.BlockSpec((tm, D_MODEL), lambda i: (i, 0)), _const_spec(g.shape)]
        + [_const_spec(w.shape) for w in ws],
        out_specs=[pl.BlockSpec((tm, w.shape[1]), lambda i: (i, 0)) for w in ws],
        out_shape=[jax.ShapeDtypeStruct((n, w.shape[1]), dt) for w, dt in zip(ws, dtypes)],
        compiler_params=_cparams("parallel"),
        name="inproj",
    )(h, g, *ws)


def _inproj_ret_kernel(x_ref, g_ref, cos_ref, sin_ref, wq_ref, wk_ref, wv_ref, wg_ref,
                       q_ref, k_ref, v_ref, gate_ref):
    xn = _rms(x_ref[...], g_ref[0:1, :]).astype(BF16)
    cos = cos_ref[...]
    sin = sin_ref[...]
    half = C_DK // 2
    for w_ref, o_ref, scale in ((wq_ref, q_ref, None), (wk_ref, k_ref, C_DK ** -0.5)):
        u = _dot(xn, w_ref[...])
        for hd in range(C_HEADS):
            x1 = u[:, hd * C_DK:hd * C_DK + half]
            x2 = u[:, hd * C_DK + half:(hd + 1) * C_DK]
            y1 = x1 * cos - x2 * sin
            y2 = x1 * sin + x2 * cos
            if scale is not None:
                y1 = y1 * scale
                y2 = y2 * scale
            o_ref[:, hd * C_DK:hd * C_DK + half] = y1.astype(o_ref.dtype)
            o_ref[:, hd * C_DK + half:(hd + 1) * C_DK] = y2.astype(o_ref.dtype)
    v_ref[...] = _dot(xn, wv_ref[...]).astype(v_ref.dtype)
    gate_ref[...] = _dot(xn, wg_ref[...]).astype(gate_ref.dtype)


def _inproj_ret(h, g, cos, sin, ws, tm, seq_tiles, dtypes):
    n = h.shape[0]
    if cos.shape[0] == 1:
        cs_spec = _const_spec(cos.shape)
    else:
        cs_spec = pl.BlockSpec((tm, cos.shape[1]), lambda i: (i % seq_tiles, 0))
    widths = [w.shape[1] for w in ws]
    return pl.pallas_call(
        _inproj_ret_kernel,
        grid=(n // tm,),
        in_specs=[pl.BlockSpec((tm, D_MODEL), lambda i: (i, 0)), _const_spec(g.shape), cs_spec, cs_spec]
        + [_const_spec(w.shape) for w in ws],
        out_specs=[pl.BlockSpec((tm, wd), lambda i: (i, 0)) for wd in widths],
        out_shape=[jax.ShapeDtypeStruct((n, wd), dt) for wd, dt in zip(widths, dtypes)],
        compiler_params=_cparams("parallel"),
        name="inproj_ret",
    )(h, g, cos, sin, *ws)


def _rw_prep_kernel(x_ref, sh_ref, mu_ref, wl_ref, vec_ref, seg_ref,
                    r_ref, w_ref, k_ref, v_ref, a_ref, b_ref, g_ref, bonus_ref, carry):
    rows = x_ref.shape[0]
    nb = sh_ref.shape[0]

    @pl.when(pl.program_id(0) == 0)
    def _():
        carry[...] = sh_ref[...]

    x = x_ref[...]
    if rows == nb:
        prev = carry[...]
    else:
        prev = jnp.concatenate([carry[...], x[:rows - nb]], axis=0)
    carry[...] = x[rows - nb:]
    xs = x + (prev - x) * mu_ref[...]
    r = xs[:, :A_W]
    k = xs[:, A_W:2 * A_W]
    v = xs[:, 2 * A_W:3 * A_W]
    lo = xs[:, 3 * A_W:]
    lane = lax.broadcasted_iota(jnp.int32, lo.shape, 1)
    act = jnp.where(lane < A_DECAY_LORA, jnp.tanh(lo),
                    jnp.where(lane < A_DECAY_LORA + A_AAA_LORA, lo, _sigmoid(lo)))
    lora = _dot(act.astype(BF16), wl_ref[...])
    w0, a0, kkw, kaw, rkw = (vec_ref[i:i + 1, :] for i in range(5))
    w_log = -_softplus(-(w0 + lora[:, :A_W])) - 0.5
    decay = jnp.exp(-jnp.exp(w_log))
    a = _sigmoid(a0 + lora[:, A_W:2 * A_W])
    g = lora[:, 2 * A_W:]
    seg = seg_ref[...]
    kk = k * kkw
    kk = kk / jnp.maximum(jnp.sqrt(_dot_sel(kk * kk, seg)), 1e-12)
    k2 = k * (1.0 + (a - 1.0) * kaw)
    bonus = _dot_sel(r * k2 * rkw, seg) * v
    r_ref[...] = r
    w_ref[...] = decay
    k_ref[...] = k2
    v_ref[...] = v
    a_ref[...] = -kk
    b_ref[...] = kk * a
    g_ref[...] = g
    bonus_ref[...] = bonus


def _rw_prep(ua_tm, shift, mu, wl, vec, seg, rows):
    n = ua_tm.shape[0]
    nb = shift.shape[0]
    row_spec = lambda w: pl.BlockSpec((rows, w), lambda i: (i, 0))
    return pl.pallas_call(
        _rw_prep_kernel,
        grid=(n // rows,),
        in_specs=[row_spec(A_SHIFT_W), _const_spec(shift.shape), _const_spec(mu.shape),
                  _const_spec(wl.shape), _const_spec(vec.shape), _const_spec(seg.shape)],
        out_specs=[row_spec(A_W)] * 8,
        out_shape=[jax.ShapeDtypeStruct((n, A_W), F32)] * 8,
        scratch_shapes=[pltpu.VMEM((nb, A_SHIFT_W), F32)],
        compiler_params=_cparams("arbitrary"),
        name="rw_prep",
    )(ua_tm, shift, mu, wl, vec, seg)


def _rw_rec_kernel(r_ref, w_ref, k_ref, v_ref, a_ref, b_ref, s0_ref, y_ref, so_ref, s_scr):
    tb, bb, _ = r_ref.shape
    pairs = A_HEADS // 2
    j = pl.program_id(1)

    @pl.when(j == 0)
    def _():
        for bi in range(bb):
            for p in range(pairs):
                s_scr[bi * pairs + p, :, 0:A_HD] = s0_ref[bi, 2 * p]
                s_scr[bi * pairs + p, :, A_HD:2 * A_HD] = s0_ref[bi, 2 * p + 1]

    row = lax.broadcasted_iota(jnp.int32, (A_HD, LANES), 0)
    col = lax.broadcasted_iota(jnp.int32, (A_HD, LANES), 1)
    eye2 = (jnp.bitwise_and(col, A_HD - 1) == row).astype(F32)
    seg2 = _seg_matrix(LANES, A_HD)

    def step(t, carry):
        rt, wt, kt, vt, at, bt = (ref[t] for ref in (r_ref, w_ref, k_ref, v_ref, a_ref, b_ref))
        for bi in range(bb):
            for p in range(pairs):
                sl = slice(p * LANES, (p + 1) * LANES)
                rr, ww, kr, vr, ar, br = (x[bi:bi + 1, sl] for x in (rt, wt, kt, vt, at, bt))
                s = s_scr[bi * pairs + p]
                both = jnp.concatenate([s * ar, eye2 * vr], axis=0)
                red = _dot_sel(both, seg2)
                sa = red[:A_HD]
                vb = red[A_HD:]
                sn = s * ww + sa * br + vb * kr
                s_scr[bi * pairs + p] = sn
                yb = _dot((sn * rr).astype(BF16), seg2)
                y_ref[t, bi:bi + 1, sl] = jnp.sum(yb * eye2, axis=0, keepdims=True)
        return carry

    lax.fori_loop(0, tb, step, 0)

    @pl.when(j == pl.num_programs(1) - 1)
    def _():
        for bi in range(bb):
            for p in range(pairs):
                s = s_scr[bi * pairs + p]
                so_ref[bi, 2 * p] = s[:, 0:A_HD]
                so_ref[bi, 2 * p + 1] = s[:, A_HD:2 * A_HD]


def _rw_rec(r, w, k, v, a, b, s0, tb, bb):
    t_len, bsz, _ = r.shape
    seq_spec = pl.BlockSpec((tb, bb, A_W), lambda i, j: (j, i, 0))
    st_spec = pl.BlockSpec((bb, A_HEADS, A_HD, A_HD), lambda i, j: (i, 0, 0, 0))
    return pl.pallas_call(
        _rw_rec_kernel,
        grid=(bsz // bb, t_len // tb),
        in_specs=[seq_spec] * 6 + [st_spec],
        out_specs=[seq_spec, st_spec],
        out_shape=[jax.ShapeDtypeStruct((t_len, bsz, A_W), F32),
                   jax.ShapeDtypeStruct(s0.shape, F32)],
        scratch_shapes=[pltpu.VMEM((bb * A_HEADS // 2, A_HD, LANES), F32)],
        compiler_params=_cparams("parallel", "arbitrary"),
        name="rw_rec",
    )(r, w, k, v, a, b, s0)


def _rw_post_kernel(y_ref, bonus_ref, g_ref, ln_ref, seg_ref, o_ref):
    seg = seg_ref[...]
    y = y_ref[...]
    mu = _dot_sel(y, seg) * (1.0 / A_HD)
    d = y - mu
    var = _dot_sel(d * d, seg) * (1.0 / A_HD)
    yn = d * lax.rsqrt(var + A_LN_EPS) * ln_ref[...]
    o_ref[...] = ((yn + bonus_ref[...]) * g_ref[...]).astype(o_ref.dtype)


def _rw_post(y, bonus, g, ln, seg, rows):
    n = y.shape[0]
    row_spec = pl.BlockSpec((rows, A_W), lambda i: (i, 0))
    return pl.pallas_call(
        _rw_post_kernel,
        grid=(n // rows,),
        in_specs=[row_spec] * 3 + [_const_spec(ln.shape), _const_spec(seg.shape)],
        out_specs=row_spec,
        out_shape=jax.ShapeDtypeStruct((n, A_W), BF16),
        compiler_params=_cparams("parallel"),
        name="rw_post",
    )(y, bonus, g, ln, seg)


def _head_norm_lanes(x, g, eps):
    mu = jnp.mean(x, axis=-1, keepdims=True)
    d = x - mu
    var = jnp.mean(d * d, axis=-1, keepdims=True)
    return d * lax.rsqrt(var + eps) * g


def _mlstm_chunk_kernel(ub_ref, g_ref, gt_ref, conv_ref, cw_ref, cb_ref, biasr_ref, biasc_ref, gn_ref,
                        c0_ref, n0_ref, m0_ref,
                        y_ref, co_ref, no_ref, mo_ref,
                        tail_s, c_s, n_s, m_s):
    L = ub_ref.shape[0]
    c = pl.program_id(1)

    @pl.when(c == 0)
    def _():
        tail_s[...] = conv_ref[0]
        c_s[...] = c0_ref[0]
        n_s[...] = n0_ref[0]
        m_s[...] = m0_ref[0]

    ub = ub_ref[...]
    raw = ub[:, :2 * B_W]
    ext = jnp.concatenate([tail_s[...], raw], axis=0)
    tail_s[...] = raw[L - SUBLANES:]
    acc = cb_ref[...] + raw * cw_ref[B_CONV - 1:B_CONV, :]
    for jj in range(B_CONV - 1):
        sh = B_CONV - 1 - jj
        acc = acc + pltpu.roll(ext, sh, axis=0)[SUBLANES:] * cw_ref[jj:jj + 1, :]
    qk = _silu(acc)
    q_all = qk[:, :B_W]
    k_all = qk[:, B_W:] * (B_HD ** -0.5)

    r_i = lax.broadcasted_iota(jnp.int32, (L, L), 0)
    c_i = lax.broadcasted_iota(jnp.int32, (L, L), 1)
    causal = c_i <= r_i
    tri = causal.astype(BF16)
    tri_u = (r_i <= c_i).astype(BF16)

    gc = g_ref[...] + biasr_ref[...]
    b_cols = _dot_sel_left(tri, _log_sigmoid(gc))
    gr = gt_ref[0] + biasc_ref[...]
    b_rows = _dot_sel(_log_sigmoid(gr), tri_u)

    for hd in range(B_HEADS):
        sl = slice(hd * B_HD, (hd + 1) * B_HD)
        q = q_all[:, sl]
        k = k_all[:, sl]
        v = ub[:, 2 * B_W + hd * B_HD:2 * B_W + (hd + 1) * B_HD]
        o = ub[:, 3 * B_W + hd * B_HD:3 * B_W + (hd + 1) * B_HD]
        bc = _lane_col(b_cols, B_HEADS + hd)
        ic = _lane_col(gc, hd)
        br = b_rows[B_HEADS + hd:B_HEADS + hd + 1, :]
        ir = gr[hd:hd + 1, :]
        m_prev = m_s[hd:hd + 1, 0:1]
        c_prev = c_s[hd]
        n_prev = n_s[hd:hd + 1, :]

        dlog = jnp.where(causal, bc - br + ir, -jnp.inf)
        inter = bc + m_prev
        m_t = jnp.maximum(inter, jnp.max(dlog, axis=1, keepdims=True))
        qb = q.astype(BF16)
        kb = k.astype(BF16)
        vb = v.astype(BF16)
        s = _dot_nt(qb, kb) * jnp.exp(dlog - m_t)
        iw = jnp.exp(inter - m_t)
        num = _dot(s.astype(BF16), vb) + iw * _dot(qb, c_prev.astype(BF16))
        den = jnp.sum(s, axis=1, keepdims=True) + iw * jnp.sum(q * n_prev, axis=1, keepdims=True)
        hh = num / jnp.maximum(jnp.abs(den), jnp.exp(-m_t))

        b_last = _row_pick(bc, L - 1)
        gi = b_last - bc + ic
        m_new = jnp.maximum(b_last + m_prev, jnp.max(gi, axis=0, keepdims=True))
        wi = jnp.exp(gi - m_new)
        wc = jnp.exp(b_last + m_prev - m_new)
        kw = k * wi
        c_s[hd] = wc * c_prev + _dot_tn(kw.astype(BF16), vb)
        n_s[hd:hd + 1, :] = wc * n_prev + jnp.sum(kw, axis=0, keepdims=True)
        m_s[hd:hd + 1, :] = jnp.broadcast_to(m_new, (1, LANES))

        hb = _sigmoid(o) * hh
        y_ref[:, sl] = _head_norm_lanes(hb, gn_ref[:, sl], GN_EPS).astype(y_ref.dtype)

    @pl.when(c == pl.num_programs(1) - 1)
    def _():
        co_ref[0] = c_s[...]
        no_ref[0] = n_s[...]
        mo_ref[0] = m_s[...]


def _mlstm_chunk(ub, gates, gates_t, conv8, cw, cb, bias_r, bias_c, gn, c0, n0, m0, bsz, t_len):
    L = CHUNK
    nc = t_len // L
    row = lambda w: pl.BlockSpec((L, w), lambda b, c: (b * nc + c, 0))
    per_b = lambda shp: pl.BlockSpec((1,) + shp, lambda b, c: (b,) + (0,) * len(shp))
    return pl.pallas_call(
        _mlstm_chunk_kernel,
        grid=(bsz, nc),
        in_specs=[row(4 * B_W), row(LANES),
                  pl.BlockSpec((1, SUBLANES, L), lambda b, c: (b, 0, c)),
                  per_b((SUBLANES, 2 * B_W)),
                  _const_spec(cw.shape), _const_spec(cb.shape), _const_spec(bias_r.shape),
                  _const_spec(bias_c.shape), _const_spec(gn.shape),
                  per_b((B_HEADS, B_HD, B_HD)), per_b((SUBLANES, B_HD)), per_b((SUBLANES, LANES))],
        out_specs=[row(B_W), per_b((B_HEADS, B_HD, B_HD)), per_b((SUBLANES, B_HD)),
                   per_b((SUBLANES, LANES))],
        out_shape=[jax.ShapeDtypeStruct((bsz * t_len, B_W), BF16),
                   jax.ShapeDtypeStruct((bsz, B_HEADS, B_HD, B_HD), F32),
                   jax.ShapeDtypeStruct((bsz, SUBLANES, B_HD), F32),
                   jax.ShapeDtypeStruct((bsz, SUBLANES, LANES), F32)],
        scratch_shapes=[pltpu.VMEM((SUBLANES, 2 * B_W), F32),
                        pltpu.VMEM((B_HEADS, B_HD, B_HD), F32),
                        pltpu.VMEM((SUBLANES, B_HD), F32),
                        pltpu.VMEM((SUBLANES, LANES), F32)],
        compiler_params=_cparams("parallel", "arbitrary"),
        name="mlstm_chunk",
    )(ub, gates, gates_t, conv8, cw, cb, bias_r, bias_c, gn, c0, n0, m0)


def _mlstm_step_kernel(ub_ref, g_ref, conv_ref, cw_ref, cb_ref, biasr_ref, gn_ref, c_ref, n_ref, m_ref,
                       y_ref, co_ref, no_ref, mo_ref):
    bb = ub_ref.shape[0]
    ub = ub_ref[...]
    raw = ub[:, :2 * B_W]
    conv = conv_ref[...]
    acc = cb_ref[...] + raw * cw_ref[B_CONV - 1:B_CONV, :]
    for jj in range(B_CONV - 1):
        acc = acc + conv[:, jj * 2 * B_W:(jj + 1) * 2 * B_W] * cw_ref[jj:jj + 1, :]
    qk = _silu(acc)
    q_all = qk[:, :B_W]
    k_all = qk[:, B_W:] * (B_HD ** -0.5)
    gc = g_ref[...] + biasr_ref[...]
    lf = _log_sigmoid(gc)
    m_all = m_ref[...]
    row_pad = lax.broadcasted_iota(jnp.int32, (LANES, LANES), 0)
    pad = jnp.zeros((LANES - bb, B_HD), F32)
    m_out = jnp.zeros((bb, LANES), F32)
    lane_m = lax.broadcasted_iota(jnp.int32, (bb, LANES), 1)
    for hd in range(B_HEADS):
        sl = slice(hd * B_HD, (hd + 1) * B_HD)
        q = q_all[:, sl]
        k = k_all[:, sl]
        v = ub[:, 2 * B_W + hd * B_HD:2 * B_W + (hd + 1) * B_HD]
        o = ub[:, 3 * B_W + hd * B_HD:3 * B_W + (hd + 1) * B_HD]
        ic = _lane_col(gc, hd)
        f = _lane_col(lf, B_HEADS + hd)
        m_prev = _lane_col(m_all, hd)
        n_prev = n_ref[:, sl]
        m_t = jnp.maximum(f + m_prev, ic)
        s = jnp.sum(q * k, axis=1, keepdims=True) * jnp.exp(ic - m_t)
        iw = jnp.exp(f + m_prev - m_t)
        wi = jnp.exp(ic - m_t)
        qb = q.astype(BF16)
        kw_t = jnp.concatenate([k * wi, pad], axis=0).T
        v_pad = jnp.concatenate([v, pad], axis=0)
        qc = jnp.zeros((bb, B_HD), F32)
        row_b = lax.broadcasted_iota(jnp.int32, (bb, B_HD), 0)
        for bi in range(bb):
            c_prev = c_ref[bi, hd]
            qc = qc + jnp.where(row_b == bi, _dot(qb, c_prev.astype(BF16)), 0.0)
            v_one = jnp.where(row_pad == bi, v_pad, 0.0)
            co_ref[bi, hd] = _row_pick(iw, bi) * c_prev + _dot3(kw_t, v_one, _dot)
        num = s * v + iw * qc
        den = s + iw * jnp.sum(q * n_prev, axis=1, keepdims=True)
        hh = num / jnp.maximum(jnp.abs(den), jnp.exp(-m_t))
        no_ref[:, sl] = iw * n_prev + wi * k
        m_out = jnp.where(lane_m == hd, m_t, m_out)
        hb = _sigmoid(o) * hh
        y_ref[:, sl] = _head_norm_lanes(hb, gn_ref[:, sl], GN_EPS).astype(y_ref.dtype)
    mo_ref[...] = m_out


def _mlstm_step(ub, gates, conv, cw, cb, bias_r, gn, c0, n0, m0, bb):
    bsz = ub.shape[0]
    row = lambda w: pl.BlockSpec((bb, w), lambda i: (i, 0))
    c_spec = pl.BlockSpec((bb, B_HEADS, B_HD, B_HD), lambda i: (i, 0, 0, 0))
    return pl.pallas_call(
        _mlstm_step_kernel,
        grid=(bsz // bb,),
        in_specs=[row(4 * B_W), row(LANES), row((B_CONV - 1) * 2 * B_W),
                  _const_spec(cw.shape), _const_spec(cb.shape), _const_spec(bias_r.shape),
                  _const_spec(gn.shape), c_spec, row(B_W), row(LANES)],
        out_specs=[row(B_W), c_spec, row(B_W), row(LANES)],
        out_shape=[jax.ShapeDtypeStruct((bsz, B_W), BF16),
                   jax.ShapeDtypeStruct(c0.shape, F32),
                   jax.ShapeDtypeStruct((bsz, B_W), F32),
                   jax.ShapeDtypeStruct((bsz, LANES), F32)],
        compiler_params=_cparams("parallel"),
        name="mlstm_step",
    )(ub, gates, conv, cw, cb, bias_r, gn, c0, n0, m0)


def _ret_log_gamma(hd):
    return math.log1p(-(2.0 ** (-5.0 - hd)))


def _ret_chunk_kernel(q_ref, k_ref, v_ref, g_ref, gn_ref, s0_ref, y_ref, so_ref, s_s):
    L = q_ref.shape[0]
    c = pl.program_id(1)

    @pl.when(c == 0)
    def _():
        s_s[...] = s0_ref[0]

    r_i = lax.broadcasted_iota(jnp.int32, (L, L), 0)
    c_i = lax.broadcasted_iota(jnp.int32, (L, L), 1)
    diff = (r_i - c_i).astype(F32)
    idx = lax.broadcasted_iota(jnp.int32, (L, C_DK), 0).astype(F32)
    for hd in range(C_HEADS):
        lg = _ret_log_gamma(hd)
        mask = jnp.where(diff >= 0, jnp.exp(jnp.maximum(diff, 0.0) * lg), 0.0)
        q_dec = jnp.exp((idx + 1.0) * lg)
        k_dec = jnp.exp((L - 1.0 - idx) * lg)
        c_dec = math.exp(L * lg)
        q = q_ref[:, hd * C_DK:(hd + 1) * C_DK]
        k = k_ref[:, hd * C_DK:(hd + 1) * C_DK]
        v = v_ref[:, hd * C_DV:(hd + 1) * C_DV]
        s_prev = s_s[hd]
        s = _dot_nt(q, k) * mask
        o = _dot(s.astype(BF16), v) + _dot((q.astype(F32) * q_dec).astype(BF16), s_prev.astype(BF16))
        s_s[hd] = c_dec * s_prev + _dot_tn((k.astype(F32) * k_dec).astype(BF16), v)
        sl = slice(hd * C_DV, (hd + 1) * C_DV)
        y = _head_norm_lanes(o, gn_ref[:, sl], GN_EPS) * _silu(g_ref[:, sl])
        y_ref[:, sl] = y.astype(y_ref.dtype)

    @pl.when(c == pl.num_programs(1) - 1)
    def _():
        so_ref[0] = s_s[...]


def _ret_chunk(q, k, v, g, gn, s0, bsz, t_len):
    L = CHUNK
    nc = t_len // L
    row = lambda w: pl.BlockSpec((L, w), lambda b, c: (b * nc + c, 0))
    st = pl.BlockSpec((1, C_HEADS, C_DK, C_DV), lambda b, c: (b, 0, 0, 0))
    return pl.pallas_call(
        _ret_chunk_kernel,
        grid=(bsz, nc),
        in_specs=[row(C_HEADS * C_DK), row(C_HEADS * C_DK), row(C_OUT_W), row(C_OUT_W),
                  _const_spec(gn.shape), st],
        out_specs=[row(C_OUT_W), st],
        out_shape=[jax.ShapeDtypeStruct((bsz * t_len, C_OUT_W), BF16),
                   jax.ShapeDtypeStruct(s0.shape, F32)],
        scratch_shapes=[pltpu.VMEM((C_HEADS, C_DK, C_DV), F32)],
        compiler_params=_cparams("parallel", "arbitrary"),
        name="ret_chunk",
    )(q, k, v, g, gn, s0)


def _ret_step_kernel(q_ref, k_ref, v_ref, g_ref, gn_ref, s_ref, y_ref, so_ref):
    bb = q_ref.shape[0]
    hd = pl.program_id(1)
    gam = 1.0 - jnp.exp2(-5.0 - jnp.full((1, 1), hd).astype(F32))
    q = q_ref[...]
    k = k_ref[...]
    v = v_ref[...]
    s = jnp.sum(q * k, axis=1, keepdims=True)
    pad_k = jnp.zeros((LANES - bb, C_DK), F32)
    pad_v = jnp.zeros((LANES - bb, C_DV), F32)
    k_t = jnp.concatenate([k, pad_k], axis=0).T
    v_pad = jnp.concatenate([v, pad_v], axis=0)
    row_pad = lax.broadcasted_iota(jnp.int32, (LANES, C_DV), 0)
    row_b = lax.broadcasted_iota(jnp.int32, (bb, C_DV), 0)
    qg = (q * gam).astype(BF16)
    qs = jnp.zeros((bb, C_DV), F32)
    for bi in range(bb):
        s_prev = s_ref[bi, 0]
        qs = qs + jnp.where(row_b == bi, _dot(qg, s_prev.astype(BF16)), 0.0)
        v_one = jnp.where(row_pad == bi, v_pad, 0.0)
        so_ref[bi, 0] = gam * s_prev + _dot3(k_t, v_one, _dot)
    o = s * v + qs
    y = _head_norm_lanes(o, gn_ref[...], GN_EPS) * _silu(g_ref[...])
    y_ref[...] = y.astype(y_ref.dtype)


def _ret_step(q, k, v, g, gn, s0, bb):
    bsz = q.shape[0]
    st = pl.BlockSpec((bb, 1, C_DK, C_DV), lambda i, h: (i, h, 0, 0))
    return pl.pallas_call(
        _ret_step_kernel,
        grid=(bsz // bb, C_HEADS),
        in_specs=[pl.BlockSpec((bb, C_DK), lambda i, h: (i, h)),
                  pl.BlockSpec((bb, C_DK), lambda i, h: (i, h)),
                  pl.BlockSpec((bb, C_DV), lambda i, h: (i, h)),
                  pl.BlockSpec((bb, C_DV), lambda i, h: (i, h)),
                  pl.BlockSpec((1, C_DV), lambda i, h: (0, h)), st],
        out_specs=[pl.BlockSpec((bb, C_DV), lambda i, h: (i, h)), st],
        out_shape=[jax.ShapeDtypeStruct((bsz, C_OUT_W), BF16),
                   jax.ShapeDtypeStruct(s0.shape, F32)],
        compiler_params=_cparams("parallel", "parallel"),
        name="ret_step",
    )(q, k, v, g, gn, s0)


def _post_kernel(*refs, n_mix):
    h_ref = refs[0]
    mix_refs = refs[1:1 + n_mix]
    wout_refs = refs[1 + n_mix:1 + 2 * n_mix]
    (p_ref, g_ref, up_ref, down_ref, gate_ref, proj_ref, o_ref, h1_s, xn_s, acc_s) = refs[1 + 2 * n_mix:]
    f = pl.program_id(1)

    @pl.when(f == 0)
    def _():
        mix = _dot(mix_refs[0][...], wout_refs[0][...])
        for m_ref, w_ref in zip(mix_refs[1:], wout_refs[1:]):
            mix = mix + _dot(m_ref[...], w_ref[...])
        h1 = h_ref[...] + _rms(mix, g_ref[1:2, :])
        h1_s[...] = h1
        xn_s[...] = _rms(h1, g_ref[2:3, :]).astype(BF16)
        acc_s[...] = jnp.zeros_like(acc_s)

    hid = jnp.square(jnp.maximum(_dot(xn_s[...], up_ref[...]), 0.0))
    acc_s[...] += _dot(hid.astype(BF16), down_ref[...])

    @pl.when(f == pl.num_programs(1) - 1)
    def _():
        h2 = h1_s[...] + _rms(acc_s[...], g_ref[3:4, :])
        gate = _sigmoid(_dot(h2.astype(BF16), gate_ref[...]))
        emb = _dot(p_ref[...].astype(BF16), proj_ref[...])
        o_ref[...] = h2 + gate * emb


def _post(h, mixes, wouts, p, g, up, down, gate, proj, tm, tf):
    n = h.shape[0]
    n_mix = len(mixes)
    row = lambda w: pl.BlockSpec((tm, w), lambda i, f: (i, 0))
    return pl.pallas_call(
        functools.partial(_post_kernel, n_mix=n_mix),
        grid=(n // tm, D_FF // tf),
        in_specs=[row(D_MODEL)] + [row(m.shape[1]) for m in mixes]
        + [_const_spec(w.shape) for w in wouts]
        + [row(D_PLE), _const_spec(g.shape),
           pl.BlockSpec((D_MODEL, tf), lambda i, f: (0, f)),
           pl.BlockSpec((tf, D_MODEL), lambda i, f: (f, 0)),
           _const_spec(gate.shape), _const_spec(proj.shape)],
        out_specs=row(D_MODEL),
        out_shape=jax.ShapeDtypeStruct((n, D_MODEL), F32),
        scratch_shapes=[pltpu.VMEM((tm, D_MODEL), F32), pltpu.VMEM((tm, D_MODEL), BF16),
                        pltpu.VMEM((tm, D_MODEL), F32)],
        compiler_params=_cparams("parallel", "arbitrary"),
        name="post",
    )(h, *mixes, *wouts, p, g, up, down, gate, proj)


def _prep_weights(W):
    f = {}
    w_in = W['mix0_w_in'][0]
    f['w_a'] = w_in[:, :A_SHIFT_W].astype(BF16)
    f['w_b'] = w_in[:, A_SHIFT_W:A_SHIFT_W + 4 * B_W].astype(BF16)
    f['w_g'] = jnp.pad(w_in[:, A_SHIFT_W + 4 * B_W:], ((0, 0), (0, LANES - 2 * B_HEADS))).astype(BF16)
    wl = jnp.zeros((A_LORA_W, 3 * A_W), F32)
    wl = wl.at[:A_DECAY_LORA, :A_W].set(W['rw_w2'][0])
    wl = wl.at[A_DECAY_LORA:A_DECAY_LORA + A_AAA_LORA, A_W:2 * A_W].set(W['rw_a2'][0])
    wl = wl.at[A_DECAY_LORA + A_AAA_LORA:, 2 * A_W:].set(W['rw_g2'][0])
    f['w_lora'] = wl.astype(BF16)
    f['rw_vec'] = jnp.concatenate(
        [W['rw_w0'], W['rw_a0'], W['rw_kk'], W['rw_ka'], W['rw_rk'][0].reshape(1, A_W),
         jnp.zeros((3, A_W), F32)], axis=0)
    f['rw_mu'] = W['rw_mu']
    f['rw_ln'] = W['rw_ln']
    f['seg'] = (jnp.arange(A_W)[:, None] // A_HD == jnp.arange(A_W)[None, :] // A_HD).astype(BF16)
    f['cw'] = W['ml_conv_w'][0]
    f['cb'] = W['ml_conv_b']
    gate_bias = jnp.concatenate([W['ml_i_bias'][0], W['ml_f_bias'][0]])
    f['bias_r'] = jnp.pad(gate_bias, (0, LANES - 2 * B_HEADS)).reshape(1, LANES)
    f['bias_c'] = gate_bias.reshape(2 * B_HEADS, 1)
    f['ml_gn'] = W['ml_gn']
    w_out0 = W['mix0_w_out'][0].astype(BF16)
    f['w_out_a'] = w_out0[:A_W]
    f['w_out_b'] = w_out0[A_W:]
    w_ret = W['ret_w_in'][0]
    nq = C_HEADS * C_DK
    f['w_rq'] = w_ret[:, :nq].astype(BF16)
    f['w_rk'] = w_ret[:, nq:2 * nq].astype(BF16)
    f['w_rv'] = w_ret[:, 2 * nq:2 * nq + C_OUT_W].astype(BF16)
    f['w_rg'] = w_ret[:, 2 * nq + C_OUT_W:].astype(BF16)
    f['ret_gn'] = W['ret_gn']
    f['w_ret_out'] = W['ret_w_out'][0].astype(BF16)
    for name in ('ffn_up', 'ffn_down', 'ple_gate', 'ple_proj'):
        f[name] = W[name].astype(BF16)
    f['norm_g'] = W['norm_g']
    return f


def _rope_tables(pos):
    half = C_DK // 2
    inv = ROPE_BASE ** (-jnp.arange(half, dtype=F32) / half)
    ang = pos.astype(F32)[:, None] * inv[None, :]
    return jnp.cos(ang), jnp.sin(ang)


def _even_layer(h, p, st, f, bsz, t_len):
    shift, s_a, conv, c0, n0, m0 = st
    n = bsz * t_len
    seq = t_len > 1
    tm = 512 if seq else n
    ua, ub, gates = _inproj(h, f['norm_g'][0], [f['w_a'], f['w_b'], f['w_g']], [F32, F32, F32], tm)

    to_tm = lambda x: x.reshape(bsz, t_len, -1).transpose(1, 0, 2).reshape(n, -1)
    from_tm = lambda x: x.reshape(t_len, bsz, -1).transpose(1, 0, 2).reshape(n, -1)
    rows = 512 if seq else n
    outs = _rw_prep(to_tm(ua), shift, f['rw_mu'], f['w_lora'], f['rw_vec'], f['seg'], rows)
    r, w, k, v, a, b, g, bonus = outs
    as3 = lambda x: x.reshape(t_len, bsz, A_W)
    y, s_a_new = _rw_rec(as3(r), as3(w), as3(k), as3(v), as3(a), as3(b), s_a,
                         tb=128 if seq else 1, bb=SUBLANES)
    ya = from_tm(_rw_post(y.reshape(n, A_W), bonus, g, f['rw_ln'], f['seg'], rows))

    if seq:
        conv8 = jnp.pad(conv, ((0, 0), (SUBLANES - (B_CONV - 1), 0), (0, 0)))
        gates_t = gates.reshape(bsz, t_len, LANES)[:, :, :SUBLANES].transpose(0, 2, 1)
        n0p = jnp.pad(n0, ((0, 0), (0, SUBLANES - B_HEADS), (0, 0)))
        m0p = jnp.broadcast_to(jnp.pad(m0, ((0, 0), (0, SUBLANES - B_HEADS)))[:, :, None],
                               (bsz, SUBLANES, LANES))
        yb, c_new, n_new, m_new = _mlstm_chunk(ub, gates, gates_t, conv8, f['cw'], f['cb'], f['bias_r'],
                                               f['bias_c'], f['ml_gn'], c0, n0p, m0p, bsz, t_len)
        n_new = n_new[:, :B_HEADS]
        m_new = m_new[:, :B_HEADS, 0]
    else:
        m0p = jnp.pad(m0, ((0, 0), (0, LANES - B_HEADS)))
        yb, c_new, n_new, m_new = _mlstm_step(ub, gates, conv.reshape(bsz, -1), f['cw'], f['cb'],
                                              f['bias_r'], f['ml_gn'], c0, n0.reshape(bsz, B_W), m0p,
                                              bb=SUBLANES)
        n_new = n_new.reshape(bsz, B_HEADS, B_HD)
        m_new = m_new[:, :B_HEADS]

    h_next = _post(h, [ya, yb], [f['w_out_a'], f['w_out_b']], p, f['norm_g'][0], f['ffn_up'][0],
                   f['ffn_down'][0], f['ple_gate'][0], f['ple_proj'][0], tm=tm, tf=512)

    shift_new = ua.reshape(bsz, t_len, A_SHIFT_W)[:, -1]
    qk_raw = ub.reshape(bsz, t_len, 4 * B_W)[:, :, :2 * B_W]
    conv_new = jnp.concatenate([conv, qk_raw], axis=1)[:, -(B_CONV - 1):]
    return h_next, (shift_new, s_a_new, conv_new, c_new, n_new, m_new)


def _odd_layer(h, p, s0, pos, f, bsz, t_len):
    n = bsz * t_len
    seq = t_len > 1
    cos, sin = _rope_tables(pos)
    tm = 256 if seq else n
    ws = [f['w_rq'], f['w_rk'], f['w_rv'], f['w_rg']]
    if seq:
        q, k, v, g = _inproj_ret(h, f['norm_g'][1], cos, sin, ws, tm, t_len // tm, [BF16, BF16, BF16, F32])
        y, s_new = _ret_chunk(q, k, v, g, f['ret_gn'], s0, bsz, t_len)
    else:
        q, k, v, g = _inproj_ret(h, f['norm_g'][1], cos, sin, ws, tm, 1, [F32, F32, F32, F32])
        y, s_new = _ret_step(q, k, v, g, f['ret_gn'], s0, bb=SUBLANES)
    h_next = _post(h, [y], [f['w_ret_out']], p, f['norm_g'][1], f['ffn_up'][1], f['ffn_down'][1],
                   f['ple_gate'][1], f['ple_proj'][1], tm=512 if seq else n, tf=512)
    return h_next, s_new


def _trunk(x, p, pos, ev_states, od_state, f):
    bsz, t_len, _ = x.shape
    n = bsz * t_len
    h = x.reshape(n, D_MODEL)
    h, ev_new = _even_layer(h, p[0].reshape(n, D_PLE), tuple(s[0] for s in ev_states), f, bsz, t_len)
    h, od_new = _odd_layer(h, p[1].reshape(n, D_PLE), od_state[0], pos, f, bsz, t_len)
    return h.reshape(bsz, t_len, D_MODEL), [s[None] for s in ev_new], od_new[None]


def kernel(x_prompt, x_sample, state_rwkv_shift, state_rwkv_S, state_mlstm_conv, state_mlstm_C,
           state_mlstm_n, state_mlstm_m, state_ret_S, p_prompt, p_sample, norm_g, ffn_up, ffn_down,
           ple_gate, ple_proj, mix0_w_in, rw_mu, rw_w0, rw_w2, rw_a0, rw_a2, rw_g2, rw_kk, rw_ka,
           rw_rk, rw_ln, ml_conv_w, ml_conv_b, ml_i_bias, ml_f_bias, ml_gn, mix0_w_out,
           ret_w_in, ret_gn, ret_w_out):
    W = dict(norm_g=norm_g, ffn_up=ffn_up, ffn_down=ffn_down, ple_gate=ple_gate, ple_proj=ple_proj,
             mix0_w_in=mix0_w_in, rw_mu=rw_mu, rw_w0=rw_w0, rw_w2=rw_w2, rw_a0=rw_a0, rw_a2=rw_a2,
             rw_g2=rw_g2, rw_kk=rw_kk, rw_ka=rw_ka, rw_rk=rw_rk, rw_ln=rw_ln, ml_conv_w=ml_conv_w,
             ml_conv_b=ml_conv_b, ml_i_bias=ml_i_bias, ml_f_bias=ml_f_bias, ml_gn=ml_gn,
             mix0_w_out=mix0_w_out, ret_w_in=ret_w_in, ret_gn=ret_gn, ret_w_out=ret_w_out)
    f = _prep_weights(W)
    bp, tp = x_prompt.shape[:2]
    ts = x_sample.shape[1]
    ev0 = (jnp.zeros((1, bp, A_SHIFT_W), F32),
           jnp.zeros((1, bp, A_HEADS, A_HD, A_HD), F32),
           jnp.zeros((1, bp, B_CONV - 1, 2 * B_W), F32),
           jnp.zeros((1, bp, B_HEADS, B_HD, B_HD), F32),
           jnp.zeros((1, bp, B_HEADS, B_HD), F32),
           jnp.zeros((1, bp, B_HEADS), F32))
    od0 = jnp.zeros((1, bp, C_HEADS, C_DK, C_DV), F32)
    y_prompt, ev_p, ret_p = _trunk(x_prompt, p_prompt, jnp.arange(tp), ev0, od0, f)
    ev_s_in = (state_rwkv_shift, state_rwkv_S, state_mlstm_conv, state_mlstm_C, state_mlstm_n,
               state_mlstm_m)
    y_sample, ev_s, ret_s = _trunk(x_sample, p_sample, PAST_LEN + jnp.arange(ts), ev_s_in, state_ret_S, f)
    return (y_prompt, y_sample, *ev_p, ret_p, *ev_s, ret_s)
```

```python
import functools
import math

import jax
import jax.numpy as jnp
from jax import lax
from jax.experimental import pallas as pl
from jax.experimental.pallas import tpu as pltpu

F32 = jnp.float32
BF16 = jnp.bfloat16

D_MODEL = 1024
D_PLE = 256
D_FF = 4 * D_MODEL
RMS_EPS = 1e-6
GN_EPS = 1e-5
CHUNK = 128

A_HEADS = 8
A_HD = 64
A_W = A_HEADS * A_HD
A_DECAY_LORA = 64
A_AAA_LORA = 64
A_GATE_LORA = 128
A_LORA_W = A_DECAY_LORA + A_AAA_LORA + A_GATE_LORA
A_SHIFT_W = 3 * A_W + A_LORA_W
A_LN_EPS = 64e-5

B_HEADS = 4
B_HD = 128
B_W = B_HEADS * B_HD
B_CONV = 4

C_HEADS = 4
C_DK = 256
C_DV = 512
C_OUT_W = C_HEADS * C_DV
ROPE_BASE = 10000.0
PAST_LEN = 16384

LANES = 128
SUBLANES = 8
VMEM_LIMIT = 56 * 1024 * 1024


def _cparams(*sem):
    return pltpu.CompilerParams(dimension_semantics=sem, vmem_limit_bytes=VMEM_LIMIT)


def _const_spec(shape):
    nd = len(shape)
    return pl.BlockSpec(shape, lambda *_: (0,) * nd)


def _rms(x, g):
    return x * lax.rsqrt(jnp.mean(x * x, axis=-1, keepdims=True) + RMS_EPS) * g


def _dot(a, b):
    return jnp.dot(a, b, preferred_element_type=F32)


def _bdot(a, b):
    return jnp.dot(a.astype(BF16), b.astype(BF16), preferred_element_type=F32)


def _dot_nt(a, b):
    return lax.dot_general(a, b, (((1,), (1,)), ((), ())), preferred_element_type=F32)


def _dot_tn(a, b):
    return lax.dot_general(a, b, (((0,), (0,)), ((), ())), preferred_element_type=F32)


def _hilo(x):
    hi = x.astype(BF16)
    lo = (x - hi.astype(F32)).astype(BF16)
    return hi, lo


def _dot_sel(x, sel):
    hi, lo = _hilo(x)
    return _dot(hi, sel) + _dot(lo, sel)


def _dot_sel_left(sel, x):
    hi, lo = _hilo(x)
    return _dot(sel, hi) + _dot(sel, lo)


def _dot3(a, b, dot):
    ah, al = _hilo(a)
    bh, bl = _hilo(b)
    return dot(ah, bh) + dot(ah, bl) + dot(al, bh)


def _sigmoid(x):
    return 1.0 / (1.0 + jnp.exp(-x))


def _softplus(x):
    return jnp.maximum(x, 0.0) + jnp.log1p(jnp.exp(-jnp.abs(x)))


def _log_sigmoid(x):
    return -_softplus(-x)


def _silu(x):
    return x * _sigmoid(x)


def _lane_col(x, idx):
    lane = lax.broadcasted_iota(jnp.int32, x.shape, 1)
    return jnp.sum(jnp.where(lane == idx, x, 0.0), axis=1, keepdims=True)


def _row_pick(x, idx):
    row = lax.broadcasted_iota(jnp.int32, x.shape, 0)
    return jnp.sum(jnp.where(row == idx, x, 0.0), axis=0, keepdims=True)


def _seg_matrix(n, seg):
    sh = seg.bit_length() - 1
    r = jnp.right_shift(lax.broadcasted_iota(jnp.int32, (n, n), 0), sh)
    c = jnp.right_shift(lax.broadcasted_iota(jnp.int32, (n, n), 1), sh)
    return (r == c).astype(BF16)


def _inproj_kernel(x_ref, g_ref, *refs):
    n = len(refs) // 2
    xn = _rms(x_ref[...], g_ref[0:1, :]).astype(BF16)
    for w_ref, o_ref in zip(refs[:n], refs[n:]):
        o_ref[...] = _dot(xn, w_ref[...]).astype(o_ref.dtype)


def _inproj(h, g, ws, dtypes, tm):
    n = h.shape[0]
    return pl.pallas_call(
        _inproj_kernel,
        grid=(n // tm,),
        in_specs=[pl.BlockSpec((tm, D_MODEL), lambda i: (i, 0)), _const_spec(g.shape)]
        + [_const_spec(w.shape) for w in ws],
        out_specs=[pl.BlockSpec((tm, w.shape[1]), lambda i: (i, 0)) for w in ws],
        out_shape=[jax.ShapeDtypeStruct((n, w.shape[1]), dt) for w, dt in zip(ws, dtypes)],
        compiler_params=_cparams("parallel"),
        name="inproj",
    )(h, g, *ws)


def _inproj_ret_kernel(x_ref, g_ref, cos_ref, sin_ref, wq_ref, wk_ref, wv_ref, wg_ref,
                       q_ref, k_ref, v_ref, gate_ref):
    xn = _rms(x_ref[...], g_ref[0:1, :]).astype(BF16)
    cos = cos_ref[...]
    sin = sin_ref[...]
    half = C_DK // 2
    for w_ref, o_ref, scale in ((wq_ref, q_ref, None), (wk_ref, k_ref, C_DK ** -0.5)):
        u = _dot(xn, w_ref[...])
        for hd in range(C_HEADS):
            x1 = u[:, hd * C_DK:hd * C_DK + half]
            x2 = u[:, hd * C_DK + half:(hd + 1) * C_DK]
            y1 = x1 * cos - x2 * sin
            y2 = x1 * sin + x2 * cos
            if scale is not None:
                y1 = y1 * scale
                y2 = y2 * scale
            o_ref[:, hd * C_DK:hd * C_DK + half] = y1.astype(o_ref.dtype)
            o_ref[:, hd * C_DK + half:(hd + 1) * C_DK] = y2.astype(o_ref.dtype)
    v_ref[...] = _dot(xn, wv_ref[...]).astype(v_ref.dtype)
    gate_ref[...] = _dot(xn, wg_ref[...]).astype(gate_ref.dtype)


def _inproj_ret(h, g, cos, sin, ws, tm, seq_tiles, dtypes):
    n = h.shape[0]
    if cos.shape[0] == 1:
        cs_spec = _const_spec(cos.shape)
    else:
        cs_spec = pl.BlockSpec((tm, cos.shape[1]), lambda i: (i % seq_tiles, 0))
    widths = [w.shape[1] for w in ws]
    return pl.pallas_call(
        _inproj_ret_kernel,
        grid=(n // tm,),
        in_specs=[pl.BlockSpec((tm, D_MODEL), lambda i: (i, 0)), _const_spec(g.shape), cs_spec, cs_spec]
        + [_const_spec(w.shape) for w in ws],
        out_specs=[pl.BlockSpec((tm, wd), lambda i: (i, 0)) for wd in widths],
        out_shape=[jax.ShapeDtypeStruct((n, wd), dt) for wd, dt in zip(widths, dtypes)],
        compiler_params=_cparams("parallel"),
        name="inproj_ret",
    )(h, g, cos, sin, *ws)


def _rw_prep_kernel(x_ref, sh_ref, mu_ref, wl_ref, vec_ref, seg_ref,
                    r_ref, w_ref, k_ref, v_ref, a_ref, b_ref, g_ref, bonus_ref, carry):
    rows = x_ref.shape[0]
    nb = sh_ref.shape[0]

    @pl.when(pl.program_id(0) == 0)
    def _():
        carry[...] = sh_ref[...]

    x = x_ref[...]
    if rows == nb:
        prev = carry[...]
    else:
        prev = jnp.concatenate([carry[...], x[:rows - nb]], axis=0)
    carry[...] = x[rows - nb:]
    xs = x + (prev - x) * mu_ref[...]
    r = xs[:, :A_W]
    k = xs[:, A_W:2 * A_W]
    v = xs[:, 2 * A_W:3 * A_W]
    lo = xs[:, 3 * A_W:]
    lane = lax.broadcasted_iota(jnp.int32, lo.shape, 1)
    act = jnp.where(lane < A_DECAY_LORA, jnp.tanh(lo),
                    jnp.where(lane < A_DECAY_LORA + A_AAA_LORA, lo, _sigmoid(lo)))
    lora = _dot(act.astype(BF16), wl_ref[...])
    w0, a0, kkw, kaw, rkw = (vec_ref[i:i + 1, :] for i in range(5))
    w_log = -_softplus(-(w0 + lora[:, :A_W])) - 0.5
    decay = jnp.exp(-jnp.exp(w_log))
    a = _sigmoid(a0 + lora[:, A_W:2 * A_W])
    g = lora[:, 2 * A_W:]
    seg = seg_ref[...]
    kk = k * kkw
    kk = kk / jnp.maximum(jnp.sqrt(_dot_sel(kk * kk, seg)), 1e-12)
    k2 = k * (1.0 + (a - 1.0) * kaw)
    bonus = _dot_sel(r * k2 * rkw, seg) * v
    r_ref[...] = r
    w_ref[...] = decay
    k_ref[...] = k2
    v_ref[...] = v
    a_ref[...] = -kk
    b_ref[...] = kk * a
    g_ref[...] = g
    bonus_ref[...] = bonus


def _rw_prep(ua_tm, shift, mu, wl, vec, seg, rows):
    n = ua_tm.shape[0]
    nb = shift.shape[0]
    row_spec = lambda w: pl.BlockSpec((rows, w), lambda i: (i, 0))
    return pl.pallas_call(
        _rw_prep_kernel,
        grid=(n // rows,),
        in_specs=[row_spec(A_SHIFT_W), _const_spec(shift.shape), _const_spec(mu.shape),
                  _const_spec(wl.shape), _const_spec(vec.shape), _const_spec(seg.shape)],
        out_specs=[row_spec(A_W)] * 8,
        out_shape=[jax.ShapeDtypeStruct((n, A_W), F32)] * 8,
        scratch_shapes=[pltpu.VMEM((nb, A_SHIFT_W), F32)],
        compiler_params=_cparams("arbitrary"),
        name="rw_prep",
    )(ua_tm, shift, mu, wl, vec, seg)


RW_HALF = 2 * LANES


def _rw_rec_kernel(r_ref, w_ref, k_ref, v_ref, a_ref, b_ref, s0_ref, y_ref, so_ref,
                   s_scr, lhs_scr, red_scr, snb_scr):
    tb, bb, _ = r_ref.shape
    nq = A_W // RW_HALF
    units = bb * nq
    j = pl.program_id(1)

    @pl.when(j == 0)
    def _():
        for bi in range(bb):
            for hd in range(A_HEADS):
                s_scr[bi, :, hd * A_HD:(hd + 1) * A_HD] = s0_ref[bi, hd]

    row = lax.broadcasted_iota(jnp.int32, (A_HD, RW_HALF), 0)
    col = lax.broadcasted_iota(jnp.int32, (A_HD, RW_HALF), 1)
    key = jnp.bitwise_and(col, A_HD - 1)
    on_diag = key == row
    off_diag = key == jnp.bitwise_and(row + 1, A_HD - 1)
    seg = _seg_matrix(RW_HALF, A_HD)
    head_of_lane = jnp.right_shift(lax.broadcasted_iota(jnp.int32, (A_HEADS, A_W), 1), 6)
    own_head = head_of_lane == lax.broadcasted_iota(jnp.int32, (A_HEADS, A_W), 0)
    first_key = jnp.bitwise_and(lax.broadcasted_iota(jnp.int32, (bb, A_W), 1), A_HD - 1) == 0

    def step(t, carry):
        rt, wt, kt, vt, at, bt = (ref[t] for ref in (r_ref, w_ref, k_ref, v_ref, a_ref, b_ref))
        v_hi = vt.astype(BF16).astype(F32)
        v_lo = vt - v_hi
        v_lo = jnp.concatenate(
            [jnp.where(first_key[:, :LANES],
                       pltpu.roll(v_lo[:, c * LANES:(c + 1) * LANES], LANES - (A_HD - 1), axis=1),
                       pltpu.roll(v_lo[:, c * LANES:(c + 1) * LANES], 1, axis=1))
             for c in range(A_W // LANES)], axis=1)
        for bi in range(bb):
            pa = (s_scr[bi] * at[bi:bi + 1]).astype(BF16)
            for q in range(nq):
                u = bi * nq + q
                sl = slice(q * RW_HALF, (q + 1) * RW_HALF)
                lhs_scr[u * A_HD:(u + 1) * A_HD, :] = pa[:, sl]
                dv = jnp.where(on_diag, v_hi[bi:bi + 1, sl], jnp.where(off_diag, v_lo[bi:bi + 1, sl], 0.0))
                lhs_scr[(units + u) * A_HD:(units + u + 1) * A_HD, :] = dv.astype(BF16)
        red_scr[...] = _dot(lhs_scr[...], seg)
        for bi in range(bb):
            for q in range(nq):
                u = bi * nq + q
                sl = slice(q * RW_HALF, (q + 1) * RW_HALF)
                sa = red_scr[u * A_HD:(u + 1) * A_HD, :]
                vb = red_scr[(units + u) * A_HD:(units + u + 1) * A_HD, :]
                sn = s_scr[bi, :, sl] * wt[bi:bi + 1, sl] + sa * bt[bi:bi + 1, sl] + vb * kt[bi:bi + 1, sl]
                s_scr[bi, :, sl] = sn
                snb_scr[bi, :, sl] = sn.astype(BF16)
        for bi in range(bb):
            r_heads = jnp.where(own_head, jnp.broadcast_to(rt[bi:bi + 1], (A_HEADS, A_W)), 0.0)
            y_ref[t, bi] = _dot_nt(r_heads.astype(BF16), snb_scr[bi])
        return carry

    lax.fori_loop(0, tb, step, 0)

    @pl.when(j == pl.num_programs(1) - 1)
    def _():
        for bi in range(bb):
            for hd in range(A_HEADS):
                so_ref[bi, hd] = s_scr[bi, :, hd * A_HD:(hd + 1) * A_HD]


def _rw_rec(r, w, k, v, a, b, s0, tb, bb):
    t_len, bsz, _ = r.shape
    seq_spec = pl.BlockSpec((tb, bb, A_W), lambda i, j: (j, i, 0))
    st_spec = pl.BlockSpec((bb, A_HEADS, A_HD, A_HD), lambda i, j: (i, 0, 0, 0))
    units = bb * (A_W // RW_HALF)
    return pl.pallas_call(
        _rw_rec_kernel,
        grid=(bsz // bb, t_len // tb),
        in_specs=[seq_spec] * 6 + [st_spec],
        out_specs=[pl.BlockSpec((tb, bb, A_HEADS, A_HD), lambda i, j: (j, i, 0, 0)), st_spec],
        out_shape=[jax.ShapeDtypeStruct((t_len, bsz, A_HEADS, A_HD), F32),
                   jax.ShapeDtypeStruct(s0.shape, F32)],
        scratch_shapes=[pltpu.VMEM((bb, A_HD, A_W), F32),
                        pltpu.VMEM((2 * units * A_HD, RW_HALF), BF16),
                        pltpu.VMEM((2 * units * A_HD, RW_HALF), F32),
                        pltpu.VMEM((bb, A_HD, A_W), BF16)],
        compiler_params=_cparams("parallel", "arbitrary"),
        name="rw_rec",
    )(r, w, k, v, a, b, s0)


def _rw_post_kernel(y_ref, bonus_ref, g_ref, ln_ref, seg_ref, o_ref):
    seg = seg_ref[...]
    y = y_ref[...]
    mu = _dot_sel(y, seg) * (1.0 / A_HD)
    d = y - mu
    var = _dot_sel(d * d, seg) * (1.0 / A_HD)
    yn = d * lax.rsqrt(var + A_LN_EPS) * ln_ref[...]
    o_ref[...] = ((yn + bonus_ref[...]) * g_ref[...]).astype(o_ref.dtype)


def _rw_post(y, bonus, g, ln, seg, rows):
    n = y.shape[0]
    row_spec = pl.BlockSpec((rows, A_W), lambda i: (i, 0))
    return pl.pallas_call(
        _rw_post_kernel,
        grid=(n // rows,),
        in_specs=[row_spec] * 3 + [_const_spec(ln.shape), _const_spec(seg.shape)],
        out_specs=row_spec,
        out_shape=jax.ShapeDtypeStruct((n, A_W), BF16),
        compiler_params=_cparams("parallel"),
        name="rw_post",
    )(y, bonus, g, ln, seg)


def _head_norm_lanes(x, g, eps):
    mu = jnp.mean(x, axis=-1, keepdims=True)
    d = x - mu
    var = jnp.mean(d * d, axis=-1, keepdims=True)
    return d * lax.rsqrt(var + eps) * g


def _mlstm_chunk_kernel(ub_ref, g_ref, gt_ref, conv_ref, cw_ref, cb_ref, biasr_ref, biasc_ref, gn_ref,
                        c0_ref, n0_ref, m0_ref,
                        y_ref, co_ref, no_ref, mo_ref,
                        tail_s, c_s, n_s, m_s):
    L = ub_ref.shape[0]
    c = pl.program_id(1)

    @pl.when(c == 0)
    def _():
        tail_s[...] = conv_ref[0]
        c_s[...] = c0_ref[0]
        n_s[...] = n0_ref[0]
        m_s[...] = m0_ref[0]

    ub = ub_ref[...]
    raw = ub[:, :2 * B_W]
    ext = jnp.concatenate([tail_s[...], raw], axis=0)
    tail_s[...] = raw[L - SUBLANES:]
    acc = cb_ref[...] + raw * cw_ref[B_CONV - 1:B_CONV, :]
    for jj in range(B_CONV - 1):
        sh = B_CONV - 1 - jj
        acc = acc + pltpu.roll(ext, sh, axis=0)[SUBLANES:] * cw_ref[jj:jj + 1, :]
    qk = _silu(acc)
    q_all = qk[:, :B_W]
    k_all = qk[:, B_W:] * (B_HD ** -0.5)

    r_i = lax.broadcasted_iota(jnp.int32, (L, L), 0)
    c_i = lax.broadcasted_iota(jnp.int32, (L, L), 1)
    causal = c_i <= r_i
    tri = causal.astype(BF16)
    tri_u = (r_i <= c_i).astype(BF16)

    gc = g_ref[...] + biasr_ref[...]
    b_cols = _dot_sel_left(tri, _log_sigmoid(gc))
    gr = gt_ref[0] + biasc_ref[...]
    b_rows = _dot_sel(_log_sigmoid(gr), tri_u)

    for hd in range(B_HEADS):
        sl = slice(hd * B_HD, (hd + 1) * B_HD)
        q = q_all[:, sl]
        k = k_all[:, sl]
        v = ub[:, 2 * B_W + hd * B_HD:2 * B_W + (hd + 1) * B_HD]
        o = ub[:, 3 * B_W + hd * B_HD:3 * B_W + (hd + 1) * B_HD]
        bc = _lane_col(b_cols, B_HEADS + hd)
        ic = _lane_col(gc, hd)
        br = b_rows[B_HEADS + hd:B_HEADS + hd + 1, :]
        ir = gr[hd:hd + 1, :]
        m_prev = m_s[hd:hd + 1, 0:1]
        c_prev = c_s[hd]
        n_prev = n_s[hd:hd + 1, :]

        dlog = jnp.where(causal, bc - br + ir, -jnp.inf)
        inter = bc + m_prev
        m_t = jnp.maximum(inter, jnp.max(dlog, axis=1, keepdims=True))
        qb = q.astype(BF16)
        kb = k.astype(BF16)
        vb = v.astype(BF16)
        s = _dot_nt(qb, kb) * jnp.exp(dlog - m_t)
        iw = jnp.exp(inter - m_t)
        num = _dot(s.astype(BF16), vb) + iw * _dot(qb, c_prev.astype(BF16))
        den = jnp.sum(s, axis=1, keepdims=True) + iw * jnp.sum(q * n_prev, axis=1, keepdims=True)
        hh = num / jnp.maximum(jnp.abs(den), jnp.exp(-m_t))

        b_last = _row_pick(bc, L - 1)
        gi = b_last - bc + ic
        m_new = jnp.maximum(b_last + m_prev, jnp.max(gi, axis=0, keepdims=True))
        wi = jnp.exp(gi - m_new)
        wc = jnp.exp(b_last + m_prev - m_new)
        kw = k * wi
        c_s[hd] = wc * c_prev + _dot_tn(kw.astype(BF16), vb)
        n_s[hd:hd + 1, :] = wc * n_prev + jnp.sum(kw, axis=0, keepdims=True)
        m_s[hd:hd + 1, :] = jnp.broadcast_to(m_new, (1, LANES))

        hb = _sigmoid(o) * hh
        y_ref[:, sl] = _head_norm_lanes(hb, gn_ref[:, sl], GN_EPS).astype(y_ref.dtype)

    @pl.when(c == pl.num_programs(1) - 1)
    def _():
        co_ref[0] = c_s[...]
        no_ref[0] = n_s[...]
        mo_ref[0] = m_s[...]


def _mlstm_chunk(ub, gates, gates_t, conv8, cw, cb, bias_r, bias_c, gn, c0, n0, m0, bsz, t_len):
    L = CHUNK
    nc = t_len // L
    row = lambda w: pl.BlockSpec((L, w), lambda b, c: (b * nc + c, 0))
    per_b = lambda shp: pl.BlockSpec((1,) + shp, lambda b, c: (b,) + (0,) * len(shp))
    return pl.pallas_call(
        _mlstm_chunk_kernel,
        grid=(bsz, nc),
        in_specs=[row(4 * B_W), row(LANES),
                  pl.BlockSpec((1, SUBLANES, L), lambda b, c: (b, 0, c)),
                  per_b((SUBLANES, 2 * B_W)),
                  _const_spec(cw.shape), _const_spec(cb.shape), _const_spec(bias_r.shape),
                  _const_spec(bias_c.shape), _const_spec(gn.shape),
                  per_b((B_HEADS, B_HD, B_HD)), per_b((SUBLANES, B_HD)), per_b((SUBLANES, LANES))],
        out_specs=[row(B_W), per_b((B_HEADS, B_HD, B_HD)), per_b((SUBLANES, B_HD)),
                   per_b((SUBLANES, LANES))],
        out_shape=[jax.ShapeDtypeStruct((bsz * t_len, B_W), BF16),
                   jax.ShapeDtypeStruct((bsz, B_HEADS, B_HD, B_HD), F32),
                   jax.ShapeDtypeStruct((bsz, SUBLANES, B_HD), F32),
                   jax.ShapeDtypeStruct((bsz, SUBLANES, LANES), F32)],
        scratch_shapes=[pltpu.VMEM((SUBLANES, 2 * B_W), F32),
                        pltpu.VMEM((B_HEADS, B_HD, B_HD), F32),
                        pltpu.VMEM((SUBLANES, B_HD), F32),
                        pltpu.VMEM((SUBLANES, LANES), F32)],
        compiler_params=_cparams("parallel", "arbitrary"),
        name="mlstm_chunk",
    )(ub, gates, gates_t, conv8, cw, cb, bias_r, bias_c, gn, c0, n0, m0)


def _mlstm_step_kernel(ub_ref, g_ref, conv_ref, cw_ref, cb_ref, biasr_ref, gn_ref, c_ref, n_ref, m_ref,
                       y_ref, co_ref, no_ref, mo_ref):
    bb = ub_ref.shape[0]
    ub = ub_ref[...]
    raw = ub[:, :2 * B_W]
    conv = conv_ref[...]
    acc = cb_ref[...] + raw * cw_ref[B_CONV - 1:B_CONV, :]
    for jj in range(B_CONV - 1):
        acc = acc + conv[:, jj * 2 * B_W:(jj + 1) * 2 * B_W] * cw_ref[jj:jj + 1, :]
    qk = _silu(acc)
    q_all = qk[:, :B_W]
    k_all = qk[:, B_W:] * (B_HD ** -0.5)
    gc = g_ref[...] + biasr_ref[...]
    lf = _log_sigmoid(gc)
    m_all = m_ref[...]
    row_pad = lax.broadcasted_iota(jnp.int32, (LANES, LANES), 0)
    pad = jnp.zeros((LANES - bb, B_HD), F32)
    m_out = jnp.zeros((bb, LANES), F32)
    lane_m = lax.broadcasted_iota(jnp.int32, (bb, LANES), 1)
    for hd in range(B_HEADS):
        sl = slice(hd * B_HD, (hd + 1) * B_HD)
        q = q_all[:, sl]
        k = k_all[:, sl]
        v = ub[:, 2 * B_W + hd * B_HD:2 * B_W + (hd + 1) * B_HD]
        o = ub[:, 3 * B_W + hd * B_HD:3 * B_W + (hd + 1) * B_HD]
        ic = _lane_col(gc, hd)
        f = _lane_col(lf, B_HEADS + hd)
        m_prev = _lane_col(m_all, hd)
        n_prev = n_ref[:, sl]
        m_t = jnp.maximum(f + m_prev, ic)
        s = jnp.sum(q * k, axis=1, keepdims=True) * jnp.exp(ic - m_t)
        iw = jnp.exp(f + m_prev - m_t)
        wi = jnp.exp(ic - m_t)
        qb = q.astype(BF16)
        kw_t = jnp.concatenate([k * wi, pad], axis=0).T
        v_pad = jnp.concatenate([v, pad], axis=0)
        qc = jnp.zeros((bb, B_HD), F32)
        row_b = lax.broadcasted_iota(jnp.int32, (bb, B_HD), 0)
        for bi in range(bb):
            c_prev = c_ref[bi, hd]
            qc = qc + jnp.where(row_b == bi, _dot(qb, c_prev.astype(BF16)), 0.0)
            v_one = jnp.where(row_pad == bi, v_pad, 0.0)
            co_ref[bi, hd] = _row_pick(iw, bi) * c_prev + _dot3(kw_t, v_one, _dot)
        num = s * v + iw * qc
        den = s + iw * jnp.sum(q * n_prev, axis=1, keepdims=True)
        hh = num / jnp.maximum(jnp.abs(den), jnp.exp(-m_t))
        no_ref[:, sl] = iw * n_prev + wi * k
        m_out = jnp.where(lane_m == hd, m_t, m_out)
        hb = _sigmoid(o) * hh
        y_ref[:, sl] = _head_norm_lanes(hb, gn_ref[:, sl], GN_EPS).astype(y_ref.dtype)
    mo_ref[...] = m_out


def _mlstm_step(ub, gates, conv, cw, cb, bias_r, gn, c0, n0, m0, bb):
    bsz = ub.shape[0]
    row = lambda w: pl.BlockSpec((bb, w), lambda i: (i, 0))
    c_spec = pl.BlockSpec((bb, B_HEADS, B_HD, B_HD), lambda i: (i, 0, 0, 0))
    return pl.pallas_call(
        _mlstm_step_kernel,
        grid=(bsz // bb,),
        in_specs=[row(4 * B_W), row(LANES), row((B_CONV - 1) * 2 * B_W),
                  _const_spec(cw.shape), _const_spec(cb.shape), _const_spec(bias_r.shape),
                  _const_spec(gn.shape), c_spec, row(B_W), row(LANES)],
        out_specs=[row(B_W), c_spec, row(B_W), row(LANES)],
        out_shape=[jax.ShapeDtypeStruct((bsz, B_W), BF16),
                   jax.ShapeDtypeStruct(c0.shape, F32),
                   jax.ShapeDtypeStruct((bsz, B_W), F32),
                   jax.ShapeDtypeStruct((bsz, LANES), F32)],
        compiler_params=_cparams("parallel"),
        name="mlstm_step",
    )(ub, gates, conv, cw, cb, bias_r, gn, c0, n0, m0)


def _ret_log_gamma(hd):
    return math.log1p(-(2.0 ** (-5.0 - hd)))


def _ret_chunk_kernel(q_ref, k_ref, v_ref, g_ref, gn_ref, s0_ref, y_ref, so_ref, s_s):
    L = q_ref.shape[0]
    c = pl.program_id(1)

    @pl.when(c == 0)
    def _():
        s_s[...] = s0_ref[0]

    r_i = lax.broadcasted_iota(jnp.int32, (L, L), 0)
    c_i = lax.broadcasted_iota(jnp.int32, (L, L), 1)
    diff = (r_i - c_i).astype(F32)
    idx = lax.broadcasted_iota(jnp.int32, (L, C_DK), 0).astype(F32)
    for hd in range(C_HEADS):
        lg = _ret_log_gamma(hd)
        mask = jnp.where(diff >= 0, jnp.exp(jnp.maximum(diff, 0.0) * lg), 0.0)
        q_dec = jnp.exp((idx + 1.0) * lg)
        k_dec = jnp.exp((L - 1.0 - idx) * lg)
        c_dec = math.exp(L * lg)
        q = q_ref[:, hd * C_DK:(hd + 1) * C_DK]
        k = k_ref[:, hd * C_DK:(hd + 1) * C_DK]
        v = v_ref[:, hd * C_DV:(hd + 1) * C_DV]
        s_prev = s_s[hd]
        s = _dot_nt(q, k) * mask
        o = _dot(s.astype(BF16), v) + _dot((q.astype(F32) * q_dec).astype(BF16), s_prev.astype(BF16))
        s_s[hd] = c_dec * s_prev + _dot_tn((k.astype(F32) * k_dec).astype(BF16), v)
        sl = slice(hd * C_DV, (hd + 1) * C_DV)
        y = _head_norm_lanes(o, gn_ref[:, sl], GN_EPS) * _silu(g_ref[:, sl])
        y_ref[:, sl] = y.astype(y_ref.dtype)

    @pl.when(c == pl.num_programs(1) - 1)
    def _():
        so_ref[0] = s_s[...]


def _ret_chunk(q, k, v, g, gn, s0, bsz, t_len):
    L = CHUNK
    nc = t_len // L
    row = lambda w: pl.BlockSpec((L, w), lambda b, c: (b * nc + c, 0))
    st = pl.BlockSpec((1, C_HEADS, C_DK, C_DV), lambda b, c: (b, 0, 0, 0))
    return pl.pallas_call(
        _ret_chunk_kernel,
        grid=(bsz, nc),
        in_specs=[row(C_HEADS * C_DK), row(C_HEADS * C_DK), row(C_OUT_W), row(C_OUT_W),
                  _const_spec(gn.shape), st],
        out_specs=[row(C_OUT_W), st],
        out_shape=[jax.ShapeDtypeStruct((bsz * t_len, C_OUT_W), BF16),
                   jax.ShapeDtypeStruct(s0.shape, F32)],
        scratch_shapes=[pltpu.VMEM((C_HEADS, C_DK, C_DV), F32)],
        compiler_params=_cparams("parallel", "arbitrary"),
        name="ret_chunk",
    )(q, k, v, g, gn, s0)


def _ret_step_kernel(q_ref, k_ref, v_ref, g_ref, gn_ref, s_ref, y_ref, so_ref):
    bb = q_ref.shape[0]
    hd = pl.program_id(1)
    gam = 1.0 - jnp.exp2(-5.0 - jnp.full((1, 1), hd).astype(F32))
    q = q_ref[...]
    k = k_ref[...]
    v = v_ref[...]
    s = jnp.sum(q * k, axis=1, keepdims=True)
    pad_k = jnp.zeros((LANES - bb, C_DK), F32)
    pad_v = jnp.zeros((LANES - bb, C_DV), F32)
    k_t = jnp.concatenate([k, pad_k], axis=0).T
    v_pad = jnp.concatenate([v, pad_v], axis=0)
    row_pad = lax.broadcasted_iota(jnp.int32, (LANES, C_DV), 0)
    row_b = lax.broadcasted_iota(jnp.int32, (bb, C_DV), 0)
    qg = (q * gam).astype(BF16)
    qs = jnp.zeros((bb, C_DV), F32)
    for bi in range(bb):
        s_prev = s_ref[bi, 0]
        qs = qs + jnp.where(row_b == bi, _dot(qg, s_prev.astype(BF16)), 0.0)
        v_one = jnp.where(row_pad == bi, v_pad, 0.0)
        so_ref[bi, 0] = gam * s_prev + _dot3(k_t, v_one, _dot)
    o = s * v + qs
    y = _head_norm_lanes(o, gn_ref[...], GN_EPS) * _silu(g_ref[...])
    y_ref[...] = y.astype(y_ref.dtype)


def _ret_step(q, k, v, g, gn, s0, bb):
    bsz = q.shape[0]
    st = pl.BlockSpec((bb, 1, C_DK, C_DV), lambda i, h: (i, h, 0, 0))
    return pl.pallas_call(
        _ret_step_kernel,
        grid=(bsz // bb, C_HEADS),
        in_specs=[pl.BlockSpec((bb, C_DK), lambda i, h: (i, h)),
                  pl.BlockSpec((bb, C_DK), lambda i, h: (i, h)),
                  pl.BlockSpec((bb, C_DV), lambda i, h: (i, h)),
                  pl.BlockSpec((bb, C_DV), lambda i, h: (i, h)),
                  pl.BlockSpec((1, C_DV), lambda i, h: (0, h)), st],
        out_specs=[pl.BlockSpec((bb, C_DV), lambda i, h: (i, h)), st],
        out_shape=[jax.ShapeDtypeStruct((bsz, C_OUT_W), BF16),
                   jax.ShapeDtypeStruct(s0.shape, F32)],
        compiler_params=_cparams("parallel", "parallel"),
        name="ret_step",
    )(q, k, v, g, gn, s0)


def _post_kernel(*refs, n_mix):
    h_ref = refs[0]
    mix_refs = refs[1:1 + n_mix]
    wout_refs = refs[1 + n_mix:1 + 2 * n_mix]
    (p_ref, g_ref, up_ref, down_ref, gate_ref, proj_ref, o_ref, h1_s, xn_s, acc_s) = refs[1 + 2 * n_mix:]
    f = pl.program_id(1)

    @pl.when(f == 0)
    def _():
        mix = _dot(mix_refs[0][...], wout_refs[0][...])
        for m_ref, w_ref in zip(mix_refs[1:], wout_refs[1:]):
            mix = mix + _dot(m_ref[...], w_ref[...])
        h1 = h_ref[...] + _rms(mix, g_ref[1:2, :])
        h1_s[...] = h1
        xn_s[...] = _rms(h1, g_ref[2:3, :]).astype(BF16)
        acc_s[...] = jnp.zeros_like(acc_s)

    hid = jnp.square(jnp.maximum(_dot(xn_s[...], up_ref[...]), 0.0))
    acc_s[...] += _dot(hid.astype(BF16), down_ref[...])

    @pl.when(f == pl.num_programs(1) - 1)
    def _():
        h2 = h1_s[...] + _rms(acc_s[...], g_ref[3:4, :])
        gate = _sigmoid(_dot(h2.astype(BF16), gate_ref[...]))
        emb = _dot(p_ref[...].astype(BF16), proj_ref[...])
        o_ref[...] = h2 + gate * emb


def _post(h, mixes, wouts, p, g, up, down, gate, proj, tm, tf):
    n = h.shape[0]
    n_mix = len(mixes)
    row = lambda w: pl.BlockSpec((tm, w), lambda i, f: (i, 0))
    return pl.pallas_call(
        functools.partial(_post_kernel, n_mix=n_mix),
        grid=(n // tm, D_FF // tf),
        in_specs=[row(D_MODEL)] + [row(m.shape[1]) for m in mixes]
        + [_const_spec(w.shape) for w in wouts]
        + [row(D_PLE), _const_spec(g.shape),
           pl.BlockSpec((D_MODEL, tf), lambda i, f: (0, f)),
           pl.BlockSpec((tf, D_MODEL), lambda i, f: (f, 0)),
           _const_spec(gate.shape), _const_spec(proj.shape)],
        out_specs=row(D_MODEL),
        out_shape=jax.ShapeDtypeStruct((n, D_MODEL), F32),
        scratch_shapes=[pltpu.VMEM((tm, D_MODEL), F32), pltpu.VMEM((tm, D_MODEL), BF16),
                        pltpu.VMEM((tm, D_MODEL), F32)],
        compiler_params=_cparams("parallel", "arbitrary"),
        name="post",
    )(h, *mixes, *wouts, p, g, up, down, gate, proj)


def _prep_weights(W):
    f = {}
    w_in = W['mix0_w_in'][0]
    f['w_a'] = w_in[:, :A_SHIFT_W].astype(BF16)
    f['w_b'] = w_in[:, A_SHIFT_W:A_SHIFT_W + 4 * B_W].astype(BF16)
    f['w_g'] = jnp.pad(w_in[:, A_SHIFT_W + 4 * B_W:], ((0, 0), (0, LANES - 2 * B_HEADS))).astype(BF16)
    wl = jnp.zeros((A_LORA_W, 3 * A_W), F32)
    wl = wl.at[:A_DECAY_LORA, :A_W].set(W['rw_w2'][0])
    wl = wl.at[A_DECAY_LORA:A_DECAY_LORA + A_AAA_LORA, A_W:2 * A_W].set(W['rw_a2'][0])
    wl = wl.at[A_DECAY_LORA + A_AAA_LORA:, 2 * A_W:].set(W['rw_g2'][0])
    f['w_lora'] = wl.astype(BF16)
    f['rw_vec'] = jnp.concatenate(
        [W['rw_w0'], W['rw_a0'], W['rw_kk'], W['rw_ka'], W['rw_rk'][0].reshape(1, A_W),
         jnp.zeros((3, A_W), F32)], axis=0)
    f['rw_mu'] = W['rw_mu']
    f['rw_ln'] = W['rw_ln']
    f['seg'] = (jnp.arange(A_W)[:, None] // A_HD == jnp.arange(A_W)[None, :] // A_HD).astype(BF16)
    f['cw'] = W['ml_conv_w'][0]
    f['cb'] = W['ml_conv_b']
    gate_bias = jnp.concatenate([W['ml_i_bias'][0], W['ml_f_bias'][0]])
    f['bias_r'] = jnp.pad(gate_bias, (0, LANES - 2 * B_HEADS)).reshape(1, LANES)
    f['bias_c'] = gate_bias.reshape(2 * B_HEADS, 1)
    f['ml_gn'] = W['ml_gn']
    w_out0 = W['mix0_w_out'][0].astype(BF16)
    f['w_out_a'] = w_out0[:A_W]
    f['w_out_b'] = w_out0[A_W:]
    w_ret = W['ret_w_in'][0]
    nq = C_HEADS * C_DK
    f['w_rq'] = w_ret[:, :nq].astype(BF16)
    f['w_rk'] = w_ret[:, nq:2 * nq].astype(BF16)
    f['w_rv'] = w_ret[:, 2 * nq:2 * nq + C_OUT_W].astype(BF16)
    f['w_rg'] = w_ret[:, 2 * nq + C_OUT_W:].astype(BF16)
    f['ret_gn'] = W['ret_gn']
    f['w_ret_out'] = W['ret_w_out'][0].astype(BF16)
    for name in ('ffn_up', 'ffn_down', 'ple_gate', 'ple_proj'):
        f[name] = W[name].astype(BF16)
    f['norm_g'] = W['norm_g']
    return f


def _rope_tables(pos):
    half = C_DK // 2
    inv = ROPE_BASE ** (-jnp.arange(half, dtype=F32) / half)
    ang = pos.astype(F32)[:, None] * inv[None, :]
    return jnp.cos(ang), jnp.sin(ang)


def _even_layer(h, p, st, f, bsz, t_len):
    shift, s_a, conv, c0, n0, m0 = st
    n = bsz * t_len
    seq = t_len > 1
    tm = 512 if seq else n
    ua, ub, gates = _inproj(h, f['norm_g'][0], [f['w_a'], f['w_b'], f['w_g']], [F32, F32, F32], tm)

    to_tm = lambda x: x.reshape(bsz, t_len, -1).transpose(1, 0, 2).reshape(n, -1)
    from_tm = lambda x: x.reshape(t_len, bsz, -1).transpose(1, 0, 2).reshape(n, -1)
    rows = 512 if seq else n
    outs = _rw_prep(to_tm(ua), shift, f['rw_mu'], f['w_lora'], f['rw_vec'], f['seg'], rows)
    r, w, k, v, a, b, g, bonus = outs
    as3 = lambda x: x.reshape(t_len, bsz, A_W)
    y, s_a_new = _rw_rec(as3(r), as3(w), as3(k), as3(v), as3(a), as3(b), s_a,
                         tb=128 if seq else 1, bb=SUBLANES)
    ya = from_tm(_rw_post(y.reshape(n, A_W), bonus, g, f['rw_ln'], f['seg'], rows))

    if seq:
        conv8 = jnp.pad(conv, ((0, 0), (SUBLANES - (B_CONV - 1), 0), (0, 0)))
        gates_t = gates.reshape(bsz, t_len, LANES)[:, :, :SUBLANES].transpose(0, 2, 1)
        n0p = jnp.pad(n0, ((0, 0), (0, SUBLANES - B_HEADS), (0, 0)))
        m0p = jnp.broadcast_to(jnp.pad(m0, ((0, 0), (0, SUBLANES - B_HEADS)))[:, :, None],
                               (bsz, SUBLANES, LANES))
        yb, c_new, n_new, m_new = _mlstm_chunk(ub, gates, gates_t, conv8, f['cw'], f['cb'], f['bias_r'],
                                               f['bias_c'], f['ml_gn'], c0, n0p, m0p, bsz, t_len)
        n_new = n_new[:, :B_HEADS]
        m_new = m_new[:, :B_HEADS, 0]
    else:
        m0p = jnp.pad(m0, ((0, 0), (0, LANES - B_HEADS)))
        yb, c_new, n_new, m_new = _mlstm_step(ub, gates, conv.reshape(bsz, -1), f['cw'], f['cb'],
                                              f['bias_r'], f['ml_gn'], c0, n0.reshape(bsz, B_W), m0p,
                                              bb=SUBLANES)
        n_new = n_new.reshape(bsz, B_HEADS, B_HD)
        m_new = m_new[:, :B_HEADS]

    h_next = _post(h, [ya, yb], [f['w_out_a'], f['w_out_b']], p, f['norm_g'][0], f['ffn_up'][0],
                   f['ffn_down'][0], f['ple_gate'][0], f['ple_proj'][0], tm=tm, tf=512)

    shift_new = ua.reshape(bsz, t_len, A_SHIFT_W)[:, -1]
    qk_raw = ub.reshape(bsz, t_len, 4 * B_W)[:, :, :2 * B_W]
    conv_new = jnp.concatenate([conv, qk_raw], axis=1)[:, -(B_CONV - 1):]
    return h_next, (shift_new, s_a_new, conv_new, c_new, n_new, m_new)


def _odd_layer(h, p, s0, pos, f, bsz, t_len):
    n = bsz * t_len
    seq = t_len > 1
    cos, sin = _rope_tables(pos)
    tm = 256 if seq else n
    ws = [f['w_rq'], f['w_rk'], f['w_rv'], f['w_rg']]
    if seq:
        q, k, v, g = _inproj_ret(h, f['norm_g'][1], cos, sin, ws, tm, t_len // tm, [BF16, BF16, BF16, F32])
        y, s_new = _ret_chunk(q, k, v, g, f['ret_gn'], s0, bsz, t_len)
    else:
        q, k, v, g = _inproj_ret(h, f['norm_g'][1], cos, sin, ws, tm, 1, [F32, F32, F32, F32])
        y, s_new = _ret_step(q, k, v, g, f['ret_gn'], s0, bb=SUBLANES)
    h_next = _post(h, [y], [f['w_ret_out']], p, f['norm_g'][1], f['ffn_up'][1], f['ffn_down'][1],
                   f['ple_gate'][1], f['ple_proj'][1], tm=512 if seq else n, tf=512)
    return h_next, s_new


def _trunk(x, p, pos, ev_states, od_state, f):
    bsz, t_len, _ = x.shape
    n = bsz * t_len
    h = x.reshape(n, D_MODEL)
    h, ev_new = _even_layer(h, p[0].reshape(n, D_PLE), tuple(s[0] for s in ev_states), f, bsz, t_len)
    h, od_new = _odd_layer(h, p[1].reshape(n, D_PLE), od_state[0], pos, f, bsz, t_len)
    return h.reshape(bsz, t_len, D_MODEL), [s[None] for s in ev_new], od_new[None]


def kernel(x_prompt, x_sample, state_rwkv_shift, state_rwkv_S, state_mlstm_conv, state_mlstm_C,
           state_mlstm_n, state_mlstm_m, state_ret_S, p_prompt, p_sample, norm_g, ffn_up, ffn_down,
           ple_gate, ple_proj, mix0_w_in, rw_mu, rw_w0, rw_w2, rw_a0, rw_a2, rw_g2, rw_kk, rw_ka,
           rw_rk, rw_ln, ml_conv_w, ml_conv_b, ml_i_bias, ml_f_bias, ml_gn, mix0_w_out,
           ret_w_in, ret_gn, ret_w_out):
    W = dict(norm_g=norm_g, ffn_up=ffn_up, ffn_down=ffn_down, ple_gate=ple_gate, ple_proj=ple_proj,
             mix0_w_in=mix0_w_in, rw_mu=rw_mu, rw_w0=rw_w0, rw_w2=rw_w2, rw_a0=rw_a0, rw_a2=rw_a2,
             rw_g2=rw_g2, rw_kk=rw_kk, rw_ka=rw_ka, rw_rk=rw_rk, rw_ln=rw_ln, ml_conv_w=ml_conv_w,
             ml_conv_b=ml_conv_b, ml_i_bias=ml_i_bias, ml_f_bias=ml_f_bias, ml_gn=ml_gn,
             mix0_w_out=mix0_w_out, ret_w_in=ret_w_in, ret_gn=ret_gn, ret_w_out=ret_w_out)
    f = _prep_weights(W)
    bp, tp = x_prompt.shape[:2]
    ts = x_sample.shape[1]
    ev0 = (jnp.zeros((1, bp, A_SHIFT_W), F32),
           jnp.zeros((1, bp, A_HEADS, A_HD, A_HD), F32),
           jnp.zeros((1, bp, B_CONV - 1, 2 * B_W), F32),
           jnp.zeros((1, bp, B_HEADS, B_HD, B_HD), F32),
           jnp.zeros((1, bp, B_HEADS, B_HD), F32),
           jnp.zeros((1, bp, B_HEADS), F32))
    od0 = jnp.zeros((1, bp, C_HEADS, C_DK, C_DV), F32)
    y_prompt, ev_p, ret_p = _trunk(x_prompt, p_prompt, jnp.arange(tp), ev0, od0, f)
    ev_s_in = (state_rwkv_shift, state_rwkv_S, state_mlstm_conv, state_mlstm_C, state_mlstm_n,
               state_mlstm_m)
    y_sample, ev_s, ret_s = _trunk(x_sample, p_sample, PAST_LEN + jnp.arange(ts), ev_s_in, state_ret_S, f)
    return (y_prompt, y_sample, *ev_p, ret_p, *ev_s, ret_s)
```

```python
import functools
import math

import jax
import jax.numpy as jnp
from jax import lax
from jax.experimental import pallas as pl
from jax.experimental.pallas import tpu as pltpu

F32 = jnp.float32
BF16 = jnp.bfloat16

D_MODEL = 1024
D_PLE = 256
D_FF = 4 * D_MODEL
RMS_EPS = 1e-6
GN_EPS = 1e-5
CHUNK = 128

A_HEADS = 8
A_HD = 64
A_W = A_HEADS * A_HD
A_DECAY_LORA = 64
A_AAA_LORA = 64
A_GATE_LORA = 128
A_LORA_W = A_DECAY_LORA + A_AAA_LORA + A_GATE_LORA
A_SHIFT_W = 3 * A_W + A_LORA_W
A_LN_EPS = 64e-5

B_HEADS = 4
B_HD = 128
B_W = B_HEADS * B_HD
B_CONV = 4

C_HEADS = 4
C_DK = 256
C_DV = 512
C_OUT_W = C_HEADS * C_DV
ROPE_BASE = 10000.0
PAST_LEN = 16384

LANES = 128
SUBLANES = 8
VMEM_LIMIT = 56 * 1024 * 1024


def _cparams(*sem):
    return pltpu.CompilerParams(dimension_semantics=sem, vmem_limit_bytes=VMEM_LIMIT)


def _const_spec(shape):
    nd = len(shape)
    return pl.BlockSpec(shape, lambda *_: (0,) * nd)


def _rms(x, g):
    return x * lax.rsqrt(jnp.mean(x * x, axis=-1, keepdims=True) + RMS_EPS) * g


def _dot(a, b):
    return jnp.dot(a, b, preferred_element_type=F32)


def _bdot(a, b):
    return jnp.dot(a.astype(BF16), b.astype(BF16), preferred_element_type=F32)


def _dot_nt(a, b):
    return lax.dot_general(a, b, (((1,), (1,)), ((), ())), preferred_element_type=F32)


def _dot_tn(a, b):
    return lax.dot_general(a, b, (((0,), (0,)), ((), ())), preferred_element_type=F32)


def _hilo(x):
    hi = x.astype(BF16)
    lo = (x - hi.astype(F32)).astype(BF16)
    return hi, lo


def _dot_sel(x, sel):
    hi, lo = _hilo(x)
    return _dot(hi, sel) + _dot(lo, sel)


def _dot_sel_left(sel, x):
    hi, lo = _hilo(x)
    return _dot(sel, hi) + _dot(sel, lo)


def _dot3(a, b, dot):
    ah, al = _hilo(a)
    bh, bl = _hilo(b)
    return dot(ah, bh) + dot(ah, bl) + dot(al, bh)


def _sigmoid(x):
    return 1.0 / (1.0 + jnp.exp(-x))


def _softplus(x):
    return jnp.maximum(x, 0.0) + jnp.log1p(jnp.exp(-jnp.abs(x)))


def _log_sigmoid(x):
    return -_softplus(-x)


def _silu(x):
    return x * _sigmoid(x)


def _lane_col(x, idx):
    lane = lax.broadcasted_iota(jnp.int32, x.shape, 1)
    return jnp.sum(jnp.where(lane == idx, x, 0.0), axis=1, keepdims=True)


def _row_pick(x, idx):
    row = lax.broadcasted_iota(jnp.int32, x.shape, 0)
    return jnp.sum(jnp.where(row == idx, x, 0.0), axis=0, keepdims=True)


def _seg_matrix(n, seg):
    sh = seg.bit_length() - 1
    r = jnp.right_shift(lax.broadcasted_iota(jnp.int32, (n, n), 0), sh)
    c = jnp.right_shift(lax.broadcasted_iota(jnp.int32, (n, n), 1), sh)
    return (r == c).astype(BF16)


def _inproj_kernel(x_ref, g_ref, *refs):
    n = len(refs) // 2
    xn = _rms(x_ref[...], g_ref[0:1, :]).astype(BF16)
    for w_ref, o_ref in zip(refs[:n], refs[n:]):
        o_ref[...] = _dot(xn, w_ref[...]).astype(o_ref.dtype)


def _inproj(h, g, ws, dtypes, tm):
    n = h.shape[0]
    return pl.pallas_call(
        _inproj_kernel,
        grid=(n // tm,),
        in_specs=[pl.BlockSpec((tm, D_MODEL), lambda i: (i, 0)), _const_spec(g.shape)]
        + [_const_spec(w.shape) for w in ws],
        out_specs=[pl.BlockSpec((tm, w.shape[1]), lambda i: (i, 0)) for w in ws],
        out_shape=[jax.ShapeDtypeStruct((n, w.shape[1]), dt) for w, dt in zip(ws, dtypes)],
        compiler_params=_cparams("parallel"),
        name="inproj",
    )(h, g, *ws)


def _inproj_ret_kernel(x_ref, g_ref, cos_ref, sin_ref, wq_ref, wk_ref, wv_ref, wg_ref,
                       q_ref, k_ref, v_ref, gate_ref):
    xn = _rms(x_ref[...], g_ref[0:1, :]).astype(BF16)
    cos = cos_ref[...]
    sin = sin_ref[...]
    half = C_DK // 2
    for w_ref, o_ref, scale in ((wq_ref, q_ref, None), (wk_ref, k_ref, C_DK ** -0.5)):
        u = _dot(xn, w_ref[...])
        for hd in range(C_HEADS):
            x1 = u[:, hd * C_DK:hd * C_DK + half]
            x2 = u[:, hd * C_DK + half:(hd + 1) * C_DK]
            y1 = x1 * cos - x2 * sin
            y2 = x1 * sin + x2 * cos
            if scale is not None:
                y1 = y1 * scale
                y2 = y2 * scale
            o_ref[:, hd * C_DK:hd * C_DK + half] = y1.astype(o_ref.dtype)
            o_ref[:, hd * C_DK + half:(hd + 1) * C_DK] = y2.astype(o_ref.dtype)
    v_ref[...] = _dot(xn, wv_ref[...]).astype(v_ref.dtype)
    gate_ref[...] = _dot(xn, wg_ref[...]).astype(gate_ref.dtype)


def _inproj_ret(h, g, cos, sin, ws, tm, seq_tiles, dtypes):
    n = h.shape[0]
    if cos.shape[0] == 1:
        cs_spec = _const_spec(cos.shape)
    else:
        cs_spec = pl.BlockSpec((tm, cos.shape[1]), lambda i: (i % seq_tiles, 0))
    widths = [w.shape[1] for w in ws]
    return pl.pallas_call(
        _inproj_ret_kernel,
        grid=(n // tm,),
        in_specs=[pl.BlockSpec((tm, D_MODEL), lambda i: (i, 0)), _const_spec(g.shape), cs_spec, cs_spec]
        + [_const_spec(w.shape) for w in ws],
        out_specs=[pl.BlockSpec((tm, wd), lambda i: (i, 0)) for wd in widths],
        out_shape=[jax.ShapeDtypeStruct((n, wd), dt) for wd, dt in zip(widths, dtypes)],
        compiler_params=_cparams("parallel"),
        name="inproj_ret",
    )(h, g, cos, sin, *ws)


def _rw_prep_kernel(x_ref, sh_ref, mu_ref, wl_ref, vec_ref, seg_ref,
                    r_ref, w_ref, k_ref, v_ref, a_ref, b_ref, g_ref, bonus_ref, carry):
    rows = x_ref.shape[0]
    nb = sh_ref.shape[0]

    @pl.when(pl.program_id(0) == 0)
    def _():
        carry[...] = sh_ref[...]

    x = x_ref[...]
    if rows == nb:
        prev = carry[...]
    else:
        prev = jnp.concatenate([carry[...], x[:rows - nb]], axis=0)
    carry[...] = x[rows - nb:]
    xs = x + (prev - x) * mu_ref[...]
    r = xs[:, :A_W]
    k = xs[:, A_W:2 * A_W]
    v = xs[:, 2 * A_W:3 * A_W]
    lo = xs[:, 3 * A_W:]
    lane = lax.broadcasted_iota(jnp.int32, lo.shape, 1)
    act = jnp.where(lane < A_DECAY_LORA, jnp.tanh(lo),
                    jnp.where(lane < A_DECAY_LORA + A_AAA_LORA, lo, _sigmoid(lo)))
    lora = _dot(act.astype(BF16), wl_ref[...])
    w0, a0, kkw, kaw, rkw = (vec_ref[i:i + 1, :] for i in range(5))
    w_log = -_softplus(-(w0 + lora[:, :A_W])) - 0.5
    decay = jnp.exp(-jnp.exp(w_log))
    a = _sigmoid(a0 + lora[:, A_W:2 * A_W])
    g = lora[:, 2 * A_W:]
    seg = seg_ref[...]
    kk = k * kkw
    kk = kk / jnp.maximum(jnp.sqrt(_dot_sel(kk * kk, seg)), 1e-12)
    k2 = k * (1.0 + (a - 1.0) * kaw)
    bonus = _dot_sel(r * k2 * rkw, seg) * v
    r_ref[...] = r
    w_ref[...] = decay
    k_ref[...] = k2
    v_ref[...] = v
    a_ref[...] = -kk
    b_ref[...] = kk * a
    g_ref[...] = g
    bonus_ref[...] = bonus


def _rw_prep(ua_tm, shift, mu, wl, vec, seg, rows):
    n = ua_tm.shape[0]
    nb = shift.shape[0]
    row_spec = lambda w: pl.BlockSpec((rows, w), lambda i: (i, 0))
    return pl.pallas_call(
        _rw_prep_kernel,
        grid=(n // rows,),
        in_specs=[row_spec(A_SHIFT_W), _const_spec(shift.shape), _const_spec(mu.shape),
                  _const_spec(wl.shape), _const_spec(vec.shape), _const_spec(seg.shape)],
        out_specs=[row_spec(A_W)] * 8,
        out_shape=[jax.ShapeDtypeStruct((n, A_W), F32)] * 8,
        scratch_shapes=[pltpu.VMEM((nb, A_SHIFT_W), F32)],
        compiler_params=_cparams("arbitrary"),
        name="rw_prep",
    )(ua_tm, shift, mu, wl, vec, seg)


RW_HALF = 2 * LANES


def _rw_rec_kernel(r_ref, w_ref, k_ref, v_ref, a_ref, b_ref, s0_ref, y_ref, so_ref,
                   s_scr, lhs_scr, red_scr, snb_scr):
    tb, bb, _ = r_ref.shape
    nq = A_W // RW_HALF
    units = bb * nq
    j = pl.program_id(1)

    @pl.when(j == 0)
    def _():
        for bi in range(bb):
            for hd in range(A_HEADS):
                s_scr[bi, :, hd * A_HD:(hd + 1) * A_HD] = s0_ref[bi, hd]

    row = lax.broadcasted_iota(jnp.int32, (A_HD, RW_HALF), 0)
    col = lax.broadcasted_iota(jnp.int32, (A_HD, RW_HALF), 1)
    key = jnp.bitwise_and(col, A_HD - 1)
    on_diag = key == row
    off_diag = key == jnp.bitwise_and(row + 1, A_HD - 1)
    seg = _seg_matrix(RW_HALF, A_HD)
    head_of_lane = jnp.right_shift(lax.broadcasted_iota(jnp.int32, (A_HEADS, A_W), 1), 6)
    own_head = head_of_lane == lax.broadcasted_iota(jnp.int32, (A_HEADS, A_W), 0)
    first_key = jnp.bitwise_and(lax.broadcasted_iota(jnp.int32, (bb, A_W), 1), A_HD - 1) == 0

    def step(t, carry):
        rt, wt, kt, vt, at, bt = (ref[t] for ref in (r_ref, w_ref, k_ref, v_ref, a_ref, b_ref))
        v_hi = vt.astype(BF16).astype(F32)
        v_lo = vt - v_hi
        v_lo = jnp.concatenate(
            [jnp.where(first_key[:, :LANES],
                       pltpu.roll(v_lo[:, c * LANES:(c + 1) * LANES], LANES - (A_HD - 1), axis=1),
                       pltpu.roll(v_lo[:, c * LANES:(c + 1) * LANES], 1, axis=1))
             for c in range(A_W // LANES)], axis=1)
        for bi in range(bb):
            pa = (s_scr[bi] * at[bi:bi + 1]).astype(BF16)
            for q in range(nq):
                u = bi * nq + q
                sl = slice(q * RW_HALF, (q + 1) * RW_HALF)
                lhs_scr[u * A_HD:(u + 1) * A_HD, :] = pa[:, sl]
                dv = jnp.where(on_diag, v_hi[bi:bi + 1, sl], jnp.where(off_diag, v_lo[bi:bi + 1, sl], 0.0))
                lhs_scr[(units + u) * A_HD:(units + u + 1) * A_HD, :] = dv.astype(BF16)
        red_scr[...] = _dot(lhs_scr[...], seg)
        for bi in range(bb):
            for q in range(nq):
                u = bi * nq + q
                sl = slice(q * RW_HALF, (q + 1) * RW_HALF)
                sa = red_scr[u * A_HD:(u + 1) * A_HD, :]
                vb = red_scr[(units + u) * A_HD:(units + u + 1) * A_HD, :]
                sn = s_scr[bi, :, sl] * wt[bi:bi + 1, sl] + sa * bt[bi:bi + 1, sl] + vb * kt[bi:bi + 1, sl]
                s_scr[bi, :, sl] = sn
                snb_scr[bi, :, sl] = sn.astype(BF16)
        for bi in range(bb):
            r_heads = jnp.where(own_head, jnp.broadcast_to(rt[bi:bi + 1], (A_HEADS, A_W)), 0.0)
            y_ref[t, bi] = _dot_nt(r_heads.astype(BF16), snb_scr[bi])
        return carry

    lax.fori_loop(0, tb, step, 0)

    @pl.when(j == pl.num_programs(1) - 1)
    def _():
        for bi in range(bb):
            for hd in range(A_HEADS):
                so_ref[bi, hd] = s_scr[bi, :, hd * A_HD:(hd + 1) * A_HD]


def _rw_rec(r, w, k, v, a, b, s0, tb, bb):
    t_len, bsz, _ = r.shape
    seq_spec = pl.BlockSpec((tb, bb, A_W), lambda i, j: (j, i, 0))
    st_spec = pl.BlockSpec((bb, A_HEADS, A_HD, A_HD), lambda i, j: (i, 0, 0, 0))
    units = bb * (A_W // RW_HALF)
    return pl.pallas_call(
        _rw_rec_kernel,
        grid=(bsz // bb, t_len // tb),
        in_specs=[seq_spec] * 6 + [st_spec],
        out_specs=[pl.BlockSpec((tb, bb, A_HEADS, A_HD), lambda i, j: (j, i, 0, 0)), st_spec],
        out_shape=[jax.ShapeDtypeStruct((t_len, bsz, A_HEADS, A_HD), F32),
                   jax.ShapeDtypeStruct(s0.shape, F32)],
        scratch_shapes=[pltpu.VMEM((bb, A_HD, A_W), F32),
                        pltpu.VMEM((2 * units * A_HD, RW_HALF), BF16),
                        pltpu.VMEM((2 * units * A_HD, RW_HALF), F32),
                        pltpu.VMEM((bb, A_HD, A_W), BF16)],
        compiler_params=_cparams("parallel", "arbitrary"),
        name="rw_rec",
    )(r, w, k, v, a, b, s0)


RW_C = 64
(F_RT, F_PM1, F_IP, F_DC, F_KH, F_KC, F_V, F_KK, F_ASIG, F_RKK, F_G, F_PC) = range(12)


def _rw_chunk_kernel(x_ref, sh_ref, mu_ref, wl_ref, vec_ref, ln_ref, s0_ref, o_ref, so_ref,
                     carry, s_scr, f_scr):
    rt = x_ref.shape[0]
    i = pl.program_id(1)

    @pl.when(i == 0)
    def _():
        carry[...] = sh_ref[0]
        s_scr[...] = jnp.zeros_like(s_scr)
        for p in range(A_HEADS // 2):
            s_scr[p, :A_HD, :A_HD] = s0_ref[0, 2 * p]
            s_scr[p, A_HD:, A_HD:] = s0_ref[0, 2 * p + 1]

    x = x_ref[...]
    row = lax.broadcasted_iota(jnp.int32, x.shape, 0)
    prev = jnp.where(row == 0, carry[...], pltpu.roll(x, 1, axis=0))
    carry[...] = x[rt - 1:rt, :]
    xs = x + (prev - x) * mu_ref[...]
    r = xs[:, :A_W]
    k = xs[:, A_W:2 * A_W]
    lo = xs[:, 3 * A_W:]
    lane = lax.broadcasted_iota(jnp.int32, lo.shape, 1)
    act = jnp.where(lane < A_DECAY_LORA, jnp.tanh(lo),
                    jnp.where(lane < A_DECAY_LORA + A_AAA_LORA, lo, _sigmoid(lo)))
    lora = _dot(act.astype(BF16), wl_ref[...])
    w0, a0, kkw, kaw, rkw = (vec_ref[n:n + 1, :] for n in range(5))
    lw = -jnp.exp(-_softplus(-(w0 + lora[:, :A_W])) - 0.5)
    a_sig = _sigmoid(a0 + lora[:, A_W:2 * A_W])
    k2 = k * (1.0 + (a_sig - 1.0) * kaw)

    ri = lax.broadcasted_iota(jnp.int32, (rt, rt), 0)
    ci = lax.broadcasted_iota(jnp.int32, (rt, rt), 1)
    same = jnp.right_shift(ri, 6) == jnp.right_shift(ci, 6)
    chunk_ones = same.astype(BF16)
    chunk_tri = jnp.where(ci <= ri, chunk_ones, jnp.zeros_like(chunk_ones))
    lp = _dot_sel_left(chunk_tri, lw)
    lpc = _dot_sel_left(chunk_ones, lw)
    ip = jnp.exp(-lp)
    dc = jnp.exp(lpc - lp)
    f_scr[F_RT] = r * jnp.exp(lp)
    f_scr[F_PM1] = jnp.exp(lp - lw)
    f_scr[F_IP] = ip
    f_scr[F_DC] = dc
    f_scr[F_KH] = k2 * ip
    f_scr[F_KC] = k2 * dc
    f_scr[F_V] = xs[:, 2 * A_W:3 * A_W]
    f_scr[F_KK] = k * kkw
    f_scr[F_ASIG] = a_sig
    f_scr[F_RKK] = r * k2 * rkw
    f_scr[F_G] = lora[:, 2 * A_W:]
    f_scr[F_PC] = jnp.exp(lpc)

    C = RW_C
    W2 = 2 * A_HD
    npairs = A_HEADS // 2
    lane_lo = lax.broadcasted_iota(jnp.int32, (C, W2), 1) < A_HD
    own = (lax.broadcasted_iota(jnp.int32, (2 * C, W2), 1) < A_HD) == \
          (lax.broadcasted_iota(jnp.int32, (2 * C, W2), 0) < C)
    r4 = lax.broadcasted_iota(jnp.int32, (4 * C, 4 * C), 0)
    c4 = lax.broadcasted_iota(jnp.int32, (4 * C, 4 * C), 1)
    keep = jnp.bitwise_and(c4, C - 1) < jnp.bitwise_and(r4, C - 1) + jnp.right_shift(r4, 7)
    eye_f = (lax.broadcasted_iota(jnp.int32, (W2, W2), 0) ==
             lax.broadcasted_iota(jnp.int32, (W2, W2), 1)).astype(F32)
    zeros_ww = jnp.zeros((W2, W2), F32)
    bf = lambda z: z.astype(BF16)
    stack = lambda z: jnp.concatenate([jnp.where(lane_lo, z, 0.0), jnp.where(lane_lo, 0.0, z)], axis=0)
    spread = lambda col: jnp.where(lane_lo, col[:C], col[C:])
    chunks_per_body = 2
    probs = [(cc, p) for cc in range(chunks_per_body) for p in range(npairs)]

    def body(j, _):
        rows = [pl.ds(pl.multiple_of((j * chunks_per_body + cc) * C, C), C) for cc in range(chunks_per_body)]
        ld = lambda f, q: f_scr[f, rows[q[0]], q[1] * W2:(q[1] + 1) * W2]
        a_l, r_l, bc_l, kc_l, v_l, g_l = [], [], [], [], [], []
        for q in probs:
            kk = ld(F_KK, q)
            kk_s = stack(kk)
            inv = 1.0 / jnp.maximum(jnp.sqrt(jnp.sum(kk_s * kk_s, axis=-1, keepdims=True)), 1e-12)
            kk = kk * spread(inv)
            b = kk * ld(F_ASIG, q)
            a_l.append(stack(-kk * ld(F_PM1, q)))
            r_l.append(stack(ld(F_RT, q)))
            bc_l.append(stack(b * ld(F_DC, q)))
            kc_l.append(stack(ld(F_KC, q)))
            v_l.append(stack(ld(F_V, q)))
            ar = jnp.concatenate([a_l[-1], r_l[-1]], axis=0)
            bk = jnp.concatenate([stack(b * ld(F_IP, q)), stack(ld(F_KH, q))], axis=0)
            g_l.append(jnp.where(keep, _dot_nt(bf(ar), bf(bk)), 0.0))
        n = range(len(probs))
        x_l = [g_l[e][:W2, :W2] for e in n]
        t_l = [eye_f + x_l[e] for e in n]
        x_l = [_dot(bf(x_l[e]), bf(x_l[e])) for e in n]
        m = 2
        while m < C:
            last = 2 * m >= C
            for e in n:
                rhs = t_l[e] if last else jnp.concatenate([x_l[e], t_l[e]], axis=1)
                prod = _dot(bf(x_l[e]), bf(rhs))
                if last:
                    t_l[e] = t_l[e] + prod
                else:
                    x_l[e] = prod[:, :W2]
                    t_l[e] = t_l[e] + prod[:, W2:]
            m *= 2
        lv_l = [_dot(bf(g_l[e][:W2, W2:]), bf(v_l[e])) for e in n]
        apu_l = [_dot(bf(t_l[e]), bf(jnp.concatenate([a_l[e], lv_l[e]], axis=1))) for e in n]
        qy_l = []
        for e in n:
            low = jnp.concatenate([zeros_ww, v_l[e]], axis=1)
            qy_l.append(_dot(bf(g_l[e][W2:, :]), bf(jnp.concatenate([apu_l[e], low], axis=0))))
        mm_l = [_dot_tn(bf(apu_l[e][:, :W2]), bf(bc_l[e])) for e in n]
        n_l = []
        for e in n:
            uv = jnp.concatenate([apu_l[e][:, W2:], v_l[e]], axis=0)
            n_l.append(_dot_tn(bf(uv), bf(jnp.concatenate([bc_l[e], kc_l[e]], axis=0))))
        for e, q in enumerate(probs):
            p = q[1]
            s0 = s_scr[p]
            s0b = bf(s0)
            y_s = _dot_nt(bf(r_l[e] + qy_l[e][:, :W2]), s0b) + qy_l[e][:, W2:]
            s_scr[p] = s0 * ld(F_PC, q)[0:1, :] + _dot(s0b, bf(mm_l[e])) + n_l[e]
            mu_h = jnp.sum(y_s, axis=-1, keepdims=True) * (1.0 / A_HD)
            d = jnp.where(own, y_s - mu_h, 0.0)
            var = jnp.sum(d * d, axis=-1, keepdims=True) * (1.0 / A_HD)
            d = d * lax.rsqrt(var + A_LN_EPS)
            yn = (d[:C] + d[C:]) * ln_ref[:, p * W2:(p + 1) * W2]
            bonus = spread(jnp.sum(stack(ld(F_RKK, q)), axis=-1, keepdims=True)) * ld(F_V, q)
            o_ref[rows[q[0]], p * W2:(p + 1) * W2] = ((yn + bonus) * ld(F_G, q)).astype(o_ref.dtype)
        return 0

    lax.fori_loop(0, rt // (C * chunks_per_body), body, 0)

    @pl.when(i == pl.num_programs(1) - 1)
    def _():
        for p in range(npairs):
            so_ref[0, 2 * p] = s_scr[p, :A_HD, :A_HD]
            so_ref[0, 2 * p + 1] = s_scr[p, A_HD:, A_HD:]


def _rw_chunk(ua, shift, mu, wl, vec, ln, s0, bsz, t_len, rt):
    nt = t_len // rt
    st_spec = pl.BlockSpec((1, A_HEADS, A_HD, A_HD), lambda b, i: (b, 0, 0, 0))
    return pl.pallas_call(
        _rw_chunk_kernel,
        grid=(bsz, nt),
        in_specs=[pl.BlockSpec((rt, A_SHIFT_W), lambda b, i: (b * nt + i, 0)),
                  pl.BlockSpec((1, 1, A_SHIFT_W), lambda b, i: (b, 0, 0)),
                  _const_spec(mu.shape), _const_spec(wl.shape), _const_spec(vec.shape),
                  _const_spec(ln.shape), st_spec],
        out_specs=[pl.BlockSpec((rt, A_W), lambda b, i: (b * nt + i, 0)), st_spec],
        out_shape=[jax.ShapeDtypeStruct((bsz * t_len, A_W), BF16),
                   jax.ShapeDtypeStruct(s0.shape, F32)],
        scratch_shapes=[pltpu.VMEM((1, A_SHIFT_W), F32),
                        pltpu.VMEM((A_HEADS // 2, 2 * A_HD, 2 * A_HD), F32),
                        pltpu.VMEM((12, rt, A_W), F32)],
        compiler_params=_cparams("parallel", "arbitrary"),
        name="rw_chunk",
    )(ua, shift, mu, wl, vec, ln, s0)


def _rw_post_kernel(y_ref, bonus_ref, g_ref, ln_ref, seg_ref, o_ref):
    seg = seg_ref[...]
    y = y_ref[...]
    mu = _dot_sel(y, seg) * (1.0 / A_HD)
    d = y - mu
    var = _dot_sel(d * d, seg) * (1.0 / A_HD)
    yn = d * lax.rsqrt(var + A_LN_EPS) * ln_ref[...]
    o_ref[...] = ((yn + bonus_ref[...]) * g_ref[...]).astype(o_ref.dtype)


def _rw_post(y, bonus, g, ln, seg, rows):
    n = y.shape[0]
    row_spec = pl.BlockSpec((rows, A_W), lambda i: (i, 0))
    return pl.pallas_call(
        _rw_post_kernel,
        grid=(n // rows,),
        in_specs=[row_spec] * 3 + [_const_spec(ln.shape), _const_spec(seg.shape)],
        out_specs=row_spec,
        out_shape=jax.ShapeDtypeStruct((n, A_W), BF16),
        compiler_params=_cparams("parallel"),
        name="rw_post",
    )(y, bonus, g, ln, seg)


def _head_norm_lanes(x, g, eps):
    mu = jnp.mean(x, axis=-1, keepdims=True)
    d = x - mu
    var = jnp.mean(d * d, axis=-1, keepdims=True)
    return d * lax.rsqrt(var + eps) * g


def _mlstm_chunk_kernel(ub_ref, g_ref, gt_ref, conv_ref, cw_ref, cb_ref, biasr_ref, biasc_ref, gn_ref,
                        c0_ref, n0_ref, m0_ref,
                        y_ref, co_ref, no_ref, mo_ref,
                        tail_s, c_s, n_s, m_s):
    L = ub_ref.shape[0]
    c = pl.program_id(1)

    @pl.when(c == 0)
    def _():
        tail_s[...] = conv_ref[0]
        c_s[...] = c0_ref[0]
        n_s[...] = n0_ref[0]
        m_s[...] = m0_ref[0]

    ub = ub_ref[...]
    raw = ub[:, :2 * B_W]
    ext = jnp.concatenate([tail_s[...], raw], axis=0)
    tail_s[...] = raw[L - SUBLANES:]
    acc = cb_ref[...] + raw * cw_ref[B_CONV - 1:B_CONV, :]
    for jj in range(B_CONV - 1):
        sh = B_CONV - 1 - jj
        acc = acc + pltpu.roll(ext, sh, axis=0)[SUBLANES:] * cw_ref[jj:jj + 1, :]
    qk = _silu(acc)
    q_all = qk[:, :B_W]
    k_all = qk[:, B_W:] * (B_HD ** -0.5)

    r_i = lax.broadcasted_iota(jnp.int32, (L, L), 0)
    c_i = lax.broadcasted_iota(jnp.int32, (L, L), 1)
    causal = c_i <= r_i
    tri = causal.astype(BF16)
    tri_u = (r_i <= c_i).astype(BF16)

    gc = g_ref[...] + biasr_ref[...]
    b_cols = _dot_sel_left(tri, _log_sigmoid(gc))
    gr = gt_ref[0] + biasc_ref[...]
    b_rows = _dot_sel(_log_sigmoid(gr), tri_u)

    for hd in range(B_HEADS):
        sl = slice(hd * B_HD, (hd + 1) * B_HD)
        q = q_all[:, sl]
        k = k_all[:, sl]
        v = ub[:, 2 * B_W + hd * B_HD:2 * B_W + (hd + 1) * B_HD]
        o = ub[:, 3 * B_W + hd * B_HD:3 * B_W + (hd + 1) * B_HD]
        bc = _lane_col(b_cols, B_HEADS + hd)
        ic = _lane_col(gc, hd)
        br = b_rows[B_HEADS + hd:B_HEADS + hd + 1, :]
        ir = gr[hd:hd + 1, :]
        m_prev = m_s[hd:hd + 1, 0:1]
        c_prev = c_s[hd]
        n_prev = n_s[hd:hd + 1, :]

        dlog = jnp.where(causal, bc - br + ir, -jnp.inf)
        inter = bc + m_prev
        m_t = jnp.maximum(inter, jnp.max(dlog, axis=1, keepdims=True))
        qb = q.astype(BF16)
        kb = k.astype(BF16)
        vb = v.astype(BF16)
        s = _dot_nt(qb, kb) * jnp.exp(dlog - m_t)
        iw = jnp.exp(inter - m_t)
        num = _dot(s.astype(BF16), vb) + iw * _dot(qb, c_prev.astype(BF16))
        den = jnp.sum(s, axis=1, keepdims=True) + iw * jnp.sum(q * n_prev, axis=1, keepdims=True)
        hh = num / jnp.maximum(jnp.abs(den), jnp.exp(-m_t))

        b_last = _row_pick(bc, L - 1)
        gi = b_last - bc + ic
        m_new = jnp.maximum(b_last + m_prev, jnp.max(gi, axis=0, keepdims=True))
        wi = jnp.exp(gi - m_new)
        wc = jnp.exp(b_last + m_prev - m_new)
        kw = k * wi
        c_s[hd] = wc * c_prev + _dot_tn(kw.astype(BF16), vb)
        n_s[hd:hd + 1, :] = wc * n_prev + jnp.sum(kw, axis=0, keepdims=True)
        m_s[hd:hd + 1, :] = jnp.broadcast_to(m_new, (1, LANES))

        hb = _sigmoid(o) * hh
        y_ref[:, sl] = _head_norm_lanes(hb, gn_ref[:, sl], GN_EPS).astype(y_ref.dtype)

    @pl.when(c == pl.num_programs(1) - 1)
    def _():
        co_ref[0] = c_s[...]
        no_ref[0] = n_s[...]
        mo_ref[0] = m_s[...]


def _mlstm_chunk(ub, gates, gates_t, conv8, cw, cb, bias_r, bias_c, gn, c0, n0, m0, bsz, t_len):
    L = CHUNK
    nc = t_len // L
    row = lambda w: pl.BlockSpec((L, w), lambda b, c: (b * nc + c, 0))
    per_b = lambda shp: pl.BlockSpec((1,) + shp, lambda b, c: (b,) + (0,) * len(shp))
    return pl.pallas_call(
        _mlstm_chunk_kernel,
        grid=(bsz, nc),
        in_specs=[row(4 * B_W), row(LANES),
                  pl.BlockSpec((1, SUBLANES, L), lambda b, c: (b, 0, c)),
                  per_b((SUBLANES, 2 * B_W)),
                  _const_spec(cw.shape), _const_spec(cb.shape), _const_spec(bias_r.shape),
                  _const_spec(bias_c.shape), _const_spec(gn.shape),
                  per_b((B_HEADS, B_HD, B_HD)), per_b((SUBLANES, B_HD)), per_b((SUBLANES, LANES))],
        out_specs=[row(B_W), per_b((B_HEADS, B_HD, B_HD)), per_b((SUBLANES, B_HD)),
                   per_b((SUBLANES, LANES))],
        out_shape=[jax.ShapeDtypeStruct((bsz * t_len, B_W), BF16),
                   jax.ShapeDtypeStruct((bsz, B_HEADS, B_HD, B_HD), F32),
                   jax.ShapeDtypeStruct((bsz, SUBLANES, B_HD), F32),
                   jax.ShapeDtypeStruct((bsz, SUBLANES, LANES), F32)],
        scratch_shapes=[pltpu.VMEM((SUBLANES, 2 * B_W), F32),
                        pltpu.VMEM((B_HEADS, B_HD, B_HD), F32),
                        pltpu.VMEM((SUBLANES, B_HD), F32),
                        pltpu.VMEM((SUBLANES, LANES), F32)],
        compiler_params=_cparams("parallel", "arbitrary"),
        name="mlstm_chunk",
    )(ub, gates, gates_t, conv8, cw, cb, bias_r, bias_c, gn, c0, n0, m0)


def _mlstm_step_kernel(ub_ref, g_ref, conv_ref, cw_ref, cb_ref, biasr_ref, gn_ref, c_ref, n_ref, m_ref,
                       y_ref, co_ref, no_ref, mo_ref):
    bb = ub_ref.shape[0]
    ub = ub_ref[...]
    raw = ub[:, :2 * B_W]
    conv = conv_ref[...]
    acc = cb_ref[...] + raw * cw_ref[B_CONV - 1:B_CONV, :]
    for jj in range(B_CONV - 1):
        acc = acc + conv[:, jj * 2 * B_W:(jj + 1) * 2 * B_W] * cw_ref[jj:jj + 1, :]
    qk = _silu(acc)
    q_all = qk[:, :B_W]
    k_all = qk[:, B_W:] * (B_HD ** -0.5)
    gc = g_ref[...] + biasr_ref[...]
    lf = _log_sigmoid(gc)
    m_all = m_ref[...]
    row_pad = lax.broadcasted_iota(jnp.int32, (LANES, LANES), 0)
    pad = jnp.zeros((LANES - bb, B_HD), F32)
    m_out = jnp.zeros((bb, LANES), F32)
    lane_m = lax.broadcasted_iota(jnp.int32, (bb, LANES), 1)
    for hd in range(B_HEADS):
        sl = slice(hd * B_HD, (hd + 1) * B_HD)
        q = q_all[:, sl]
        k = k_all[:, sl]
        v = ub[:, 2 * B_W + hd * B_HD:2 * B_W + (hd + 1) * B_HD]
        o = ub[:, 3 * B_W + hd * B_HD:3 * B_W + (hd + 1) * B_HD]
        ic = _lane_col(gc, hd)
        f = _lane_col(lf, B_HEADS + hd)
        m_prev = _lane_col(m_all, hd)
        n_prev = n_ref[:, sl]
        m_t = jnp.maximum(f + m_prev, ic)
        s = jnp.sum(q * k, axis=1, keepdims=True) * jnp.exp(ic - m_t)
        iw = jnp.exp(f + m_prev - m_t)
        wi = jnp.exp(ic - m_t)
        qb = q.astype(BF16)
        kw_t = jnp.concatenate([k * wi, pad], axis=0).T
        v_pad = jnp.concatenate([v, pad], axis=0)
        qc = jnp.zeros((bb, B_HD), F32)
        row_b = lax.broadcasted_iota(jnp.int32, (bb, B_HD), 0)
        for bi in range(bb):
            c_prev = c_ref[bi, hd]
            qc = qc + jnp.where(row_b == bi, _dot(qb, c_prev.astype(BF16)), 0.0)
            v_one = jnp.where(row_pad == bi, v_pad, 0.0)
            co_ref[bi, hd] = _row_pick(iw, bi) * c_prev + _dot3(kw_t, v_one, _dot)
        num = s * v + iw * qc
        den = s + iw * jnp.sum(q * n_prev, axis=1, keepdims=True)
        hh = num / jnp.maximum(jnp.abs(den), jnp.exp(-m_t))
        no_ref[:, sl] = iw * n_prev + wi * k
        m_out = jnp.where(lane_m == hd, m_t, m_out)
        hb = _sigmoid(o) * hh
        y_ref[:, sl] = _head_norm_lanes(hb, gn_ref[:, sl], GN_EPS).astype(y_ref.dtype)
    mo_ref[...] = m_out


def _mlstm_step(ub, gates, conv, cw, cb, bias_r, gn, c0, n0, m0, bb):
    bsz = ub.shape[0]
    row = lambda w: pl.BlockSpec((bb, w), lambda i: (i, 0))
    c_spec = pl.BlockSpec((bb, B_HEADS, B_HD, B_HD), lambda i: (i, 0, 0, 0))
    return pl.pallas_call(
        _mlstm_step_kernel,
        grid=(bsz // bb,),
        in_specs=[row(4 * B_W), row(LANES), row((B_CONV - 1) * 2 * B_W),
                  _const_spec(cw.shape), _const_spec(cb.shape), _const_spec(bias_r.shape),
                  _const_spec(gn.shape), c_spec, row(B_W), row(LANES)],
        out_specs=[row(B_W), c_spec, row(B_W), row(LANES)],
        out_shape=[jax.ShapeDtypeStruct((bsz, B_W), BF16),
                   jax.ShapeDtypeStruct(c0.shape, F32),
                   jax.ShapeDtypeStruct((bsz, B_W), F32),
                   jax.ShapeDtypeStruct((bsz, LANES), F32)],
        compiler_params=_cparams("parallel"),
        name="mlstm_step",
    )(ub, gates, conv, cw, cb, bias_r, gn, c0, n0, m0)


def _ret_log_gamma(hd):
    return math.log1p(-(2.0 ** (-5.0 - hd)))


def _ret_chunk_kernel(q_ref, k_ref, v_ref, g_ref, gn_ref, s0_ref, y_ref, so_ref, s_s):
    L = q_ref.shape[0]
    c = pl.program_id(1)

    @pl.when(c == 0)
    def _():
        s_s[...] = s0_ref[0]

    r_i = lax.broadcasted_iota(jnp.int32, (L, L), 0)
    c_i = lax.broadcasted_iota(jnp.int32, (L, L), 1)
    diff = (r_i - c_i).astype(F32)
    idx = lax.broadcasted_iota(jnp.int32, (L, C_DK), 0).astype(F32)
    for hd in range(C_HEADS):
        lg = _ret_log_gamma(hd)
        mask = jnp.where(diff >= 0, jnp.exp(jnp.maximum(diff, 0.0) * lg), 0.0)
        q_dec = jnp.exp((idx + 1.0) * lg)
        k_dec = jnp.exp((L - 1.0 - idx) * lg)
        c_dec = math.exp(L * lg)
        q = q_ref[:, hd * C_DK:(hd + 1) * C_DK]
        k = k_ref[:, hd * C_DK:(hd + 1) * C_DK]
        v = v_ref[:, hd * C_DV:(hd + 1) * C_DV]
        s_prev = s_s[hd]
        s = _dot_nt(q, k) * mask
        o = _dot(s.astype(BF16), v) + _dot((q.astype(F32) * q_dec).astype(BF16), s_prev.astype(BF16))
        s_s[hd] = c_dec * s_prev + _dot_tn((k.astype(F32) * k_dec).astype(BF16), v)
        sl = slice(hd * C_DV, (hd + 1) * C_DV)
        y = _head_norm_lanes(o, gn_ref[:, sl], GN_EPS) * _silu(g_ref[:, sl])
        y_ref[:, sl] = y.astype(y_ref.dtype)

    @pl.when(c == pl.num_programs(1) - 1)
    def _():
        so_ref[0] = s_s[...]


def _ret_chunk(q, k, v, g, gn, s0, bsz, t_len):
    L = CHUNK
    nc = t_len // L
    row = lambda w: pl.BlockSpec((L, w), lambda b, c: (b * nc + c, 0))
    st = pl.BlockSpec((1, C_HEADS, C_DK, C_DV), lambda b, c: (b, 0, 0, 0))
    return pl.pallas_call(
        _ret_chunk_kernel,
        grid=(bsz, nc),
        in_specs=[row(C_HEADS * C_DK), row(C_HEADS * C_DK), row(C_OUT_W), row(C_OUT_W),
                  _const_spec(gn.shape), st],
        out_specs=[row(C_OUT_W), st],
        out_shape=[jax.ShapeDtypeStruct((bsz * t_len, C_OUT_W), BF16),
                   jax.ShapeDtypeStruct(s0.shape, F32)],
        scratch_shapes=[pltpu.VMEM((C_HEADS, C_DK, C_DV), F32)],
        compiler_params=_cparams("parallel", "arbitrary"),
        name="ret_chunk",
    )(q, k, v, g, gn, s0)


def _ret_step_kernel(q_ref, k_ref, v_ref, g_ref, gn_ref, s_ref, y_ref, so_ref):
    bb = q_ref.shape[0]
    hd = pl.program_id(1)
    gam = 1.0 - jnp.exp2(-5.0 - jnp.full((1, 1), hd).astype(F32))
    q = q_ref[...]
    k = k_ref[...]
    v = v_ref[...]
    s = jnp.sum(q * k, axis=1, keepdims=True)
    pad_k = jnp.zeros((LANES - bb, C_DK), F32)
    pad_v = jnp.zeros((LANES - bb, C_DV), F32)
    k_t = jnp.concatenate([k, pad_k], axis=0).T
    v_pad = jnp.concatenate([v, pad_v], axis=0)
    row_pad = lax.broadcasted_iota(jnp.int32, (LANES, C_DV), 0)
    row_b = lax.broadcasted_iota(jnp.int32, (bb, C_DV), 0)
    qg = (q * gam).astype(BF16)
    qs = jnp.zeros((bb, C_DV), F32)
    for bi in range(bb):
        s_prev = s_ref[bi, 0]
        qs = qs + jnp.where(row_b == bi, _dot(qg, s_prev.astype(BF16)), 0.0)
        v_one = jnp.where(row_pad == bi, v_pad, 0.0)
        so_ref[bi, 0] = gam * s_prev + _dot3(k_t, v_one, _dot)
    o = s * v + qs
    y = _head_norm_lanes(o, gn_ref[...], GN_EPS) * _silu(g_ref[...])
    y_ref[...] = y.astype(y_ref.dtype)


def _ret_step(q, k, v, g, gn, s0, bb):
    bsz = q.shape[0]
    st = pl.BlockSpec((bb, 1, C_DK, C_DV), lambda i, h: (i, h, 0, 0))
    return pl.pallas_call(
        _ret_step_kernel,
        grid=(bsz // bb, C_HEADS),
        in_specs=[pl.BlockSpec((bb, C_DK), lambda i, h: (i, h)),
                  pl.BlockSpec((bb, C_DK), lambda i, h: (i, h)),
                  pl.BlockSpec((bb, C_DV), lambda i, h: (i, h)),
                  pl.BlockSpec((bb, C_DV), lambda i, h: (i, h)),
                  pl.BlockSpec((1, C_DV), lambda i, h: (0, h)), st],
        out_specs=[pl.BlockSpec((bb, C_DV), lambda i, h: (i, h)), st],
        out_shape=[jax.ShapeDtypeStruct((bsz, C_OUT_W), BF16),
                   jax.ShapeDtypeStruct(s0.shape, F32)],
        compiler_params=_cparams("parallel", "parallel"),
        name="ret_step",
    )(q, k, v, g, gn, s0)


def _post_kernel(*refs, n_mix):
    h_ref = refs[0]
    mix_refs = refs[1:1 + n_mix]
    wout_refs = refs[1 + n_mix:1 + 2 * n_mix]
    (p_ref, g_ref, up_ref, down_ref, gate_ref, proj_ref, o_ref, h1_s, xn_s, acc_s) = refs[1 + 2 * n_mix:]
    f = pl.program_id(1)

    @pl.when(f == 0)
    def _():
        mix = _dot(mix_refs[0][...], wout_refs[0][...])
        for m_ref, w_ref in zip(mix_refs[1:], wout_refs[1:]):
            mix = mix + _dot(m_ref[...], w_ref[...])
        h1 = h_ref[...] + _rms(mix, g_ref[1:2, :])
        h1_s[...] = h1
        xn_s[...] = _rms(h1, g_ref[2:3, :]).astype(BF16)
        acc_s[...] = jnp.zeros_like(acc_s)

    hid = jnp.square(jnp.maximum(_dot(xn_s[...], up_ref[...]), 0.0))
    acc_s[...] += _dot(hid.astype(BF16), down_ref[...])

    @pl.when(f == pl.num_programs(1) - 1)
    def _():
        h2 = h1_s[...] + _rms(acc_s[...], g_ref[3:4, :])
        gate = _sigmoid(_dot(h2.astype(BF16), gate_ref[...]))
        emb = _dot(p_ref[...].astype(BF16), proj_ref[...])
        o_ref[...] = h2 + gate * emb


def _post(h, mixes, wouts, p, g, up, down, gate, proj, tm, tf):
    n = h.shape[0]
    n_mix = len(mixes)
    row = lambda w: pl.BlockSpec((tm, w), lambda i, f: (i, 0))
    return pl.pallas_call(
        functools.partial(_post_kernel, n_mix=n_mix),
        grid=(n // tm, D_FF // tf),
        in_specs=[row(D_MODEL)] + [row(m.shape[1]) for m in mixes]
        + [_const_spec(w.shape) for w in wouts]
        + [row(D_PLE), _const_spec(g.shape),
           pl.BlockSpec((D_MODEL, tf), lambda i, f: (0, f)),
           pl.BlockSpec((tf, D_MODEL), lambda i, f: (f, 0)),
           _const_spec(gate.shape), _const_spec(proj.shape)],
        out_specs=row(D_MODEL),
        out_shape=jax.ShapeDtypeStruct((n, D_MODEL), F32),
        scratch_shapes=[pltpu.VMEM((tm, D_MODEL), F32), pltpu.VMEM((tm, D_MODEL), BF16),
                        pltpu.VMEM((tm, D_MODEL), F32)],
        compiler_params=_cparams("parallel", "arbitrary"),
        name="post",
    )(h, *mixes, *wouts, p, g, up, down, gate, proj)


def _prep_weights(W):
    f = {}
    w_in = W['mix0_w_in'][0]
    f['w_a'] = w_in[:, :A_SHIFT_W].astype(BF16)
    f['w_b'] = w_in[:, A_SHIFT_W:A_SHIFT_W + 4 * B_W].astype(BF16)
    f['w_g'] = jnp.pad(w_in[:, A_SHIFT_W + 4 * B_W:], ((0, 0), (0, LANES - 2 * B_HEADS))).astype(BF16)
    wl = jnp.zeros((A_LORA_W, 3 * A_W), F32)
    wl = wl.at[:A_DECAY_LORA, :A_W].set(W['rw_w2'][0])
    wl = wl.at[A_DECAY_LORA:A_DECAY_LORA + A_AAA_LORA, A_W:2 * A_W].set(W['rw_a2'][0])
    wl = wl.at[A_DECAY_LORA + A_AAA_LORA:, 2 * A_W:].set(W['rw_g2'][0])
    f['w_lora'] = wl.astype(BF16)
    f['rw_vec'] = jnp.concatenate(
        [W['rw_w0'], W['rw_a0'], W['rw_kk'], W['rw_ka'], W['rw_rk'][0].reshape(1, A_W),
         jnp.zeros((3, A_W), F32)], axis=0)
    f['rw_mu'] = W['rw_mu']
    f['rw_ln'] = W['rw_ln']
    f['seg'] = (jnp.arange(A_W)[:, None] // A_HD == jnp.arange(A_W)[None, :] // A_HD).astype(BF16)
    f['cw'] = W['ml_conv_w'][0]
    f['cb'] = W['ml_conv_b']
    gate_bias = jnp.concatenate([W['ml_i_bias'][0], W['ml_f_bias'][0]])
    f['bias_r'] = jnp.pad(gate_bias, (0, LANES - 2 * B_HEADS)).reshape(1, LANES)
    f['bias_c'] = gate_bias.reshape(2 * B_HEADS, 1)
    f['ml_gn'] = W['ml_gn']
    w_out0 = W['mix0_w_out'][0].astype(BF16)
    f['w_out_a'] = w_out0[:A_W]
    f['w_out_b'] = w_out0[A_W:]
    w_ret = W['ret_w_in'][0]
    nq = C_HEADS * C_DK
    f['w_rq'] = w_ret[:, :nq].astype(BF16)
    f['w_rk'] = w_ret[:, nq:2 * nq].astype(BF16)
    f['w_rv'] = w_ret[:, 2 * nq:2 * nq + C_OUT_W].astype(BF16)
    f['w_rg'] = w_ret[:, 2 * nq + C_OUT_W:].astype(BF16)
    f['ret_gn'] = W['ret_gn']
    f['w_ret_out'] = W['ret_w_out'][0].astype(BF16)
    for name in ('ffn_up', 'ffn_down', 'ple_gate', 'ple_proj'):
        f[name] = W[name].astype(BF16)
    f['norm_g'] = W['norm_g']
    return f


def _rope_tables(pos):
    half = C_DK // 2
    inv = ROPE_BASE ** (-jnp.arange(half, dtype=F32) / half)
    ang = pos.astype(F32)[:, None] * inv[None, :]
    return jnp.cos(ang), jnp.sin(ang)


def _even_layer(h, p, st, f, bsz, t_len):
    shift, s_a, conv, c0, n0, m0 = st
    n = bsz * t_len
    seq = t_len > 1
    tm = 512 if seq else n
    ua, ub, gates = _inproj(h, f['norm_g'][0], [f['w_a'], f['w_b'], f['w_g']], [F32, F32, F32], tm)

    if seq:
        ya, s_a_new = _rw_chunk(ua, shift[:, None, :], f['rw_mu'], f['w_lora'], f['rw_vec'], f['rw_ln'],
                                s_a, bsz, t_len, rt=256)
    else:
        outs = _rw_prep(ua, shift, f['rw_mu'], f['w_lora'], f['rw_vec'], f['seg'], n)
        r, w, k, v, a, b, g, bonus = outs
        as3 = lambda x: x.reshape(1, n, A_W)
        y, s_a_new = _rw_rec(as3(r), as3(w), as3(k), as3(v), as3(a), as3(b), s_a, tb=1, bb=SUBLANES)
        ya = _rw_post(y.reshape(n, A_W), bonus, g, f['rw_ln'], f['seg'], n)

    if seq:
        conv8 = jnp.pad(conv, ((0, 0), (SUBLANES - (B_CONV - 1), 0), (0, 0)))
        gates_t = gates.reshape(bsz, t_len, LANES)[:, :, :SUBLANES].transpose(0, 2, 1)
        n0p = jnp.pad(n0, ((0, 0), (0, SUBLANES - B_HEADS), (0, 0)))
        m0p = jnp.broadcast_to(jnp.pad(m0, ((0, 0), (0, SUBLANES - B_HEADS)))[:, :, None],
                               (bsz, SUBLANES, LANES))
        yb, c_new, n_new, m_new = _mlstm_chunk(ub, gates, gates_t, conv8, f['cw'], f['cb'], f['bias_r'],
                                               f['bias_c'], f['ml_gn'], c0, n0p, m0p, bsz, t_len)
        n_new = n_new[:, :B_HEADS]
        m_new = m_new[:, :B_HEADS, 0]
    else:
        m0p = jnp.pad(m0, ((0, 0), (0, LANES - B_HEADS)))
        yb, c_new, n_new, m_new = _mlstm_step(ub, gates, conv.reshape(bsz, -1), f['cw'], f['cb'],
                                              f['bias_r'], f['ml_gn'], c0, n0.reshape(bsz, B_W), m0p,
                                              bb=SUBLANES)
        n_new = n_new.reshape(bsz, B_HEADS, B_HD)
        m_new = m_new[:, :B_HEADS]

    h_next = _post(h, [ya, yb], [f['w_out_a'], f['w_out_b']], p, f['norm_g'][0], f['ffn_up'][0],
                   f['ffn_down'][0], f['ple_gate'][0], f['ple_proj'][0], tm=tm, tf=512)

    shift_new = ua.reshape(bsz, t_len, A_SHIFT_W)[:, -1]
    qk_raw = ub.reshape(bsz, t_len, 4 * B_W)[:, :, :2 * B_W]
    conv_new = jnp.concatenate([conv, qk_raw], axis=1)[:, -(B_CONV - 1):]
    return h_next, (shift_new, s_a_new, conv_new, c_new, n_new, m_new)


def _odd_layer(h, p, s0, pos, f, bsz, t_len):
    n = bsz * t_len
    seq = t_len > 1
    cos, sin = _rope_tables(pos)
    tm = 256 if seq else n
    ws = [f['w_rq'], f['w_rk'], f['w_rv'], f['w_rg']]
    if seq:
        q, k, v, g = _inproj_ret(h, f['norm_g'][1], cos, sin, ws, tm, t_len // tm, [BF16, BF16, BF16, F32])
        y, s_new = _ret_chunk(q, k, v, g, f['ret_gn'], s0, bsz, t_len)
    else:
        q, k, v, g = _inproj_ret(h, f['norm_g'][1], cos, sin, ws, tm, 1, [F32, F32, F32, F32])
        y, s_new = _ret_step(q, k, v, g, f['ret_gn'], s0, bb=SUBLANES)
    h_next = _post(h, [y], [f['w_ret_out']], p, f['norm_g'][1], f['ffn_up'][1], f['ffn_down'][1],
                   f['ple_gate'][1], f['ple_proj'][1], tm=512 if seq else n, tf=512)
    return h_next, s_new


def _trunk(x, p, pos, ev_states, od_state, f):
    bsz, t_len, _ = x.shape
    n = bsz * t_len
    h = x.reshape(n, D_MODEL)
    h, ev_new = _even_layer(h, p[0].reshape(n, D_PLE), tuple(s[0] for s in ev_states), f, bsz, t_len)
    h, od_new = _odd_layer(h, p[1].reshape(n, D_PLE), od_state[0], pos, f, bsz, t_len)
    return h.reshape(bsz, t_len, D_MODEL), [s[None] for s in ev_new], od_new[None]


def kernel(x_prompt, x_sample, state_rwkv_shift, state_rwkv_S, state_mlstm_conv, state_mlstm_C,
           state_mlstm_n, state_mlstm_m, state_ret_S, p_prompt, p_sample, norm_g, ffn_up, ffn_down,
           ple_gate, ple_proj, mix0_w_in, rw_mu, rw_w0, rw_w2, rw_a0, rw_a2, rw_g2, rw_kk, rw_ka,
           rw_rk, rw_ln, ml_conv_w, ml_conv_b, ml_i_bias, ml_f_bias, ml_gn, mix0_w_out,
           ret_w_in, ret_gn, ret_w_out):
    W = dict(norm_g=norm_g, ffn_up=ffn_up, ffn_down=ffn_down, ple_gate=ple_gate, ple_proj=ple_proj,
             mix0_w_in=mix0_w_in, rw_mu=rw_mu, rw_w0=rw_w0, rw_w2=rw_w2, rw_a0=rw_a0, rw_a2=rw_a2,
             rw_g2=rw_g2, rw_kk=rw_kk, rw_ka=rw_ka, rw_rk=rw_rk, rw_ln=rw_ln, ml_conv_w=ml_conv_w,
             ml_conv_b=ml_conv_b, ml_i_bias=ml_i_bias, ml_f_bias=ml_f_bias, ml_gn=ml_gn,
             mix0_w_out=mix0_w_out, ret_w_in=ret_w_in, ret_gn=ret_gn, ret_w_out=ret_w_out)
    f = _prep_weights(W)
    bp, tp = x_prompt.shape[:2]
    ts = x_sample.shape[1]
    ev0 = (jnp.zeros((1, bp, A_SHIFT_W), F32),
           jnp.zeros((1, bp, A_HEADS, A_HD, A_HD), F32),
           jnp.zeros((1, bp, B_CONV - 1, 2 * B_W), F32),
           jnp.zeros((1, bp, B_HEADS, B_HD, B_HD), F32),
           jnp.zeros((1, bp, B_HEADS, B_HD), F32),
           jnp.zeros((1, bp, B_HEADS), F32))
    od0 = jnp.zeros((1, bp, C_HEADS, C_DK, C_DV), F32)
    y_prompt, ev_p, ret_p = _trunk(x_prompt, p_prompt, jnp.arange(tp), ev0, od0, f)
    ev_s_in = (state_rwkv_shift, state_rwkv_S, state_mlstm_conv, state_mlstm_C, state_mlstm_n,
               state_mlstm_m)
    y_sample, ev_s, ret_s = _trunk(x_sample, p_sample, PAST_LEN + jnp.arange(ts), ev_s_in, state_ret_S, f)
    return (y_prompt, y_sample, *ev_p, ret_p, *ev_s, ret_s)
```

```python
import functools
import math

import jax
import jax.numpy as jnp
from jax import lax
from jax.experimental import pallas as pl
from jax.experimental.pallas import tpu as pltpu

F32 = jnp.float32
BF16 = jnp.bfloat16

D_MODEL = 1024
D_PLE = 256
D_FF = 4 * D_MODEL
RMS_EPS = 1e-6
GN_EPS = 1e-5
CHUNK = 128

A_HEADS = 8
A_HD = 64
A_W = A_HEADS * A_HD
A_DECAY_LORA = 64
A_AAA_LORA = 64
A_GATE_LORA = 128
A_LORA_W = A_DECAY_LORA + A_AAA_LORA + A_GATE_LORA
A_SHIFT_W = 3 * A_W + A_LORA_W
A_LN_EPS = 64e-5

B_HEADS = 4
B_HD = 128
B_W = B_HEADS * B_HD
B_CONV = 4

C_HEADS = 4
C_DK = 256
C_DV = 512
C_OUT_W = C_HEADS * C_DV
ROPE_BASE = 10000.0
PAST_LEN = 16384

LANES = 128
SUBLANES = 8
VMEM_LIMIT = 56 * 1024 * 1024


def _cparams(*sem):
    return pltpu.CompilerParams(dimension_semantics=sem, vmem_limit_bytes=VMEM_LIMIT)


def _const_spec(shape):
    nd = len(shape)
    return pl.BlockSpec(shape, lambda *_: (0,) * nd)


def _rms(x, g):
    return x * lax.rsqrt(jnp.mean(x * x, axis=-1, keepdims=True) + RMS_EPS) * g


def _dot(a, b):
    return jnp.dot(a, b, preferred_element_type=F32)


def _bdot(a, b):
    return jnp.dot(a.astype(BF16), b.astype(BF16), preferred_element_type=F32)


def _dot_nt(a, b):
    return lax.dot_general(a, b, (((1,), (1,)), ((), ())), preferred_element_type=F32)


def _dot_tn(a, b):
    return lax.dot_general(a, b, (((0,), (0,)), ((), ())), preferred_element_type=F32)


def _hilo(x):
    hi = x.astype(BF16)
    lo = (x - hi.astype(F32)).astype(BF16)
    return hi, lo


def _dot_sel(x, sel):
    hi, lo = _hilo(x)
    return _dot(hi, sel) + _dot(lo, sel)


def _dot_sel_left(sel, x):
    hi, lo = _hilo(x)
    return _dot(sel, hi) + _dot(sel, lo)


def _dot3(a, b, dot):
    ah, al = _hilo(a)
    bh, bl = _hilo(b)
    return dot(ah, bh) + dot(ah, bl) + dot(al, bh)


def _sigmoid(x):
    return 1.0 / (1.0 + jnp.exp(-x))


def _softplus(x):
    return jnp.maximum(x, 0.0) + jnp.log1p(jnp.exp(-jnp.abs(x)))


def _log_sigmoid(x):
    return -_softplus(-x)


def _silu(x):
    return x * _sigmoid(x)


def _lane_col(x, idx):
    lane = lax.broadcasted_iota(jnp.int32, x.shape, 1)
    return jnp.sum(jnp.where(lane == idx, x, 0.0), axis=1, keepdims=True)


def _row_pick(x, idx):
    row = lax.broadcasted_iota(jnp.int32, x.shape, 0)
    return jnp.sum(jnp.where(row == idx, x, 0.0), axis=0, keepdims=True)


def _seg_matrix(n, seg):
    sh = seg.bit_length() - 1
    r = jnp.right_shift(lax.broadcasted_iota(jnp.int32, (n, n), 0), sh)
    c = jnp.right_shift(lax.broadcasted_iota(jnp.int32, (n, n), 1), sh)
    return (r == c).astype(BF16)


def _inproj_kernel(x_ref, g_ref, *refs):
    n = len(refs) // 2
    xn = _rms(x_ref[...], g_ref[0:1, :]).astype(BF16)
    for w_ref, o_ref in zip(refs[:n], refs[n:]):
        o_ref[...] = _dot(xn, w_ref[...]).astype(o_ref.dtype)


def _inproj(h, g, ws, dtypes, tm):
    n = h.shape[0]
    return pl.pallas_call(
        _inproj_kernel,
        grid=(n // tm,),
        in_specs=[pl.BlockSpec((tm, D_MODEL), lambda i: (i, 0)), _const_spec(g.shape)]
        + [_const_spec(w.shape) for w in ws],
        out_specs=[pl.BlockSpec((tm, w.shape[1]), lambda i: (i, 0)) for w in ws],
        out_shape=[jax.ShapeDtypeStruct((n, w.shape[1]), dt) for w, dt in zip(ws, dtypes)],
        compiler_params=_cparams("parallel"),
        name="inproj",
    )(h, g, *ws)


def _inproj_ret_kernel(x_ref, g_ref, gn_ref, cos_ref, sin_ref, wq_ref, wk_ref, wv_ref, wg_ref, *out_refs,
                       decayed):
    if decayed:
        q_ref, qd_ref, k_ref, kd_ref, v_ref, gate_ref = out_refs
    else:
        q_ref, k_ref, v_ref, gate_ref = out_refs
        qd_ref = kd_ref = None
    xn = _rms(x_ref[...], g_ref[0:1, :]).astype(BF16)
    cos = cos_ref[...]
    sin = sin_ref[...]
    half = C_DK // 2
    tm = x_ref.shape[0]
    pos = jnp.bitwise_and(lax.broadcasted_iota(jnp.int32, (tm, half), 0), CHUNK - 1).astype(F32)
    for w_ref, o_ref, d_ref, scale in ((wq_ref, q_ref, qd_ref, None), (wk_ref, k_ref, kd_ref, C_DK ** -0.5)):
        u = _dot(xn, w_ref[...])
        for hd in range(C_HEADS):
            x1 = u[:, hd * C_DK:hd * C_DK + half]
            x2 = u[:, hd * C_DK + half:(hd + 1) * C_DK]
            y1 = x1 * cos - x2 * sin
            y2 = x1 * sin + x2 * cos
            if scale is not None:
                y1 = y1 * scale
                y2 = y2 * scale
            o_ref[:, hd * C_DK:hd * C_DK + half] = y1.astype(o_ref.dtype)
            o_ref[:, hd * C_DK + half:(hd + 1) * C_DK] = y2.astype(o_ref.dtype)
            if decayed:
                lg = _ret_log_gamma(hd)
                dec = jnp.exp((pos + 1.0) * lg) if d_ref is qd_ref else jnp.exp((CHUNK - 1.0 - pos) * lg)
                d_ref[:, hd * C_DK:hd * C_DK + half] = (y1 * dec).astype(d_ref.dtype)
                d_ref[:, hd * C_DK + half:(hd + 1) * C_DK] = (y2 * dec).astype(d_ref.dtype)
    v_ref[...] = _dot(xn, wv_ref[...]).astype(v_ref.dtype)
    gate_ref[...] = (_silu(_dot(xn, wg_ref[...])) * gn_ref[...]).astype(gate_ref.dtype)


def _inproj_ret(h, g, gn, cos, sin, ws, tm, seq_tiles, dtypes, decayed):
    n = h.shape[0]
    if cos.shape[0] == 1:
        cs_spec = _const_spec(cos.shape)
    else:
        cs_spec = pl.BlockSpec((tm, cos.shape[1]), lambda i: (i % seq_tiles, 0))
    widths = [w.shape[1] for w in ws]
    if decayed:
        widths = [widths[0], widths[0], widths[1], widths[1], widths[2], widths[3]]
    return pl.pallas_call(
        functools.partial(_inproj_ret_kernel, decayed=decayed),
        grid=(n // tm,),
        in_specs=[pl.BlockSpec((tm, D_MODEL), lambda i: (i, 0)), _const_spec(g.shape), _const_spec(gn.shape),
                  cs_spec, cs_spec] + [_const_spec(w.shape) for w in ws],
        out_specs=[pl.BlockSpec((tm, wd), lambda i: (i, 0)) for wd in widths],
        out_shape=[jax.ShapeDtypeStruct((n, wd), dt) for wd, dt in zip(widths, dtypes)],
        compiler_params=_cparams("parallel"),
        name="inproj_ret",
    )(h, g, gn, cos, sin, *ws)


def _rw_prep_kernel(x_ref, sh_ref, mu_ref, wl_ref, vec_ref, seg_ref,
                    r_ref, w_ref, k_ref, v_ref, a_ref, b_ref, g_ref, bonus_ref, carry):
    rows = x_ref.shape[0]
    nb = sh_ref.shape[0]

    @pl.when(pl.program_id(0) == 0)
    def _():
        carry[...] = sh_ref[...]

    x = x_ref[...]
    if rows == nb:
        prev = carry[...]
    else:
        prev = jnp.concatenate([carry[...], x[:rows - nb]], axis=0)
    carry[...] = x[rows - nb:]
    xs = x + (prev - x) * mu_ref[...]
    r = xs[:, :A_W]
    k = xs[:, A_W:2 * A_W]
    v = xs[:, 2 * A_W:3 * A_W]
    lo = xs[:, 3 * A_W:]
    lane = lax.broadcasted_iota(jnp.int32, lo.shape, 1)
    act = jnp.where(lane < A_DECAY_LORA, jnp.tanh(lo),
                    jnp.where(lane < A_DECAY_LORA + A_AAA_LORA, lo, _sigmoid(lo)))
    lora = _dot(act.astype(BF16), wl_ref[...])
    w0, a0, kkw, kaw, rkw = (vec_ref[i:i + 1, :] for i in range(5))
    w_log = -_softplus(-(w0 + lora[:, :A_W])) - 0.5
    decay = jnp.exp(-jnp.exp(w_log))
    a = _sigmoid(a0 + lora[:, A_W:2 * A_W])
    g = lora[:, 2 * A_W:]
    seg = seg_ref[...]
    kk = k * kkw
    kk = kk / jnp.maximum(jnp.sqrt(_dot_sel(kk * kk, seg)), 1e-12)
    k2 = k * (1.0 + (a - 1.0) * kaw)
    bonus = _dot_sel(r * k2 * rkw, seg) * v
    r_ref[...] = r
    w_ref[...] = decay
    k_ref[...] = k2
    v_ref[...] = v
    a_ref[...] = -kk
    b_ref[...] = kk * a
    g_ref[...] = g
    bonus_ref[...] = bonus


def _rw_prep(ua_tm, shift, mu, wl, vec, seg, rows):
    n = ua_tm.shape[0]
    nb = shift.shape[0]
    row_spec = lambda w: pl.BlockSpec((rows, w), lambda i: (i, 0))
    return pl.pallas_call(
        _rw_prep_kernel,
        grid=(n // rows,),
        in_specs=[row_spec(A_SHIFT_W), _const_spec(shift.shape), _const_spec(mu.shape),
                  _const_spec(wl.shape), _const_spec(vec.shape), _const_spec(seg.shape)],
        out_specs=[row_spec(A_W)] * 8,
        out_shape=[jax.ShapeDtypeStruct((n, A_W), F32)] * 8,
        scratch_shapes=[pltpu.VMEM((nb, A_SHIFT_W), F32)],
        compiler_params=_cparams("arbitrary"),
        name="rw_prep",
    )(ua_tm, shift, mu, wl, vec, seg)


RW_HALF = 2 * LANES


def _rw_rec_kernel(r_ref, w_ref, k_ref, v_ref, a_ref, b_ref, s0_ref, y_ref, so_ref,
                   s_scr, lhs_scr, red_scr, snb_scr):
    tb, bb, _ = r_ref.shape
    nq = A_W // RW_HALF
    units = bb * nq
    j = pl.program_id(1)

    @pl.when(j == 0)
    def _():
        for bi in range(bb):
            for hd in range(A_HEADS):
                s_scr[bi, :, hd * A_HD:(hd + 1) * A_HD] = s0_ref[bi, hd]

    row = lax.broadcasted_iota(jnp.int32, (A_HD, RW_HALF), 0)
    col = lax.broadcasted_iota(jnp.int32, (A_HD, RW_HALF), 1)
    key = jnp.bitwise_and(col, A_HD - 1)
    on_diag = key == row
    off_diag = key == jnp.bitwise_and(row + 1, A_HD - 1)
    seg = _seg_matrix(RW_HALF, A_HD)
    head_of_lane = jnp.right_shift(lax.broadcasted_iota(jnp.int32, (A_HEADS, A_W), 1), 6)
    own_head = head_of_lane == lax.broadcasted_iota(jnp.int32, (A_HEADS, A_W), 0)
    first_key = jnp.bitwise_and(lax.broadcasted_iota(jnp.int32, (bb, A_W), 1), A_HD - 1) == 0

    def step(t, carry):
        rt, wt, kt, vt, at, bt = (ref[t] for ref in (r_ref, w_ref, k_ref, v_ref, a_ref, b_ref))
        v_hi = vt.astype(BF16).astype(F32)
        v_lo = vt - v_hi
        v_lo = jnp.concatenate(
            [jnp.where(first_key[:, :LANES],
                       pltpu.roll(v_lo[:, c * LANES:(c + 1) * LANES], LANES - (A_HD - 1), axis=1),
                       pltpu.roll(v_lo[:, c * LANES:(c + 1) * LANES], 1, axis=1))
             for c in range(A_W // LANES)], axis=1)
        for bi in range(bb):
            pa = (s_scr[bi] * at[bi:bi + 1]).astype(BF16)
            for q in range(nq):
                u = bi * nq + q
                sl = slice(q * RW_HALF, (q + 1) * RW_HALF)
                lhs_scr[u * A_HD:(u + 1) * A_HD, :] = pa[:, sl]
                dv = jnp.where(on_diag, v_hi[bi:bi + 1, sl], jnp.where(off_diag, v_lo[bi:bi + 1, sl], 0.0))
                lhs_scr[(units + u) * A_HD:(units + u + 1) * A_HD, :] = dv.astype(BF16)
        red_scr[...] = _dot(lhs_scr[...], seg)
        for bi in range(bb):
            for q in range(nq):
                u = bi * nq + q
                sl = slice(q * RW_HALF, (q + 1) * RW_HALF)
                sa = red_scr[u * A_HD:(u + 1) * A_HD, :]
                vb = red_scr[(units + u) * A_HD:(units + u + 1) * A_HD, :]
                sn = s_scr[bi, :, sl] * wt[bi:bi + 1, sl] + sa * bt[bi:bi + 1, sl] + vb * kt[bi:bi + 1, sl]
                s_scr[bi, :, sl] = sn
                snb_scr[bi, :, sl] = sn.astype(BF16)
        for bi in range(bb):
            r_heads = jnp.where(own_head, jnp.broadcast_to(rt[bi:bi + 1], (A_HEADS, A_W)), 0.0)
            y_ref[t, bi] = _dot_nt(r_heads.astype(BF16), snb_scr[bi])
        return carry

    lax.fori_loop(0, tb, step, 0)

    @pl.when(j == pl.num_programs(1) - 1)
    def _():
        for bi in range(bb):
            for hd in range(A_HEADS):
                so_ref[bi, hd] = s_scr[bi, :, hd * A_HD:(hd + 1) * A_HD]


def _rw_rec(r, w, k, v, a, b, s0, tb, bb):
    t_len, bsz, _ = r.shape
    seq_spec = pl.BlockSpec((tb, bb, A_W), lambda i, j: (j, i, 0))
    st_spec = pl.BlockSpec((bb, A_HEADS, A_HD, A_HD), lambda i, j: (i, 0, 0, 0))
    units = bb * (A_W // RW_HALF)
    return pl.pallas_call(
        _rw_rec_kernel,
        grid=(bsz // bb, t_len // tb),
        in_specs=[seq_spec] * 6 + [st_spec],
        out_specs=[pl.BlockSpec((tb, bb, A_HEADS, A_HD), lambda i, j: (j, i, 0, 0)), st_spec],
        out_shape=[jax.ShapeDtypeStruct((t_len, bsz, A_HEADS, A_HD), F32),
                   jax.ShapeDtypeStruct(s0.shape, F32)],
        scratch_shapes=[pltpu.VMEM((bb, A_HD, A_W), F32),
                        pltpu.VMEM((2 * units * A_HD, RW_HALF), BF16),
                        pltpu.VMEM((2 * units * A_HD, RW_HALF), F32),
                        pltpu.VMEM((bb, A_HD, A_W), BF16)],
        compiler_params=_cparams("parallel", "arbitrary"),
        name="rw_rec",
    )(r, w, k, v, a, b, s0)


RW_C = 64
(F_RT, F_PM1, F_IP, F_DC, F_KH, F_KC, F_V, F_KK, F_ASIG, F_RKK, F_G, F_PC) = range(12)


def _rw_chunk_kernel(x_ref, mu_ref, wl_ref, vec_ref, ln_ref, o_ref, so_ref, carry, s_scr, f_scr):
    rt = x_ref.shape[0]
    i = pl.program_id(1)

    @pl.when(i == 0)
    def _():
        carry[...] = jnp.zeros_like(carry)
        s_scr[...] = jnp.zeros_like(s_scr)

    x = x_ref[...]
    row = lax.broadcasted_iota(jnp.int32, x.shape, 0)
    prev = jnp.where(row == 0, carry[...], pltpu.roll(x, 1, axis=0))
    carry[...] = x[rt - 1:rt, :]
    xs = x + (prev - x) * mu_ref[...]
    r = xs[:, :A_W]
    k = xs[:, A_W:2 * A_W]
    lo = xs[:, 3 * A_W:]
    lane = lax.broadcasted_iota(jnp.int32, lo.shape, 1)
    act = jnp.where(lane < A_DECAY_LORA, jnp.tanh(lo),
                    jnp.where(lane < A_DECAY_LORA + A_AAA_LORA, lo, _sigmoid(lo)))
    lora = _dot(act.astype(BF16), wl_ref[...])
    w0, a0, kkw, kaw, rkw = (vec_ref[n:n + 1, :] for n in range(5))
    lw = -jnp.exp(-_softplus(-(w0 + lora[:, :A_W])) - 0.5)
    a_sig = _sigmoid(a0 + lora[:, A_W:2 * A_W])
    k2 = k * (1.0 + (a_sig - 1.0) * kaw)

    ri = lax.broadcasted_iota(jnp.int32, (rt, rt), 0)
    ci = lax.broadcasted_iota(jnp.int32, (rt, rt), 1)
    same = jnp.right_shift(ri, 6) == jnp.right_shift(ci, 6)
    chunk_ones = same.astype(BF16)
    chunk_tri = jnp.where(ci <= ri, chunk_ones, jnp.zeros_like(chunk_ones))
    lp = _dot_sel_left(chunk_tri, lw)
    lpc = _dot_sel_left(chunk_ones, lw)
    ip = jnp.exp(-lp)
    dc = jnp.exp(lpc - lp)
    f_scr[F_RT] = r * jnp.exp(lp)
    f_scr[F_PM1] = jnp.exp(lp - lw)
    f_scr[F_IP] = ip
    f_scr[F_DC] = dc
    f_scr[F_KH] = k2 * ip
    f_scr[F_KC] = k2 * dc
    f_scr[F_V] = xs[:, 2 * A_W:3 * A_W]
    f_scr[F_KK] = k * kkw
    f_scr[F_ASIG] = a_sig
    f_scr[F_RKK] = r * k2 * rkw
    f_scr[F_G] = lora[:, 2 * A_W:]
    f_scr[F_PC] = jnp.exp(lpc)

    C = RW_C
    W2 = 2 * A_HD
    npairs = A_HEADS // 2
    lane_lo = lax.broadcasted_iota(jnp.int32, (C, W2), 1) < A_HD
    own = (lax.broadcasted_iota(jnp.int32, (2 * C, W2), 1) < A_HD) == \
          (lax.broadcasted_iota(jnp.int32, (2 * C, W2), 0) < C)
    r4 = lax.broadcasted_iota(jnp.int32, (4 * C, 4 * C), 0)
    c4 = lax.broadcasted_iota(jnp.int32, (4 * C, 4 * C), 1)
    keep = jnp.bitwise_and(c4, C - 1) < jnp.bitwise_and(r4, C - 1) + jnp.right_shift(r4, 7)
    eye_f = (lax.broadcasted_iota(jnp.int32, (W2, W2), 0) ==
             lax.broadcasted_iota(jnp.int32, (W2, W2), 1)).astype(F32)
    zeros_ww = jnp.zeros((W2, W2), F32)
    bf = lambda z: z.astype(BF16)
    stack = lambda z: jnp.concatenate([jnp.where(lane_lo, z, 0.0), jnp.where(lane_lo, 0.0, z)], axis=0)
    spread = lambda col: jnp.where(lane_lo, col[:C], col[C:])
    chunks_per_body = 2
    probs = [(cc, p) for cc in range(chunks_per_body) for p in range(npairs)]

    def body(j, _):
        rows = [pl.ds(pl.multiple_of((j * chunks_per_body + cc) * C, C), C) for cc in range(chunks_per_body)]
        ld = lambda f, q: f_scr[f, rows[q[0]], q[1] * W2:(q[1] + 1) * W2]
        a_l, r_l, bc_l, kc_l, v_l, g_l = [], [], [], [], [], []
        for q in probs:
            kk = ld(F_KK, q)
            kk_s = stack(kk)
            inv = 1.0 / jnp.maximum(jnp.sqrt(jnp.sum(kk_s * kk_s, axis=-1, keepdims=True)), 1e-12)
            kk = kk * spread(inv)
            b = kk * ld(F_ASIG, q)
            a_l.append(stack(-kk * ld(F_PM1, q)))
            r_l.append(stack(ld(F_RT, q)))
            bc_l.append(stack(b * ld(F_DC, q)))
            kc_l.append(stack(ld(F_KC, q)))
            v_l.append(stack(ld(F_V, q)))
            ar = jnp.concatenate([a_l[-1], r_l[-1]], axis=0)
            bk = jnp.concatenate([stack(b * ld(F_IP, q)), stack(ld(F_KH, q))], axis=0)
            g_l.append(jnp.where(keep, _dot_nt(bf(ar), bf(bk)), 0.0))
        n = range(len(probs))
        x_l = [g_l[e][:W2, :W2] for e in n]
        t_l = [eye_f + x_l[e] for e in n]
        x_l = [_dot(bf(x_l[e]), bf(x_l[e])) for e in n]
        m = 2
        while m < C:
            last = 2 * m >= C
            for e in n:
                rhs = t_l[e] if last else jnp.concatenate([x_l[e], t_l[e]], axis=1)
                prod = _dot(bf(x_l[e]), bf(rhs))
                if last:
                    t_l[e] = t_l[e] + prod
                else:
                    x_l[e] = prod[:, :W2]
                    t_l[e] = t_l[e] + prod[:, W2:]
            m *= 2
        lv_l = [_dot(bf(g_l[e][:W2, W2:]), bf(v_l[e])) for e in n]
        apu_l = [_dot(bf(t_l[e]), bf(jnp.concatenate([a_l[e], lv_l[e]], axis=1))) for e in n]
        qy_l = []
        for e in n:
            low = jnp.concatenate([zeros_ww, v_l[e]], axis=1)
            qy_l.append(_dot(bf(g_l[e][W2:, :]), bf(jnp.concatenate([apu_l[e], low], axis=0))))
        mm_l = [_dot_tn(bf(apu_l[e][:, :W2]), bf(bc_l[e])) for e in n]
        n_l = []
        for e in n:
            uv = jnp.concatenate([apu_l[e][:, W2:], v_l[e]], axis=0)
            n_l.append(_dot_tn(bf(uv), bf(jnp.concatenate([bc_l[e], kc_l[e]], axis=0))))
        for e, q in enumerate(probs):
            p = q[1]
            s0 = s_scr[p]
            s0b = bf(s0)
            y_s = _dot_nt(bf(r_l[e] + qy_l[e][:, :W2]), s0b) + qy_l[e][:, W2:]
            s_scr[p] = s0 * ld(F_PC, q)[0:1, :] + _dot(s0b, bf(mm_l[e])) + n_l[e]
            mu_h = jnp.sum(y_s, axis=-1, keepdims=True) * (1.0 / A_HD)
            d = jnp.where(own, y_s - mu_h, 0.0)
            var = jnp.sum(d * d, axis=-1, keepdims=True) * (1.0 / A_HD)
            d = d * lax.rsqrt(var + A_LN_EPS)
            yn = (d[:C] + d[C:]) * ln_ref[:, p * W2:(p + 1) * W2]
            bonus = spread(jnp.sum(stack(ld(F_RKK, q)), axis=-1, keepdims=True)) * ld(F_V, q)
            o_ref[rows[q[0]], p * W2:(p + 1) * W2] = ((yn + bonus) * ld(F_G, q)).astype(o_ref.dtype)
        return 0

    lax.fori_loop(0, rt // (C * chunks_per_body), body, 0)

    @pl.when(i == pl.num_programs(1) - 1)
    def _():
        for p in range(npairs):
            so_ref[0, 2 * p] = s_scr[p, :A_HD, :A_HD]
            so_ref[0, 2 * p + 1] = s_scr[p, A_HD:, A_HD:]


def _rw_chunk(ua, mu, wl, vec, ln, bsz, t_len, rt):
    nt = t_len // rt
    st_spec = pl.BlockSpec((1, A_HEADS, A_HD, A_HD), lambda b, i: (b, 0, 0, 0))
    return pl.pallas_call(
        _rw_chunk_kernel,
        grid=(bsz, nt),
        in_specs=[pl.BlockSpec((rt, A_SHIFT_W), lambda b, i: (b * nt + i, 0)),
                  _const_spec(mu.shape), _const_spec(wl.shape), _const_spec(vec.shape),
                  _const_spec(ln.shape)],
        out_specs=[pl.BlockSpec((rt, A_W), lambda b, i: (b * nt + i, 0)), st_spec],
        out_shape=[jax.ShapeDtypeStruct((bsz * t_len, A_W), BF16),
                   jax.ShapeDtypeStruct((bsz, A_HEADS, A_HD, A_HD), F32)],
        scratch_shapes=[pltpu.VMEM((1, A_SHIFT_W), F32),
                        pltpu.VMEM((A_HEADS // 2, 2 * A_HD, 2 * A_HD), F32),
                        pltpu.VMEM((12, rt, A_W), F32)],
        compiler_params=_cparams("parallel", "arbitrary"),
        name="rw_chunk",
    )(ua, mu, wl, vec, ln)


def _rw_post_kernel(y_ref, bonus_ref, g_ref, ln_ref, seg_ref, o_ref):
    seg = seg_ref[...]
    y = y_ref[...]
    mu = _dot_sel(y, seg) * (1.0 / A_HD)
    d = y - mu
    var = _dot_sel(d * d, seg) * (1.0 / A_HD)
    yn = d * lax.rsqrt(var + A_LN_EPS) * ln_ref[...]
    o_ref[...] = ((yn + bonus_ref[...]) * g_ref[...]).astype(o_ref.dtype)


def _rw_post(y, bonus, g, ln, seg, rows):
    n = y.shape[0]
    row_spec = pl.BlockSpec((rows, A_W), lambda i: (i, 0))
    return pl.pallas_call(
        _rw_post_kernel,
        grid=(n // rows,),
        in_specs=[row_spec] * 3 + [_const_spec(ln.shape), _const_spec(seg.shape)],
        out_specs=row_spec,
        out_shape=jax.ShapeDtypeStruct((n, A_W), BF16),
        compiler_params=_cparams("parallel"),
        name="rw_post",
    )(y, bonus, g, ln, seg)


def _head_norm_lanes(x, g, eps):
    mu = jnp.mean(x, axis=-1, keepdims=True)
    d = x - mu
    var = jnp.mean(d * d, axis=-1, keepdims=True)
    return d * lax.rsqrt(var + eps) * g


def _mlstm_chunk_kernel(ub_ref, g_ref, gt_ref, cw_ref, cb_ref, biasr_ref, biasc_ref, gn_ref,
                        y_ref, co_ref, no_ref, mo_ref,
                        tail_s, c_s, n_s, m_s):
    L = ub_ref.shape[0]
    c = pl.program_id(1)

    @pl.when(c == 0)
    def _():
        tail_s[...] = jnp.zeros_like(tail_s)
        c_s[...] = jnp.zeros_like(c_s)
        n_s[...] = jnp.zeros_like(n_s)
        m_s[...] = jnp.zeros_like(m_s)

    ub = ub_ref[...]
    raw = ub[:, :2 * B_W]
    ext = jnp.concatenate([tail_s[...], raw], axis=0)
    tail_s[...] = raw[L - SUBLANES:]
    acc = cb_ref[...] + raw * cw_ref[B_CONV - 1:B_CONV, :]
    for jj in range(B_CONV - 1):
        sh = B_CONV - 1 - jj
        acc = acc + pltpu.roll(ext, sh, axis=0)[SUBLANES:] * cw_ref[jj:jj + 1, :]
    qk = _silu(acc)
    q_all = qk[:, :B_W]
    k_all = qk[:, B_W:] * (B_HD ** -0.5)

    r_i = lax.broadcasted_iota(jnp.int32, (L, L), 0)
    c_i = lax.broadcasted_iota(jnp.int32, (L, L), 1)
    causal = c_i <= r_i
    tri = causal.astype(BF16)
    tri_u = (r_i <= c_i).astype(BF16)

    gc = g_ref[...] + biasr_ref[...]
    b_cols = _dot_sel_left(tri, _log_sigmoid(gc))
    gr = gt_ref[0] + biasc_ref[...]
    b_rows = _dot_sel(_log_sigmoid(gr), tri_u)

    heads = range(B_HEADS)
    sls = [slice(hd * B_HD, (hd + 1) * B_HD) for hd in heads]
    qb = [q_all[:, sl].astype(BF16) for sl in sls]
    vb = [ub[:, 2 * B_W + hd * B_HD:2 * B_W + (hd + 1) * B_HD].astype(BF16) for hd in heads]
    qk = [_dot_nt(qb[hd], k_all[:, sls[hd]].astype(BF16)) for hd in heads]
    qc = [_dot(qb[hd], c_s[hd].astype(BF16)) for hd in heads]
    for hd in heads:
        sl = sls[hd]
        bc = _lane_col(b_cols, B_HEADS + hd)
        ic = _lane_col(gc, hd)
        br = b_rows[B_HEADS + hd:B_HEADS + hd + 1, :]
        ir = gr[hd:hd + 1, :]
        m_prev = m_s[hd:hd + 1, 0:1]
        n_prev = n_s[hd:hd + 1, :]
        dlog = jnp.where(causal, bc - br + ir, -jnp.inf)
        inter = bc + m_prev
        m_t = jnp.maximum(inter, jnp.max(dlog, axis=1, keepdims=True))
        s = qk[hd] * jnp.exp(dlog - m_t)
        iw = jnp.exp(inter - m_t)
        b_last = _row_pick(bc, L - 1)
        gi = b_last - bc + ic
        m_new = jnp.maximum(b_last + m_prev, jnp.max(gi, axis=0, keepdims=True))
        wc = jnp.exp(b_last + m_prev - m_new)
        kw = k_all[:, sl] * jnp.exp(gi - m_new)
        num = _dot(s.astype(BF16), vb[hd]) + iw * qc[hd]
        c_s[hd] = wc * c_s[hd] + _dot_tn(kw.astype(BF16), vb[hd])
        n_s[hd:hd + 1, :] = wc * n_prev + jnp.sum(kw, axis=0, keepdims=True)
        m_s[hd:hd + 1, :] = jnp.broadcast_to(m_new, (1, LANES))
        den = jnp.sum(s, axis=1, keepdims=True) + iw * jnp.sum(q_all[:, sl] * n_prev, axis=1, keepdims=True)
        hh = num / jnp.maximum(jnp.abs(den), jnp.exp(-m_t))
        hb = _sigmoid(ub[:, 3 * B_W + hd * B_HD:3 * B_W + (hd + 1) * B_HD]) * hh
        y_ref[:, sl] = _head_norm_lanes(hb, gn_ref[:, sl], GN_EPS).astype(y_ref.dtype)

    @pl.when(c == pl.num_programs(1) - 1)
    def _():
        co_ref[0] = c_s[...]
        no_ref[0] = n_s[...]
        mo_ref[0] = m_s[...]


def _mlstm_chunk(ub, gates, gates_t, cw, cb, bias_r, bias_c, gn, bsz, t_len):
    L = CHUNK
    nc = t_len // L
    row = lambda w: pl.BlockSpec((L, w), lambda b, c: (b * nc + c, 0))
    per_b = lambda shp: pl.BlockSpec((1,) + shp, lambda b, c: (b,) + (0,) * len(shp))
    return pl.pallas_call(
        _mlstm_chunk_kernel,
        grid=(bsz, nc),
        in_specs=[row(4 * B_W), row(LANES),
                  pl.BlockSpec((1, SUBLANES, L), lambda b, c: (b, 0, c)),
                  _const_spec(cw.shape), _const_spec(cb.shape), _const_spec(bias_r.shape),
                  _const_spec(bias_c.shape), _const_spec(gn.shape)],
        out_specs=[row(B_W), per_b((B_HEADS, B_HD, B_HD)), per_b((SUBLANES, B_HD)),
                   per_b((SUBLANES, LANES))],
        out_shape=[jax.ShapeDtypeStruct((bsz * t_len, B_W), BF16),
                   jax.ShapeDtypeStruct((bsz, B_HEADS, B_HD, B_HD), F32),
                   jax.ShapeDtypeStruct((bsz, SUBLANES, B_HD), F32),
                   jax.ShapeDtypeStruct((bsz, SUBLANES, LANES), F32)],
        scratch_shapes=[pltpu.VMEM((SUBLANES, 2 * B_W), F32),
                        pltpu.VMEM((B_HEADS, B_HD, B_HD), F32),
                        pltpu.VMEM((SUBLANES, B_HD), F32),
                        pltpu.VMEM((SUBLANES, LANES), F32)],
        compiler_params=_cparams("parallel", "arbitrary"),
        name="mlstm_chunk",
    )(ub, gates, gates_t, cw, cb, bias_r, bias_c, gn)


def _mlstm_step_kernel(ub_ref, g_ref, conv_ref, cw_ref, cb_ref, biasr_ref, gn_ref, c_ref, n_ref, m_ref,
                       y_ref, co_ref, no_ref, mo_ref):
    bb = ub_ref.shape[0]
    ub = ub_ref[...]
    raw = ub[:, :2 * B_W]
    conv = conv_ref[...]
    acc = cb_ref[...] + raw * cw_ref[B_CONV - 1:B_CONV, :]
    for jj in range(B_CONV - 1):
        acc = acc + conv[:, jj * 2 * B_W:(jj + 1) * 2 * B_W] * cw_ref[jj:jj + 1, :]
    qk = _silu(acc)
    q_all = qk[:, :B_W]
    k_all = qk[:, B_W:] * (B_HD ** -0.5)
    gc = g_ref[...] + biasr_ref[...]
    lf = _log_sigmoid(gc)
    m_all = m_ref[...]
    row_pad = lax.broadcasted_iota(jnp.int32, (LANES, LANES), 0)
    pad = jnp.zeros((LANES - bb, B_HD), F32)
    m_out = jnp.zeros((bb, LANES), F32)
    lane_m = lax.broadcasted_iota(jnp.int32, (bb, LANES), 1)
    for hd in range(B_HEADS):
        sl = slice(hd * B_HD, (hd + 1) * B_HD)
        q = q_all[:, sl]
        k = k_all[:, sl]
        v = ub[:, 2 * B_W + hd * B_HD:2 * B_W + (hd + 1) * B_HD]
        o = ub[:, 3 * B_W + hd * B_HD:3 * B_W + (hd + 1) * B_HD]
        ic = _lane_col(gc, hd)
        f = _lane_col(lf, B_HEADS + hd)
        m_prev = _lane_col(m_all, hd)
        n_prev = n_ref[:, sl]
        m_t = jnp.maximum(f + m_prev, ic)
        s = jnp.sum(q * k, axis=1, keepdims=True) * jnp.exp(ic - m_t)
        iw = jnp.exp(f + m_prev - m_t)
        wi = jnp.exp(ic - m_t)
        qb = q.astype(BF16)
        kw_t = jnp.concatenate([k * wi, pad], axis=0).T
        v_pad = jnp.concatenate([v, pad], axis=0)
        qc = jnp.zeros((bb, B_HD), F32)
        row_b = lax.broadcasted_iota(jnp.int32, (bb, B_HD), 0)
        for bi in range(bb):
            c_prev = c_ref[bi, hd]
            qc = qc + jnp.where(row_b == bi, _dot(qb, c_prev.astype(BF16)), 0.0)
            v_one = jnp.where(row_pad == bi, v_pad, 0.0)
            co_ref[bi, hd] = _row_pick(iw, bi) * c_prev + _dot3(kw_t, v_one, _dot)
        num = s * v + iw * qc
        den = s + iw * jnp.sum(q * n_prev, axis=1, keepdims=True)
        hh = num / jnp.maximum(jnp.abs(den), jnp.exp(-m_t))
        no_ref[:, sl] = iw * n_prev + wi * k
        m_out = jnp.where(lane_m == hd, m_t, m_out)
        hb = _sigmoid(o) * hh
        y_ref[:, sl] = _head_norm_lanes(hb, gn_ref[:, sl], GN_EPS).astype(y_ref.dtype)
    mo_ref[...] = m_out


def _mlstm_step(ub, gates, conv, cw, cb, bias_r, gn, c0, n0, m0, bb):
    bsz = ub.shape[0]
    row = lambda w: pl.BlockSpec((bb, w), lambda i: (i, 0))
    c_spec = pl.BlockSpec((bb, B_HEADS, B_HD, B_HD), lambda i: (i, 0, 0, 0))
    return pl.pallas_call(
        _mlstm_step_kernel,
        grid=(bsz // bb,),
        in_specs=[row(4 * B_W), row(LANES), row((B_CONV - 1) * 2 * B_W),
                  _const_spec(cw.shape), _const_spec(cb.shape), _const_spec(bias_r.shape),
                  _const_spec(gn.shape), c_spec, row(B_W), row(LANES)],
        out_specs=[row(B_W), c_spec, row(B_W), row(LANES)],
        out_shape=[jax.ShapeDtypeStruct((bsz, B_W), BF16),
                   jax.ShapeDtypeStruct(c0.shape, F32),
                   jax.ShapeDtypeStruct((bsz, B_W), F32),
                   jax.ShapeDtypeStruct((bsz, LANES), F32)],
        compiler_params=_cparams("parallel"),
        name="mlstm_step",
    )(ub, gates, conv, cw, cb, bias_r, gn, c0, n0, m0)


def _ret_log_gamma(hd):
    return math.log1p(-(2.0 ** (-5.0 - hd)))


def _ret_chunk_kernel(q_ref, qd_ref, k_ref, kd_ref, v_ref, g_ref, y_ref, so_ref, s_s, mask_s):
    L = q_ref.shape[0]
    c = pl.program_id(1)

    @pl.when(c == 0)
    def _():
        s_s[...] = jnp.zeros_like(s_s)
        r_i = lax.broadcasted_iota(jnp.int32, (L, L), 0)
        c_i = lax.broadcasted_iota(jnp.int32, (L, L), 1)
        diff = (r_i - c_i).astype(F32)
        for hd in range(C_HEADS):
            mask_s[hd] = jnp.where(diff >= 0, jnp.exp(jnp.maximum(diff, 0.0) * _ret_log_gamma(hd)), 0.0)

    heads = range(C_HEADS)
    kls = [slice(hd * C_DK, (hd + 1) * C_DK) for hd in heads]
    sls = [slice(hd * C_DV, (hd + 1) * C_DV) for hd in heads]
    sc = [(_dot_nt(q_ref[:, kls[hd]], k_ref[:, kls[hd]]) * mask_s[hd]).astype(BF16) for hd in heads]
    for hd in heads:
        sl = sls[hd]
        c_dec = math.exp(L * _ret_log_gamma(hd))
        s_prev = s_s[hd]
        o = _dot(sc[hd], v_ref[:, sl]) + _dot(qd_ref[:, kls[hd]], s_prev.astype(BF16))
        s_s[hd] = c_dec * s_prev + _dot_tn(kd_ref[:, kls[hd]], v_ref[:, sl])
        y_ref[:, sl] = _head_norm_lanes(o, g_ref[:, sl], GN_EPS).astype(y_ref.dtype)

    @pl.when(c == pl.num_programs(1) - 1)
    def _():
        so_ref[0] = s_s[...]


def _ret_chunk(q, qd, k, kd, v, g, bsz, t_len):
    L = CHUNK
    nc = t_len // L
    row = lambda w: pl.BlockSpec((L, w), lambda b, c: (b * nc + c, 0))
    st = pl.BlockSpec((1, C_HEADS, C_DK, C_DV), lambda b, c: (b, 0, 0, 0))
    return pl.pallas_call(
        _ret_chunk_kernel,
        grid=(bsz, nc),
        in_specs=[row(C_HEADS * C_DK)] * 4 + [row(C_OUT_W), row(C_OUT_W)],
        out_specs=[row(C_OUT_W), st],
        out_shape=[jax.ShapeDtypeStruct((bsz * t_len, C_OUT_W), BF16),
                   jax.ShapeDtypeStruct((bsz, C_HEADS, C_DK, C_DV), F32)],
        scratch_shapes=[pltpu.VMEM((C_HEADS, C_DK, C_DV), F32), pltpu.VMEM((C_HEADS, L, L), F32)],
        compiler_params=_cparams("parallel", "arbitrary"),
        name="ret_chunk",
    )(q, qd, k, kd, v, g)


def _ret_step_kernel(q_ref, k_ref, v_ref, g_ref, s_ref, y_ref, so_ref):
    bb = q_ref.shape[0]
    hd = pl.program_id(1)
    gam = 1.0 - jnp.exp2(-5.0 - jnp.full((1, 1), hd).astype(F32))
    q = q_ref[...]
    k = k_ref[...]
    v = v_ref[...]
    s = jnp.sum(q * k, axis=1, keepdims=True)
    pad = jnp.zeros((LANES - bb, C_DK), F32)
    k_t = jnp.concatenate([k, pad], axis=0).T
    q_t = jnp.concatenate([q * gam, pad], axis=0).T
    row_b = lax.broadcasted_iota(jnp.int32, (bb, C_DV), 0)
    qs = jnp.zeros((bb, C_DV), F32)
    for bi in range(bb):
        s_prev = s_ref[bi, 0]
        qs = jnp.where(row_b == bi, jnp.sum(_lane_col(q_t, bi) * s_prev, axis=0, keepdims=True), qs)
        so_ref[bi, 0] = gam * s_prev + _lane_col(k_t, bi) * v[bi:bi + 1, :]
    o = s * v + qs
    y = _head_norm_lanes(o, g_ref[...], GN_EPS)
    y_ref[...] = y.astype(y_ref.dtype)


def _ret_step(q, k, v, g, s0, bb):
    bsz = q.shape[0]
    st = pl.BlockSpec((bb, 1, C_DK, C_DV), lambda i, h: (i, h, 0, 0))
    return pl.pallas_call(
        _ret_step_kernel,
        grid=(bsz // bb, C_HEADS),
        in_specs=[pl.BlockSpec((bb, C_DK), lambda i, h: (i, h)),
                  pl.BlockSpec((bb, C_DK), lambda i, h: (i, h)),
                  pl.BlockSpec((bb, C_DV), lambda i, h: (i, h)),
                  pl.BlockSpec((bb, C_DV), lambda i, h: (i, h)), st],
        out_specs=[pl.BlockSpec((bb, C_DV), lambda i, h: (i, h)), st],
        out_shape=[jax.ShapeDtypeStruct((bsz, C_OUT_W), BF16),
                   jax.ShapeDtypeStruct(s0.shape, F32)],
        compiler_params=_cparams("parallel", "parallel"),
        name="ret_step",
    )(q, k, v, g, s0)


def _post_kernel(*refs, n_mix):
    h_ref = refs[0]
    mix_refs = refs[1:1 + n_mix]
    wout_refs = refs[1 + n_mix:1 + 2 * n_mix]
    (p_ref, g_ref, up_ref, down_ref, gate_ref, proj_ref, o_ref, h1_s, xn_s, acc_s) = refs[1 + 2 * n_mix:]
    f = pl.program_id(1)

    @pl.when(f == 0)
    def _():
        mix = _dot(mix_refs[0][...], wout_refs[0][...])
        for m_ref, w_ref in zip(mix_refs[1:], wout_refs[1:]):
            mix = mix + _dot(m_ref[...], w_ref[...])
        h1 = h_ref[...] + _rms(mix, g_ref[1:2, :])
        h1_s[...] = h1
        xn_s[...] = _rms(h1, g_ref[2:3, :]).astype(BF16)
        acc_s[...] = jnp.zeros_like(acc_s)

    hid = jnp.square(jnp.maximum(_dot(xn_s[...], up_ref[...]), 0.0))
    acc_s[...] += _dot(hid.astype(BF16), down_ref[...])

    @pl.when(f == pl.num_programs(1) - 1)
    def _():
        h2 = h1_s[...] + _rms(acc_s[...], g_ref[3:4, :])
        gate = _sigmoid(_dot(h2.astype(BF16), gate_ref[...]))
        emb = _dot(p_ref[...].astype(BF16), proj_ref[...])
        o_ref[...] = h2 + gate * emb


def _post(h, mixes, wouts, p, g, up, down, gate, proj, tm, tf):
    n = h.shape[0]
    n_mix = len(mixes)
    row = lambda w: pl.BlockSpec((tm, w), lambda i, f: (i, 0))
    return pl.pallas_call(
        functools.partial(_post_kernel, n_mix=n_mix),
        grid=(n // tm, D_FF // tf),
        in_specs=[row(D_MODEL)] + [row(m.shape[1]) for m in mixes]
        + [_const_spec(w.shape) for w in wouts]
        + [row(D_PLE), _const_spec(g.shape),
           pl.BlockSpec((D_MODEL, tf), lambda i, f: (0, f)),
           pl.BlockSpec((tf, D_MODEL), lambda i, f: (f, 0)),
           _const_spec(gate.shape), _const_spec(proj.shape)],
        out_specs=row(D_MODEL),
        out_shape=jax.ShapeDtypeStruct((n, D_MODEL), F32),
        scratch_shapes=[pltpu.VMEM((tm, D_MODEL), F32), pltpu.VMEM((tm, D_MODEL), BF16),
                        pltpu.VMEM((tm, D_MODEL), F32)],
        compiler_params=_cparams("parallel", "arbitrary"),
        name="post",
    )(h, *mixes, *wouts, p, g, up, down, gate, proj)


def _prep_weights(W):
    f = {}
    w_in = W['mix0_w_in'][0]
    f['w_a'] = w_in[:, :A_SHIFT_W].astype(BF16)
    f['w_b'] = w_in[:, A_SHIFT_W:A_SHIFT_W + 4 * B_W].astype(BF16)
    f['w_g'] = jnp.pad(w_in[:, A_SHIFT_W + 4 * B_W:], ((0, 0), (0, LANES - 2 * B_HEADS))).astype(BF16)
    wl = jnp.zeros((A_LORA_W, 3 * A_W), F32)
    wl = wl.at[:A_DECAY_LORA, :A_W].set(W['rw_w2'][0])
    wl = wl.at[A_DECAY_LORA:A_DECAY_LORA + A_AAA_LORA, A_W:2 * A_W].set(W['rw_a2'][0])
    wl = wl.at[A_DECAY_LORA + A_AAA_LORA:, 2 * A_W:].set(W['rw_g2'][0])
    f['w_lora'] = wl.astype(BF16)
    f['rw_vec'] = jnp.concatenate(
        [W['rw_w0'], W['rw_a0'], W['rw_kk'], W['rw_ka'], W['rw_rk'][0].reshape(1, A_W),
         jnp.zeros((3, A_W), F32)], axis=0)
    f['rw_mu'] = W['rw_mu']
    f['rw_ln'] = W['rw_ln']
    f['seg'] = (jnp.arange(A_W)[:, None] // A_HD == jnp.arange(A_W)[None, :] // A_HD).astype(BF16)
    f['cw'] = W['ml_conv_w'][0]
    f['cb'] = W['ml_conv_b']
    gate_bias = jnp.concatenate([W['ml_i_bias'][0], W['ml_f_bias'][0]])
    f['bias_r'] = jnp.pad(gate_bias, (0, LANES - 2 * B_HEADS)).reshape(1, LANES)
    f['bias_c'] = gate_bias.reshape(2 * B_HEADS, 1)
    f['ml_gn'] = W['ml_gn']
    w_out0 = W['mix0_w_out'][0].astype(BF16)
    f['w_out_a'] = w_out0[:A_W]
    f['w_out_b'] = w_out0[A_W:]
    w_ret = W['ret_w_in'][0]
    nq = C_HEADS * C_DK
    f['w_rq'] = w_ret[:, :nq].astype(BF16)
    f['w_rk'] = w_ret[:, nq:2 * nq].astype(BF16)
    f['w_rv'] = w_ret[:, 2 * nq:2 * nq + C_OUT_W].astype(BF16)
    f['w_rg'] = w_ret[:, 2 * nq + C_OUT_W:].astype(BF16)
    f['ret_gn'] = W['ret_gn']
    f['w_ret_out'] = W['ret_w_out'][0].astype(BF16)
    for name in ('ffn_up', 'ffn_down', 'ple_gate', 'ple_proj'):
        f[name] = W[name].astype(BF16)
    f['norm_g'] = W['norm_g']
    return f


def _rope_tables(pos):
    half = C_DK // 2
    inv = ROPE_BASE ** (-jnp.arange(half, dtype=F32) / half)
    ang = pos.astype(F32)[:, None] * inv[None, :]
    return jnp.cos(ang), jnp.sin(ang)


def _even_layer(h, p, st, f, bsz, t_len):
    n = bsz * t_len
    seq = st is None
    tm = 512 if seq else n
    ua, ub, gates = _inproj(h, f['norm_g'][0], [f['w_a'], f['w_b'], f['w_g']], [F32, F32, F32], tm)
    qk_raw = ub.reshape(bsz, t_len, 4 * B_W)[:, :, :2 * B_W]

    if seq:
        ya, s_a_new = _rw_chunk(ua, f['rw_mu'], f['w_lora'], f['rw_vec'], f['rw_ln'], bsz, t_len, rt=256)
        conv_new = qk_raw[:, -(B_CONV - 1):]
    else:
        shift, s_a, conv, c0, n0, m0 = st
        conv_new = jnp.concatenate([conv, qk_raw], axis=1)[:, -(B_CONV - 1):]
        outs = _rw_prep(ua, shift, f['rw_mu'], f['w_lora'], f['rw_vec'], f['seg'], n)
        r, w, k, v, a, b, g, bonus = outs
        as3 = lambda x: x.reshape(1, n, A_W)
        y, s_a_new = _rw_rec(as3(r), as3(w), as3(k), as3(v), as3(a), as3(b), s_a, tb=1, bb=SUBLANES)
        ya = _rw_post(y.reshape(n, A_W), bonus, g, f['rw_ln'], f['seg'], n)

    if seq:
        gates_t = gates.reshape(bsz, t_len, LANES)[:, :, :SUBLANES].transpose(0, 2, 1)
        yb, c_new, n_new, m_new = _mlstm_chunk(ub, gates, gates_t, f['cw'], f['cb'], f['bias_r'],
                                               f['bias_c'], f['ml_gn'], bsz, t_len)
        n_new = n_new[:, :B_HEADS]
        m_new = m_new[:, :B_HEADS, 0]
    else:
        m0p = jnp.pad(m0, ((0, 0), (0, LANES - B_HEADS)))
        yb, c_new, n_new, m_new = _mlstm_step(ub, gates, conv.reshape(bsz, -1), f['cw'], f['cb'],
                                              f['bias_r'], f['ml_gn'], c0, n0.reshape(bsz, B_W), m0p,
                                              bb=SUBLANES)
        n_new = n_new.reshape(bsz, B_HEADS, B_HD)
        m_new = m_new[:, :B_HEADS]

    h_next = _post(h, [ya, yb], [f['w_out_a'], f['w_out_b']], p, f['norm_g'][0], f['ffn_up'][0],
                   f['ffn_down'][0], f['ple_gate'][0], f['ple_proj'][0], tm=tm, tf=512)

    shift_new = ua.reshape(bsz, t_len, A_SHIFT_W)[:, -1]
    return h_next, (shift_new, s_a_new, conv_new, c_new, n_new, m_new)


def _odd_layer(h, p, s0, pos, f, bsz, t_len):
    n = bsz * t_len
    seq = s0 is None
    cos, sin = _rope_tables(pos)
    tm = 256 if seq else n
    ws = [f['w_rq'], f['w_rk'], f['w_rv'], f['w_rg']]
    if seq:
        q, qd, k, kd, v, g = _inproj_ret(h, f['norm_g'][1], f['ret_gn'], cos, sin, ws, tm, t_len // tm,
                                         [BF16] * 5 + [F32], decayed=True)
        y, s_new = _ret_chunk(q, qd, k, kd, v, g, bsz, t_len)
    else:
        q, k, v, g = _inproj_ret(h, f['norm_g'][1], f['ret_gn'], cos, sin, ws, tm, 1, [F32] * 4,
                                 decayed=False)
        y, s_new = _ret_step(q, k, v, g, s0, bb=SUBLANES)
    h_next = _post(h, [y], [f['w_ret_out']], p, f['norm_g'][1], f['ffn_up'][1], f['ffn_down'][1],
                   f['ple_gate'][1], f['ple_proj'][1], tm=512 if seq else n, tf=512)
    return h_next, s_new


def _trunk(x, p, pos, ev_states, od_state, f):
    bsz, t_len, _ = x.shape
    n = bsz * t_len
    h = x.reshape(n, D_MODEL)
    ev_in = None if ev_states is None else tuple(s[0] for s in ev_states)
    od_in = None if od_state is None else od_state[0]
    h, ev_new = _even_layer(h, p[0].reshape(n, D_PLE), ev_in, f, bsz, t_len)
    h, od_new = _odd_layer(h, p[1].reshape(n, D_PLE), od_in, pos, f, bsz, t_len)
    return h.reshape(bsz, t_len, D_MODEL), [s[None] for s in ev_new], od_new[None]


def kernel(x_prompt, x_sample, state_rwkv_shift, state_rwkv_S, state_mlstm_conv, state_mlstm_C,
           state_mlstm_n, state_mlstm_m, state_ret_S, p_prompt, p_sample, norm_g, ffn_up, ffn_down,
           ple_gate, ple_proj, mix0_w_in, rw_mu, rw_w0, rw_w2, rw_a0, rw_a2, rw_g2, rw_kk, rw_ka,
           rw_rk, rw_ln, ml_conv_w, ml_conv_b, ml_i_bias, ml_f_bias, ml_gn, mix0_w_out,
           ret_w_in, ret_gn, ret_w_out):
    W = dict(norm_g=norm_g, ffn_up=ffn_up, ffn_down=ffn_down, ple_gate=ple_gate, ple_proj=ple_proj,
             mix0_w_in=mix0_w_in, rw_mu=rw_mu, rw_w0=rw_w0, rw_w2=rw_w2, rw_a0=rw_a0, rw_a2=rw_a2,
             rw_g2=rw_g2, rw_kk=rw_kk, rw_ka=rw_ka, rw_rk=rw_rk, rw_ln=rw_ln, ml_conv_w=ml_conv_w,
             ml_conv_b=ml_conv_b, ml_i_bias=ml_i_bias, ml_f_bias=ml_f_bias, ml_gn=ml_gn,
             mix0_w_out=mix0_w_out, ret_w_in=ret_w_in, ret_gn=ret_gn, ret_w_out=ret_w_out)
    f = _prep_weights(W)
    bp, tp = x_prompt.shape[:2]
    ts = x_sample.shape[1]
    y_prompt, ev_p, ret_p = _trunk(x_prompt, p_prompt, jnp.arange(tp), None, None, f)
    ev_s_in = (state_rwkv_shift, state_rwkv_S, state_mlstm_conv, state_mlstm_C, state_mlstm_n,
               state_mlstm_m)
    y_sample, ev_s, ret_s = _trunk(x_sample, p_sample, PAST_LEN + jnp.arange(ts), ev_s_in, state_ret_S, f)
    return (y_prompt, y_sample, *ev_p, ret_p, *ev_s, ret_s)
```

```python
import functools
import math

import jax
import jax.numpy as jnp
from jax import lax
from jax.experimental import pallas as pl
from jax.experimental.pallas import tpu as pltpu

F32 = jnp.float32
BF16 = jnp.bfloat16

D_MODEL = 1024
D_PLE = 256
D_FF = 4 * D_MODEL
RMS_EPS = 1e-6
GN_EPS = 1e-5
CHUNK = 128

A_HEADS = 8
A_HD = 64
A_W = A_HEADS * A_HD
A_DECAY_LORA = 64
A_AAA_LORA = 64
A_GATE_LORA = 128
A_LORA_W = A_DECAY_LORA + A_AAA_LORA + A_GATE_LORA
A_SHIFT_W = 3 * A_W + A_LORA_W
A_LN_EPS = 64e-5

B_HEADS = 4
B_HD = 128
B_W = B_HEADS * B_HD
B_CONV = 4

C_HEADS = 4
C_DK = 256
C_DV = 512
C_OUT_W = C_HEADS * C_DV
ROPE_BASE = 10000.0
PAST_LEN = 16384

LANES = 128
SUBLANES = 8
VMEM_LIMIT = 56 * 1024 * 1024
POST_TF = 1024


def _cparams(*sem):
    return pltpu.CompilerParams(dimension_semantics=sem, vmem_limit_bytes=VMEM_LIMIT)


def _const_spec(shape):
    nd = len(shape)
    return pl.BlockSpec(shape, lambda *_: (0,) * nd)


def _rms(x, g):
    return x * lax.rsqrt(jnp.mean(x * x, axis=-1, keepdims=True) + RMS_EPS) * g


def _dot(a, b):
    return jnp.dot(a, b, preferred_element_type=F32)


def _bdot(a, b):
    return jnp.dot(a.astype(BF16), b.astype(BF16), preferred_element_type=F32)


def _dot_nt(a, b):
    return lax.dot_general(a, b, (((1,), (1,)), ((), ())), preferred_element_type=F32)


def _dot_tn(a, b):
    return lax.dot_general(a, b, (((0,), (0,)), ((), ())), preferred_element_type=F32)


def _hilo(x):
    hi = x.astype(BF16)
    lo = (x - hi.astype(F32)).astype(BF16)
    return hi, lo


def _dot_sel(x, sel):
    hi, lo = _hilo(x)
    return _dot(hi, sel) + _dot(lo, sel)


def _dot_sel_left(sel, x):
    hi, lo = _hilo(x)
    return _dot(sel, hi) + _dot(sel, lo)


def _dot3(a, b, dot):
    ah, al = _hilo(a)
    bh, bl = _hilo(b)
    return dot(ah, bh) + dot(ah, bl) + dot(al, bh)


def _sigmoid(x):
    return 1.0 / (1.0 + jnp.exp(-x))


def _softplus(x):
    return jnp.maximum(x, 0.0) + jnp.log1p(jnp.exp(-jnp.abs(x)))


def _log_sigmoid(x):
    return -_softplus(-x)


def _silu(x):
    return x * _sigmoid(x)


def _lane_col(x, idx):
    lane = lax.broadcasted_iota(jnp.int32, x.shape, 1)
    return jnp.sum(jnp.where(lane == idx, x, 0.0), axis=1, keepdims=True)


def _row_pick(x, idx):
    row = lax.broadcasted_iota(jnp.int32, x.shape, 0)
    return jnp.sum(jnp.where(row == idx, x, 0.0), axis=0, keepdims=True)


def _seg_matrix(n, seg):
    sh = seg.bit_length() - 1
    r = jnp.right_shift(lax.broadcasted_iota(jnp.int32, (n, n), 0), sh)
    c = jnp.right_shift(lax.broadcasted_iota(jnp.int32, (n, n), 1), sh)
    return (r == c).astype(BF16)


def _inproj_kernel(x_ref, g_ref, *refs):
    n = len(refs) // 2
    xn = _rms(x_ref[...], g_ref[0:1, :]).astype(BF16)
    for w_ref, o_ref in zip(refs[:n], refs[n:]):
        o_ref[...] = _dot(xn, w_ref[...]).astype(o_ref.dtype)


def _inproj(h, g, ws, dtypes, tm):
    n = h.shape[0]
    return pl.pallas_call(
        _inproj_kernel,
        grid=(n // tm,),
        in_specs=[pl.BlockSpec((tm, D_MODEL), lambda i: (i, 0)), _const_spec(g.shape)]
        + [_const_spec(w.shape) for w in ws],
        out_specs=[pl.BlockSpec((tm, w.shape[1]), lambda i: (i, 0)) for w in ws],
        out_shape=[jax.ShapeDtypeStruct((n, w.shape[1]), dt) for w, dt in zip(ws, dtypes)],
        compiler_params=_cparams("parallel"),
        name="inproj",
    )(h, g, *ws)


def _inproj_ret_kernel(x_ref, g_ref, gn_ref, cos_ref, sin_ref, wq_ref, wk_ref, wv_ref, wg_ref, *out_refs,
                       decayed):
    if decayed:
        q_ref, qd_ref, k_ref, kd_ref, v_ref, gate_ref = out_refs
    else:
        q_ref, k_ref, v_ref, gate_ref = out_refs
        qd_ref = kd_ref = None
    xn = _rms(x_ref[...], g_ref[0:1, :]).astype(BF16)
    cos = cos_ref[...]
    sin = sin_ref[...]
    half = C_DK // 2
    tm = x_ref.shape[0]
    pos = jnp.bitwise_and(lax.broadcasted_iota(jnp.int32, (tm, half), 0), CHUNK - 1).astype(F32)
    for w_ref, o_ref, d_ref, scale in ((wq_ref, q_ref, qd_ref, None), (wk_ref, k_ref, kd_ref, C_DK ** -0.5)):
        u = _dot(xn, w_ref[...])
        for hd in range(C_HEADS):
            x1 = u[:, hd * C_DK:hd * C_DK + half]
            x2 = u[:, hd * C_DK + half:(hd + 1) * C_DK]
            y1 = x1 * cos - x2 * sin
            y2 = x1 * sin + x2 * cos
            if scale is not None:
                y1 = y1 * scale
                y2 = y2 * scale
            o_ref[:, hd * C_DK:hd * C_DK + half] = y1.astype(o_ref.dtype)
            o_ref[:, hd * C_DK + half:(hd + 1) * C_DK] = y2.astype(o_ref.dtype)
            if decayed:
                lg = _ret_log_gamma(hd)
                dec = jnp.exp((pos + 1.0) * lg) if d_ref is qd_ref else jnp.exp((CHUNK - 1.0 - pos) * lg)
                d_ref[:, hd * C_DK:hd * C_DK + half] = (y1 * dec).astype(d_ref.dtype)
                d_ref[:, hd * C_DK + half:(hd + 1) * C_DK] = (y2 * dec).astype(d_ref.dtype)
    v_ref[...] = _dot(xn, wv_ref[...]).astype(v_ref.dtype)
    gate_ref[...] = (_silu(_dot(xn, wg_ref[...])) * gn_ref[...]).astype(gate_ref.dtype)


def _inproj_ret(h, g, gn, cos, sin, ws, tm, seq_tiles, dtypes, decayed):
    n = h.shape[0]
    if cos.shape[0] == 1:
        cs_spec = _const_spec(cos.shape)
    else:
        cs_spec = pl.BlockSpec((tm, cos.shape[1]), lambda i: (i % seq_tiles, 0))
    widths = [w.shape[1] for w in ws]
    if decayed:
        widths = [widths[0], widths[0], widths[1], widths[1], widths[2], widths[3]]
    return pl.pallas_call(
        functools.partial(_inproj_ret_kernel, decayed=decayed),
        grid=(n // tm,),
        in_specs=[pl.BlockSpec((tm, D_MODEL), lambda i: (i, 0)), _const_spec(g.shape), _const_spec(gn.shape),
                  cs_spec, cs_spec] + [_const_spec(w.shape) for w in ws],
        out_specs=[pl.BlockSpec((tm, wd), lambda i: (i, 0)) for wd in widths],
        out_shape=[jax.ShapeDtypeStruct((n, wd), dt) for wd, dt in zip(widths, dtypes)],
        compiler_params=_cparams("parallel"),
        name="inproj_ret",
    )(h, g, gn, cos, sin, *ws)


def _rw_prep_kernel(x_ref, sh_ref, mu_ref, wl_ref, vec_ref, seg_ref,
                    r_ref, w_ref, k_ref, v_ref, a_ref, b_ref, g_ref, bonus_ref, carry):
    rows = x_ref.shape[0]
    nb = sh_ref.shape[0]

    @pl.when(pl.program_id(0) == 0)
    def _():
        carry[...] = sh_ref[...]

    x = x_ref[...]
    if rows == nb:
        prev = carry[...]
    else:
        prev = jnp.concatenate([carry[...], x[:rows - nb]], axis=0)
    carry[...] = x[rows - nb:]
    xs = x + (prev - x) * mu_ref[...]
    r = xs[:, :A_W]
    k = xs[:, A_W:2 * A_W]
    v = xs[:, 2 * A_W:3 * A_W]
    lo = xs[:, 3 * A_W:]
    lane = lax.broadcasted_iota(jnp.int32, lo.shape, 1)
    act = jnp.where(lane < A_DECAY_LORA, jnp.tanh(lo),
                    jnp.where(lane < A_DECAY_LORA + A_AAA_LORA, lo, _sigmoid(lo)))
    lora = _dot(act.astype(BF16), wl_ref[...])
    w0, a0, kkw, kaw, rkw = (vec_ref[i:i + 1, :] for i in range(5))
    w_log = -_softplus(-(w0 + lora[:, :A_W])) - 0.5
    decay = jnp.exp(-jnp.exp(w_log))
    a = _sigmoid(a0 + lora[:, A_W:2 * A_W])
    g = lora[:, 2 * A_W:]
    seg = seg_ref[...]
    kk = k * kkw
    kk = kk / jnp.maximum(jnp.sqrt(_dot_sel(kk * kk, seg)), 1e-12)
    k2 = k * (1.0 + (a - 1.0) * kaw)
    bonus = _dot_sel(r * k2 * rkw, seg) * v
    r_ref[...] = r
    w_ref[...] = decay
    k_ref[...] = k2
    v_ref[...] = v
    a_ref[...] = -kk
    b_ref[...] = kk * a
    g_ref[...] = g
    bonus_ref[...] = bonus


def _rw_prep(ua_tm, shift, mu, wl, vec, seg, rows):
    n = ua_tm.shape[0]
    nb = shift.shape[0]
    row_spec = lambda w: pl.BlockSpec((rows, w), lambda i: (i, 0))
    return pl.pallas_call(
        _rw_prep_kernel,
        grid=(n // rows,),
        in_specs=[row_spec(A_SHIFT_W), _const_spec(shift.shape), _const_spec(mu.shape),
                  _const_spec(wl.shape), _const_spec(vec.shape), _const_spec(seg.shape)],
        out_specs=[row_spec(A_W)] * 8,
        out_shape=[jax.ShapeDtypeStruct((n, A_W), F32)] * 8,
        scratch_shapes=[pltpu.VMEM((nb, A_SHIFT_W), F32)],
        compiler_params=_cparams("arbitrary"),
        name="rw_prep",
    )(ua_tm, shift, mu, wl, vec, seg)


RW_HALF = 2 * LANES


def _rw_rec_kernel(r_ref, w_ref, k_ref, v_ref, a_ref, b_ref, s0_ref, y_ref, so_ref,
                   s_scr, lhs_scr, red_scr, snb_scr):
    tb, bb, _ = r_ref.shape
    nq = A_W // RW_HALF
    units = bb * nq
    j = pl.program_id(1)

    @pl.when(j == 0)
    def _():
        for bi in range(bb):
            for hd in range(A_HEADS):
                s_scr[bi, :, hd * A_HD:(hd + 1) * A_HD] = s0_ref[bi, hd]

    row = lax.broadcasted_iota(jnp.int32, (A_HD, RW_HALF), 0)
    col = lax.broadcasted_iota(jnp.int32, (A_HD, RW_HALF), 1)
    key = jnp.bitwise_and(col, A_HD - 1)
    on_diag = key == row
    off_diag = key == jnp.bitwise_and(row + 1, A_HD - 1)
    seg = _seg_matrix(RW_HALF, A_HD)
    head_of_lane = jnp.right_shift(lax.broadcasted_iota(jnp.int32, (A_HEADS, A_W), 1), 6)
    own_head = head_of_lane == lax.broadcasted_iota(jnp.int32, (A_HEADS, A_W), 0)
    first_key = jnp.bitwise_and(lax.broadcasted_iota(jnp.int32, (bb, A_W), 1), A_HD - 1) == 0

    def step(t, carry):
        rt, wt, kt, vt, at, bt = (ref[t] for ref in (r_ref, w_ref, k_ref, v_ref, a_ref, b_ref))
        v_hi = vt.astype(BF16).astype(F32)
        v_lo = vt - v_hi
        v_lo = jnp.concatenate(
            [jnp.where(first_key[:, :LANES],
                       pltpu.roll(v_lo[:, c * LANES:(c + 1) * LANES], LANES - (A_HD - 1), axis=1),
                       pltpu.roll(v_lo[:, c * LANES:(c + 1) * LANES], 1, axis=1))
             for c in range(A_W // LANES)], axis=1)
        for bi in range(bb):
            pa = (s_scr[bi] * at[bi:bi + 1]).astype(BF16)
            for q in range(nq):
                u = bi * nq + q
                sl = slice(q * RW_HALF, (q + 1) * RW_HALF)
                lhs_scr[u * A_HD:(u + 1) * A_HD, :] = pa[:, sl]
                dv = jnp.where(on_diag, v_hi[bi:bi + 1, sl], jnp.where(off_diag, v_lo[bi:bi + 1, sl], 0.0))
                lhs_scr[(units + u) * A_HD:(units + u + 1) * A_HD, :] = dv.astype(BF16)
        red_scr[...] = _dot(lhs_scr[...], seg)
        for bi in range(bb):
            for q in range(nq):
                u = bi * nq + q
                sl = slice(q * RW_HALF, (q + 1) * RW_HALF)
                sa = red_scr[u * A_HD:(u + 1) * A_HD, :]
                vb = red_scr[(units + u) * A_HD:(units + u + 1) * A_HD, :]
                sn = s_scr[bi, :, sl] * wt[bi:bi + 1, sl] + sa * bt[bi:bi + 1, sl] + vb * kt[bi:bi + 1, sl]
                s_scr[bi, :, sl] = sn
                snb_scr[bi, :, sl] = sn.astype(BF16)
        for bi in range(bb):
            r_heads = jnp.where(own_head, jnp.broadcast_to(rt[bi:bi + 1], (A_HEADS, A_W)), 0.0)
            y_ref[t, bi] = _dot_nt(r_heads.astype(BF16), snb_scr[bi])
        return carry

    lax.fori_loop(0, tb, step, 0)

    @pl.when(j == pl.num_programs(1) - 1)
    def _():
        for bi in range(bb):
            for hd in range(A_HEADS):
                so_ref[bi, hd] = s_scr[bi, :, hd * A_HD:(hd + 1) * A_HD]


def _rw_rec(r, w, k, v, a, b, s0, tb, bb):
    t_len, bsz, _ = r.shape
    seq_spec = pl.BlockSpec((tb, bb, A_W), lambda i, j: (j, i, 0))
    st_spec = pl.BlockSpec((bb, A_HEADS, A_HD, A_HD), lambda i, j: (i, 0, 0, 0))
    units = bb * (A_W // RW_HALF)
    return pl.pallas_call(
        _rw_rec_kernel,
        grid=(bsz // bb, t_len // tb),
        in_specs=[seq_spec] * 6 + [st_spec],
        out_specs=[pl.BlockSpec((tb, bb, A_HEADS, A_HD), lambda i, j: (j, i, 0, 0)), st_spec],
        out_shape=[jax.ShapeDtypeStruct((t_len, bsz, A_HEADS, A_HD), F32),
                   jax.ShapeDtypeStruct(s0.shape, F32)],
        scratch_shapes=[pltpu.VMEM((bb, A_HD, A_W), F32),
                        pltpu.VMEM((2 * units * A_HD, RW_HALF), BF16),
                        pltpu.VMEM((2 * units * A_HD, RW_HALF), F32),
                        pltpu.VMEM((bb, A_HD, A_W), BF16)],
        compiler_params=_cparams("parallel", "arbitrary"),
        name="rw_rec",
    )(r, w, k, v, a, b, s0)


RW_C = 64
(F_RT, F_PM1, F_IP, F_DC, F_KH, F_KC, F_V, F_KK, F_ASIG, F_RKK, F_G, F_PC) = range(12)


def _rw_chunk_kernel(x_ref, mu_ref, wl_ref, vec_ref, ln_ref, o_ref, so_ref, carry, s_scr, f_scr):
    rt = x_ref.shape[0]
    i = pl.program_id(1)

    @pl.when(i == 0)
    def _():
        carry[...] = jnp.zeros_like(carry)
        s_scr[...] = jnp.zeros_like(s_scr)

    x = x_ref[...]
    row = lax.broadcasted_iota(jnp.int32, x.shape, 0)
    prev = jnp.where(row == 0, carry[...], pltpu.roll(x, 1, axis=0))
    carry[...] = x[rt - 1:rt, :]
    xs = x + (prev - x) * mu_ref[...]
    r = xs[:, :A_W]
    k = xs[:, A_W:2 * A_W]
    lo = xs[:, 3 * A_W:]
    lane = lax.broadcasted_iota(jnp.int32, lo.shape, 1)
    act = jnp.where(lane < A_DECAY_LORA, jnp.tanh(lo),
                    jnp.where(lane < A_DECAY_LORA + A_AAA_LORA, lo, _sigmoid(lo)))
    lora = _dot(act.astype(BF16), wl_ref[...])
    w0, a0, kkw, kaw, rkw = (vec_ref[n:n + 1, :] for n in range(5))
    lw = -math.exp(-0.5) * _sigmoid(w0 + lora[:, :A_W])
    a_sig = _sigmoid(a0 + lora[:, A_W:2 * A_W])
    k2 = k * (1.0 + (a_sig - 1.0) * kaw)

    ri = lax.broadcasted_iota(jnp.int32, (rt, rt), 0)
    ci = lax.broadcasted_iota(jnp.int32, (rt, rt), 1)
    same = jnp.right_shift(ri, 6) == jnp.right_shift(ci, 6)
    chunk_ones = same.astype(BF16)
    chunk_tri = jnp.where(ci <= ri, chunk_ones, jnp.zeros_like(chunk_ones))
    lp = _dot_sel_left(chunk_tri, lw)
    lpc = _dot_sel_left(chunk_ones, lw)
    ip = jnp.exp(-lp)
    dc = jnp.exp(lpc - lp)
    f_scr[F_RT] = r * jnp.exp(lp)
    f_scr[F_PM1] = jnp.exp(lp - lw)
    f_scr[F_IP] = ip
    f_scr[F_DC] = dc
    f_scr[F_KH] = k2 * ip
    f_scr[F_KC] = k2 * dc
    f_scr[F_V] = xs[:, 2 * A_W:3 * A_W]
    f_scr[F_KK] = k * kkw
    f_scr[F_ASIG] = a_sig
    f_scr[F_RKK] = r * k2 * rkw
    f_scr[F_G] = lora[:, 2 * A_W:]
    f_scr[F_PC] = jnp.exp(lpc)

    C = RW_C
    W2 = 2 * A_HD
    npairs = A_HEADS // 2
    lane_lo = lax.broadcasted_iota(jnp.int32, (C, W2), 1) < A_HD
    own = (lax.broadcasted_iota(jnp.int32, (2 * C, W2), 1) < A_HD) == \
          (lax.broadcasted_iota(jnp.int32, (2 * C, W2), 0) < C)
    r4 = lax.broadcasted_iota(jnp.int32, (4 * C, 4 * C), 0)
    c4 = lax.broadcasted_iota(jnp.int32, (4 * C, 4 * C), 1)
    keep = jnp.bitwise_and(c4, C - 1) < jnp.bitwise_and(r4, C - 1) + jnp.right_shift(r4, 7)
    eye_f = (lax.broadcasted_iota(jnp.int32, (W2, W2), 0) ==
             lax.broadcasted_iota(jnp.int32, (W2, W2), 1)).astype(F32)
    zeros_ww = jnp.zeros((W2, W2), F32)
    bf = lambda z: z.astype(BF16)
    stack = lambda z: jnp.concatenate([jnp.where(lane_lo, z, 0.0), jnp.where(lane_lo, 0.0, z)], axis=0)
    spread = lambda col: jnp.where(lane_lo, col[:C], col[C:])
    chunks_per_body = 2
    probs = [(cc, p) for cc in range(chunks_per_body) for p in range(npairs)]

    def body(j, _):
        rows = [pl.ds(pl.multiple_of((j * chunks_per_body + cc) * C, C), C) for cc in range(chunks_per_body)]
        ld = lambda f, q: f_scr[f, rows[q[0]], q[1] * W2:(q[1] + 1) * W2]
        a_l, r_l, bc_l, kc_l, v_l, g_l = [], [], [], [], [], []
        for q in probs:
            kk = ld(F_KK, q)
            kk_s = stack(kk)
            inv = 1.0 / jnp.maximum(jnp.sqrt(jnp.sum(kk_s * kk_s, axis=-1, keepdims=True)), 1e-12)
            kk = kk * spread(inv)
            b = kk * ld(F_ASIG, q)
            a_l.append(stack(-kk * ld(F_PM1, q)))
            r_l.append(stack(ld(F_RT, q)))
            bc_l.append(stack(b * ld(F_DC, q)))
            kc_l.append(stack(ld(F_KC, q)))
            v_l.append(stack(ld(F_V, q)))
            ar = jnp.concatenate([a_l[-1], r_l[-1]], axis=0)
            bk = jnp.concatenate([stack(b * ld(F_IP, q)), stack(ld(F_KH, q))], axis=0)
            g_l.append(jnp.where(keep, _dot_nt(bf(ar), bf(bk)), 0.0))
        n = range(len(probs))
        x_l = [g_l[e][:W2, :W2] for e in n]
        t_l = [eye_f + x_l[e] for e in n]
        x_l = [_dot(bf(x_l[e]), bf(x_l[e])) for e in n]
        m = 2
        while m < C:
            last = 2 * m >= C
            for e in n:
                rhs = t_l[e] if last else jnp.concatenate([x_l[e], t_l[e]], axis=1)
                prod = _dot(bf(x_l[e]), bf(rhs))
                if last:
                    t_l[e] = t_l[e] + prod
                else:
                    x_l[e] = prod[:, :W2]
                    t_l[e] = t_l[e] + prod[:, W2:]
            m *= 2
        lv_l = [_dot(bf(g_l[e][:W2, W2:]), bf(v_l[e])) for e in n]
        apu_l = [_dot(bf(t_l[e]), bf(jnp.concatenate([a_l[e], lv_l[e]], axis=1))) for e in n]
        qy_l = []
        for e in n:
            low = jnp.concatenate([zeros_ww, v_l[e]], axis=1)
            qy_l.append(_dot(bf(g_l[e][W2:, :]), bf(jnp.concatenate([apu_l[e], low], axis=0))))
        mm_l = [_dot_tn(bf(apu_l[e][:, :W2]), bf(bc_l[e])) for e in n]
        n_l = []
        for e in n:
            uv = jnp.concatenate([apu_l[e][:, W2:], v_l[e]], axis=0)
            n_l.append(_dot_tn(bf(uv), bf(jnp.concatenate([bc_l[e], kc_l[e]], axis=0))))
        for e, q in enumerate(probs):
            p = q[1]
            s0 = s_scr[p]
            s0b = bf(s0)
            y_s = _dot_nt(bf(r_l[e] + qy_l[e][:, :W2]), s0b) + qy_l[e][:, W2:]
            s_scr[p] = s0 * ld(F_PC, q)[0:1, :] + _dot(s0b, bf(mm_l[e])) + n_l[e]
            mu_h = jnp.sum(y_s, axis=-1, keepdims=True) * (1.0 / A_HD)
            d = jnp.where(own, y_s - mu_h, 0.0)
            var = jnp.sum(d * d, axis=-1, keepdims=True) * (1.0 / A_HD)
            d = d * lax.rsqrt(var + A_LN_EPS)
            yn = (d[:C] + d[C:]) * ln_ref[:, p * W2:(p + 1) * W2]
            bonus = spread(jnp.sum(stack(ld(F_RKK, q)), axis=-1, keepdims=True)) * ld(F_V, q)
            o_ref[rows[q[0]], p * W2:(p + 1) * W2] = ((yn + bonus) * ld(F_G, q)).astype(o_ref.dtype)
        return 0

    lax.fori_loop(0, rt // (C * chunks_per_body), body, 0)

    @pl.when(i == pl.num_programs(1) - 1)
    def _():
        for p in range(npairs):
            so_ref[0, 2 * p] = s_scr[p, :A_HD, :A_HD]
            so_ref[0, 2 * p + 1] = s_scr[p, A_HD:, A_HD:]


def _rw_chunk(ua, mu, wl, vec, ln, bsz, t_len, rt):
    nt = t_len // rt
    st_spec = pl.BlockSpec((1, A_HEADS, A_HD, A_HD), lambda b, i: (b, 0, 0, 0))
    return pl.pallas_call(
        _rw_chunk_kernel,
        grid=(bsz, nt),
        in_specs=[pl.BlockSpec((rt, A_SHIFT_W), lambda b, i: (b * nt + i, 0)),
                  _const_spec(mu.shape), _const_spec(wl.shape), _const_spec(vec.shape),
                  _const_spec(ln.shape)],
        out_specs=[pl.BlockSpec((rt, A_W), lambda b, i: (b * nt + i, 0)), st_spec],
        out_shape=[jax.ShapeDtypeStruct((bsz * t_len, A_W), BF16),
                   jax.ShapeDtypeStruct((bsz, A_HEADS, A_HD, A_HD), F32)],
        scratch_shapes=[pltpu.VMEM((1, A_SHIFT_W), F32),
                        pltpu.VMEM((A_HEADS // 2, 2 * A_HD, 2 * A_HD), F32),
                        pltpu.VMEM((12, rt, A_W), F32)],
        compiler_params=_cparams("parallel", "arbitrary"),
        name="rw_chunk",
    )(ua, mu, wl, vec, ln)


def _rw_post_kernel(y_ref, bonus_ref, g_ref, ln_ref, seg_ref, o_ref):
    seg = seg_ref[...]
    y = y_ref[...]
    mu = _dot_sel(y, seg) * (1.0 / A_HD)
    d = y - mu
    var = _dot_sel(d * d, seg) * (1.0 / A_HD)
    yn = d * lax.rsqrt(var + A_LN_EPS) * ln_ref[...]
    o_ref[...] = ((yn + bonus_ref[...]) * g_ref[...]).astype(o_ref.dtype)


def _rw_post(y, bonus, g, ln, seg, rows):
    n = y.shape[0]
    row_spec = pl.BlockSpec((rows, A_W), lambda i: (i, 0))
    return pl.pallas_call(
        _rw_post_kernel,
        grid=(n // rows,),
        in_specs=[row_spec] * 3 + [_const_spec(ln.shape), _const_spec(seg.shape)],
        out_specs=row_spec,
        out_shape=jax.ShapeDtypeStruct((n, A_W), BF16),
        compiler_params=_cparams("parallel"),
        name="rw_post",
    )(y, bonus, g, ln, seg)


def _head_norm_lanes(x, g, eps):
    mu = jnp.mean(x, axis=-1, keepdims=True)
    d = x - mu
    var = jnp.mean(d * d, axis=-1, keepdims=True)
    return d * lax.rsqrt(var + eps) * g


def _mlstm_chunk_kernel(ub_ref, g_ref, gt_ref, cw_ref, cb_ref, biasr_ref, biasc_ref, gn_ref,
                        y_ref, co_ref, no_ref, mo_ref,
                        tail_s, c_s, n_s, m_s):
    nb, L, _ = ub_ref.shape
    c = pl.program_id(1)

    @pl.when(c == 0)
    def _():
        tail_s[...] = jnp.zeros_like(tail_s)
        c_s[...] = jnp.zeros_like(c_s)
        n_s[...] = jnp.zeros_like(n_s)
        m_s[...] = jnp.zeros_like(m_s)

    r_i = lax.broadcasted_iota(jnp.int32, (L, L), 0)
    c_i = lax.broadcasted_iota(jnp.int32, (L, L), 1)
    causal = c_i <= r_i
    tri = causal.astype(BF16)
    tri_u = (r_i <= c_i).astype(BF16)
    heads = range(B_HEADS)
    sls = [slice(hd * B_HD, (hd + 1) * B_HD) for hd in heads]

    for bi in range(nb):
        ub = ub_ref[bi]
        raw = ub[:, :2 * B_W]
        ext = jnp.concatenate([tail_s[bi], raw], axis=0)
        tail_s[bi] = raw[L - SUBLANES:]
        acc = cb_ref[...] + raw * cw_ref[B_CONV - 1:B_CONV, :]
        for jj in range(B_CONV - 1):
            sh = B_CONV - 1 - jj
            acc = acc + pltpu.roll(ext, sh, axis=0)[SUBLANES:] * cw_ref[jj:jj + 1, :]
        act = _silu(acc)
        q_all = act[:, :B_W]
        k_all = act[:, B_W:] * (B_HD ** -0.5)

        gc = g_ref[bi] + biasr_ref[...]
        b_cols = _dot_sel_left(tri, _log_sigmoid(gc))
        gr = gt_ref[bi] + biasc_ref[...]
        b_rows = _dot_sel(_log_sigmoid(gr), tri_u)

        qb = [q_all[:, sl].astype(BF16) for sl in sls]
        vb = [ub[:, 2 * B_W + hd * B_HD:2 * B_W + (hd + 1) * B_HD].astype(BF16) for hd in heads]
        qk = [_dot_nt(qb[hd], k_all[:, sls[hd]].astype(BF16)) for hd in heads]
        qc = [_dot(qb[hd], c_s[bi * B_HEADS + hd].astype(BF16)) for hd in heads]
        for hd in heads:
            sl = sls[hd]
            bc = _lane_col(b_cols, B_HEADS + hd)
            ic = _lane_col(gc, hd)
            br = b_rows[B_HEADS + hd:B_HEADS + hd + 1, :]
            ir = gr[hd:hd + 1, :]
            m_prev = m_s[bi, hd:hd + 1, 0:1]
            n_prev = n_s[bi, hd:hd + 1, :]
            dlog = jnp.where(causal, bc - br + ir, -jnp.inf)
            inter = bc + m_prev
            m_t = jnp.maximum(inter, jnp.max(dlog, axis=1, keepdims=True))
            s = qk[hd] * jnp.exp(dlog - m_t)
            iw = jnp.exp(inter - m_t)
            b_last = _row_pick(bc, L - 1)
            gi = b_last - bc + ic
            m_new = jnp.maximum(b_last + m_prev, jnp.max(gi, axis=0, keepdims=True))
            wc = jnp.exp(b_last + m_prev - m_new)
            kw = k_all[:, sl] * jnp.exp(gi - m_new)
            num = _dot(s.astype(BF16), vb[hd]) + iw * qc[hd]
            c_s[bi * B_HEADS + hd] = wc * c_s[bi * B_HEADS + hd] + _dot_tn(kw.astype(BF16), vb[hd])
            n_s[bi, hd:hd + 1, :] = wc * n_prev + jnp.sum(kw, axis=0, keepdims=True)
            m_s[bi, hd:hd + 1, :] = jnp.broadcast_to(m_new, (1, LANES))
            den = jnp.sum(s, axis=1, keepdims=True) + \
                iw * jnp.sum(q_all[:, sl] * n_prev, axis=1, keepdims=True)
            hh = num / jnp.maximum(jnp.abs(den), jnp.exp(-m_t))
            hb = _sigmoid(ub[:, 3 * B_W + hd * B_HD:3 * B_W + (hd + 1) * B_HD]) * hh
            y_ref[bi, :, sl] = _head_norm_lanes(hb, gn_ref[:, sl], GN_EPS).astype(y_ref.dtype)

    @pl.when(c == pl.num_programs(1) - 1)
    def _():
        for bi in range(nb):
            for hd in heads:
                co_ref[bi, hd] = c_s[bi * B_HEADS + hd]
        no_ref[...] = n_s[...]
        mo_ref[...] = m_s[...]


def _mlstm_chunk(ub, gates, gates_t, cw, cb, bias_r, bias_c, gn, bsz, t_len, nb):
    L = CHUNK
    seq = lambda w: pl.BlockSpec((nb, L, w), lambda b, c: (b, c, 0))
    per_b = lambda shp: pl.BlockSpec((nb,) + shp, lambda b, c: (b,) + (0,) * len(shp))
    return pl.pallas_call(
        _mlstm_chunk_kernel,
        grid=(bsz // nb, t_len // L),
        in_specs=[seq(4 * B_W), seq(LANES),
                  pl.BlockSpec((nb, SUBLANES, L), lambda b, c: (b, 0, c)),
                  _const_spec(cw.shape), _const_spec(cb.shape), _const_spec(bias_r.shape),
                  _const_spec(bias_c.shape), _const_spec(gn.shape)],
        out_specs=[seq(B_W), per_b((B_HEADS, B_HD, B_HD)), per_b((SUBLANES, B_HD)),
                   per_b((SUBLANES, LANES))],
        out_shape=[jax.ShapeDtypeStruct((bsz, t_len, B_W), BF16),
                   jax.ShapeDtypeStruct((bsz, B_HEADS, B_HD, B_HD), F32),
                   jax.ShapeDtypeStruct((bsz, SUBLANES, B_HD), F32),
                   jax.ShapeDtypeStruct((bsz, SUBLANES, LANES), F32)],
        scratch_shapes=[pltpu.VMEM((nb, SUBLANES, 2 * B_W), F32),
                        pltpu.VMEM((nb * B_HEADS, B_HD, B_HD), F32),
                        pltpu.VMEM((nb, SUBLANES, B_HD), F32),
                        pltpu.VMEM((nb, SUBLANES, LANES), F32)],
        compiler_params=_cparams("parallel", "arbitrary"),
        name="mlstm_chunk",
    )(ub, gates, gates_t, cw, cb, bias_r, bias_c, gn)


def _mlstm_step_kernel(ub_ref, g_ref, conv_ref, cw_ref, cb_ref, biasr_ref, gn_ref, c_ref, n_ref, m_ref,
                       y_ref, co_ref, no_ref, mo_ref):
    bb = ub_ref.shape[0]
    ub = ub_ref[...]
    raw = ub[:, :2 * B_W]
    conv = conv_ref[...]
    acc = cb_ref[...] + raw * cw_ref[B_CONV - 1:B_CONV, :]
    for jj in range(B_CONV - 1):
        acc = acc + conv[:, jj * 2 * B_W:(jj + 1) * 2 * B_W] * cw_ref[jj:jj + 1, :]
    qk = _silu(acc)
    q_all = qk[:, :B_W]
    k_all = qk[:, B_W:] * (B_HD ** -0.5)
    gc = g_ref[...] + biasr_ref[...]
    lf = _log_sigmoid(gc)
    m_all = m_ref[...]
    row_pad = lax.broadcasted_iota(jnp.int32, (LANES, LANES), 0)
    pad = jnp.zeros((LANES - bb, B_HD), F32)
    m_out = jnp.zeros((bb, LANES), F32)
    lane_m = lax.broadcasted_iota(jnp.int32, (bb, LANES), 1)
    for hd in range(B_HEADS):
        sl = slice(hd * B_HD, (hd + 1) * B_HD)
        q = q_all[:, sl]
        k = k_all[:, sl]
        v = ub[:, 2 * B_W + hd * B_HD:2 * B_W + (hd + 1) * B_HD]
        o = ub[:, 3 * B_W + hd * B_HD:3 * B_W + (hd + 1) * B_HD]
        ic = _lane_col(gc, hd)
        f = _lane_col(lf, B_HEADS + hd)
        m_prev = _lane_col(m_all, hd)
        n_prev = n_ref[:, sl]
        m_t = jnp.maximum(f + m_prev, ic)
        s = jnp.sum(q * k, axis=1, keepdims=True) * jnp.exp(ic - m_t)
        iw = jnp.exp(f + m_prev - m_t)
        wi = jnp.exp(ic - m_t)
        qb = q.astype(BF16)
        kw_t = jnp.concatenate([k * wi, pad], axis=0).T
        v_pad = jnp.concatenate([v, pad], axis=0)
        qc = jnp.zeros((bb, B_HD), F32)
        row_b = lax.broadcasted_iota(jnp.int32, (bb, B_HD), 0)
        for bi in range(bb):
            c_prev = c_ref[bi, hd]
            qc = qc + jnp.where(row_b == bi, _dot(qb, c_prev.astype(BF16)), 0.0)
            v_one = jnp.where(row_pad == bi, v_pad, 0.0)
            co_ref[bi, hd] = _row_pick(iw, bi) * c_prev + _dot3(kw_t, v_one, _dot)
        num = s * v + iw * qc
        den = s + iw * jnp.sum(q * n_prev, axis=1, keepdims=True)
        hh = num / jnp.maximum(jnp.abs(den), jnp.exp(-m_t))
        no_ref[:, sl] = iw * n_prev + wi * k
        m_out = jnp.where(lane_m == hd, m_t, m_out)
        hb = _sigmoid(o) * hh
        y_ref[:, sl] = _head_norm_lanes(hb, gn_ref[:, sl], GN_EPS).astype(y_ref.dtype)
    mo_ref[...] = m_out


def _mlstm_step(ub, gates, conv, cw, cb, bias_r, gn, c0, n0, m0, bb):
    bsz = ub.shape[0]
    row = lambda w: pl.BlockSpec((bb, w), lambda i: (i, 0))
    c_spec = pl.BlockSpec((bb, B_HEADS, B_HD, B_HD), lambda i: (i, 0, 0, 0))
    return pl.pallas_call(
        _mlstm_step_kernel,
        grid=(bsz // bb,),
        in_specs=[row(4 * B_W), row(LANES), row((B_CONV - 1) * 2 * B_W),
                  _const_spec(cw.shape), _const_spec(cb.shape), _const_spec(bias_r.shape),
                  _const_spec(gn.shape), c_spec, row(B_W), row(LANES)],
        out_specs=[row(B_W), c_spec, row(B_W), row(LANES)],
        out_shape=[jax.ShapeDtypeStruct((bsz, B_W), BF16),
                   jax.ShapeDtypeStruct(c0.shape, F32),
                   jax.ShapeDtypeStruct((bsz, B_W), F32),
                   jax.ShapeDtypeStruct((bsz, LANES), F32)],
        compiler_params=_cparams("parallel"),
        name="mlstm_step",
    )(ub, gates, conv, cw, cb, bias_r, gn, c0, n0, m0)


def _ret_log_gamma(hd):
    return math.log1p(-(2.0 ** (-5.0 - hd)))


def _ret_chunk_kernel(q_ref, qd_ref, k_ref, kd_ref, v_ref, g_ref, y_ref, so_ref, s_s, mask_s):
    nb, L, _ = q_ref.shape
    c = pl.program_id(1)

    @pl.when(c == 0)
    def _():
        s_s[...] = jnp.zeros_like(s_s)
        r_i = lax.broadcasted_iota(jnp.int32, (L, L), 0)
        c_i = lax.broadcasted_iota(jnp.int32, (L, L), 1)
        diff = (r_i - c_i).astype(F32)
        for hd in range(C_HEADS):
            mask_s[hd] = jnp.where(diff >= 0, jnp.exp(jnp.maximum(diff, 0.0) * _ret_log_gamma(hd)), 0.0)

    heads = range(C_HEADS)
    kls = [slice(hd * C_DK, (hd + 1) * C_DK) for hd in heads]
    sls = [slice(hd * C_DV, (hd + 1) * C_DV) for hd in heads]
    sc = [[(_dot_nt(q_ref[bi, :, kls[hd]], k_ref[bi, :, kls[hd]]) * mask_s[hd]).astype(BF16) for hd in heads]
          for bi in range(nb)]
    for bi in range(nb):
        for hd in heads:
            sl = sls[hd]
            c_dec = math.exp(L * _ret_log_gamma(hd))
            s_prev = s_s[bi * C_HEADS + hd]
            o = _dot(sc[bi][hd], v_ref[bi, :, sl]) + _dot(qd_ref[bi, :, kls[hd]], s_prev.astype(BF16))
            s_s[bi * C_HEADS + hd] = c_dec * s_prev + _dot_tn(kd_ref[bi, :, kls[hd]], v_ref[bi, :, sl])
            y_ref[bi, :, sl] = _head_norm_lanes(o, g_ref[bi, :, sl], GN_EPS).astype(y_ref.dtype)

    @pl.when(c == pl.num_programs(1) - 1)
    def _():
        for bi in range(nb):
            for hd in heads:
                so_ref[bi, hd] = s_s[bi * C_HEADS + hd]


def _ret_chunk(q, qd, k, kd, v, g, bsz, t_len, nb):
    L = CHUNK
    seq = lambda w: pl.BlockSpec((nb, L, w), lambda b, c: (b, c, 0))
    st = pl.BlockSpec((nb, C_HEADS, C_DK, C_DV), lambda b, c: (b, 0, 0, 0))
    return pl.pallas_call(
        _ret_chunk_kernel,
        grid=(bsz // nb, t_len // L),
        in_specs=[seq(C_HEADS * C_DK)] * 4 + [seq(C_OUT_W), seq(C_OUT_W)],
        out_specs=[seq(C_OUT_W), st],
        out_shape=[jax.ShapeDtypeStruct((bsz, t_len, C_OUT_W), BF16),
                   jax.ShapeDtypeStruct((bsz, C_HEADS, C_DK, C_DV), F32)],
        scratch_shapes=[pltpu.VMEM((nb * C_HEADS, C_DK, C_DV), F32), pltpu.VMEM((C_HEADS, L, L), F32)],
        compiler_params=_cparams("parallel", "arbitrary"),
        name="ret_chunk",
    )(q, qd, k, kd, v, g)


def _ret_step_kernel(q_ref, k_ref, v_ref, g_ref, s_ref, y_ref, so_ref):
    bb = q_ref.shape[0]
    hd = pl.program_id(1)
    gam = 1.0 - jnp.exp2(-5.0 - jnp.full((1, 1), hd).astype(F32))
    q = q_ref[...]
    k = k_ref[...]
    v = v_ref[...]
    s = jnp.sum(q * k, axis=1, keepdims=True)
    pad = jnp.zeros((LANES - bb, C_DK), F32)
    k_t = jnp.concatenate([k, pad], axis=0).T
    q_t = jnp.concatenate([q * gam, pad], axis=0).T
    row_b = lax.broadcasted_iota(jnp.int32, (bb, C_DV), 0)
    qs = jnp.zeros((bb, C_DV), F32)
    for bi in range(bb):
        s_prev = s_ref[bi, 0]
        qs = jnp.where(row_b == bi, jnp.sum(_lane_col(q_t, bi) * s_prev, axis=0, keepdims=True), qs)
        so_ref[bi, 0] = gam * s_prev + _lane_col(k_t, bi) * v[bi:bi + 1, :]
    o = s * v + qs
    y = _head_norm_lanes(o, g_ref[...], GN_EPS)
    y_ref[...] = y.astype(y_ref.dtype)


def _ret_step(q, k, v, g, s0, bb):
    bsz = q.shape[0]
    st = pl.BlockSpec((bb, 1, C_DK, C_DV), lambda i, h: (i, h, 0, 0))
    return pl.pallas_call(
        _ret_step_kernel,
        grid=(bsz // bb, C_HEADS),
        in_specs=[pl.BlockSpec((bb, C_DK), lambda i, h: (i, h)),
                  pl.BlockSpec((bb, C_DK), lambda i, h: (i, h)),
                  pl.BlockSpec((bb, C_DV), lambda i, h: (i, h)),
                  pl.BlockSpec((bb, C_DV), lambda i, h: (i, h)), st],
        out_specs=[pl.BlockSpec((bb, C_DV), lambda i, h: (i, h)), st],
        out_shape=[jax.ShapeDtypeStruct((bsz, C_OUT_W), BF16),
                   jax.ShapeDtypeStruct(s0.shape, F32)],
        compiler_params=_cparams("parallel", "parallel"),
        name="ret_step",
    )(q, k, v, g, s0)


def _post_kernel(*refs, n_mix):
    h_ref = refs[0]
    mix_refs = refs[1:1 + n_mix]
    wout_refs = refs[1 + n_mix:1 + 2 * n_mix]
    (p_ref, g_ref, up_ref, down_ref, gate_ref, proj_ref, o_ref, h1_s, xn_s, acc_s) = refs[1 + 2 * n_mix:]
    f = pl.program_id(1)

    @pl.when(f == 0)
    def _():
        mix = _dot(mix_refs[0][...], wout_refs[0][...])
        for m_ref, w_ref in zip(mix_refs[1:], wout_refs[1:]):
            mix = mix + _dot(m_ref[...], w_ref[...])
        h1 = h_ref[...] + _rms(mix, g_ref[1:2, :])
        h1_s[...] = h1
        xn_s[...] = _rms(h1, g_ref[2:3, :]).astype(BF16)
        acc_s[...] = jnp.zeros_like(acc_s)

    hid = jnp.square(jnp.maximum(_dot(xn_s[...], up_ref[...]), 0.0))
    acc_s[...] += _dot(hid.astype(BF16), down_ref[...])

    @pl.when(f == pl.num_programs(1) - 1)
    def _():
        h2 = h1_s[...] + _rms(acc_s[...], g_ref[3:4, :])
        gate = _sigmoid(_dot(h2.astype(BF16), gate_ref[...]))
        emb = _dot(p_ref[...].astype(BF16), proj_ref[...])
        o_ref[...] = h2 + gate * emb


def _post(h, mixes, wouts, p, g, up, down, gate, proj, li, tm, tf):
    n = h.shape[0]
    n_mix = len(mixes)
    row = lambda w: pl.BlockSpec((tm, w), lambda i, f: (i, 0))
    layer = lambda a: pl.BlockSpec((None,) + a.shape[1:], lambda i, f: (li,) + (0,) * (a.ndim - 1))
    return pl.pallas_call(
        functools.partial(_post_kernel, n_mix=n_mix),
        grid=(n // tm, D_FF // tf),
        in_specs=[row(D_MODEL)] + [row(m.shape[1]) for m in mixes]
        + [_const_spec(w.shape) for w in wouts]
        + [pl.BlockSpec((None, tm, D_PLE), lambda i, f: (li, i, 0)), layer(g),
           pl.BlockSpec((None, D_MODEL, tf), lambda i, f: (li, 0, f)),
           pl.BlockSpec((None, tf, D_MODEL), lambda i, f: (li, f, 0)),
           layer(gate), layer(proj)],
        out_specs=row(D_MODEL),
        out_shape=jax.ShapeDtypeStruct((n, D_MODEL), F32),
        scratch_shapes=[pltpu.VMEM((tm, D_MODEL), F32), pltpu.VMEM((tm, D_MODEL), BF16),
                        pltpu.VMEM((tm, D_MODEL), F32)],
        compiler_params=_cparams("parallel", "arbitrary"),
        name="post",
    )(h, *mixes, *wouts, p, g, up, down, gate, proj)


def _prep_weights(W):
    f = {}
    w_in = W['mix0_w_in'][0]
    f['w_a'] = w_in[:, :A_SHIFT_W].astype(BF16)
    f['w_b'] = w_in[:, A_SHIFT_W:A_SHIFT_W + 4 * B_W].astype(BF16)
    f['w_g'] = jnp.pad(w_in[:, A_SHIFT_W + 4 * B_W:], ((0, 0), (0, LANES - 2 * B_HEADS))).astype(BF16)
    wl = jnp.zeros((A_LORA_W, 3 * A_W), F32)
    wl = wl.at[:A_DECAY_LORA, :A_W].set(W['rw_w2'][0])
    wl = wl.at[A_DECAY_LORA:A_DECAY_LORA + A_AAA_LORA, A_W:2 * A_W].set(W['rw_a2'][0])
    wl = wl.at[A_DECAY_LORA + A_AAA_LORA:, 2 * A_W:].set(W['rw_g2'][0])
    f['w_lora'] = wl.astype(BF16)
    f['rw_vec'] = jnp.concatenate(
        [W['rw_w0'], W['rw_a0'], W['rw_kk'], W['rw_ka'], W['rw_rk'][0].reshape(1, A_W),
         jnp.zeros((3, A_W), F32)], axis=0)
    f['rw_mu'] = W['rw_mu']
    f['rw_ln'] = W['rw_ln']
    f['seg'] = (jnp.arange(A_W)[:, None] // A_HD == jnp.arange(A_W)[None, :] // A_HD).astype(BF16)
    f['cw'] = W['ml_conv_w'][0]
    f['cb'] = W['ml_conv_b']
    gate_bias = jnp.concatenate([W['ml_i_bias'][0], W['ml_f_bias'][0]])
    f['bias_r'] = jnp.pad(gate_bias, (0, LANES - 2 * B_HEADS)).reshape(1, LANES)
    f['bias_c'] = gate_bias.reshape(2 * B_HEADS, 1)
    f['ml_gn'] = W['ml_gn']
    w_out0 = W['mix0_w_out'][0].astype(BF16)
    f['w_out_a'] = w_out0[:A_W]
    f['w_out_b'] = w_out0[A_W:]
    w_ret = W['ret_w_in'][0]
    nq = C_HEADS * C_DK
    f['w_rq'] = w_ret[:, :nq].astype(BF16)
    f['w_rk'] = w_ret[:, nq:2 * nq].astype(BF16)
    f['w_rv'] = w_ret[:, 2 * nq:2 * nq + C_OUT_W].astype(BF16)
    f['w_rg'] = w_ret[:, 2 * nq + C_OUT_W:].astype(BF16)
    f['ret_gn'] = W['ret_gn']
    f['w_ret_out'] = W['ret_w_out'][0].astype(BF16)
    for name in ('ffn_up', 'ffn_down', 'ple_gate', 'ple_proj'):
        f[name] = W[name].astype(BF16)
    f['norm_g'] = W['norm_g']
    return f


def _rope_tables(pos):
    half = C_DK // 2
    inv = ROPE_BASE ** (-jnp.arange(half, dtype=F32) / half)
    ang = pos.astype(F32)[:, None] * inv[None, :]
    return jnp.cos(ang), jnp.sin(ang)


def _even_layer(h, p, st, f, bsz, t_len):
    n = bsz * t_len
    seq = st is None
    tm = 512 if seq else n
    ua, ub, gates = _inproj(h, f['norm_g'][0], [f['w_a'], f['w_b'], f['w_g']], [F32, F32, F32], tm)
    qk_raw = ub.reshape(bsz, t_len, 4 * B_W)[:, :, :2 * B_W]

    if seq:
        ya, s_a_new = _rw_chunk(ua, f['rw_mu'], f['w_lora'], f['rw_vec'], f['rw_ln'], bsz, t_len, rt=256)
        conv_new = qk_raw[:, -(B_CONV - 1):]
    else:
        shift, s_a, conv, c0, n0, m0 = st
        conv_new = jnp.concatenate([conv, qk_raw], axis=1)[:, -(B_CONV - 1):]
        outs = _rw_prep(ua, shift, f['rw_mu'], f['w_lora'], f['rw_vec'], f['seg'], n)
        r, w, k, v, a, b, g, bonus = outs
        as3 = lambda x: x.reshape(1, n, A_W)
        y, s_a_new = _rw_rec(as3(r), as3(w), as3(k), as3(v), as3(a), as3(b), s_a, tb=1, bb=SUBLANES)
        ya = _rw_post(y.reshape(n, A_W), bonus, g, f['rw_ln'], f['seg'], n)

    if seq:
        gates3 = gates.reshape(bsz, t_len, LANES)
        gates_t = gates3[:, :, :SUBLANES].transpose(0, 2, 1)
        yb, c_new, n_new, m_new = _mlstm_chunk(ub.reshape(bsz, t_len, 4 * B_W), gates3, gates_t, f['cw'],
                                               f['cb'], f['bias_r'], f['bias_c'], f['ml_gn'], bsz, t_len,
                                               nb=1)
        yb = yb.reshape(n, B_W)
        n_new = n_new[:, :B_HEADS]
        m_new = m_new[:, :B_HEADS, 0]
    else:
        m0p = jnp.pad(m0, ((0, 0), (0, LANES - B_HEADS)))
        yb, c_new, n_new, m_new = _mlstm_step(ub, gates, conv.reshape(bsz, -1), f['cw'], f['cb'],
                                              f['bias_r'], f['ml_gn'], c0, n0.reshape(bsz, B_W), m0p,
                                              bb=SUBLANES)
        n_new = n_new.reshape(bsz, B_HEADS, B_HD)
        m_new = m_new[:, :B_HEADS]

    h_next = _post(h, [ya, yb], [f['w_out_a'], f['w_out_b']], p, f['norm_g'], f['ffn_up'],
                   f['ffn_down'], f['ple_gate'], f['ple_proj'], li=0, tm=tm, tf=POST_TF)

    shift_new = ua.reshape(bsz, t_len, A_SHIFT_W)[:, -1]
    return h_next, (shift_new, s_a_new, conv_new, c_new, n_new, m_new)


def _odd_layer(h, p, s0, pos, f, bsz, t_len):
    n = bsz * t_len
    seq = s0 is None
    cos, sin = _rope_tables(pos)
    tm = 256 if seq else n
    ws = [f['w_rq'], f['w_rk'], f['w_rv'], f['w_rg']]
    if seq:
        q, qd, k, kd, v, g = _inproj_ret(h, f['norm_g'][1], f['ret_gn'], cos, sin, ws, tm, t_len // tm,
                                         [BF16] * 5 + [F32], decayed=True)
        as3 = lambda a: a.reshape(bsz, t_len, a.shape[-1])
        y, s_new = _ret_chunk(as3(q), as3(qd), as3(k), as3(kd), as3(v), as3(g), bsz, t_len, nb=1)
        y = y.reshape(n, C_OUT_W)
    else:
        q, k, v, g = _inproj_ret(h, f['norm_g'][1], f['ret_gn'], cos, sin, ws, tm, 1, [F32] * 4,
                                 decayed=False)
        y, s_new = _ret_step(q, k, v, g, s0, bb=SUBLANES)
    h_next = _post(h, [y], [f['w_ret_out']], p, f['norm_g'], f['ffn_up'], f['ffn_down'],
                   f['ple_gate'], f['ple_proj'], li=1, tm=512 if seq else n, tf=POST_TF)
    return h_next, s_new


def _trunk(x, p, pos, ev_states, od_state, f):
    bsz, t_len, _ = x.shape
    n = bsz * t_len
    h = x.reshape(n, D_MODEL)
    ev_in = None if ev_states is None else tuple(s[0] for s in ev_states)
    od_in = None if od_state is None else od_state[0]
    p = p.reshape(p.shape[0], n, D_PLE)
    h, ev_new = _even_layer(h, p, ev_in, f, bsz, t_len)
    h, od_new = _odd_layer(h, p, od_in, pos, f, bsz, t_len)
    return h.reshape(bsz, t_len, D_MODEL), [s[None] for s in ev_new], od_new[None]


def kernel(x_prompt, x_sample, state_rwkv_shift, state_rwkv_S, state_mlstm_conv, state_mlstm_C,
           state_mlstm_n, state_mlstm_m, state_ret_S, p_prompt, p_sample, norm_g, ffn_up, ffn_down,
           ple_gate, ple_proj, mix0_w_in, rw_mu, rw_w0, rw_w2, rw_a0, rw_a2, rw_g2, rw_kk, rw_ka,
           rw_rk, rw_ln, ml_conv_w, ml_conv_b, ml_i_bias, ml_f_bias, ml_gn, mix0_w_out,
           ret_w_in, ret_gn, ret_w_out):
    W = dict(norm_g=norm_g, ffn_up=ffn_up, ffn_down=ffn_down, ple_gate=ple_gate, ple_proj=ple_proj,
             mix0_w_in=mix0_w_in, rw_mu=rw_mu, rw_w0=rw_w0, rw_w2=rw_w2, rw_a0=rw_a0, rw_a2=rw_a2,
             rw_g2=rw_g2, rw_kk=rw_kk, rw_ka=rw_ka, rw_rk=rw_rk, rw_ln=rw_ln, ml_conv_w=ml_conv_w,
             ml_conv_b=ml_conv_b, ml_i_bias=ml_i_bias, ml_f_bias=ml_f_bias, ml_gn=ml_gn,
             mix0_w_out=mix0_w_out, ret_w_in=ret_w_in, ret_gn=ret_gn, ret_w_out=ret_w_out)
    f = _prep_weights(W)
    bp, tp = x_prompt.shape[:2]
    ts = x_sample.shape[1]
    y_prompt, ev_p, ret_p = _trunk(x_prompt, p_prompt, jnp.arange(tp), None, None, f)
    ev_s_in = (state_rwkv_shift, state_rwkv_S, state_mlstm_conv, state_mlstm_C, state_mlstm_n,
               state_mlstm_m)
    y_sample, ev_s, ret_s = _trunk(x_sample, p_sample, PAST_LEN + jnp.arange(ts), ev_s_in, state_ret_S, f)
    return (y_prompt, y_sample, *ev_p, ret_p, *ev_s, ret_s)
```

```python
import functools
import math

import jax
import jax.numpy as jnp
from jax import lax
from jax.experimental import pallas as pl
from jax.experimental.pallas import tpu as pltpu

F32 = jnp.float32
BF16 = jnp.bfloat16

D_MODEL = 1024
D_PLE = 256
D_FF = 4 * D_MODEL
RMS_EPS = 1e-6
GN_EPS = 1e-5
CHUNK = 128

A_HEADS = 8
A_HD = 64
A_W = A_HEADS * A_HD
A_DECAY_LORA = 64
A_AAA_LORA = 64
A_GATE_LORA = 128
A_LORA_W = A_DECAY_LORA + A_AAA_LORA + A_GATE_LORA
A_SHIFT_W = 3 * A_W + A_LORA_W
A_LN_EPS = 64e-5

B_HEADS = 4
B_HD = 128
B_W = B_HEADS * B_HD
B_CONV = 4

C_HEADS = 4
C_DK = 256
C_DV = 512
C_OUT_W = C_HEADS * C_DV
ROPE_BASE = 10000.0
PAST_LEN = 16384

LANES = 128
SUBLANES = 8
VMEM_LIMIT = 56 * 1024 * 1024
POST_TF = 1024


def _cparams(*sem):
    return pltpu.CompilerParams(dimension_semantics=sem, vmem_limit_bytes=VMEM_LIMIT)


def _const_spec(shape):
    nd = len(shape)
    return pl.BlockSpec(shape, lambda *_: (0,) * nd)


def _rms(x, g):
    return x * lax.rsqrt(jnp.mean(x * x, axis=-1, keepdims=True) + RMS_EPS) * g


def _dot(a, b):
    return jnp.dot(a, b, preferred_element_type=F32)


def _bdot(a, b):
    return jnp.dot(a.astype(BF16), b.astype(BF16), preferred_element_type=F32)


def _dot_nt(a, b):
    return lax.dot_general(a, b, (((1,), (1,)), ((), ())), preferred_element_type=F32)


def _dot_tn(a, b):
    return lax.dot_general(a, b, (((0,), (0,)), ((), ())), preferred_element_type=F32)


def _hilo(x):
    hi = x.astype(BF16)
    lo = (x - hi.astype(F32)).astype(BF16)
    return hi, lo


def _dot_sel(x, sel):
    hi, lo = _hilo(x)
    return _dot(hi, sel) + _dot(lo, sel)


def _dot_sel_left(sel, x):
    hi, lo = _hilo(x)
    return _dot(sel, hi) + _dot(sel, lo)


def _dot3(a, b, dot):
    ah, al = _hilo(a)
    bh, bl = _hilo(b)
    return dot(ah, bh) + dot(ah, bl) + dot(al, bh)


def _sigmoid(x):
    return 1.0 / (1.0 + jnp.exp(-x))


def _softplus(x):
    return jnp.maximum(x, 0.0) + jnp.log1p(jnp.exp(-jnp.abs(x)))


def _log_sigmoid(x):
    return -_softplus(-x)


def _silu(x):
    return x * _sigmoid(x)


def _lane_col(x, idx):
    lane = lax.broadcasted_iota(jnp.int32, x.shape, 1)
    return jnp.sum(jnp.where(lane == idx, x, 0.0), axis=1, keepdims=True)


def _row_pick(x, idx):
    row = lax.broadcasted_iota(jnp.int32, x.shape, 0)
    return jnp.sum(jnp.where(row == idx, x, 0.0), axis=0, keepdims=True)


def _seg_matrix(n, seg):
    sh = seg.bit_length() - 1
    r = jnp.right_shift(lax.broadcasted_iota(jnp.int32, (n, n), 0), sh)
    c = jnp.right_shift(lax.broadcasted_iota(jnp.int32, (n, n), 1), sh)
    return (r == c).astype(BF16)


def _inproj_kernel(x_ref, g_ref, *refs):
    n = len(refs) // 2
    xn = _rms(x_ref[...], g_ref[0:1, :]).astype(BF16)
    for w_ref, o_ref in zip(refs[:n], refs[n:]):
        o_ref[...] = _dot(xn, w_ref[...]).astype(o_ref.dtype)


def _inproj(h, g, ws, dtypes, tm):
    n = h.shape[0]
    return pl.pallas_call(
        _inproj_kernel,
        grid=(n // tm,),
        in_specs=[pl.BlockSpec((tm, D_MODEL), lambda i: (i, 0)), _const_spec(g.shape)]
        + [_const_spec(w.shape) for w in ws],
        out_specs=[pl.BlockSpec((tm, w.shape[1]), lambda i: (i, 0)) for w in ws],
        out_shape=[jax.ShapeDtypeStruct((n, w.shape[1]), dt) for w, dt in zip(ws, dtypes)],
        compiler_params=_cparams("parallel"),
        name="inproj",
    )(h, g, *ws)


def _inproj_ret_kernel(x_ref, g_ref, gn_ref, cos_ref, sin_ref, wq_ref, wk_ref, wv_ref, wg_ref, *out_refs,
                       decayed):
    if decayed:
        q_ref, qd_ref, k_ref, kd_ref, v_ref, gate_ref = out_refs
    else:
        q_ref, k_ref, v_ref, gate_ref = out_refs
        qd_ref = kd_ref = None
    xn = _rms(x_ref[...], g_ref[0:1, :]).astype(BF16)
    cos = cos_ref[...]
    sin = sin_ref[...]
    half = C_DK // 2
    tm = x_ref.shape[0]
    pos = jnp.bitwise_and(lax.broadcasted_iota(jnp.int32, (tm, half), 0), CHUNK - 1).astype(F32)
    for w_ref, o_ref, d_ref, scale in ((wq_ref, q_ref, qd_ref, None), (wk_ref, k_ref, kd_ref, C_DK ** -0.5)):
        u = _dot(xn, w_ref[...])
        for hd in range(C_HEADS):
            x1 = u[:, hd * C_DK:hd * C_DK + half]
            x2 = u[:, hd * C_DK + half:(hd + 1) * C_DK]
            y1 = x1 * cos - x2 * sin
            y2 = x1 * sin + x2 * cos
            if scale is not None:
                y1 = y1 * scale
                y2 = y2 * scale
            o_ref[:, hd * C_DK:hd * C_DK + half] = y1.astype(o_ref.dtype)
            o_ref[:, hd * C_DK + half:(hd + 1) * C_DK] = y2.astype(o_ref.dtype)
            if decayed:
                lg = _ret_log_gamma(hd)
                dec = jnp.exp((pos + 1.0) * lg) if d_ref is qd_ref else jnp.exp((CHUNK - 1.0 - pos) * lg)
                d_ref[:, hd * C_DK:hd * C_DK + half] = (y1 * dec).astype(d_ref.dtype)
                d_ref[:, hd * C_DK + half:(hd + 1) * C_DK] = (y2 * dec).astype(d_ref.dtype)
    v_ref[...] = _dot(xn, wv_ref[...]).astype(v_ref.dtype)
    gate_ref[...] = (_silu(_dot(xn, wg_ref[...])) * gn_ref[...]).astype(gate_ref.dtype)


def _inproj_ret(h, g, gn, cos, sin, ws, tm, seq_tiles, dtypes, decayed):
    n = h.shape[0]
    if cos.shape[0] == 1:
        cs_spec = _const_spec(cos.shape)
    else:
        cs_spec = pl.BlockSpec((tm, cos.shape[1]), lambda i: (i % seq_tiles, 0))
    widths = [w.shape[1] for w in ws]
    if decayed:
        widths = [widths[0], widths[0], widths[1], widths[1], widths[2], widths[3]]
    return pl.pallas_call(
        functools.partial(_inproj_ret_kernel, decayed=decayed),
        grid=(n // tm,),
        in_specs=[pl.BlockSpec((tm, D_MODEL), lambda i: (i, 0)), _const_spec(g.shape), _const_spec(gn.shape),
                  cs_spec, cs_spec] + [_const_spec(w.shape) for w in ws],
        out_specs=[pl.BlockSpec((tm, wd), lambda i: (i, 0)) for wd in widths],
        out_shape=[jax.ShapeDtypeStruct((n, wd), dt) for wd, dt in zip(widths, dtypes)],
        compiler_params=_cparams("parallel"),
        name="inproj_ret",
    )(h, g, gn, cos, sin, *ws)


def _rw_prep_kernel(x_ref, sh_ref, mu_ref, wl_ref, vec_ref, seg_ref,
                    r_ref, w_ref, k_ref, v_ref, a_ref, b_ref, g_ref, bonus_ref, carry):
    rows = x_ref.shape[0]
    nb = sh_ref.shape[0]

    @pl.when(pl.program_id(0) == 0)
    def _():
        carry[...] = sh_ref[...]

    x = x_ref[...]
    if rows == nb:
        prev = carry[...]
    else:
        prev = jnp.concatenate([carry[...], x[:rows - nb]], axis=0)
    carry[...] = x[rows - nb:]
    xs = x + (prev - x) * mu_ref[...]
    r = xs[:, :A_W]
    k = xs[:, A_W:2 * A_W]
    v = xs[:, 2 * A_W:3 * A_W]
    lo = xs[:, 3 * A_W:]
    lane = lax.broadcasted_iota(jnp.int32, lo.shape, 1)
    act = jnp.where(lane < A_DECAY_LORA, jnp.tanh(lo),
                    jnp.where(lane < A_DECAY_LORA + A_AAA_LORA, lo, _sigmoid(lo)))
    lora = _dot(act.astype(BF16), wl_ref[...])
    w0, a0, kkw, kaw, rkw = (vec_ref[i:i + 1, :] for i in range(5))
    w_log = -_softplus(-(w0 + lora[:, :A_W])) - 0.5
    decay = jnp.exp(-jnp.exp(w_log))
    a = _sigmoid(a0 + lora[:, A_W:2 * A_W])
    g = lora[:, 2 * A_W:]
    seg = seg_ref[...]
    kk = k * kkw
    kk = kk / jnp.maximum(jnp.sqrt(_dot_sel(kk * kk, seg)), 1e-12)
    k2 = k * (1.0 + (a - 1.0) * kaw)
    bonus = _dot_sel(r * k2 * rkw, seg) * v
    r_ref[...] = r
    w_ref[...] = decay
    k_ref[...] = k2
    v_ref[...] = v
    a_ref[...] = -kk
    b_ref[...] = kk * a
    g_ref[...] = g
    bonus_ref[...] = bonus


def _rw_prep(ua_tm, shift, mu, wl, vec, seg, rows):
    n = ua_tm.shape[0]
    nb = shift.shape[0]
    row_spec = lambda w: pl.BlockSpec((rows, w), lambda i: (i, 0))
    return pl.pallas_call(
        _rw_prep_kernel,
        grid=(n // rows,),
        in_specs=[row_spec(A_SHIFT_W), _const_spec(shift.shape), _const_spec(mu.shape),
                  _const_spec(wl.shape), _const_spec(vec.shape), _const_spec(seg.shape)],
        out_specs=[row_spec(A_W)] * 8,
        out_shape=[jax.ShapeDtypeStruct((n, A_W), F32)] * 8,
        scratch_shapes=[pltpu.VMEM((nb, A_SHIFT_W), F32)],
        compiler_params=_cparams("arbitrary"),
        name="rw_prep",
    )(ua_tm, shift, mu, wl, vec, seg)


RW_HALF = 2 * LANES


def _rw_rec_kernel(r_ref, w_ref, k_ref, v_ref, a_ref, b_ref, s0_ref, y_ref, so_ref,
                   s_scr, lhs_scr, red_scr, snb_scr):
    tb, bb, _ = r_ref.shape
    nq = A_W // RW_HALF
    units = bb * nq
    j = pl.program_id(1)

    @pl.when(j == 0)
    def _():
        for bi in range(bb):
            for hd in range(A_HEADS):
                s_scr[bi, :, hd * A_HD:(hd + 1) * A_HD] = s0_ref[bi, hd]

    row = lax.broadcasted_iota(jnp.int32, (A_HD, RW_HALF), 0)
    col = lax.broadcasted_iota(jnp.int32, (A_HD, RW_HALF), 1)
    key = jnp.bitwise_and(col, A_HD - 1)
    on_diag = key == row
    off_diag = key == jnp.bitwise_and(row + 1, A_HD - 1)
    seg = _seg_matrix(RW_HALF, A_HD)
    head_of_lane = jnp.right_shift(lax.broadcasted_iota(jnp.int32, (A_HEADS, A_W), 1), 6)
    own_head = head_of_lane == lax.broadcasted_iota(jnp.int32, (A_HEADS, A_W), 0)
    first_key = jnp.bitwise_and(lax.broadcasted_iota(jnp.int32, (bb, A_W), 1), A_HD - 1) == 0

    def step(t, carry):
        rt, wt, kt, vt, at, bt = (ref[t] for ref in (r_ref, w_ref, k_ref, v_ref, a_ref, b_ref))
        v_hi = vt.astype(BF16).astype(F32)
        v_lo = vt - v_hi
        v_lo = jnp.concatenate(
            [jnp.where(first_key[:, :LANES],
                       pltpu.roll(v_lo[:, c * LANES:(c + 1) * LANES], LANES - (A_HD - 1), axis=1),
                       pltpu.roll(v_lo[:, c * LANES:(c + 1) * LANES], 1, axis=1))
             for c in range(A_W // LANES)], axis=1)
        for bi in range(bb):
            pa = (s_scr[bi] * at[bi:bi + 1]).astype(BF16)
            for q in range(nq):
                u = bi * nq + q
                sl = slice(q * RW_HALF, (q + 1) * RW_HALF)
                lhs_scr[u * A_HD:(u + 1) * A_HD, :] = pa[:, sl]
                dv = jnp.where(on_diag, v_hi[bi:bi + 1, sl], jnp.where(off_diag, v_lo[bi:bi + 1, sl], 0.0))
                lhs_scr[(units + u) * A_HD:(units + u + 1) * A_HD, :] = dv.astype(BF16)
        red_scr[...] = _dot(lhs_scr[...], seg)
        for bi in range(bb):
            for q in range(nq):
                u = bi * nq + q
                sl = slice(q * RW_HALF, (q + 1) * RW_HALF)
                sa = red_scr[u * A_HD:(u + 1) * A_HD, :]
                vb = red_scr[(units + u) * A_HD:(units + u + 1) * A_HD, :]
                sn = s_scr[bi, :, sl] * wt[bi:bi + 1, sl] + sa * bt[bi:bi + 1, sl] + vb * kt[bi:bi + 1, sl]
                s_scr[bi, :, sl] = sn
                snb_scr[bi, :, sl] = sn.astype(BF16)
        for bi in range(bb):
            r_heads = jnp.where(own_head, jnp.broadcast_to(rt[bi:bi + 1], (A_HEADS, A_W)), 0.0)
            y_ref[t, bi] = _dot_nt(r_heads.astype(BF16), snb_scr[bi])
        return carry

    lax.fori_loop(0, tb, step, 0)

    @pl.when(j == pl.num_programs(1) - 1)
    def _():
        for bi in range(bb):
            for hd in range(A_HEADS):
                so_ref[bi, hd] = s_scr[bi, :, hd * A_HD:(hd + 1) * A_HD]


def _rw_rec(r, w, k, v, a, b, s0, tb, bb):
    t_len, bsz, _ = r.shape
    seq_spec = pl.BlockSpec((tb, bb, A_W), lambda i, j: (j, i, 0))
    st_spec = pl.BlockSpec((bb, A_HEADS, A_HD, A_HD), lambda i, j: (i, 0, 0, 0))
    units = bb * (A_W // RW_HALF)
    return pl.pallas_call(
        _rw_rec_kernel,
        grid=(bsz // bb, t_len // tb),
        in_specs=[seq_spec] * 6 + [st_spec],
        out_specs=[pl.BlockSpec((tb, bb, A_HEADS, A_HD), lambda i, j: (j, i, 0, 0)), st_spec],
        out_shape=[jax.ShapeDtypeStruct((t_len, bsz, A_HEADS, A_HD), F32),
                   jax.ShapeDtypeStruct(s0.shape, F32)],
        scratch_shapes=[pltpu.VMEM((bb, A_HD, A_W), F32),
                        pltpu.VMEM((2 * units * A_HD, RW_HALF), BF16),
                        pltpu.VMEM((2 * units * A_HD, RW_HALF), F32),
                        pltpu.VMEM((bb, A_HD, A_W), BF16)],
        compiler_params=_cparams("parallel", "arbitrary"),
        name="rw_rec",
    )(r, w, k, v, a, b, s0)


RW_C = 64
(F_RT, F_PM1, F_IP, F_DC, F_KH, F_KC, F_V, F_KK, F_ASIG, F_RKK, F_G, F_PC) = range(12)


def _rw_chunk_kernel(x_ref, mu_ref, wl_ref, vec_ref, ln_ref, o_ref, so_ref, carry, s_scr, f_scr):
    rt = x_ref.shape[0]
    i = pl.program_id(1)

    @pl.when(i == 0)
    def _():
        carry[...] = jnp.zeros_like(carry)
        s_scr[...] = jnp.zeros_like(s_scr)

    x = x_ref[...]
    row = lax.broadcasted_iota(jnp.int32, x.shape, 0)
    prev = jnp.where(row == 0, carry[...], pltpu.roll(x, 1, axis=0))
    carry[...] = x[rt - 1:rt, :]
    xs = x + (prev - x) * mu_ref[...]
    r = xs[:, :A_W]
    k = xs[:, A_W:2 * A_W]
    lo = xs[:, 3 * A_W:]
    lane = lax.broadcasted_iota(jnp.int32, lo.shape, 1)
    act = jnp.where(lane < A_DECAY_LORA, jnp.tanh(lo),
                    jnp.where(lane < A_DECAY_LORA + A_AAA_LORA, lo, _sigmoid(lo)))
    lora = _dot(act.astype(BF16), wl_ref[...])
    w0, a0, kkw, kaw, rkw = (vec_ref[n:n + 1, :] for n in range(5))
    lw = -math.exp(-0.5) * _sigmoid(w0 + lora[:, :A_W])
    a_sig = _sigmoid(a0 + lora[:, A_W:2 * A_W])
    k2 = k * (1.0 + (a_sig - 1.0) * kaw)

    ri = lax.broadcasted_iota(jnp.int32, (rt, rt), 0)
    ci = lax.broadcasted_iota(jnp.int32, (rt, rt), 1)
    same = jnp.right_shift(ri, 6) == jnp.right_shift(ci, 6)
    chunk_ones = same.astype(BF16)
    chunk_tri = jnp.where(ci <= ri, chunk_ones, jnp.zeros_like(chunk_ones))
    lp = _dot_sel_left(chunk_tri, lw)
    lpc = _dot_sel_left(chunk_ones, lw)
    ip = jnp.exp(-lp)
    dc = jnp.exp(lpc - lp)
    f_scr[F_RT] = r * jnp.exp(lp)
    f_scr[F_PM1] = jnp.exp(lp - lw)
    f_scr[F_IP] = ip
    f_scr[F_DC] = dc
    f_scr[F_KH] = k2 * ip
    f_scr[F_KC] = k2 * dc
    f_scr[F_V] = xs[:, 2 * A_W:3 * A_W]
    f_scr[F_KK] = k * kkw
    f_scr[F_ASIG] = a_sig
    f_scr[F_RKK] = r * k2 * rkw
    f_scr[F_G] = lora[:, 2 * A_W:]
    f_scr[F_PC] = jnp.exp(lpc)

    C = RW_C
    W2 = 2 * A_HD
    npairs = A_HEADS // 2
    lane_lo = lax.broadcasted_iota(jnp.int32, (C, W2), 1) < A_HD
    own = (lax.broadcasted_iota(jnp.int32, (2 * C, W2), 1) < A_HD) == \
          (lax.broadcasted_iota(jnp.int32, (2 * C, W2), 0) < C)
    r4 = lax.broadcasted_iota(jnp.int32, (4 * C, 4 * C), 0)
    c4 = lax.broadcasted_iota(jnp.int32, (4 * C, 4 * C), 1)
    keep = jnp.bitwise_and(c4, C - 1) < jnp.bitwise_and(r4, C - 1) + jnp.right_shift(r4, 7)
    eye_f = (lax.broadcasted_iota(jnp.int32, (W2, W2), 0) ==
             lax.broadcasted_iota(jnp.int32, (W2, W2), 1)).astype(F32)
    zeros_ww = jnp.zeros((W2, W2), F32)
    bf = lambda z: z.astype(BF16)
    stack = lambda z: jnp.concatenate([jnp.where(lane_lo, z, 0.0), jnp.where(lane_lo, 0.0, z)], axis=0)
    spread = lambda col: jnp.where(lane_lo, col[:C], col[C:])
    chunks_per_body = 2
    probs = [(cc, p) for cc in range(chunks_per_body) for p in range(npairs)]

    def body(j, _):
        rows = [pl.ds(pl.multiple_of((j * chunks_per_body + cc) * C, C), C) for cc in range(chunks_per_body)]
        ld = lambda f, q: f_scr[f, rows[q[0]], q[1] * W2:(q[1] + 1) * W2]
        a_l, r_l, bc_l, kc_l, v_l, g_l = [], [], [], [], [], []
        kk_sq = [jnp.sum(jnp.square(stack(ld(F_KK, q))), axis=-1, keepdims=True) for q in probs]
        bonus_l = [jnp.sum(stack(ld(F_RKK, q)), axis=-1, keepdims=True) for q in probs]
        for e, q in enumerate(probs):
            kk = ld(F_KK, q) * spread(1.0 / jnp.maximum(jnp.sqrt(kk_sq[e]), 1e-12))
            b = kk * ld(F_ASIG, q)
            a_l.append(stack(-kk * ld(F_PM1, q)))
            r_l.append(stack(ld(F_RT, q)))
            bc_l.append(stack(b * ld(F_DC, q)))
            kc_l.append(stack(ld(F_KC, q)))
            v_l.append(stack(ld(F_V, q)))
            ar = jnp.concatenate([a_l[-1], r_l[-1]], axis=0)
            bk = jnp.concatenate([stack(b * ld(F_IP, q)), stack(ld(F_KH, q))], axis=0)
            g_l.append(jnp.where(keep, _dot_nt(bf(ar), bf(bk)), 0.0))
        n = range(len(probs))
        x_l = [g_l[e][:W2, :W2] for e in n]
        t_l = [eye_f + x_l[e] for e in n]
        x_l = [_dot(bf(x_l[e]), bf(x_l[e])) for e in n]
        m = 2
        while m < C:
            last = 2 * m >= C
            for e in n:
                rhs = t_l[e] if last else jnp.concatenate([x_l[e], t_l[e]], axis=1)
                prod = _dot(bf(x_l[e]), bf(rhs))
                if last:
                    t_l[e] = t_l[e] + prod
                else:
                    x_l[e] = prod[:, :W2]
                    t_l[e] = t_l[e] + prod[:, W2:]
            m *= 2
        lv_l = [_dot(bf(g_l[e][:W2, W2:]), bf(v_l[e])) for e in n]
        apu_l = [_dot(bf(t_l[e]), bf(jnp.concatenate([a_l[e], lv_l[e]], axis=1))) for e in n]
        qy_l = []
        for e in n:
            low = jnp.concatenate([zeros_ww, v_l[e]], axis=1)
            qy_l.append(_dot(bf(g_l[e][W2:, :]), bf(jnp.concatenate([apu_l[e], low], axis=0))))
        mm_l = [_dot_tn(bf(apu_l[e][:, :W2]), bf(bc_l[e])) for e in n]
        n_l = []
        for e in n:
            uv = jnp.concatenate([apu_l[e][:, W2:], v_l[e]], axis=0)
            n_l.append(_dot_tn(bf(uv), bf(jnp.concatenate([bc_l[e], kc_l[e]], axis=0))))
        y_l = []
        for e, q in enumerate(probs):
            p = q[1]
            s0 = s_scr[p]
            s0b = bf(s0)
            y_l.append(_dot_nt(bf(r_l[e] + qy_l[e][:, :W2]), s0b) + qy_l[e][:, W2:])
            s_scr[p] = s0 * ld(F_PC, q)[0:1, :] + _dot(s0b, bf(mm_l[e])) + n_l[e]
        mu_l = [jnp.sum(y_l[e], axis=-1, keepdims=True) * (1.0 / A_HD) for e in n]
        d_l = [jnp.where(own, y_l[e] - mu_l[e], 0.0) for e in n]
        var_l = [jnp.sum(d_l[e] * d_l[e], axis=-1, keepdims=True) * (1.0 / A_HD) for e in n]
        for e, q in enumerate(probs):
            p = q[1]
            d = d_l[e] * lax.rsqrt(var_l[e] + A_LN_EPS)
            yn = (d[:C] + d[C:]) * ln_ref[:, p * W2:(p + 1) * W2]
            bonus = spread(bonus_l[e]) * ld(F_V, q)
            o_ref[rows[q[0]], p * W2:(p + 1) * W2] = ((yn + bonus) * ld(F_G, q)).astype(o_ref.dtype)
        return 0

    lax.fori_loop(0, rt // (C * chunks_per_body), body, 0)

    @pl.when(i == pl.num_programs(1) - 1)
    def _():
        for p in range(npairs):
            so_ref[0, 2 * p] = s_scr[p, :A_HD, :A_HD]
            so_ref[0, 2 * p + 1] = s_scr[p, A_HD:, A_HD:]


def _rw_chunk(ua, mu, wl, vec, ln, bsz, t_len, rt):
    nt = t_len // rt
    st_spec = pl.BlockSpec((1, A_HEADS, A_HD, A_HD), lambda b, i: (b, 0, 0, 0))
    return pl.pallas_call(
        _rw_chunk_kernel,
        grid=(bsz, nt),
        in_specs=[pl.BlockSpec((rt, A_SHIFT_W), lambda b, i: (b * nt + i, 0)),
                  _const_spec(mu.shape), _const_spec(wl.shape), _const_spec(vec.shape),
                  _const_spec(ln.shape)],
        out_specs=[pl.BlockSpec((rt, A_W), lambda b, i: (b * nt + i, 0)), st_spec],
        out_shape=[jax.ShapeDtypeStruct((bsz * t_len, A_W), BF16),
                   jax.ShapeDtypeStruct((bsz, A_HEADS, A_HD, A_HD), F32)],
        scratch_shapes=[pltpu.VMEM((1, A_SHIFT_W), F32),
                        pltpu.VMEM((A_HEADS // 2, 2 * A_HD, 2 * A_HD), F32),
                        pltpu.VMEM((12, rt, A_W), F32)],
        compiler_params=_cparams("parallel", "arbitrary"),
        name="rw_chunk",
    )(ua, mu, wl, vec, ln)


def _rw_post_kernel(y_ref, bonus_ref, g_ref, ln_ref, seg_ref, o_ref):
    seg = seg_ref[...]
    y = y_ref[...]
    mu = _dot_sel(y, seg) * (1.0 / A_HD)
    d = y - mu
    var = _dot_sel(d * d, seg) * (1.0 / A_HD)
    yn = d * lax.rsqrt(var + A_LN_EPS) * ln_ref[...]
    o_ref[...] = ((yn + bonus_ref[...]) * g_ref[...]).astype(o_ref.dtype)


def _rw_post(y, bonus, g, ln, seg, rows):
    n = y.shape[0]
    row_spec = pl.BlockSpec((rows, A_W), lambda i: (i, 0))
    return pl.pallas_call(
        _rw_post_kernel,
        grid=(n // rows,),
        in_specs=[row_spec] * 3 + [_const_spec(ln.shape), _const_spec(seg.shape)],
        out_specs=row_spec,
        out_shape=jax.ShapeDtypeStruct((n, A_W), BF16),
        compiler_params=_cparams("parallel"),
        name="rw_post",
    )(y, bonus, g, ln, seg)


def _head_norm_lanes(x, g, eps):
    mu = jnp.mean(x, axis=-1, keepdims=True)
    d = x - mu
    var = jnp.mean(d * d, axis=-1, keepdims=True)
    return d * lax.rsqrt(var + eps) * g


def _mlstm_chunk_kernel(ub_ref, g_ref, gt_ref, cw_ref, cb_ref, biasr_ref, biasc_ref, gn_ref,
                        y_ref, co_ref, no_ref, mo_ref,
                        tail_s, c_s, n_s, m_s):
    nb, L, _ = ub_ref.shape
    c = pl.program_id(1)

    @pl.when(c == 0)
    def _():
        tail_s[...] = jnp.zeros_like(tail_s)
        c_s[...] = jnp.zeros_like(c_s)
        n_s[...] = jnp.zeros_like(n_s)
        m_s[...] = jnp.zeros_like(m_s)

    r_i = lax.broadcasted_iota(jnp.int32, (L, L), 0)
    c_i = lax.broadcasted_iota(jnp.int32, (L, L), 1)
    causal = c_i <= r_i
    tri = causal.astype(BF16)
    tri_u = (r_i <= c_i).astype(BF16)
    heads = range(B_HEADS)
    sls = [slice(hd * B_HD, (hd + 1) * B_HD) for hd in heads]

    seqs = range(nb)
    ub_l, q_l, k_l, gc_l, gr_l, bcols_l, brows_l = [], [], [], [], [], [], []
    for bi in seqs:
        ub = ub_ref[bi]
        raw = ub[:, :2 * B_W]
        ext = jnp.concatenate([tail_s[bi], raw], axis=0)
        tail_s[bi] = raw[L - SUBLANES:]
        acc = cb_ref[...] + raw * cw_ref[B_CONV - 1:B_CONV, :]
        for jj in range(B_CONV - 1):
            sh = B_CONV - 1 - jj
            acc = acc + pltpu.roll(ext, sh, axis=0)[SUBLANES:] * cw_ref[jj:jj + 1, :]
        act = _silu(acc)
        ub_l.append(ub)
        q_l.append(act[:, :B_W])
        k_l.append(act[:, B_W:] * (B_HD ** -0.5))
        gc = g_ref[bi] + biasr_ref[...]
        gr = gt_ref[bi] + biasc_ref[...]
        gc_l.append(gc)
        gr_l.append(gr)
        bcols_l.append(_dot_sel_left(tri, _log_sigmoid(gc)))
        brows_l.append(_dot_sel(_log_sigmoid(gr), tri_u))
    m_all = [m_s[bi] for bi in seqs]
    n_all = [n_s[bi] for bi in seqs]

    probs = [(bi, hd) for bi in seqs for hd in heads]
    qb = [q_l[bi][:, sls[hd]].astype(BF16) for bi, hd in probs]
    vb = [ub_l[bi][:, 2 * B_W + hd * B_HD:2 * B_W + (hd + 1) * B_HD].astype(BF16) for bi, hd in probs]
    qk = [_dot_nt(qb[e], k_l[bi][:, sls[hd]].astype(BF16)) for e, (bi, hd) in enumerate(probs)]
    qc = [_dot(qb[e], c_s[bi * B_HEADS + hd].astype(BF16)) for e, (bi, hd) in enumerate(probs)]
    m_prev = [m_all[bi][hd:hd + 1, 0:1] for bi, hd in probs]
    n_prev = [n_all[bi][hd:hd + 1, :] for bi, hd in probs]
    bc = [_lane_col(bcols_l[bi], B_HEADS + hd) for bi, hd in probs]
    ic = [_lane_col(gc_l[bi], hd) for bi, hd in probs]
    dlog, inter, b_last, gi = [], [], [], []
    for e, (bi, hd) in enumerate(probs):
        br = brows_l[bi][B_HEADS + hd:B_HEADS + hd + 1, :]
        ir = gr_l[bi][hd:hd + 1, :]
        dlog.append(jnp.where(causal, bc[e] - br + ir, -jnp.inf))
        inter.append(bc[e] + m_prev[e])
        b_last.append(_row_pick(bc[e], L - 1))
        gi.append(b_last[e] - bc[e] + ic[e])
    row_max = [jnp.max(dlog[e], axis=1, keepdims=True) for e in range(len(probs))]
    gi_max = [jnp.max(gi[e], axis=0, keepdims=True) for e in range(len(probs))]
    s_l, iw, m_t, wc, kw, m_new = [], [], [], [], [], []
    for e, (bi, hd) in enumerate(probs):
        m_t.append(jnp.maximum(inter[e], row_max[e]))
        s_l.append(qk[e] * jnp.exp(dlog[e] - m_t[e]))
        iw.append(jnp.exp(inter[e] - m_t[e]))
        m_new.append(jnp.maximum(b_last[e] + m_prev[e], gi_max[e]))
        wc.append(jnp.exp(b_last[e] + m_prev[e] - m_new[e]))
        kw.append(k_l[bi][:, sls[hd]] * jnp.exp(gi[e] - m_new[e]))
    sv = [_dot(s_l[e].astype(BF16), vb[e]) for e in range(len(probs))]
    kv = [_dot_tn(kw[e].astype(BF16), vb[e]) for e in range(len(probs))]
    s_sum = [jnp.sum(s_l[e], axis=1, keepdims=True) for e in range(len(probs))]
    qn = [jnp.sum(q_l[bi][:, sls[hd]] * n_prev[e], axis=1, keepdims=True) for e, (bi, hd) in enumerate(probs)]
    k_sum = [jnp.sum(kw[e], axis=0, keepdims=True) for e in range(len(probs))]
    hb = []
    for e, (bi, hd) in enumerate(probs):
        den = s_sum[e] + iw[e] * qn[e]
        hh = (sv[e] + iw[e] * qc[e]) / jnp.maximum(jnp.abs(den), jnp.exp(-m_t[e]))
        hb.append(_sigmoid(ub_l[bi][:, 3 * B_W + hd * B_HD:3 * B_W + (hd + 1) * B_HD]) * hh)
        c_s[bi * B_HEADS + hd] = wc[e] * c_s[bi * B_HEADS + hd] + kv[e]
    mean = [jnp.sum(hb[e], axis=-1, keepdims=True) * (1.0 / B_HD) for e in range(len(probs))]
    dev = [hb[e] - mean[e] for e in range(len(probs))]
    var = [jnp.sum(dev[e] * dev[e], axis=-1, keepdims=True) * (1.0 / B_HD) for e in range(len(probs))]
    for e, (bi, hd) in enumerate(probs):
        y_ref[bi, :, sls[hd]] = (dev[e] * lax.rsqrt(var[e] + GN_EPS) * gn_ref[:, sls[hd]]).astype(y_ref.dtype)
    pad_rows = jnp.zeros((SUBLANES - B_HEADS, LANES), F32)
    for bi in seqs:
        es = [e for e, (bj, _) in enumerate(probs) if bj == bi]
        n_s[bi] = jnp.concatenate([wc[e] * n_prev[e] + k_sum[e] for e in es] + [pad_rows], axis=0)
        m_s[bi] = jnp.concatenate([jnp.broadcast_to(m_new[e], (1, LANES)) for e in es] + [pad_rows], axis=0)

    @pl.when(c == pl.num_programs(1) - 1)
    def _():
        for bi in range(nb):
            for hd in heads:
                co_ref[bi, hd] = c_s[bi * B_HEADS + hd]
        no_ref[...] = n_s[...]
        mo_ref[...] = m_s[...]


def _mlstm_chunk(ub, gates, gates_t, cw, cb, bias_r, bias_c, gn, bsz, t_len, nb):
    L = CHUNK
    seq = lambda w: pl.BlockSpec((nb, L, w), lambda b, c: (b, c, 0))
    per_b = lambda shp: pl.BlockSpec((nb,) + shp, lambda b, c: (b,) + (0,) * len(shp))
    return pl.pallas_call(
        _mlstm_chunk_kernel,
        grid=(bsz // nb, t_len // L),
        in_specs=[seq(4 * B_W), seq(LANES),
                  pl.BlockSpec((nb, SUBLANES, L), lambda b, c: (b, 0, c)),
                  _const_spec(cw.shape), _const_spec(cb.shape), _const_spec(bias_r.shape),
                  _const_spec(bias_c.shape), _const_spec(gn.shape)],
        out_specs=[seq(B_W), per_b((B_HEADS, B_HD, B_HD)), per_b((SUBLANES, B_HD)),
                   per_b((SUBLANES, LANES))],
        out_shape=[jax.ShapeDtypeStruct((bsz, t_len, B_W), BF16),
                   jax.ShapeDtypeStruct((bsz, B_HEADS, B_HD, B_HD), F32),
                   jax.ShapeDtypeStruct((bsz, SUBLANES, B_HD), F32),
                   jax.ShapeDtypeStruct((bsz, SUBLANES, LANES), F32)],
        scratch_shapes=[pltpu.VMEM((nb, SUBLANES, 2 * B_W), F32),
                        pltpu.VMEM((nb * B_HEADS, B_HD, B_HD), F32),
                        pltpu.VMEM((nb, SUBLANES, B_HD), F32),
                        pltpu.VMEM((nb, SUBLANES, LANES), F32)],
        compiler_params=_cparams("parallel", "arbitrary"),
        name="mlstm_chunk",
    )(ub, gates, gates_t, cw, cb, bias_r, bias_c, gn)


def _mlstm_step_kernel(ub_ref, g_ref, conv_ref, cw_ref, cb_ref, biasr_ref, gn_ref, c_ref, n_ref, m_ref,
                       y_ref, co_ref, no_ref, mo_ref):
    bb = ub_ref.shape[0]
    ub = ub_ref[...]
    raw = ub[:, :2 * B_W]
    conv = conv_ref[...]
    acc = cb_ref[...] + raw * cw_ref[B_CONV - 1:B_CONV, :]
    for jj in range(B_CONV - 1):
        acc = acc + conv[:, jj * 2 * B_W:(jj + 1) * 2 * B_W] * cw_ref[jj:jj + 1, :]
    qk = _silu(acc)
    q_all = qk[:, :B_W]
    k_all = qk[:, B_W:] * (B_HD ** -0.5)
    gc = g_ref[...] + biasr_ref[...]
    lf = _log_sigmoid(gc)
    m_all = m_ref[...]
    row_pad = lax.broadcasted_iota(jnp.int32, (LANES, LANES), 0)
    pad = jnp.zeros((LANES - bb, B_HD), F32)
    m_out = jnp.zeros((bb, LANES), F32)
    lane_m = lax.broadcasted_iota(jnp.int32, (bb, LANES), 1)
    for hd in range(B_HEADS):
        sl = slice(hd * B_HD, (hd + 1) * B_HD)
        q = q_all[:, sl]
        k = k_all[:, sl]
        v = ub[:, 2 * B_W + hd * B_HD:2 * B_W + (hd + 1) * B_HD]
        o = ub[:, 3 * B_W + hd * B_HD:3 * B_W + (hd + 1) * B_HD]
        ic = _lane_col(gc, hd)
        f = _lane_col(lf, B_HEADS + hd)
        m_prev = _lane_col(m_all, hd)
        n_prev = n_ref[:, sl]
        m_t = jnp.maximum(f + m_prev, ic)
        s = jnp.sum(q * k, axis=1, keepdims=True) * jnp.exp(ic - m_t)
        iw = jnp.exp(f + m_prev - m_t)
        wi = jnp.exp(ic - m_t)
        qb = q.astype(BF16)
        kw_t = jnp.concatenate([k * wi, pad], axis=0).T
        v_pad = jnp.concatenate([v, pad], axis=0)
        qc = jnp.zeros((bb, B_HD), F32)
        row_b = lax.broadcasted_iota(jnp.int32, (bb, B_HD), 0)
        for bi in range(bb):
            c_prev = c_ref[bi, hd]
            qc = qc + jnp.where(row_b == bi, _dot(qb, c_prev.astype(BF16)), 0.0)
            v_one = jnp.where(row_pad == bi, v_pad, 0.0)
            co_ref[bi, hd] = _row_pick(iw, bi) * c_prev + _dot3(kw_t, v_one, _dot)
        num = s * v + iw * qc
        den = s + iw * jnp.sum(q * n_prev, axis=1, keepdims=True)
        hh = num / jnp.maximum(jnp.abs(den), jnp.exp(-m_t))
        no_ref[:, sl] = iw * n_prev + wi * k
        m_out = jnp.where(lane_m == hd, m_t, m_out)
        hb = _sigmoid(o) * hh
        y_ref[:, sl] = _head_norm_lanes(hb, gn_ref[:, sl], GN_EPS).astype(y_ref.dtype)
    mo_ref[...] = m_out


def _mlstm_step(ub, gates, conv, cw, cb, bias_r, gn, c0, n0, m0, bb):
    bsz = ub.shape[0]
    row = lambda w: pl.BlockSpec((bb, w), lambda i: (i, 0))
    c_spec = pl.BlockSpec((bb, B_HEADS, B_HD, B_HD), lambda i: (i, 0, 0, 0))
    return pl.pallas_call(
        _mlstm_step_kernel,
        grid=(bsz // bb,),
        in_specs=[row(4 * B_W), row(LANES), row((B_CONV - 1) * 2 * B_W),
                  _const_spec(cw.shape), _const_spec(cb.shape), _const_spec(bias_r.shape),
                  _const_spec(gn.shape), c_spec, row(B_W), row(LANES)],
        out_specs=[row(B_W), c_spec, row(B_W), row(LANES)],
        out_shape=[jax.ShapeDtypeStruct((bsz, B_W), BF16),
                   jax.ShapeDtypeStruct(c0.shape, F32),
                   jax.ShapeDtypeStruct((bsz, B_W), F32),
                   jax.ShapeDtypeStruct((bsz, LANES), F32)],
        compiler_params=_cparams("parallel"),
        name="mlstm_step",
    )(ub, gates, conv, cw, cb, bias_r, gn, c0, n0, m0)


def _ret_log_gamma(hd):
    return math.log1p(-(2.0 ** (-5.0 - hd)))


def _ret_chunk_kernel(q_ref, qd_ref, k_ref, kd_ref, v_ref, g_ref, y_ref, so_ref, s_s, mask_s, o_s):
    nb, L, _ = q_ref.shape
    c = pl.program_id(1)
    last = pl.num_programs(1) - 1

    @pl.when(c == 0)
    def _():
        s_s[...] = jnp.zeros_like(s_s)
        o_s[...] = jnp.zeros_like(o_s)
        r_i = lax.broadcasted_iota(jnp.int32, (L, L), 0)
        c_i = lax.broadcasted_iota(jnp.int32, (L, L), 1)
        diff = (r_i - c_i).astype(F32)
        for hd in range(C_HEADS):
            mask_s[hd] = jnp.where(diff >= 0, jnp.exp(jnp.maximum(diff, 0.0) * _ret_log_gamma(hd)), 0.0)

    heads = range(C_HEADS)
    kls = [slice(hd * C_DK, (hd + 1) * C_DK) for hd in heads]
    sls = [slice(hd * C_DV, (hd + 1) * C_DV) for hd in heads]
    probs = [(bi, hd) for bi in range(nb) for hd in heads]
    n = range(len(probs))

    o_prev = [o_s[e] for e in n]
    mean = [jnp.sum(o_prev[e], axis=-1, keepdims=True) * (1.0 / C_DV) for e in n]
    dev = [o_prev[e] - mean[e] for e in n]
    var = [jnp.sum(dev[e] * dev[e], axis=-1, keepdims=True) * (1.0 / C_DV) for e in n]
    for e, (bi, hd) in enumerate(probs):
        sl = sls[hd]
        y_ref[bi, :, sl] = (dev[e] * lax.rsqrt(var[e] + GN_EPS) * g_ref[bi, :, sl]).astype(y_ref.dtype)

    sc = [(_dot_nt(q_ref[bi, :, kls[hd]], k_ref[bi, :, kls[hd]]) * mask_s[hd]).astype(BF16) for bi, hd in probs]
    for e, (bi, hd) in enumerate(probs):
        sl = sls[hd]
        c_dec = math.exp(L * _ret_log_gamma(hd))
        s_prev = s_s[e]
        o_s[e] = _dot(sc[e], v_ref[bi, :, sl]) + _dot(qd_ref[bi, :, kls[hd]], s_prev.astype(BF16))
        s_s[e] = c_dec * s_prev + _dot_tn(kd_ref[bi, :, kls[hd]], v_ref[bi, :, sl])

    @pl.when(c == last - 1)
    def _():
        for e, (bi, hd) in enumerate(probs):
            so_ref[bi, hd] = s_s[e]


def _ret_chunk(q, qd, k, kd, v, g, bsz, t_len, nb):
    L = CHUNK
    nc = t_len // L
    cur = lambda w: pl.BlockSpec((nb, L, w), lambda b, c: (b, jnp.minimum(c, nc - 1), 0))
    lag = lambda w: pl.BlockSpec((nb, L, w), lambda b, c: (b, jnp.maximum(c - 1, 0), 0))
    st = pl.BlockSpec((nb, C_HEADS, C_DK, C_DV), lambda b, c: (b, 0, 0, 0))
    return pl.pallas_call(
        _ret_chunk_kernel,
        grid=(bsz // nb, nc + 1),
        in_specs=[cur(C_HEADS * C_DK)] * 4 + [cur(C_OUT_W), lag(C_OUT_W)],
        out_specs=[lag(C_OUT_W), st],
        out_shape=[jax.ShapeDtypeStruct((bsz, t_len, C_OUT_W), BF16),
                   jax.ShapeDtypeStruct((bsz, C_HEADS, C_DK, C_DV), F32)],
        scratch_shapes=[pltpu.VMEM((nb * C_HEADS, C_DK, C_DV), F32), pltpu.VMEM((C_HEADS, L, L), F32),
                        pltpu.VMEM((nb * C_HEADS, L, C_DV), F32)],
        compiler_params=_cparams("parallel", "arbitrary"),
        name="ret_chunk",
    )(q, qd, k, kd, v, g)


def _ret_step_kernel(q_ref, k_ref, v_ref, g_ref, s_ref, y_ref, so_ref):
    bb = q_ref.shape[0]
    hd = pl.program_id(1)
    gam = 1.0 - jnp.exp2(-5.0 - jnp.full((1, 1), hd).astype(F32))
    q = q_ref[...]
    k = k_ref[...]
    v = v_ref[...]
    s = jnp.sum(q * k, axis=1, keepdims=True)
    pad = jnp.zeros((LANES - bb, C_DK), F32)
    k_t = jnp.concatenate([k, pad], axis=0).T
    q_t = jnp.concatenate([q * gam, pad], axis=0).T
    row_b = lax.broadcasted_iota(jnp.int32, (bb, C_DV), 0)
    qs = jnp.zeros((bb, C_DV), F32)
    for bi in range(bb):
        s_prev = s_ref[bi, 0]
        qs = jnp.where(row_b == bi, jnp.sum(_lane_col(q_t, bi) * s_prev, axis=0, keepdims=True), qs)
        so_ref[bi, 0] = gam * s_prev + _lane_col(k_t, bi) * v[bi:bi + 1, :]
    o = s * v + qs
    y = _head_norm_lanes(o, g_ref[...], GN_EPS)
    y_ref[...] = y.astype(y_ref.dtype)


def _ret_step(q, k, v, g, s0, bb):
    bsz = q.shape[0]
    st = pl.BlockSpec((bb, 1, C_DK, C_DV), lambda i, h: (i, h, 0, 0))
    return pl.pallas_call(
        _ret_step_kernel,
        grid=(bsz // bb, C_HEADS),
        in_specs=[pl.BlockSpec((bb, C_DK), lambda i, h: (i, h)),
                  pl.BlockSpec((bb, C_DK), lambda i, h: (i, h)),
                  pl.BlockSpec((bb, C_DV), lambda i, h: (i, h)),
                  pl.BlockSpec((bb, C_DV), lambda i, h: (i, h)), st],
        out_specs=[pl.BlockSpec((bb, C_DV), lambda i, h: (i, h)), st],
        out_shape=[jax.ShapeDtypeStruct((bsz, C_OUT_W), BF16),
                   jax.ShapeDtypeStruct(s0.shape, F32)],
        compiler_params=_cparams("parallel", "parallel"),
        name="ret_step",
    )(q, k, v, g, s0)


def _post_kernel(*refs, n_mix):
    h_ref = refs[0]
    mix_refs = refs[1:1 + n_mix]
    wout_refs = refs[1 + n_mix:1 + 2 * n_mix]
    (p_ref, g_ref, up_ref, down_ref, gate_ref, proj_ref, o_ref, h1_s, xn_s, acc_s) = refs[1 + 2 * n_mix:]
    f = pl.program_id(1)

    @pl.when(f == 0)
    def _():
        mix = _dot(mix_refs[0][...], wout_refs[0][...])
        for m_ref, w_ref in zip(mix_refs[1:], wout_refs[1:]):
            mix = mix + _dot(m_ref[...], w_ref[...])
        h1 = h_ref[...] + _rms(mix, g_ref[1:2, :])
        h1_s[...] = h1
        xn_s[...] = _rms(h1, g_ref[2:3, :]).astype(BF16)
        acc_s[...] = jnp.zeros_like(acc_s)

    hid = jnp.square(jnp.maximum(_dot(xn_s[...], up_ref[...]), 0.0))
    acc_s[...] += _dot(hid.astype(BF16), down_ref[...])

    @pl.when(f == pl.num_programs(1) - 1)
    def _():
        h2 = h1_s[...] + _rms(acc_s[...], g_ref[3:4, :])
        gate = _sigmoid(_dot(h2.astype(BF16), gate_ref[...]))
        emb = _dot(p_ref[...].astype(BF16), proj_ref[...])
        o_ref[...] = h2 + gate * emb


def _post(h, mixes, wouts, p, g, up, down, gate, proj, li, tm, tf):
    n = h.shape[0]
    n_mix = len(mixes)
    row = lambda w: pl.BlockSpec((tm, w), lambda i, f: (i, 0))
    layer = lambda a: pl.BlockSpec((None,) + a.shape[1:], lambda i, f: (li,) + (0,) * (a.ndim - 1))
    return pl.pallas_call(
        functools.partial(_post_kernel, n_mix=n_mix),
        grid=(n // tm, D_FF // tf),
        in_specs=[row(D_MODEL)] + [row(m.shape[1]) for m in mixes]
        + [_const_spec(w.shape) for w in wouts]
        + [pl.BlockSpec((None, tm, D_PLE), lambda i, f: (li, i, 0)), layer(g),
           pl.BlockSpec((None, D_MODEL, tf), lambda i, f: (li, 0, f)),
           pl.BlockSpec((None, tf, D_MODEL), lambda i, f: (li, f, 0)),
           layer(gate), layer(proj)],
        out_specs=row(D_MODEL),
        out_shape=jax.ShapeDtypeStruct((n, D_MODEL), F32),
        scratch_shapes=[pltpu.VMEM((tm, D_MODEL), F32), pltpu.VMEM((tm, D_MODEL), BF16),
                        pltpu.VMEM((tm, D_MODEL), F32)],
        compiler_params=_cparams("parallel", "arbitrary"),
        name="post",
    )(h, *mixes, *wouts, p, g, up, down, gate, proj)


def _prep_weights(W):
    f = {}
    w_in = W['mix0_w_in'][0]
    f['w_a'] = w_in[:, :A_SHIFT_W].astype(BF16)
    f['w_b'] = w_in[:, A_SHIFT_W:A_SHIFT_W + 4 * B_W].astype(BF16)
    f['w_g'] = jnp.pad(w_in[:, A_SHIFT_W + 4 * B_W:], ((0, 0), (0, LANES - 2 * B_HEADS))).astype(BF16)
    wl = jnp.zeros((A_LORA_W, 3 * A_W), F32)
    wl = wl.at[:A_DECAY_LORA, :A_W].set(W['rw_w2'][0])
    wl = wl.at[A_DECAY_LORA:A_DECAY_LORA + A_AAA_LORA, A_W:2 * A_W].set(W['rw_a2'][0])
    wl = wl.at[A_DECAY_LORA + A_AAA_LORA:, 2 * A_W:].set(W['rw_g2'][0])
    f['w_lora'] = wl.astype(BF16)
    f['rw_vec'] = jnp.concatenate(
        [W['rw_w0'], W['rw_a0'], W['rw_kk'], W['rw_ka'], W['rw_rk'][0].reshape(1, A_W),
         jnp.zeros((3, A_W), F32)], axis=0)
    f['rw_mu'] = W['rw_mu']
    f['rw_ln'] = W['rw_ln']
    f['seg'] = (jnp.arange(A_W)[:, None] // A_HD == jnp.arange(A_W)[None, :] // A_HD).astype(BF16)
    f['cw'] = W['ml_conv_w'][0]
    f['cb'] = W['ml_conv_b']
    gate_bias = jnp.concatenate([W['ml_i_bias'][0], W['ml_f_bias'][0]])
    f['bias_r'] = jnp.pad(gate_bias, (0, LANES - 2 * B_HEADS)).reshape(1, LANES)
    f['bias_c'] = gate_bias.reshape(2 * B_HEADS, 1)
    f['ml_gn'] = W['ml_gn']
    w_out0 = W['mix0_w_out'][0].astype(BF16)
    f['w_out_a'] = w_out0[:A_W]
    f['w_out_b'] = w_out0[A_W:]
    w_ret = W['ret_w_in'][0]
    nq = C_HEADS * C_DK
    f['w_rq'] = w_ret[:, :nq].astype(BF16)
    f['w_rk'] = w_ret[:, nq:2 * nq].astype(BF16)
    f['w_rv'] = w_ret[:, 2 * nq:2 * nq + C_OUT_W].astype(BF16)
    f['w_rg'] = w_ret[:, 2 * nq + C_OUT_W:].astype(BF16)
    f['ret_gn'] = W['ret_gn']
    f['w_ret_out'] = W['ret_w_out'][0].astype(BF16)
    for name in ('ffn_up', 'ffn_down', 'ple_gate', 'ple_proj'):
        f[name] = W[name].astype(BF16)
    f['norm_g'] = W['norm_g']
    return f


def _rope_tables(pos):
    half = C_DK // 2
    inv = ROPE_BASE ** (-jnp.arange(half, dtype=F32) / half)
    ang = pos.astype(F32)[:, None] * inv[None, :]
    return jnp.cos(ang), jnp.sin(ang)


def _even_layer(h, p, st, f, bsz, t_len):
    n = bsz * t_len
    seq = st is None
    tm = 512 if seq else n
    ua, ub, gates = _inproj(h, f['norm_g'][0], [f['w_a'], f['w_b'], f['w_g']], [F32, F32, F32], tm)
    qk_raw = ub.reshape(bsz, t_len, 4 * B_W)[:, :, :2 * B_W]

    if seq:
        ya, s_a_new = _rw_chunk(ua, f['rw_mu'], f['w_lora'], f['rw_vec'], f['rw_ln'], bsz, t_len, rt=256)
        conv_new = qk_raw[:, -(B_CONV - 1):]
    else:
        shift, s_a, conv, c0, n0, m0 = st
        conv_new = jnp.concatenate([conv, qk_raw], axis=1)[:, -(B_CONV - 1):]
        outs = _rw_prep(ua, shift, f['rw_mu'], f['w_lora'], f['rw_vec'], f['seg'], n)
        r, w, k, v, a, b, g, bonus = outs
        as3 = lambda x: x.reshape(1, n, A_W)
        y, s_a_new = _rw_rec(as3(r), as3(w), as3(k), as3(v), as3(a), as3(b), s_a, tb=1, bb=SUBLANES)
        ya = _rw_post(y.reshape(n, A_W), bonus, g, f['rw_ln'], f['seg'], n)

    if seq:
        gates3 = gates.reshape(bsz, t_len, LANES)
        gates_t = gates3[:, :, :SUBLANES].transpose(0, 2, 1)
        yb, c_new, n_new, m_new = _mlstm_chunk(ub.reshape(bsz, t_len, 4 * B_W), gates3, gates_t, f['cw'],
                                               f['cb'], f['bias_r'], f['bias_c'], f['ml_gn'], bsz, t_len,
                                               nb=2)
        yb = yb.reshape(n, B_W)
        n_new = n_new[:, :B_HEADS]
        m_new = m_new[:, :B_HEADS, 0]
    else:
        m0p = jnp.pad(m0, ((0, 0), (0, LANES - B_HEADS)))
        yb, c_new, n_new, m_new = _mlstm_step(ub, gates, conv.reshape(bsz, -1), f['cw'], f['cb'],
                                              f['bias_r'], f['ml_gn'], c0, n0.reshape(bsz, B_W), m0p,
                                              bb=SUBLANES)
        n_new = n_new.reshape(bsz, B_HEADS, B_HD)
        m_new = m_new[:, :B_HEADS]

    h_next = _post(h, [ya, yb], [f['w_out_a'], f['w_out_b']], p, f['norm_g'], f['ffn_up'],
                   f['ffn_down'], f['ple_gate'], f['ple_proj'], li=0, tm=tm, tf=POST_TF)

    shift_new = ua.reshape(bsz, t_len, A_SHIFT_W)[:, -1]
    return h_next, (shift_new, s_a_new, conv_new, c_new, n_new, m_new)


def _odd_layer(h, p, s0, pos, f, bsz, t_len):
    n = bsz * t_len
    seq = s0 is None
    cos, sin = _rope_tables(pos)
    tm = 256 if seq else n
    ws = [f['w_rq'], f['w_rk'], f['w_rv'], f['w_rg']]
    if seq:
        q, qd, k, kd, v, g = _inproj_ret(h, f['norm_g'][1], f['ret_gn'], cos, sin, ws, tm, t_len // tm,
                                         [BF16] * 5 + [F32], decayed=True)
        as3 = lambda a: a.reshape(bsz, t_len, a.shape[-1])
        y, s_new = _ret_chunk(as3(q), as3(qd), as3(k), as3(kd), as3(v), as3(g), bsz, t_len, nb=1)
        y = y.reshape(n, C_OUT_W)
    else:
        q, k, v, g = _inproj_ret(h, f['norm_g'][1], f['ret_gn'], cos, sin, ws, tm, 1, [F32] * 4,
                                 decayed=False)
        y, s_new = _ret_step(q, k, v, g, s0, bb=SUBLANES)
    h_next = _post(h, [y], [f['w_ret_out']], p, f['norm_g'], f['ffn_up'], f['ffn_down'],
                   f['ple_gate'], f['ple_proj'], li=1, tm=512 if seq else n, tf=POST_TF)
    return h_next, s_new


def _trunk(x, p, pos, ev_states, od_state, f):
    bsz, t_len, _ = x.shape
    n = bsz * t_len
    h = x.reshape(n, D_MODEL)
    ev_in = None if ev_states is None else tuple(s[0] for s in ev_states)
    od_in = None if od_state is None else od_state[0]
    p = p.reshape(p.shape[0], n, D_PLE)
    h, ev_new = _even_layer(h, p, ev_in, f, bsz, t_len)
    h, od_new = _odd_layer(h, p, od_in, pos, f, bsz, t_len)
    return h.reshape(bsz, t_len, D_MODEL), [s[None] for s in ev_new], od_new[None]


def kernel(x_prompt, x_sample, state_rwkv_shift, state_rwkv_S, state_mlstm_conv, state_mlstm_C,
           state_mlstm_n, state_mlstm_m, state_ret_S, p_prompt, p_sample, norm_g, ffn_up, ffn_down,
           ple_gate, ple_proj, mix0_w_in, rw_mu, rw_w0, rw_w2, rw_a0, rw_a2, rw_g2, rw_kk, rw_ka,
           rw_rk, rw_ln, ml_conv_w, ml_conv_b, ml_i_bias, ml_f_bias, ml_gn, mix0_w_out,
           ret_w_in, ret_gn, ret_w_out):
    W = dict(norm_g=norm_g, ffn_up=ffn_up, ffn_down=ffn_down, ple_gate=ple_gate, ple_proj=ple_proj,
             mix0_w_in=mix0_w_in, rw_mu=rw_mu, rw_w0=rw_w0, rw_w2=rw_w2, rw_a0=rw_a0, rw_a2=rw_a2,
             rw_g2=rw_g2, rw_kk=rw_kk, rw_ka=rw_ka, rw_rk=rw_rk, rw_ln=rw_ln, ml_conv_w=ml_conv_w,
             ml_conv_b=ml_conv_b, ml_i_bias=ml_i_bias, ml_f_bias=ml_f_bias, ml_gn=ml_gn,
             mix0_w_out=mix0_w_out, ret_w_in=ret_w_in, ret_gn=ret_gn, ret_w_out=ret_w_out)
    f = _prep_weights(W)
    bp, tp = x_prompt.shape[:2]
    ts = x_sample.shape[1]
    y_prompt, ev_p, ret_p = _trunk(x_prompt, p_prompt, jnp.arange(tp), None, None, f)
    ev_s_in = (state_rwkv_shift, state_rwkv_S, state_mlstm_conv, state_mlstm_C, state_mlstm_n,
               state_mlstm_m)
    y_sample, ev_s, ret_s = _trunk(x_sample, p_sample, PAST_LEN + jnp.arange(ts), ev_s_in, state_ret_S, f)
    return (y_prompt, y_sample, *ev_p, ret_p, *ev_s, ret_s)
```

```python
import functools
import math

import jax
import jax.numpy as jnp
from jax import lax
from jax.experimental import pallas as pl
from jax.experimental.pallas import tpu as pltpu

F32 = jnp.float32
BF16 = jnp.bfloat16

D_MODEL = 1024
D_PLE = 256
D_FF = 4 * D_MODEL
RMS_EPS = 1e-6
GN_EPS = 1e-5
CHUNK = 128

A_HEADS = 8
A_HD = 64
A_W = A_HEADS * A_HD
A_DECAY_LORA = 64
A_AAA_LORA = 64
A_GATE_LORA = 128
A_LORA_W = A_DECAY_LORA + A_AAA_LORA + A_GATE_LORA
A_SHIFT_W = 3 * A_W + A_LORA_W
A_LN_EPS = 64e-5

B_HEADS = 4
B_HD = 128
B_W = B_HEADS * B_HD
B_CONV = 4

C_HEADS = 4
C_DK = 256
C_DV = 512
C_OUT_W = C_HEADS * C_DV
ROPE_BASE = 10000.0
PAST_LEN = 16384

LANES = 128
SUBLANES = 8
VMEM_LIMIT = 56 * 1024 * 1024
POST_TF = 1024


def _cparams(*sem):
    return pltpu.CompilerParams(dimension_semantics=sem, vmem_limit_bytes=VMEM_LIMIT)


def _const_spec(shape):
    nd = len(shape)
    return pl.BlockSpec(shape, lambda *_: (0,) * nd)


def _rms(x, g):
    return x * lax.rsqrt(jnp.mean(x * x, axis=-1, keepdims=True) + RMS_EPS) * g


def _dot(a, b):
    return jnp.dot(a, b, preferred_element_type=F32)


def _bdot(a, b):
    return jnp.dot(a.astype(BF16), b.astype(BF16), preferred_element_type=F32)


def _dot_nt(a, b):
    return lax.dot_general(a, b, (((1,), (1,)), ((), ())), preferred_element_type=F32)


def _dot_tn(a, b):
    return lax.dot_general(a, b, (((0,), (0,)), ((), ())), preferred_element_type=F32)


def _hilo(x):
    hi = x.astype(BF16)
    lo = (x - hi.astype(F32)).astype(BF16)
    return hi, lo


def _dot_sel(x, sel):
    hi, lo = _hilo(x)
    return _dot(hi, sel) + _dot(lo, sel)


def _dot_sel_left(sel, x):
    hi, lo = _hilo(x)
    return _dot(sel, hi) + _dot(sel, lo)


def _dot3(a, b, dot):
    ah, al = _hilo(a)
    bh, bl = _hilo(b)
    return dot(ah, bh) + dot(ah, bl) + dot(al, bh)


def _sigmoid(x):
    return 1.0 / (1.0 + jnp.exp(-x))


def _softplus(x):
    return jnp.maximum(x, 0.0) + jnp.log1p(jnp.exp(-jnp.abs(x)))


def _log_sigmoid(x):
    return -_softplus(-x)


def _silu(x):
    return x * _sigmoid(x)


def _lane_col(x, idx):
    lane = lax.broadcasted_iota(jnp.int32, x.shape, 1)
    return jnp.sum(jnp.where(lane == idx, x, 0.0), axis=1, keepdims=True)


def _row_pick(x, idx):
    row = lax.broadcasted_iota(jnp.int32, x.shape, 0)
    return jnp.sum(jnp.where(row == idx, x, 0.0), axis=0, keepdims=True)


def _seg_matrix(n, seg):
    sh = seg.bit_length() - 1
    r = jnp.right_shift(lax.broadcasted_iota(jnp.int32, (n, n), 0), sh)
    c = jnp.right_shift(lax.broadcasted_iota(jnp.int32, (n, n), 1), sh)
    return (r == c).astype(BF16)


def _inproj_kernel(x_ref, g_ref, *refs):
    n = len(refs) // 2
    xn = _rms(x_ref[...], g_ref[0:1, :]).astype(BF16)
    for w_ref, o_ref in zip(refs[:n], refs[n:]):
        o_ref[...] = _dot(xn, w_ref[...]).astype(o_ref.dtype)


def _inproj(h, g, ws, dtypes, tm):
    n = h.shape[0]
    return pl.pallas_call(
        _inproj_kernel,
        grid=(n // tm,),
        in_specs=[pl.BlockSpec((tm, D_MODEL), lambda i: (i, 0)), _const_spec(g.shape)]
        + [_const_spec(w.shape) for w in ws],
        out_specs=[pl.BlockSpec((tm, w.shape[1]), lambda i: (i, 0)) for w in ws],
        out_shape=[jax.ShapeDtypeStruct((n, w.shape[1]), dt) for w, dt in zip(ws, dtypes)],
        compiler_params=_cparams("parallel"),
        name="inproj",
    )(h, g, *ws)


def _inproj_ret_kernel(x_ref, g_ref, gn_ref, cos_ref, sin_ref, wq_ref, wk_ref, wv_ref, wg_ref, *out_refs,
                       decayed):
    if decayed:
        q_ref, qd_ref, k_ref, kd_ref, v_ref, gate_ref = out_refs
    else:
        q_ref, k_ref, v_ref, gate_ref = out_refs
        qd_ref = kd_ref = None
    xn = _rms(x_ref[...], g_ref[0:1, :]).astype(BF16)
    cos = cos_ref[...]
    sin = sin_ref[...]
    half = C_DK // 2
    tm = x_ref.shape[0]
    pos = jnp.bitwise_and(lax.broadcasted_iota(jnp.int32, (tm, half), 0), CHUNK - 1).astype(F32)
    gate_ref[...] = (_silu(_dot(xn, wg_ref[...])) * gn_ref[...]).astype(gate_ref.dtype)
    for w_ref, o_ref, d_ref, scale in ((wq_ref, q_ref, qd_ref, None), (wk_ref, k_ref, kd_ref, C_DK ** -0.5)):
        u = _dot(xn, w_ref[...])
        for hd in range(C_HEADS):
            x1 = u[:, hd * C_DK:hd * C_DK + half]
            x2 = u[:, hd * C_DK + half:(hd + 1) * C_DK]
            y1 = x1 * cos - x2 * sin
            y2 = x1 * sin + x2 * cos
            if scale is not None:
                y1 = y1 * scale
                y2 = y2 * scale
            o_ref[:, hd * C_DK:hd * C_DK + half] = y1.astype(o_ref.dtype)
            o_ref[:, hd * C_DK + half:(hd + 1) * C_DK] = y2.astype(o_ref.dtype)
            if decayed:
                lg = _ret_log_gamma(hd)
                dec = jnp.exp((pos + 1.0) * lg) if d_ref is qd_ref else jnp.exp((CHUNK - 1.0 - pos) * lg)
                d_ref[:, hd * C_DK:hd * C_DK + half] = (y1 * dec).astype(d_ref.dtype)
                d_ref[:, hd * C_DK + half:(hd + 1) * C_DK] = (y2 * dec).astype(d_ref.dtype)
    v_ref[...] = _dot(xn, wv_ref[...]).astype(v_ref.dtype)


def _inproj_ret(h, g, gn, cos, sin, ws, tm, seq_tiles, dtypes, decayed):
    n = h.shape[0]
    if cos.shape[0] == 1:
        cs_spec = _const_spec(cos.shape)
    else:
        cs_spec = pl.BlockSpec((tm, cos.shape[1]), lambda i: (i % seq_tiles, 0))
    widths = [w.shape[1] for w in ws]
    if decayed:
        widths = [widths[0], widths[0], widths[1], widths[1], widths[2], widths[3]]
    return pl.pallas_call(
        functools.partial(_inproj_ret_kernel, decayed=decayed),
        grid=(n // tm,),
        in_specs=[pl.BlockSpec((tm, D_MODEL), lambda i: (i, 0)), _const_spec(g.shape), _const_spec(gn.shape),
                  cs_spec, cs_spec] + [_const_spec(w.shape) for w in ws],
        out_specs=[pl.BlockSpec((tm, wd), lambda i: (i, 0)) for wd in widths],
        out_shape=[jax.ShapeDtypeStruct((n, wd), dt) for wd, dt in zip(widths, dtypes)],
        compiler_params=_cparams("parallel"),
        name="inproj_ret",
    )(h, g, gn, cos, sin, *ws)


def _rw_prep_kernel(x_ref, sh_ref, mu_ref, wl_ref, vec_ref, seg_ref,
                    r_ref, w_ref, k_ref, v_ref, a_ref, b_ref, g_ref, bonus_ref, carry):
    rows = x_ref.shape[0]
    nb = sh_ref.shape[0]

    @pl.when(pl.program_id(0) == 0)
    def _():
        carry[...] = sh_ref[...]

    x = x_ref[...]
    if rows == nb:
        prev = carry[...]
    else:
        prev = jnp.concatenate([carry[...], x[:rows - nb]], axis=0)
    carry[...] = x[rows - nb:]
    xs = x + (prev - x) * mu_ref[...]
    r = xs[:, :A_W]
    k = xs[:, A_W:2 * A_W]
    v = xs[:, 2 * A_W:3 * A_W]
    lo = xs[:, 3 * A_W:]
    lane = lax.broadcasted_iota(jnp.int32, lo.shape, 1)
    act = jnp.where(lane < A_DECAY_LORA, jnp.tanh(lo),
                    jnp.where(lane < A_DECAY_LORA + A_AAA_LORA, lo, _sigmoid(lo)))
    lora = _dot(act.astype(BF16), wl_ref[...])
    w0, a0, kkw, kaw, rkw = (vec_ref[i:i + 1, :] for i in range(5))
    w_log = -_softplus(-(w0 + lora[:, :A_W])) - 0.5
    decay = jnp.exp(-jnp.exp(w_log))
    a = _sigmoid(a0 + lora[:, A_W:2 * A_W])
    g = lora[:, 2 * A_W:]
    seg = seg_ref[...]
    kk = k * kkw
    kk = kk / jnp.maximum(jnp.sqrt(_dot_sel(kk * kk, seg)), 1e-12)
    k2 = k * (1.0 + (a - 1.0) * kaw)
    bonus = _dot_sel(r * k2 * rkw, seg) * v
    r_ref[...] = r
    w_ref[...] = decay
    k_ref[...] = k2
    v_ref[...] = v
    a_ref[...] = -kk
    b_ref[...] = kk * a
    g_ref[...] = g
    bonus_ref[...] = bonus


def _rw_prep(ua_tm, shift, mu, wl, vec, seg, rows):
    n = ua_tm.shape[0]
    nb = shift.shape[0]
    row_spec = lambda w: pl.BlockSpec((rows, w), lambda i: (i, 0))
    return pl.pallas_call(
        _rw_prep_kernel,
        grid=(n // rows,),
        in_specs=[row_spec(A_SHIFT_W), _const_spec(shift.shape), _const_spec(mu.shape),
                  _const_spec(wl.shape), _const_spec(vec.shape), _const_spec(seg.shape)],
        out_specs=[row_spec(A_W)] * 8,
        out_shape=[jax.ShapeDtypeStruct((n, A_W), F32)] * 8,
        scratch_shapes=[pltpu.VMEM((nb, A_SHIFT_W), F32)],
        compiler_params=_cparams("arbitrary"),
        name="rw_prep",
    )(ua_tm, shift, mu, wl, vec, seg)


RW_HALF = 2 * LANES


def _rw_rec_kernel(r_ref, w_ref, k_ref, v_ref, a_ref, b_ref, s0_ref, y_ref, so_ref,
                   s_scr, lhs_scr, red_scr, snb_scr):
    tb, bb, _ = r_ref.shape
    nq = A_W // RW_HALF
    units = bb * nq
    j = pl.program_id(1)

    @pl.when(j == 0)
    def _():
        for bi in range(bb):
            for hd in range(A_HEADS):
                s_scr[bi, :, hd * A_HD:(hd + 1) * A_HD] = s0_ref[bi, hd]

    row = lax.broadcasted_iota(jnp.int32, (A_HD, RW_HALF), 0)
    col = lax.broadcasted_iota(jnp.int32, (A_HD, RW_HALF), 1)
    key = jnp.bitwise_and(col, A_HD - 1)
    on_diag = key == row
    off_diag = key == jnp.bitwise_and(row + 1, A_HD - 1)
    seg = _seg_matrix(RW_HALF, A_HD)
    head_of_lane = jnp.right_shift(lax.broadcasted_iota(jnp.int32, (A_HEADS, A_W), 1), 6)
    own_head = head_of_lane == lax.broadcasted_iota(jnp.int32, (A_HEADS, A_W), 0)
    first_key = jnp.bitwise_and(lax.broadcasted_iota(jnp.int32, (bb, A_W), 1), A_HD - 1) == 0

    def step(t, carry):
        rt, wt, kt, vt, at, bt = (ref[t] for ref in (r_ref, w_ref, k_ref, v_ref, a_ref, b_ref))
        v_hi = vt.astype(BF16).astype(F32)
        v_lo = vt - v_hi
        v_lo = jnp.concatenate(
            [jnp.where(first_key[:, :LANES],
                       pltpu.roll(v_lo[:, c * LANES:(c + 1) * LANES], LANES - (A_HD - 1), axis=1),
                       pltpu.roll(v_lo[:, c * LANES:(c + 1) * LANES], 1, axis=1))
             for c in range(A_W // LANES)], axis=1)
        for bi in range(bb):
            pa = (s_scr[bi] * at[bi:bi + 1]).astype(BF16)
            for q in range(nq):
                u = bi * nq + q
                sl = slice(q * RW_HALF, (q + 1) * RW_HALF)
                lhs_scr[u * A_HD:(u + 1) * A_HD, :] = pa[:, sl]
                dv = jnp.where(on_diag, v_hi[bi:bi + 1, sl], jnp.where(off_diag, v_lo[bi:bi + 1, sl], 0.0))
                lhs_scr[(units + u) * A_HD:(units + u + 1) * A_HD, :] = dv.astype(BF16)
        red_scr[...] = _dot(lhs_scr[...], seg)
        for bi in range(bb):
            for q in range(nq):
                u = bi * nq + q
                sl = slice(q * RW_HALF, (q + 1) * RW_HALF)
                sa = red_scr[u * A_HD:(u + 1) * A_HD, :]
                vb = red_scr[(units + u) * A_HD:(units + u + 1) * A_HD, :]
                sn = s_scr[bi, :, sl] * wt[bi:bi + 1, sl] + sa * bt[bi:bi + 1, sl] + vb * kt[bi:bi + 1, sl]
                s_scr[bi, :, sl] = sn
                snb_scr[bi, :, sl] = sn.astype(BF16)
        for bi in range(bb):
            r_heads = jnp.where(own_head, jnp.broadcast_to(rt[bi:bi + 1], (A_HEADS, A_W)), 0.0)
            y_ref[t, bi] = _dot_nt(r_heads.astype(BF16), snb_scr[bi])
        return carry

    lax.fori_loop(0, tb, step, 0)

    @pl.when(j == pl.num_programs(1) - 1)
    def _():
        for bi in range(bb):
            for hd in range(A_HEADS):
                so_ref[bi, hd] = s_scr[bi, :, hd * A_HD:(hd + 1) * A_HD]


def _rw_rec(r, w, k, v, a, b, s0, tb, bb):
    t_len, bsz, _ = r.shape
    seq_spec = pl.BlockSpec((tb, bb, A_W), lambda i, j: (j, i, 0))
    st_spec = pl.BlockSpec((bb, A_HEADS, A_HD, A_HD), lambda i, j: (i, 0, 0, 0))
    units = bb * (A_W // RW_HALF)
    return pl.pallas_call(
        _rw_rec_kernel,
        grid=(bsz // bb, t_len // tb),
        in_specs=[seq_spec] * 6 + [st_spec],
        out_specs=[pl.BlockSpec((tb, bb, A_HEADS, A_HD), lambda i, j: (j, i, 0, 0)), st_spec],
        out_shape=[jax.ShapeDtypeStruct((t_len, bsz, A_HEADS, A_HD), F32),
                   jax.ShapeDtypeStruct(s0.shape, F32)],
        scratch_shapes=[pltpu.VMEM((bb, A_HD, A_W), F32),
                        pltpu.VMEM((2 * units * A_HD, RW_HALF), BF16),
                        pltpu.VMEM((2 * units * A_HD, RW_HALF), F32),
                        pltpu.VMEM((bb, A_HD, A_W), BF16)],
        compiler_params=_cparams("parallel", "arbitrary"),
        name="rw_rec",
    )(r, w, k, v, a, b, s0)


RW_C = 64
(F_RT, F_PM1, F_IP, F_DC, F_KH, F_KC, F_V, F_KK, F_ASIG, F_RKK, F_G, F_PC) = range(12)


def _rw_chunk_kernel(x_ref, mu_ref, wl_ref, vec_ref, ln_ref, o_ref, so_ref, carry, s_scr, f_scr):
    rt = x_ref.shape[0]
    i = pl.program_id(1)

    @pl.when(i == 0)
    def _():
        carry[...] = jnp.zeros_like(carry)
        s_scr[...] = jnp.zeros_like(s_scr)

    x = x_ref[...]
    row = lax.broadcasted_iota(jnp.int32, x.shape, 0)
    prev = jnp.where(row == 0, carry[...], pltpu.roll(x, 1, axis=0))
    carry[...] = x[rt - 1:rt, :]
    xs = x + (prev - x) * mu_ref[...]
    r = xs[:, :A_W]
    k = xs[:, A_W:2 * A_W]
    lo = xs[:, 3 * A_W:]
    lane = lax.broadcasted_iota(jnp.int32, lo.shape, 1)
    act = jnp.where(lane < A_DECAY_LORA, jnp.tanh(lo),
                    jnp.where(lane < A_DECAY_LORA + A_AAA_LORA, lo, _sigmoid(lo)))
    lora = _dot(act.astype(BF16), wl_ref[...])
    w0, a0, kkw, kaw, rkw = (vec_ref[n:n + 1, :] for n in range(5))
    lw = -math.exp(-0.5) * _sigmoid(w0 + lora[:, :A_W])
    a_sig = _sigmoid(a0 + lora[:, A_W:2 * A_W])
    k2 = k * (1.0 + (a_sig - 1.0) * kaw)

    ri = lax.broadcasted_iota(jnp.int32, (rt, rt), 0)
    ci = lax.broadcasted_iota(jnp.int32, (rt, rt), 1)
    same = jnp.right_shift(ri, 6) == jnp.right_shift(ci, 6)
    chunk_ones = same.astype(BF16)
    chunk_tri = jnp.where(ci <= ri, chunk_ones, jnp.zeros_like(chunk_ones))
    lp = _dot_sel_left(chunk_tri, lw)
    lpc = _dot_sel_left(chunk_ones, lw)
    ip = jnp.exp(-lp)
    dc = jnp.exp(lpc - lp)
    f_scr[F_RT] = r * jnp.exp(lp)
    f_scr[F_PM1] = jnp.exp(lp - lw)
    f_scr[F_IP] = ip
    f_scr[F_DC] = dc
    f_scr[F_KH] = k2 * ip
    f_scr[F_KC] = k2 * dc
    f_scr[F_V] = xs[:, 2 * A_W:3 * A_W]
    f_scr[F_KK] = k * kkw
    f_scr[F_ASIG] = a_sig
    f_scr[F_RKK] = r * k2 * rkw
    f_scr[F_G] = lora[:, 2 * A_W:]
    f_scr[F_PC] = jnp.exp(lpc)

    C = RW_C
    W2 = 2 * A_HD
    npairs = A_HEADS // 2
    lane_lo = lax.broadcasted_iota(jnp.int32, (C, W2), 1) < A_HD
    own = (lax.broadcasted_iota(jnp.int32, (2 * C, W2), 1) < A_HD) == \
          (lax.broadcasted_iota(jnp.int32, (2 * C, W2), 0) < C)
    r4 = lax.broadcasted_iota(jnp.int32, (4 * C, 4 * C), 0)
    c4 = lax.broadcasted_iota(jnp.int32, (4 * C, 4 * C), 1)
    keep = jnp.bitwise_and(c4, C - 1) < jnp.bitwise_and(r4, C - 1) + jnp.right_shift(r4, 7)
    eye_f = (lax.broadcasted_iota(jnp.int32, (W2, W2), 0) ==
             lax.broadcasted_iota(jnp.int32, (W2, W2), 1)).astype(F32)
    zeros_ww = jnp.zeros((W2, W2), F32)
    bf = lambda z: z.astype(BF16)
    stack = lambda z: jnp.concatenate([jnp.where(lane_lo, z, 0.0), jnp.where(lane_lo, 0.0, z)], axis=0)
    spread = lambda col: jnp.where(lane_lo, col[:C], col[C:])
    chunks_per_body = 4
    probs = [(cc, p) for cc in range(chunks_per_body) for p in range(npairs)]

    def body(j, _):
        rows = [pl.ds(pl.multiple_of((j * chunks_per_body + cc) * C, C), C) for cc in range(chunks_per_body)]
        ld = lambda f, q: f_scr[f, rows[q[0]], q[1] * W2:(q[1] + 1) * W2]
        a_l, r_l, bc_l, kc_l, v_l, g_l = [], [], [], [], [], []
        kk_sq = [jnp.sum(jnp.square(stack(ld(F_KK, q))), axis=-1, keepdims=True) for q in probs]
        bonus_l = [jnp.sum(stack(ld(F_RKK, q)), axis=-1, keepdims=True) for q in probs]
        for e, q in enumerate(probs):
            kk = ld(F_KK, q) * spread(1.0 / jnp.maximum(jnp.sqrt(kk_sq[e]), 1e-12))
            b = kk * ld(F_ASIG, q)
            a_l.append(stack(-kk * ld(F_PM1, q)))
            r_l.append(stack(ld(F_RT, q)))
            bc_l.append(stack(b * ld(F_DC, q)))
            kc_l.append(stack(ld(F_KC, q)))
            v_l.append(stack(ld(F_V, q)))
            ar = jnp.concatenate([a_l[-1], r_l[-1]], axis=0)
            bk = jnp.concatenate([stack(b * ld(F_IP, q)), stack(ld(F_KH, q))], axis=0)
            g_l.append(jnp.where(keep, _dot_nt(bf(ar), bf(bk)), 0.0))
        n = range(len(probs))
        x_l = [g_l[e][:W2, :W2] for e in n]
        t_l = [eye_f + x_l[e] for e in n]
        x_l = [_dot(bf(x_l[e]), bf(x_l[e])) for e in n]
        m = 2
        while m < C:
            last = 2 * m >= C
            for e in n:
                rhs = t_l[e] if last else jnp.concatenate([x_l[e], t_l[e]], axis=1)
                prod = _dot(bf(x_l[e]), bf(rhs))
                if last:
                    t_l[e] = t_l[e] + prod
                else:
                    x_l[e] = prod[:, :W2]
                    t_l[e] = t_l[e] + prod[:, W2:]
            m *= 2
        lv_l = [_dot(bf(g_l[e][:W2, W2:]), bf(v_l[e])) for e in n]
        apu_l = [_dot(bf(t_l[e]), bf(jnp.concatenate([a_l[e], lv_l[e]], axis=1))) for e in n]
        qy_l = []
        for e in n:
            low = jnp.concatenate([zeros_ww, v_l[e]], axis=1)
            qy_l.append(_dot(bf(g_l[e][W2:, :]), bf(jnp.concatenate([apu_l[e], low], axis=0))))
        mm_l = [_dot_tn(bf(apu_l[e][:, :W2]), bf(bc_l[e])) for e in n]
        n_l = []
        for e in n:
            uv = jnp.concatenate([apu_l[e][:, W2:], v_l[e]], axis=0)
            n_l.append(_dot_tn(bf(uv), bf(jnp.concatenate([bc_l[e], kc_l[e]], axis=0))))
        y_l = []
        for e, q in enumerate(probs):
            p = q[1]
            s0 = s_scr[p]
            s0b = bf(s0)
            y_l.append(_dot_nt(bf(r_l[e] + qy_l[e][:, :W2]), s0b) + qy_l[e][:, W2:])
            s_scr[p] = s0 * ld(F_PC, q)[0:1, :] + _dot(s0b, bf(mm_l[e])) + n_l[e]
        mu_l = [jnp.sum(y_l[e], axis=-1, keepdims=True) * (1.0 / A_HD) for e in n]
        d_l = [jnp.where(own, y_l[e] - mu_l[e], 0.0) for e in n]
        var_l = [jnp.sum(d_l[e] * d_l[e], axis=-1, keepdims=True) * (1.0 / A_HD) for e in n]
        for e, q in enumerate(probs):
            p = q[1]
            d = d_l[e] * lax.rsqrt(var_l[e] + A_LN_EPS)
            yn = (d[:C] + d[C:]) * ln_ref[:, p * W2:(p + 1) * W2]
            bonus = spread(bonus_l[e]) * ld(F_V, q)
            o_ref[rows[q[0]], p * W2:(p + 1) * W2] = ((yn + bonus) * ld(F_G, q)).astype(o_ref.dtype)
        return 0

    lax.fori_loop(0, rt // (C * chunks_per_body), body, 0)

    @pl.when(i == pl.num_programs(1) - 1)
    def _():
        for p in range(npairs):
            so_ref[0, 2 * p] = s_scr[p, :A_HD, :A_HD]
            so_ref[0, 2 * p + 1] = s_scr[p, A_HD:, A_HD:]


def _rw_chunk(ua, mu, wl, vec, ln, bsz, t_len, rt):
    nt = t_len // rt
    st_spec = pl.BlockSpec((1, A_HEADS, A_HD, A_HD), lambda b, i: (b, 0, 0, 0))
    return pl.pallas_call(
        _rw_chunk_kernel,
        grid=(bsz, nt),
        in_specs=[pl.BlockSpec((rt, A_SHIFT_W), lambda b, i: (b * nt + i, 0)),
                  _const_spec(mu.shape), _const_spec(wl.shape), _const_spec(vec.shape),
                  _const_spec(ln.shape)],
        out_specs=[pl.BlockSpec((rt, A_W), lambda b, i: (b * nt + i, 0)), st_spec],
        out_shape=[jax.ShapeDtypeStruct((bsz * t_len, A_W), BF16),
                   jax.ShapeDtypeStruct((bsz, A_HEADS, A_HD, A_HD), F32)],
        scratch_shapes=[pltpu.VMEM((1, A_SHIFT_W), F32),
                        pltpu.VMEM((A_HEADS // 2, 2 * A_HD, 2 * A_HD), F32),
                        pltpu.VMEM((12, rt, A_W), F32)],
        compiler_params=_cparams("parallel", "arbitrary"),
        name="rw_chunk",
    )(ua, mu, wl, vec, ln)


def _rw_post_kernel(y_ref, bonus_ref, g_ref, ln_ref, seg_ref, o_ref):
    seg = seg_ref[...]
    y = y_ref[...]
    mu = _dot_sel(y, seg) * (1.0 / A_HD)
    d = y - mu
    var = _dot_sel(d * d, seg) * (1.0 / A_HD)
    yn = d * lax.rsqrt(var + A_LN_EPS) * ln_ref[...]
    o_ref[...] = ((yn + bonus_ref[...]) * g_ref[...]).astype(o_ref.dtype)


def _rw_post(y, bonus, g, ln, seg, rows):
    n = y.shape[0]
    row_spec = pl.BlockSpec((rows, A_W), lambda i: (i, 0))
    return pl.pallas_call(
        _rw_post_kernel,
        grid=(n // rows,),
        in_specs=[row_spec] * 3 + [_const_spec(ln.shape), _const_spec(seg.shape)],
        out_specs=row_spec,
        out_shape=jax.ShapeDtypeStruct((n, A_W), BF16),
        compiler_params=_cparams("parallel"),
        name="rw_post",
    )(y, bonus, g, ln, seg)


def _head_norm_lanes(x, g, eps):
    mu = jnp.mean(x, axis=-1, keepdims=True)
    d = x - mu
    var = jnp.mean(d * d, axis=-1, keepdims=True)
    return d * lax.rsqrt(var + eps) * g


def _mlstm_chunk_kernel(ub_ref, g_ref, gt_ref, cw_ref, cb_ref, biasr_ref, biasc_ref, gn_ref,
                        y_ref, co_ref, no_ref, mo_ref,
                        tail_s, c_s, n_s, m_s):
    nb, L, _ = ub_ref.shape
    c = pl.program_id(1)

    @pl.when(c == 0)
    def _():
        tail_s[...] = jnp.zeros_like(tail_s)
        c_s[...] = jnp.zeros_like(c_s)
        n_s[...] = jnp.zeros_like(n_s)
        m_s[...] = jnp.zeros_like(m_s)

    r_i = lax.broadcasted_iota(jnp.int32, (L, L), 0)
    c_i = lax.broadcasted_iota(jnp.int32, (L, L), 1)
    causal = c_i <= r_i
    tri = causal.astype(BF16)
    tri_u = (r_i <= c_i).astype(BF16)
    heads = range(B_HEADS)
    sls = [slice(hd * B_HD, (hd + 1) * B_HD) for hd in heads]

    seqs = range(nb)
    ub_l, q_l, k_l, gc_l, gr_l, bcols_l, brows_l = [], [], [], [], [], [], []
    for bi in seqs:
        ub = ub_ref[bi]
        raw = ub[:, :2 * B_W]
        ext = jnp.concatenate([tail_s[bi], raw], axis=0)
        tail_s[bi] = raw[L - SUBLANES:]
        acc = cb_ref[...] + raw * cw_ref[B_CONV - 1:B_CONV, :]
        for jj in range(B_CONV - 1):
            sh = B_CONV - 1 - jj
            acc = acc + pltpu.roll(ext, sh, axis=0)[SUBLANES:] * cw_ref[jj:jj + 1, :]
        act = _silu(acc)
        ub_l.append(ub)
        q_l.append(act[:, :B_W])
        k_l.append(act[:, B_W:] * (B_HD ** -0.5))
        gc = g_ref[bi] + biasr_ref[...]
        gr = gt_ref[bi] + biasc_ref[...]
        gc_l.append(gc)
        gr_l.append(gr)
        bcols_l.append(_dot_sel_left(tri, _log_sigmoid(gc)))
        brows_l.append(_dot_sel(_log_sigmoid(gr), tri_u))
    m_all = [m_s[bi] for bi in seqs]
    n_all = [n_s[bi] for bi in seqs]

    probs = [(bi, hd) for bi in seqs for hd in heads]
    qb = [q_l[bi][:, sls[hd]].astype(BF16) for bi, hd in probs]
    vb = [ub_l[bi][:, 2 * B_W + hd * B_HD:2 * B_W + (hd + 1) * B_HD].astype(BF16) for bi, hd in probs]
    qk = [_dot_nt(qb[e], k_l[bi][:, sls[hd]].astype(BF16)) for e, (bi, hd) in enumerate(probs)]
    qc = [_dot(qb[e], c_s[bi * B_HEADS + hd].astype(BF16)) for e, (bi, hd) in enumerate(probs)]
    m_prev = [m_all[bi][hd:hd + 1, 0:1] for bi, hd in probs]
    n_prev = [n_all[bi][hd:hd + 1, :] for bi, hd in probs]
    bc = [_lane_col(bcols_l[bi], B_HEADS + hd) for bi, hd in probs]
    ic = [_lane_col(gc_l[bi], hd) for bi, hd in probs]
    dlog, inter, b_last, gi = [], [], [], []
    for e, (bi, hd) in enumerate(probs):
        br = brows_l[bi][B_HEADS + hd:B_HEADS + hd + 1, :]
        ir = gr_l[bi][hd:hd + 1, :]
        dlog.append(jnp.where(causal, bc[e] - br + ir, -jnp.inf))
        inter.append(bc[e] + m_prev[e])
        b_last.append(_row_pick(bc[e], L - 1))
        gi.append(b_last[e] - bc[e] + ic[e])
    row_max = [jnp.max(dlog[e], axis=1, keepdims=True) for e in range(len(probs))]
    gi_max = [jnp.max(gi[e], axis=0, keepdims=True) for e in range(len(probs))]
    s_l, iw, m_t, wc, kw, m_new = [], [], [], [], [], []
    for e, (bi, hd) in enumerate(probs):
        m_t.append(jnp.maximum(inter[e], row_max[e]))
        s_l.append(qk[e] * jnp.exp(dlog[e] - m_t[e]))
        iw.append(jnp.exp(inter[e] - m_t[e]))
        m_new.append(jnp.maximum(b_last[e] + m_prev[e], gi_max[e]))
        wc.append(jnp.exp(b_last[e] + m_prev[e] - m_new[e]))
        kw.append(k_l[bi][:, sls[hd]] * jnp.exp(gi[e] - m_new[e]))
    sv = [_dot(s_l[e].astype(BF16), vb[e]) for e in range(len(probs))]
    kv = [_dot_tn(kw[e].astype(BF16), vb[e]) for e in range(len(probs))]
    s_sum = [jnp.sum(s_l[e], axis=1, keepdims=True) for e in range(len(probs))]
    qn = [jnp.sum(q_l[bi][:, sls[hd]] * n_prev[e], axis=1, keepdims=True) for e, (bi, hd) in enumerate(probs)]
    k_sum = [jnp.sum(kw[e], axis=0, keepdims=True) for e in range(len(probs))]
    hb = []
    for e, (bi, hd) in enumerate(probs):
        den = s_sum[e] + iw[e] * qn[e]
        hh = (sv[e] + iw[e] * qc[e]) / jnp.maximum(jnp.abs(den), jnp.exp(-m_t[e]))
        hb.append(_sigmoid(ub_l[bi][:, 3 * B_W + hd * B_HD:3 * B_W + (hd + 1) * B_HD]) * hh)
        c_s[bi * B_HEADS + hd] = wc[e] * c_s[bi * B_HEADS + hd] + kv[e]
    mean = [jnp.sum(hb[e], axis=-1, keepdims=True) * (1.0 / B_HD) for e in range(len(probs))]
    dev = [hb[e] - mean[e] for e in range(len(probs))]
    var = [jnp.sum(dev[e] * dev[e], axis=-1, keepdims=True) * (1.0 / B_HD) for e in range(len(probs))]
    for e, (bi, hd) in enumerate(probs):
        y_ref[bi, :, sls[hd]] = (dev[e] * lax.rsqrt(var[e] + GN_EPS) * gn_ref[:, sls[hd]]).astype(y_ref.dtype)
    pad_rows = jnp.zeros((SUBLANES - B_HEADS, LANES), F32)
    for bi in seqs:
        es = [e for e, (bj, _) in enumerate(probs) if bj == bi]
        n_s[bi] = jnp.concatenate([wc[e] * n_prev[e] + k_sum[e] for e in es] + [pad_rows], axis=0)
        m_s[bi] = jnp.concatenate([jnp.broadcast_to(m_new[e], (1, LANES)) for e in es] + [pad_rows], axis=0)

    @pl.when(c == pl.num_programs(1) - 1)
    def _():
        for bi in range(nb):
            for hd in heads:
                co_ref[bi, hd] = c_s[bi * B_HEADS + hd]
        no_ref[...] = n_s[...]
        mo_ref[...] = m_s[...]


def _mlstm_chunk(ub, gates, gates_t, cw, cb, bias_r, bias_c, gn, bsz, t_len, nb):
    L = CHUNK
    seq = lambda w: pl.BlockSpec((nb, L, w), lambda b, c: (b, c, 0))
    per_b = lambda shp: pl.BlockSpec((nb,) + shp, lambda b, c: (b,) + (0,) * len(shp))
    return pl.pallas_call(
        _mlstm_chunk_kernel,
        grid=(bsz // nb, t_len // L),
        in_specs=[seq(4 * B_W), seq(LANES),
                  pl.BlockSpec((nb, SUBLANES, L), lambda b, c: (b, 0, c)),
                  _const_spec(cw.shape), _const_spec(cb.shape), _const_spec(bias_r.shape),
                  _const_spec(bias_c.shape), _const_spec(gn.shape)],
        out_specs=[seq(B_W), per_b((B_HEADS, B_HD, B_HD)), per_b((SUBLANES, B_HD)),
                   per_b((SUBLANES, LANES))],
        out_shape=[jax.ShapeDtypeStruct((bsz, t_len, B_W), BF16),
                   jax.ShapeDtypeStruct((bsz, B_HEADS, B_HD, B_HD), F32),
                   jax.ShapeDtypeStruct((bsz, SUBLANES, B_HD), F32),
                   jax.ShapeDtypeStruct((bsz, SUBLANES, LANES), F32)],
        scratch_shapes=[pltpu.VMEM((nb, SUBLANES, 2 * B_W), F32),
                        pltpu.VMEM((nb * B_HEADS, B_HD, B_HD), F32),
                        pltpu.VMEM((nb, SUBLANES, B_HD), F32),
                        pltpu.VMEM((nb, SUBLANES, LANES), F32)],
        compiler_params=_cparams("parallel", "arbitrary"),
        name="mlstm_chunk",
    )(ub, gates, gates_t, cw, cb, bias_r, bias_c, gn)


def _mlstm_step_kernel(ub_ref, g_ref, conv_ref, cw_ref, cb_ref, biasr_ref, gn_ref, c_ref, n_ref, m_ref,
                       y_ref, co_ref, no_ref, mo_ref):
    bb = ub_ref.shape[0]
    ub = ub_ref[...]
    raw = ub[:, :2 * B_W]
    conv = conv_ref[...]
    acc = cb_ref[...] + raw * cw_ref[B_CONV - 1:B_CONV, :]
    for jj in range(B_CONV - 1):
        acc = acc + conv[:, jj * 2 * B_W:(jj + 1) * 2 * B_W] * cw_ref[jj:jj + 1, :]
    qk = _silu(acc)
    q_all = qk[:, :B_W]
    k_all = qk[:, B_W:] * (B_HD ** -0.5)
    gc = g_ref[...] + biasr_ref[...]
    lf = _log_sigmoid(gc)
    m_all = m_ref[...]
    row_pad = lax.broadcasted_iota(jnp.int32, (LANES, LANES), 0)
    pad = jnp.zeros((LANES - bb, B_HD), F32)
    m_out = jnp.zeros((bb, LANES), F32)
    lane_m = lax.broadcasted_iota(jnp.int32, (bb, LANES), 1)
    for hd in range(B_HEADS):
        sl = slice(hd * B_HD, (hd + 1) * B_HD)
        q = q_all[:, sl]
        k = k_all[:, sl]
        v = ub[:, 2 * B_W + hd * B_HD:2 * B_W + (hd + 1) * B_HD]
        o = ub[:, 3 * B_W + hd * B_HD:3 * B_W + (hd + 1) * B_HD]
        ic = _lane_col(gc, hd)
        f = _lane_col(lf, B_HEADS + hd)
        m_prev = _lane_col(m_all, hd)
        n_prev = n_ref[:, sl]
        m_t = jnp.maximum(f + m_prev, ic)
        s = jnp.sum(q * k, axis=1, keepdims=True) * jnp.exp(ic - m_t)
        iw = jnp.exp(f + m_prev - m_t)
        wi = jnp.exp(ic - m_t)
        qb = q.astype(BF16)
        kw_t = jnp.concatenate([k * wi, pad], axis=0).T
        v_pad = jnp.concatenate([v, pad], axis=0)
        qc = jnp.zeros((bb, B_HD), F32)
        row_b = lax.broadcasted_iota(jnp.int32, (bb, B_HD), 0)
        for bi in range(bb):
            c_prev = c_ref[bi, hd]
            qc = qc + jnp.where(row_b == bi, _dot(qb, c_prev.astype(BF16)), 0.0)
            v_one = jnp.where(row_pad == bi, v_pad, 0.0)
            co_ref[bi, hd] = _row_pick(iw, bi) * c_prev + _dot3(kw_t, v_one, _dot)
        num = s * v + iw * qc
        den = s + iw * jnp.sum(q * n_prev, axis=1, keepdims=True)
        hh = num / jnp.maximum(jnp.abs(den), jnp.exp(-m_t))
        no_ref[:, sl] = iw * n_prev + wi * k
        m_out = jnp.where(lane_m == hd, m_t, m_out)
        hb = _sigmoid(o) * hh
        y_ref[:, sl] = _head_norm_lanes(hb, gn_ref[:, sl], GN_EPS).astype(y_ref.dtype)
    mo_ref[...] = m_out


def _mlstm_step(ub, gates, conv, cw, cb, bias_r, gn, c0, n0, m0, bb):
    bsz = ub.shape[0]
    row = lambda w: pl.BlockSpec((bb, w), lambda i: (i, 0))
    c_spec = pl.BlockSpec((bb, B_HEADS, B_HD, B_HD), lambda i: (i, 0, 0, 0))
    return pl.pallas_call(
        _mlstm_step_kernel,
        grid=(bsz // bb,),
        in_specs=[row(4 * B_W), row(LANES), row((B_CONV - 1) * 2 * B_W),
                  _const_spec(cw.shape), _const_spec(cb.shape), _const_spec(bias_r.shape),
                  _const_spec(gn.shape), c_spec, row(B_W), row(LANES)],
        out_specs=[row(B_W), c_spec, row(B_W), row(LANES)],
        out_shape=[jax.ShapeDtypeStruct((bsz, B_W), BF16),
                   jax.ShapeDtypeStruct(c0.shape, F32),
                   jax.ShapeDtypeStruct((bsz, B_W), F32),
                   jax.ShapeDtypeStruct((bsz, LANES), F32)],
        compiler_params=_cparams("parallel"),
        name="mlstm_step",
    )(ub, gates, conv, cw, cb, bias_r, gn, c0, n0, m0)


def _ret_log_gamma(hd):
    return math.log1p(-(2.0 ** (-5.0 - hd)))


def _ret_chunk_kernel(q_ref, qd_ref, k_ref, kd_ref, v_ref, g_ref, y_ref, so_ref, s_s, mask_s, o_s):
    nb, L, _ = q_ref.shape
    c = pl.program_id(1)
    last = pl.num_programs(1) - 1

    @pl.when(c == 0)
    def _():
        s_s[...] = jnp.zeros_like(s_s)
        o_s[...] = jnp.zeros_like(o_s)
        r_i = lax.broadcasted_iota(jnp.int32, (L, L), 0)
        c_i = lax.broadcasted_iota(jnp.int32, (L, L), 1)
        diff = (r_i - c_i).astype(F32)
        for hd in range(C_HEADS):
            mask_s[hd] = jnp.where(diff >= 0, jnp.exp(jnp.maximum(diff, 0.0) * _ret_log_gamma(hd)), 0.0)

    heads = range(C_HEADS)
    kls = [slice(hd * C_DK, (hd + 1) * C_DK) for hd in heads]
    sls = [slice(hd * C_DV, (hd + 1) * C_DV) for hd in heads]
    probs = [(bi, hd) for bi in range(nb) for hd in heads]
    n = range(len(probs))

    o_prev = [o_s[e] for e in n]
    mean = [jnp.sum(o_prev[e], axis=-1, keepdims=True) * (1.0 / C_DV) for e in n]
    dev = [o_prev[e] - mean[e] for e in n]
    var = [jnp.sum(dev[e] * dev[e], axis=-1, keepdims=True) * (1.0 / C_DV) for e in n]
    for e, (bi, hd) in enumerate(probs):
        sl = sls[hd]
        y_ref[bi, :, sl] = (dev[e] * lax.rsqrt(var[e] + GN_EPS) * g_ref[bi, :, sl]).astype(y_ref.dtype)

    sc = [(_dot_nt(q_ref[bi, :, kls[hd]], k_ref[bi, :, kls[hd]]) * mask_s[hd]).astype(BF16) for bi, hd in probs]
    for e, (bi, hd) in enumerate(probs):
        sl = sls[hd]
        c_dec = math.exp(L * _ret_log_gamma(hd))
        s_prev = s_s[e]
        o_s[e] = _dot(sc[e], v_ref[bi, :, sl]) + _dot(qd_ref[bi, :, kls[hd]], s_prev.astype(BF16))
        s_s[e] = c_dec * s_prev + _dot_tn(kd_ref[bi, :, kls[hd]], v_ref[bi, :, sl])

    @pl.when(c == last - 1)
    def _():
        for e, (bi, hd) in enumerate(probs):
            so_ref[bi, hd] = s_s[e]


def _ret_chunk(q, qd, k, kd, v, g, bsz, t_len, nb):
    L = CHUNK
    nc = t_len // L
    cur = lambda w: pl.BlockSpec((nb, L, w), lambda b, c: (b, jnp.minimum(c, nc - 1), 0))
    lag = lambda w: pl.BlockSpec((nb, L, w), lambda b, c: (b, jnp.maximum(c - 1, 0), 0))
    st = pl.BlockSpec((nb, C_HEADS, C_DK, C_DV), lambda b, c: (b, 0, 0, 0))
    return pl.pallas_call(
        _ret_chunk_kernel,
        grid=(bsz // nb, nc + 1),
        in_specs=[cur(C_HEADS * C_DK)] * 4 + [cur(C_OUT_W), lag(C_OUT_W)],
        out_specs=[lag(C_OUT_W), st],
        out_shape=[jax.ShapeDtypeStruct((bsz, t_len, C_OUT_W), BF16),
                   jax.ShapeDtypeStruct((bsz, C_HEADS, C_DK, C_DV), F32)],
        scratch_shapes=[pltpu.VMEM((nb * C_HEADS, C_DK, C_DV), F32), pltpu.VMEM((C_HEADS, L, L), F32),
                        pltpu.VMEM((nb * C_HEADS, L, C_DV), F32)],
        compiler_params=_cparams("parallel", "arbitrary"),
        name="ret_chunk",
    )(q, qd, k, kd, v, g)


def _ret_step_kernel(q_ref, k_ref, v_ref, g_ref, s_ref, y_ref, so_ref):
    bb = q_ref.shape[0]
    hd = pl.program_id(1)
    gam = 1.0 - jnp.exp2(-5.0 - jnp.full((1, 1), hd).astype(F32))
    q = q_ref[...]
    k = k_ref[...]
    v = v_ref[...]
    s = jnp.sum(q * k, axis=1, keepdims=True)
    pad = jnp.zeros((LANES - bb, C_DK), F32)
    k_t = jnp.concatenate([k, pad], axis=0).T
    q_t = jnp.concatenate([q * gam, pad], axis=0).T
    row_b = lax.broadcasted_iota(jnp.int32, (bb, C_DV), 0)
    qs = jnp.zeros((bb, C_DV), F32)
    for bi in range(bb):
        s_prev = s_ref[bi, 0]
        qs = jnp.where(row_b == bi, jnp.sum(_lane_col(q_t, bi) * s_prev, axis=0, keepdims=True), qs)
        so_ref[bi, 0] = gam * s_prev + _lane_col(k_t, bi) * v[bi:bi + 1, :]
    o = s * v + qs
    y = _head_norm_lanes(o, g_ref[...], GN_EPS)
    y_ref[...] = y.astype(y_ref.dtype)


def _ret_step(q, k, v, g, s0, bb):
    bsz = q.shape[0]
    st = pl.BlockSpec((bb, 1, C_DK, C_DV), lambda i, h: (i, h, 0, 0))
    return pl.pallas_call(
        _ret_step_kernel,
        grid=(bsz // bb, C_HEADS),
        in_specs=[pl.BlockSpec((bb, C_DK), lambda i, h: (i, h)),
                  pl.BlockSpec((bb, C_DK), lambda i, h: (i, h)),
                  pl.BlockSpec((bb, C_DV), lambda i, h: (i, h)),
                  pl.BlockSpec((bb, C_DV), lambda i, h: (i, h)), st],
        out_specs=[pl.BlockSpec((bb, C_DV), lambda i, h: (i, h)), st],
        out_shape=[jax.ShapeDtypeStruct((bsz, C_OUT_W), BF16),
                   jax.ShapeDtypeStruct(s0.shape, F32)],
        compiler_params=_cparams("parallel", "parallel"),
        name="ret_step",
    )(q, k, v, g, s0)


def _post_kernel(*refs, n_mix):
    h_ref = refs[0]
    mix_refs = refs[1:1 + n_mix]
    wout_refs = refs[1 + n_mix:1 + 2 * n_mix]
    (p_ref, g_ref, up_ref, down_ref, gate_ref, proj_ref, o_ref, h1_s, xn_s, acc_s) = refs[1 + 2 * n_mix:]
    f = pl.program_id(1)

    @pl.when(f == 0)
    def _():
        mix = _dot(mix_refs[0][...], wout_refs[0][...])
        for m_ref, w_ref in zip(mix_refs[1:], wout_refs[1:]):
            mix = mix + _dot(m_ref[...], w_ref[...])
        h1 = h_ref[...] + _rms(mix, g_ref[1:2, :])
        h1_s[...] = h1
        xn_s[...] = _rms(h1, g_ref[2:3, :]).astype(BF16)
        acc_s[...] = jnp.zeros_like(acc_s)

    hid = jnp.square(jnp.maximum(_dot(xn_s[...], up_ref[...]), 0.0))
    acc_s[...] += _dot(hid.astype(BF16), down_ref[...])

    @pl.when(f == pl.num_programs(1) - 1)
    def _():
        h2 = h1_s[...] + _rms(acc_s[...], g_ref[3:4, :])
        gate = _sigmoid(_dot(h2.astype(BF16), gate_ref[...]))
        emb = _dot(p_ref[...].astype(BF16), proj_ref[...])
        o_ref[...] = h2 + gate * emb


def _post(h, mixes, wouts, p, g, up, down, gate, proj, li, tm, tf):
    n = h.shape[0]
    n_mix = len(mixes)
    row = lambda w: pl.BlockSpec((tm, w), lambda i, f: (i, 0))
    layer = lambda a: pl.BlockSpec((None,) + a.shape[1:], lambda i, f: (li,) + (0,) * (a.ndim - 1))
    return pl.pallas_call(
        functools.partial(_post_kernel, n_mix=n_mix),
        grid=(n // tm, D_FF // tf),
        in_specs=[row(D_MODEL)] + [row(m.shape[1]) for m in mixes]
        + [_const_spec(w.shape) for w in wouts]
        + [pl.BlockSpec((None, tm, D_PLE), lambda i, f: (li, i, 0)), layer(g),
           pl.BlockSpec((None, D_MODEL, tf), lambda i, f: (li, 0, f)),
           pl.BlockSpec((None, tf, D_MODEL), lambda i, f: (li, f, 0)),
           layer(gate), layer(proj)],
        out_specs=row(D_MODEL),
        out_shape=jax.ShapeDtypeStruct((n, D_MODEL), F32),
        scratch_shapes=[pltpu.VMEM((tm, D_MODEL), F32), pltpu.VMEM((tm, D_MODEL), BF16),
                        pltpu.VMEM((tm, D_MODEL), F32)],
        compiler_params=_cparams("parallel", "arbitrary"),
        name="post",
    )(h, *mixes, *wouts, p, g, up, down, gate, proj)


def _prep_weights(W):
    f = {}
    w_in = W['mix0_w_in'][0]
    f['w_a'] = w_in[:, :A_SHIFT_W].astype(BF16)
    f['w_b'] = w_in[:, A_SHIFT_W:A_SHIFT_W + 4 * B_W].astype(BF16)
    f['w_g'] = jnp.pad(w_in[:, A_SHIFT_W + 4 * B_W:], ((0, 0), (0, LANES - 2 * B_HEADS))).astype(BF16)
    wl = jnp.zeros((A_LORA_W, 3 * A_W), F32)
    wl = wl.at[:A_DECAY_LORA, :A_W].set(W['rw_w2'][0])
    wl = wl.at[A_DECAY_LORA:A_DECAY_LORA + A_AAA_LORA, A_W:2 * A_W].set(W['rw_a2'][0])
    wl = wl.at[A_DECAY_LORA + A_AAA_LORA:, 2 * A_W:].set(W['rw_g2'][0])
    f['w_lora'] = wl.astype(BF16)
    f['rw_vec'] = jnp.concatenate(
        [W['rw_w0'], W['rw_a0'], W['rw_kk'], W['rw_ka'], W['rw_rk'][0].reshape(1, A_W),
         jnp.zeros((3, A_W), F32)], axis=0)
    f['rw_mu'] = W['rw_mu']
    f['rw_ln'] = W['rw_ln']
    f['seg'] = (jnp.arange(A_W)[:, None] // A_HD == jnp.arange(A_W)[None, :] // A_HD).astype(BF16)
    f['cw'] = W['ml_conv_w'][0]
    f['cb'] = W['ml_conv_b']
    gate_bias = jnp.concatenate([W['ml_i_bias'][0], W['ml_f_bias'][0]])
    f['bias_r'] = jnp.pad(gate_bias, (0, LANES - 2 * B_HEADS)).reshape(1, LANES)
    f['bias_c'] = gate_bias.reshape(2 * B_HEADS, 1)
    f['ml_gn'] = W['ml_gn']
    w_out0 = W['mix0_w_out'][0].astype(BF16)
    f['w_out_a'] = w_out0[:A_W]
    f['w_out_b'] = w_out0[A_W:]
    w_ret = W['ret_w_in'][0]
    nq = C_HEADS * C_DK
    f['w_rq'] = w_ret[:, :nq].astype(BF16)
    f['w_rk'] = w_ret[:, nq:2 * nq].astype(BF16)
    f['w_rv'] = w_ret[:, 2 * nq:2 * nq + C_OUT_W].astype(BF16)
    f['w_rg'] = w_ret[:, 2 * nq + C_OUT_W:].astype(BF16)
    f['ret_gn'] = W['ret_gn']
    f['w_ret_out'] = W['ret_w_out'][0].astype(BF16)
    for name in ('ffn_up', 'ffn_down', 'ple_gate', 'ple_proj'):
        f[name] = W[name].astype(BF16)
    f['norm_g'] = W['norm_g']
    return f


def _rope_tables(pos):
    half = C_DK // 2
    inv = ROPE_BASE ** (-jnp.arange(half, dtype=F32) / half)
    ang = pos.astype(F32)[:, None] * inv[None, :]
    return jnp.cos(ang), jnp.sin(ang)


def _even_layer(h, p, st, f, bsz, t_len):
    n = bsz * t_len
    seq = st is None
    tm = 512 if seq else n
    ua, ub, gates = _inproj(h, f['norm_g'][0], [f['w_a'], f['w_b'], f['w_g']], [F32, F32, F32], tm)
    qk_raw = ub.reshape(bsz, t_len, 4 * B_W)[:, :, :2 * B_W]

    if seq:
        ya, s_a_new = _rw_chunk(ua, f['rw_mu'], f['w_lora'], f['rw_vec'], f['rw_ln'], bsz, t_len, rt=256)
        conv_new = qk_raw[:, -(B_CONV - 1):]
    else:
        shift, s_a, conv, c0, n0, m0 = st
        conv_new = jnp.concatenate([conv, qk_raw], axis=1)[:, -(B_CONV - 1):]
        outs = _rw_prep(ua, shift, f['rw_mu'], f['w_lora'], f['rw_vec'], f['seg'], n)
        r, w, k, v, a, b, g, bonus = outs
        as3 = lambda x: x.reshape(1, n, A_W)
        y, s_a_new = _rw_rec(as3(r), as3(w), as3(k), as3(v), as3(a), as3(b), s_a, tb=1, bb=SUBLANES)
        ya = _rw_post(y.reshape(n, A_W), bonus, g, f['rw_ln'], f['seg'], n)

    if seq:
        gates3 = gates.reshape(bsz, t_len, LANES)
        gates_t = gates3[:, :, :SUBLANES].transpose(0, 2, 1)
        yb, c_new, n_new, m_new = _mlstm_chunk(ub.reshape(bsz, t_len, 4 * B_W), gates3, gates_t, f['cw'],
                                               f['cb'], f['bias_r'], f['bias_c'], f['ml_gn'], bsz, t_len,
                                               nb=2)
        yb = yb.reshape(n, B_W)
        n_new = n_new[:, :B_HEADS]
        m_new = m_new[:, :B_HEADS, 0]
    else:
        m0p = jnp.pad(m0, ((0, 0), (0, LANES - B_HEADS)))
        yb, c_new, n_new, m_new = _mlstm_step(ub, gates, conv.reshape(bsz, -1), f['cw'], f['cb'],
                                              f['bias_r'], f['ml_gn'], c0, n0.reshape(bsz, B_W), m0p,
                                              bb=SUBLANES)
        n_new = n_new.reshape(bsz, B_HEADS, B_HD)
        m_new = m_new[:, :B_HEADS]

    h_next = _post(h, [ya, yb], [f['w_out_a'], f['w_out_b']], p, f['norm_g'], f['ffn_up'],
                   f['ffn_down'], f['ple_gate'], f['ple_proj'], li=0, tm=tm, tf=POST_TF)

    shift_new = ua.reshape(bsz, t_len, A_SHIFT_W)[:, -1]
    return h_next, (shift_new, s_a_new, conv_new, c_new, n_new, m_new)


def _odd_layer(h, p, s0, pos, f, bsz, t_len):
    n = bsz * t_len
    seq = s0 is None
    cos, sin = _rope_tables(pos)
    tm = 256 if seq else n
    ws = [f['w_rq'], f['w_rk'], f['w_rv'], f['w_rg']]
    if seq:
        q, qd, k, kd, v, g = _inproj_ret(h, f['norm_g'][1], f['ret_gn'], cos, sin, ws, tm, t_len // tm,
                                         [BF16] * 5 + [F32], decayed=True)
        as3 = lambda a: a.reshape(bsz, t_len, a.shape[-1])
        y, s_new = _ret_chunk(as3(q), as3(qd), as3(k), as3(kd), as3(v), as3(g), bsz, t_len, nb=1)
        y = y.reshape(n, C_OUT_W)
    else:
        q, k, v, g = _inproj_ret(h, f['norm_g'][1], f['ret_gn'], cos, sin, ws, tm, 1, [F32] * 4,
                                 decayed=False)
        y, s_new = _ret_step(q, k, v, g, s0, bb=SUBLANES)
    h_next = _post(h, [y], [f['w_ret_out']], p, f['norm_g'], f['ffn_up'], f['ffn_down'],
                   f['ple_gate'], f['ple_proj'], li=1, tm=512 if seq else n, tf=POST_TF)
    return h_next, s_new


def _trunk(x, p, pos, ev_states, od_state, f):
    bsz, t_len, _ = x.shape
    n = bsz * t_len
    h = x.reshape(n, D_MODEL)
    ev_in = None if ev_states is None else tuple(s[0] for s in ev_states)
    od_in = None if od_state is None else od_state[0]
    p = p.reshape(p.shape[0], n, D_PLE)
    h, ev_new = _even_layer(h, p, ev_in, f, bsz, t_len)
    h, od_new = _odd_layer(h, p, od_in, pos, f, bsz, t_len)
    return h.reshape(bsz, t_len, D_MODEL), [s[None] for s in ev_new], od_new[None]


def kernel(x_prompt, x_sample, state_rwkv_shift, state_rwkv_S, state_mlstm_conv, state_mlstm_C,
           state_mlstm_n, state_mlstm_m, state_ret_S, p_prompt, p_sample, norm_g, ffn_up, ffn_down,
           ple_gate, ple_proj, mix0_w_in, rw_mu, rw_w0, rw_w2, rw_a0, rw_a2, rw_g2, rw_kk, rw_ka,
           rw_rk, rw_ln, ml_conv_w, ml_conv_b, ml_i_bias, ml_f_bias, ml_gn, mix0_w_out,
           ret_w_in, ret_gn, ret_w_out):
    W = dict(norm_g=norm_g, ffn_up=ffn_up, ffn_down=ffn_down, ple_gate=ple_gate, ple_proj=ple_proj,
             mix0_w_in=mix0_w_in, rw_mu=rw_mu, rw_w0=rw_w0, rw_w2=rw_w2, rw_a0=rw_a0, rw_a2=rw_a2,
             rw_g2=rw_g2, rw_kk=rw_kk, rw_ka=rw_ka, rw_rk=rw_rk, rw_ln=rw_ln, ml_conv_w=ml_conv_w,
             ml_conv_b=ml_conv_b, ml_i_bias=ml_i_bias, ml_f_bias=ml_f_bias, ml_gn=ml_gn,
             mix0_w_out=mix0_w_out, ret_w_in=ret_w_in, ret_gn=ret_gn, ret_w_out=ret_w_out)
    f = _prep_weights(W)
    bp, tp = x_prompt.shape[:2]
    ts = x_sample.shape[1]
    y_prompt, ev_p, ret_p = _trunk(x_prompt, p_prompt, jnp.arange(tp), None, None, f)
    ev_s_in = (state_rwkv_shift, state_rwkv_S, state_mlstm_conv, state_mlstm_C, state_mlstm_n,
               state_mlstm_m)
    y_sample, ev_s, ret_s = _trunk(x_sample, p_sample, PAST_LEN + jnp.arange(ts), ev_s_in, state_ret_S, f)
    return (y_prompt, y_sample, *ev_p, ret_p, *ev_s, ret_s)
```

```python
import functools
import math

import jax
import jax.numpy as jnp
from jax import lax
from jax.experimental import pallas as pl
from jax.experimental.pallas import tpu as pltpu

F32 = jnp.float32
BF16 = jnp.bfloat16

D_MODEL = 1024
D_PLE = 256
D_FF = 4 * D_MODEL
RMS_EPS = 1e-6
GN_EPS = 1e-5
CHUNK = 128
RET_CHUNK = 256

A_HEADS = 8
A_HD = 64
A_W = A_HEADS * A_HD
A_DECAY_LORA = 64
A_AAA_LORA = 64
A_GATE_LORA = 128
A_LORA_W = A_DECAY_LORA + A_AAA_LORA + A_GATE_LORA
A_SHIFT_W = 3 * A_W + A_LORA_W
A_LN_EPS = 64e-5

B_HEADS = 4
B_HD = 128
B_W = B_HEADS * B_HD
B_CONV = 4

C_HEADS = 4
C_DK = 256
C_DV = 512
C_OUT_W = C_HEADS * C_DV
ROPE_BASE = 10000.0
PAST_LEN = 16384

LANES = 128
SUBLANES = 8
VMEM_LIMIT = 56 * 1024 * 1024
POST_TF = 1024


def _cparams(*sem):
    return pltpu.CompilerParams(dimension_semantics=sem, vmem_limit_bytes=VMEM_LIMIT)


def _const_spec(shape):
    nd = len(shape)
    return pl.BlockSpec(shape, lambda *_: (0,) * nd)


def _rms(x, g):
    return x * lax.rsqrt(jnp.mean(x * x, axis=-1, keepdims=True) + RMS_EPS) * g


def _dot(a, b):
    return jnp.dot(a, b, preferred_element_type=F32)


def _bdot(a, b):
    return jnp.dot(a.astype(BF16), b.astype(BF16), preferred_element_type=F32)


def _dot_nt(a, b):
    return lax.dot_general(a, b, (((1,), (1,)), ((), ())), preferred_element_type=F32)


def _dot_tn(a, b):
    return lax.dot_general(a, b, (((0,), (0,)), ((), ())), preferred_element_type=F32)


def _hilo(x):
    hi = x.astype(BF16)
    lo = (x - hi.astype(F32)).astype(BF16)
    return hi, lo


def _dot_sel(x, sel):
    hi, lo = _hilo(x)
    return _dot(hi, sel) + _dot(lo, sel)


def _dot_sel_left(sel, x):
    hi, lo = _hilo(x)
    return _dot(sel, hi) + _dot(sel, lo)


def _dot3(a, b, dot):
    ah, al = _hilo(a)
    bh, bl = _hilo(b)
    return dot(ah, bh) + dot(ah, bl) + dot(al, bh)


def _sigmoid(x):
    return 1.0 / (1.0 + jnp.exp(-x))


def _softplus(x):
    return jnp.maximum(x, 0.0) + jnp.log1p(jnp.exp(-jnp.abs(x)))


def _log_sigmoid(x):
    return -_softplus(-x)


def _silu(x):
    return x * _sigmoid(x)


def _lane_col(x, idx):
    lane = lax.broadcasted_iota(jnp.int32, x.shape, 1)
    return jnp.sum(jnp.where(lane == idx, x, 0.0), axis=1, keepdims=True)


def _row_pick(x, idx):
    row = lax.broadcasted_iota(jnp.int32, x.shape, 0)
    return jnp.sum(jnp.where(row == idx, x, 0.0), axis=0, keepdims=True)


def _seg_matrix(n, seg):
    sh = seg.bit_length() - 1
    r = jnp.right_shift(lax.broadcasted_iota(jnp.int32, (n, n), 0), sh)
    c = jnp.right_shift(lax.broadcasted_iota(jnp.int32, (n, n), 1), sh)
    return (r == c).astype(BF16)


def _inproj_kernel(x_ref, g_ref, *refs):
    n = len(refs) // 2
    xn = _rms(x_ref[...], g_ref[0:1, :]).astype(BF16)
    for w_ref, o_ref in zip(refs[:n], refs[n:]):
        o_ref[...] = _dot(xn, w_ref[...]).astype(o_ref.dtype)


def _inproj(h, g, ws, dtypes, tm):
    n = h.shape[0]
    return pl.pallas_call(
        _inproj_kernel,
        grid=(n // tm,),
        in_specs=[pl.BlockSpec((tm, D_MODEL), lambda i: (i, 0)), _const_spec(g.shape)]
        + [_const_spec(w.shape) for w in ws],
        out_specs=[pl.BlockSpec((tm, w.shape[1]), lambda i: (i, 0)) for w in ws],
        out_shape=[jax.ShapeDtypeStruct((n, w.shape[1]), dt) for w, dt in zip(ws, dtypes)],
        compiler_params=_cparams("parallel"),
        name="inproj",
    )(h, g, *ws)


def _inproj_ret_kernel(x_ref, g_ref, gn_ref, cos_ref, sin_ref, wq_ref, wk_ref, wv_ref, wg_ref, *out_refs,
                       decayed):
    if decayed:
        q_ref, qd_ref, k_ref, kd_ref, v_ref, gate_ref = out_refs
    else:
        q_ref, k_ref, v_ref, gate_ref = out_refs
        qd_ref = kd_ref = None
    xn = _rms(x_ref[...], g_ref[0:1, :]).astype(BF16)
    cos = cos_ref[...]
    sin = sin_ref[...]
    half = C_DK // 2
    tm = x_ref.shape[0]
    row0 = pl.program_id(0) * tm
    pos = jnp.bitwise_and(lax.broadcasted_iota(jnp.int32, (tm, half), 0) + row0, RET_CHUNK - 1).astype(F32)
    gate_ref[...] = (_silu(_dot(xn, wg_ref[...])) * gn_ref[...]).astype(gate_ref.dtype)
    for w_ref, o_ref, d_ref, scale in ((wq_ref, q_ref, qd_ref, None), (wk_ref, k_ref, kd_ref, C_DK ** -0.5)):
        u = _dot(xn, w_ref[...])
        for hd in range(C_HEADS):
            x1 = u[:, hd * C_DK:hd * C_DK + half]
            x2 = u[:, hd * C_DK + half:(hd + 1) * C_DK]
            y1 = x1 * cos - x2 * sin
            y2 = x1 * sin + x2 * cos
            if scale is not None:
                y1 = y1 * scale
                y2 = y2 * scale
            o_ref[:, hd * C_DK:hd * C_DK + half] = y1.astype(o_ref.dtype)
            o_ref[:, hd * C_DK + half:(hd + 1) * C_DK] = y2.astype(o_ref.dtype)
            if decayed:
                lg = _ret_log_gamma(hd)
                dec = jnp.exp((pos + 1.0) * lg) if d_ref is qd_ref else jnp.exp((RET_CHUNK - 1.0 - pos) * lg)
                d_ref[:, hd * C_DK:hd * C_DK + half] = (y1 * dec).astype(d_ref.dtype)
                d_ref[:, hd * C_DK + half:(hd + 1) * C_DK] = (y2 * dec).astype(d_ref.dtype)
    v_ref[...] = _dot(xn, wv_ref[...]).astype(v_ref.dtype)


def _inproj_ret(h, g, gn, cos, sin, ws, tm, seq_tiles, dtypes, decayed):
    n = h.shape[0]
    if cos.shape[0] == 1:
        cs_spec = _const_spec(cos.shape)
    else:
        cs_spec = pl.BlockSpec((tm, cos.shape[1]), lambda i: (i % seq_tiles, 0))
    widths = [w.shape[1] for w in ws]
    if decayed:
        widths = [widths[0], widths[0], widths[1], widths[1], widths[2], widths[3]]
    return pl.pallas_call(
        functools.partial(_inproj_ret_kernel, decayed=decayed),
        grid=(n // tm,),
        in_specs=[pl.BlockSpec((tm, D_MODEL), lambda i: (i, 0)), _const_spec(g.shape), _const_spec(gn.shape),
                  cs_spec, cs_spec] + [_const_spec(w.shape) for w in ws],
        out_specs=[pl.BlockSpec((tm, wd), lambda i: (i, 0)) for wd in widths],
        out_shape=[jax.ShapeDtypeStruct((n, wd), dt) for wd, dt in zip(widths, dtypes)],
        compiler_params=_cparams("parallel"),
        name="inproj_ret",
    )(h, g, gn, cos, sin, *ws)


def _rw_prep_kernel(x_ref, sh_ref, mu_ref, wl_ref, vec_ref, seg_ref,
                    r_ref, w_ref, k_ref, v_ref, a_ref, b_ref, g_ref, bonus_ref, carry):
    rows = x_ref.shape[0]
    nb = sh_ref.shape[0]

    @pl.when(pl.program_id(0) == 0)
    def _():
        carry[...] = sh_ref[...]

    x = x_ref[...]
    if rows == nb:
        prev = carry[...]
    else:
        prev = jnp.concatenate([carry[...], x[:rows - nb]], axis=0)
    carry[...] = x[rows - nb:]
    xs = x + (prev - x) * mu_ref[...]
    r = xs[:, :A_W]
    k = xs[:, A_W:2 * A_W]
    v = xs[:, 2 * A_W:3 * A_W]
    lo = xs[:, 3 * A_W:]
    lane = lax.broadcasted_iota(jnp.int32, lo.shape, 1)
    act = jnp.where(lane < A_DECAY_LORA, jnp.tanh(lo),
                    jnp.where(lane < A_DECAY_LORA + A_AAA_LORA, lo, _sigmoid(lo)))
    lora = _dot(act.astype(BF16), wl_ref[...])
    w0, a0, kkw, kaw, rkw = (vec_ref[i:i + 1, :] for i in range(5))
    w_log = -_softplus(-(w0 + lora[:, :A_W])) - 0.5
    decay = jnp.exp(-jnp.exp(w_log))
    a = _sigmoid(a0 + lora[:, A_W:2 * A_W])
    g = lora[:, 2 * A_W:]
    seg = seg_ref[...]
    kk = k * kkw
    kk = kk / jnp.maximum(jnp.sqrt(_dot_sel(kk * kk, seg)), 1e-12)
    k2 = k * (1.0 + (a - 1.0) * kaw)
    bonus = _dot_sel(r * k2 * rkw, seg) * v
    r_ref[...] = r
    w_ref[...] = decay
    k_ref[...] = k2
    v_ref[...] = v
    a_ref[...] = -kk
    b_ref[...] = kk * a
    g_ref[...] = g
    bonus_ref[...] = bonus


def _rw_prep(ua_tm, shift, mu, wl, vec, seg, rows):
    n = ua_tm.shape[0]
    nb = shift.shape[0]
    row_spec = lambda w: pl.BlockSpec((rows, w), lambda i: (i, 0))
    return pl.pallas_call(
        _rw_prep_kernel,
        grid=(n // rows,),
        in_specs=[row_spec(A_SHIFT_W), _const_spec(shift.shape), _const_spec(mu.shape),
                  _const_spec(wl.shape), _const_spec(vec.shape), _const_spec(seg.shape)],
        out_specs=[row_spec(A_W)] * 8,
        out_shape=[jax.ShapeDtypeStruct((n, A_W), F32)] * 8,
        scratch_shapes=[pltpu.VMEM((nb, A_SHIFT_W), F32)],
        compiler_params=_cparams("arbitrary"),
        name="rw_prep",
    )(ua_tm, shift, mu, wl, vec, seg)


RW_HALF = 2 * LANES


def _rw_rec_kernel(r_ref, w_ref, k_ref, v_ref, a_ref, b_ref, s0_ref, y_ref, so_ref,
                   s_scr, lhs_scr, red_scr, snb_scr):
    tb, bb, _ = r_ref.shape
    nq = A_W // RW_HALF
    units = bb * nq
    j = pl.program_id(1)

    @pl.when(j == 0)
    def _():
        for bi in range(bb):
            for hd in range(A_HEADS):
                s_scr[bi, :, hd * A_HD:(hd + 1) * A_HD] = s0_ref[bi, hd]

    row = lax.broadcasted_iota(jnp.int32, (A_HD, RW_HALF), 0)
    col = lax.broadcasted_iota(jnp.int32, (A_HD, RW_HALF), 1)
    key = jnp.bitwise_and(col, A_HD - 1)
    on_diag = key == row
    off_diag = key == jnp.bitwise_and(row + 1, A_HD - 1)
    seg = _seg_matrix(RW_HALF, A_HD)
    head_of_lane = jnp.right_shift(lax.broadcasted_iota(jnp.int32, (A_HEADS, A_W), 1), 6)
    own_head = head_of_lane == lax.broadcasted_iota(jnp.int32, (A_HEADS, A_W), 0)
    first_key = jnp.bitwise_and(lax.broadcasted_iota(jnp.int32, (bb, A_W), 1), A_HD - 1) == 0

    def step(t, carry):
        rt, wt, kt, vt, at, bt = (ref[t] for ref in (r_ref, w_ref, k_ref, v_ref, a_ref, b_ref))
        v_hi = vt.astype(BF16).astype(F32)
        v_lo = vt - v_hi
        v_lo = jnp.concatenate(
            [jnp.where(first_key[:, :LANES],
                       pltpu.roll(v_lo[:, c * LANES:(c + 1) * LANES], LANES - (A_HD - 1), axis=1),
                       pltpu.roll(v_lo[:, c * LANES:(c + 1) * LANES], 1, axis=1))
             for c in range(A_W // LANES)], axis=1)
        for bi in range(bb):
            pa = (s_scr[bi] * at[bi:bi + 1]).astype(BF16)
            for q in range(nq):
                u = bi * nq + q
                sl = slice(q * RW_HALF, (q + 1) * RW_HALF)
                lhs_scr[u * A_HD:(u + 1) * A_HD, :] = pa[:, sl]
                dv = jnp.where(on_diag, v_hi[bi:bi + 1, sl], jnp.where(off_diag, v_lo[bi:bi + 1, sl], 0.0))
                lhs_scr[(units + u) * A_HD:(units + u + 1) * A_HD, :] = dv.astype(BF16)
        red_scr[...] = _dot(lhs_scr[...], seg)
        for bi in range(bb):
            for q in range(nq):
                u = bi * nq + q
                sl = slice(q * RW_HALF, (q + 1) * RW_HALF)
                sa = red_scr[u * A_HD:(u + 1) * A_HD, :]
                vb = red_scr[(units + u) * A_HD:(units + u + 1) * A_HD, :]
                sn = s_scr[bi, :, sl] * wt[bi:bi + 1, sl] + sa * bt[bi:bi + 1, sl] + vb * kt[bi:bi + 1, sl]
                s_scr[bi, :, sl] = sn
                snb_scr[bi, :, sl] = sn.astype(BF16)
        for bi in range(bb):
            r_heads = jnp.where(own_head, jnp.broadcast_to(rt[bi:bi + 1], (A_HEADS, A_W)), 0.0)
            y_ref[t, bi] = _dot_nt(r_heads.astype(BF16), snb_scr[bi])
        return carry

    lax.fori_loop(0, tb, step, 0)

    @pl.when(j == pl.num_programs(1) - 1)
    def _():
        for bi in range(bb):
            for hd in range(A_HEADS):
                so_ref[bi, hd] = s_scr[bi, :, hd * A_HD:(hd + 1) * A_HD]


def _rw_rec(r, w, k, v, a, b, s0, tb, bb):
    t_len, bsz, _ = r.shape
    seq_spec = pl.BlockSpec((tb, bb, A_W), lambda i, j: (j, i, 0))
    st_spec = pl.BlockSpec((bb, A_HEADS, A_HD, A_HD), lambda i, j: (i, 0, 0, 0))
    units = bb * (A_W // RW_HALF)
    return pl.pallas_call(
        _rw_rec_kernel,
        grid=(bsz // bb, t_len // tb),
        in_specs=[seq_spec] * 6 + [st_spec],
        out_specs=[pl.BlockSpec((tb, bb, A_HEADS, A_HD), lambda i, j: (j, i, 0, 0)), st_spec],
        out_shape=[jax.ShapeDtypeStruct((t_len, bsz, A_HEADS, A_HD), F32),
                   jax.ShapeDtypeStruct(s0.shape, F32)],
        scratch_shapes=[pltpu.VMEM((bb, A_HD, A_W), F32),
                        pltpu.VMEM((2 * units * A_HD, RW_HALF), BF16),
                        pltpu.VMEM((2 * units * A_HD, RW_HALF), F32),
                        pltpu.VMEM((bb, A_HD, A_W), BF16)],
        compiler_params=_cparams("parallel", "arbitrary"),
        name="rw_rec",
    )(r, w, k, v, a, b, s0)


RW_C = 64
(F_RT, F_PM1, F_IP, F_DC, F_KH, F_KC, F_V, F_KK, F_ASIG, F_RKK, F_G, F_PC) = range(12)


def _rw_chunk_kernel(x_ref, mu_ref, wl_ref, vec_ref, ln_ref, o_ref, so_ref, carry, s_scr, f_scr):
    rt = x_ref.shape[0]
    i = pl.program_id(1)

    @pl.when(i == 0)
    def _():
        carry[...] = jnp.zeros_like(carry)
        s_scr[...] = jnp.zeros_like(s_scr)

    x = x_ref[...]
    row = lax.broadcasted_iota(jnp.int32, x.shape, 0)
    prev = jnp.where(row == 0, carry[...], pltpu.roll(x, 1, axis=0))
    carry[...] = x[rt - 1:rt, :]
    xs = x + (prev - x) * mu_ref[...]
    r = xs[:, :A_W]
    k = xs[:, A_W:2 * A_W]
    lo = xs[:, 3 * A_W:]
    lane = lax.broadcasted_iota(jnp.int32, lo.shape, 1)
    act = jnp.where(lane < A_DECAY_LORA, jnp.tanh(lo),
                    jnp.where(lane < A_DECAY_LORA + A_AAA_LORA, lo, _sigmoid(lo)))
    lora = _dot(act.astype(BF16), wl_ref[...])
    w0, a0, kkw, kaw, rkw = (vec_ref[n:n + 1, :] for n in range(5))
    lw = -math.exp(-0.5) * _sigmoid(w0 + lora[:, :A_W])
    a_sig = _sigmoid(a0 + lora[:, A_W:2 * A_W])
    k2 = k * (1.0 + (a_sig - 1.0) * kaw)

    ri = lax.broadcasted_iota(jnp.int32, (rt, rt), 0)
    ci = lax.broadcasted_iota(jnp.int32, (rt, rt), 1)
    same = jnp.right_shift(ri, 6) == jnp.right_shift(ci, 6)
    chunk_ones = same.astype(BF16)
    chunk_tri = jnp.where(ci <= ri, chunk_ones, jnp.zeros_like(chunk_ones))
    lp = _dot_sel_left(chunk_tri, lw)
    lpc = _dot_sel_left(chunk_ones, lw)
    ip = jnp.exp(-lp)
    dc = jnp.exp(lpc - lp)
    f_scr[F_RT] = r * jnp.exp(lp)
    f_scr[F_PM1] = jnp.exp(lp - lw)
    f_scr[F_IP] = ip
    f_scr[F_DC] = dc
    f_scr[F_KH] = k2 * ip
    f_scr[F_KC] = k2 * dc
    f_scr[F_V] = xs[:, 2 * A_W:3 * A_W]
    f_scr[F_KK] = k * kkw
    f_scr[F_ASIG] = a_sig
    f_scr[F_RKK] = r * k2 * rkw
    f_scr[F_G] = lora[:, 2 * A_W:]
    f_scr[F_PC] = jnp.exp(lpc)

    C = RW_C
    W2 = 2 * A_HD
    npairs = A_HEADS // 2
    lane_lo = lax.broadcasted_iota(jnp.int32, (C, W2), 1) < A_HD
    own = (lax.broadcasted_iota(jnp.int32, (2 * C, W2), 1) < A_HD) == \
          (lax.broadcasted_iota(jnp.int32, (2 * C, W2), 0) < C)
    r4 = lax.broadcasted_iota(jnp.int32, (4 * C, 4 * C), 0)
    c4 = lax.broadcasted_iota(jnp.int32, (4 * C, 4 * C), 1)
    keep = jnp.bitwise_and(c4, C - 1) < jnp.bitwise_and(r4, C - 1) + jnp.right_shift(r4, 7)
    eye_f = (lax.broadcasted_iota(jnp.int32, (W2, W2), 0) ==
             lax.broadcasted_iota(jnp.int32, (W2, W2), 1)).astype(F32)
    zeros_ww = jnp.zeros((W2, W2), F32)
    bf = lambda z: z.astype(BF16)
    stack = lambda z: jnp.concatenate([jnp.where(lane_lo, z, 0.0), jnp.where(lane_lo, 0.0, z)], axis=0)
    spread = lambda col: jnp.where(lane_lo, col[:C], col[C:])
    chunks_per_body = rt // C
    probs = [(cc, p) for cc in range(chunks_per_body) for p in range(npairs)]

    def body(j, _):
        rows = [pl.ds(pl.multiple_of((j * chunks_per_body + cc) * C, C), C) for cc in range(chunks_per_body)]
        ld = lambda f, q: f_scr[f, rows[q[0]], q[1] * W2:(q[1] + 1) * W2]
        a_l, r_l, bc_l, kc_l, v_l, g_l = [], [], [], [], [], []
        kk_sq = [jnp.sum(jnp.square(stack(ld(F_KK, q))), axis=-1, keepdims=True) for q in probs]
        bonus_l = [jnp.sum(stack(ld(F_RKK, q)), axis=-1, keepdims=True) for q in probs]
        for e, q in enumerate(probs):
            kk = ld(F_KK, q) * spread(1.0 / jnp.maximum(jnp.sqrt(kk_sq[e]), 1e-12))
            b = kk * ld(F_ASIG, q)
            a_l.append(stack(-kk * ld(F_PM1, q)))
            r_l.append(stack(ld(F_RT, q)))
            bc_l.append(stack(b * ld(F_DC, q)))
            kc_l.append(stack(ld(F_KC, q)))
            v_l.append(stack(ld(F_V, q)))
            ar = jnp.concatenate([a_l[-1], r_l[-1]], axis=0)
            bk = jnp.concatenate([stack(b * ld(F_IP, q)), stack(ld(F_KH, q))], axis=0)
            g_l.append(jnp.where(keep, _dot_nt(bf(ar), bf(bk)), 0.0))
        n = range(len(probs))
        x_l = [g_l[e][:W2, :W2] for e in n]
        t_l = [eye_f + x_l[e] for e in n]
        x_l = [_dot(bf(x_l[e]), bf(x_l[e])) for e in n]
        m = 2
        while m < C:
            last = 2 * m >= C
            for e in n:
                rhs = t_l[e] if last else jnp.concatenate([x_l[e], t_l[e]], axis=1)
                prod = _dot(bf(x_l[e]), bf(rhs))
                if last:
                    t_l[e] = t_l[e] + prod
                else:
                    x_l[e] = prod[:, :W2]
                    t_l[e] = t_l[e] + prod[:, W2:]
            m *= 2
        lv_l = [_dot(bf(g_l[e][:W2, W2:]), bf(v_l[e])) for e in n]
        apu_l = [_dot(bf(t_l[e]), bf(jnp.concatenate([a_l[e], lv_l[e]], axis=1))) for e in n]
        qy_l = []
        for e in n:
            low = jnp.concatenate([zeros_ww, v_l[e]], axis=1)
            qy_l.append(_dot(bf(g_l[e][W2:, :]), bf(jnp.concatenate([apu_l[e], low], axis=0))))
        mm_l = [_dot_tn(bf(apu_l[e][:, :W2]), bf(bc_l[e])) for e in n]
        n_l = []
        for e in n:
            uv = jnp.concatenate([apu_l[e][:, W2:], v_l[e]], axis=0)
            n_l.append(_dot_tn(bf(uv), bf(jnp.concatenate([bc_l[e], kc_l[e]], axis=0))))
        y_l = []
        for e, q in enumerate(probs):
            p = q[1]
            s0 = s_scr[p]
            s0b = bf(s0)
            y_l.append(_dot_nt(bf(r_l[e] + qy_l[e][:, :W2]), s0b) + qy_l[e][:, W2:])
            s_scr[p] = s0 * ld(F_PC, q)[0:1, :] + _dot(s0b, bf(mm_l[e])) + n_l[e]
        mu_l = [jnp.sum(y_l[e], axis=-1, keepdims=True) * (1.0 / A_HD) for e in n]
        d_l = [jnp.where(own, y_l[e] - mu_l[e], 0.0) for e in n]
        var_l = [jnp.sum(d_l[e] * d_l[e], axis=-1, keepdims=True) * (1.0 / A_HD) for e in n]
        for e, q in enumerate(probs):
            p = q[1]
            d = d_l[e] * lax.rsqrt(var_l[e] + A_LN_EPS)
            yn = (d[:C] + d[C:]) * ln_ref[:, p * W2:(p + 1) * W2]
            bonus = spread(bonus_l[e]) * ld(F_V, q)
            o_ref[rows[q[0]], p * W2:(p + 1) * W2] = ((yn + bonus) * ld(F_G, q)).astype(o_ref.dtype)
        return 0

    lax.fori_loop(0, rt // (C * chunks_per_body), body, 0)

    @pl.when(i == pl.num_programs(1) - 1)
    def _():
        for p in range(npairs):
            so_ref[0, 2 * p] = s_scr[p, :A_HD, :A_HD]
            so_ref[0, 2 * p + 1] = s_scr[p, A_HD:, A_HD:]


def _rw_chunk(ua, mu, wl, vec, ln, bsz, t_len, rt):
    nt = t_len // rt
    st_spec = pl.BlockSpec((1, A_HEADS, A_HD, A_HD), lambda b, i: (b, 0, 0, 0))
    return pl.pallas_call(
        _rw_chunk_kernel,
        grid=(bsz, nt),
        in_specs=[pl.BlockSpec((rt, A_SHIFT_W), lambda b, i: (b * nt + i, 0)),
                  _const_spec(mu.shape), _const_spec(wl.shape), _const_spec(vec.shape),
                  _const_spec(ln.shape)],
        out_specs=[pl.BlockSpec((rt, A_W), lambda b, i: (b * nt + i, 0)), st_spec],
        out_shape=[jax.ShapeDtypeStruct((bsz * t_len, A_W), BF16),
                   jax.ShapeDtypeStruct((bsz, A_HEADS, A_HD, A_HD), F32)],
        scratch_shapes=[pltpu.VMEM((1, A_SHIFT_W), F32),
                        pltpu.VMEM((A_HEADS // 2, 2 * A_HD, 2 * A_HD), F32),
                        pltpu.VMEM((12, rt, A_W), F32)],
        compiler_params=_cparams("parallel", "arbitrary"),
        name="rw_chunk",
    )(ua, mu, wl, vec, ln)


def _rw_post_kernel(y_ref, bonus_ref, g_ref, ln_ref, seg_ref, o_ref):
    seg = seg_ref[...]
    y = y_ref[...]
    mu = _dot_sel(y, seg) * (1.0 / A_HD)
    d = y - mu
    var = _dot_sel(d * d, seg) * (1.0 / A_HD)
    yn = d * lax.rsqrt(var + A_LN_EPS) * ln_ref[...]
    o_ref[...] = ((yn + bonus_ref[...]) * g_ref[...]).astype(o_ref.dtype)


def _rw_post(y, bonus, g, ln, seg, rows):
    n = y.shape[0]
    row_spec = pl.BlockSpec((rows, A_W), lambda i: (i, 0))
    return pl.pallas_call(
        _rw_post_kernel,
        grid=(n // rows,),
        in_specs=[row_spec] * 3 + [_const_spec(ln.shape), _const_spec(seg.shape)],
        out_specs=row_spec,
        out_shape=jax.ShapeDtypeStruct((n, A_W), BF16),
        compiler_params=_cparams("parallel"),
        name="rw_post",
    )(y, bonus, g, ln, seg)


def _head_norm_lanes(x, g, eps):
    mu = jnp.mean(x, axis=-1, keepdims=True)
    d = x - mu
    var = jnp.mean(d * d, axis=-1, keepdims=True)
    return d * lax.rsqrt(var + eps) * g


def _mlstm_chunk_kernel(ub_ref, g_ref, gt_ref, cw_ref, cb_ref, biasr_ref, biasc_ref, gn_ref,
                        y_ref, co_ref, no_ref, mo_ref,
                        tail_s, c_s, n_s, m_s):
    nb, L, _ = ub_ref.shape
    c = pl.program_id(1)

    @pl.when(c == 0)
    def _():
        tail_s[...] = jnp.zeros_like(tail_s)
        c_s[...] = jnp.zeros_like(c_s)
        n_s[...] = jnp.zeros_like(n_s)
        m_s[...] = jnp.zeros_like(m_s)

    r_i = lax.broadcasted_iota(jnp.int32, (L, L), 0)
    c_i = lax.broadcasted_iota(jnp.int32, (L, L), 1)
    causal = c_i <= r_i
    tri = causal.astype(BF16)
    tri_u = (r_i <= c_i).astype(BF16)
    heads = range(B_HEADS)
    sls = [slice(hd * B_HD, (hd + 1) * B_HD) for hd in heads]

    seqs = range(nb)
    ub_l, q_l, k_l, gc_l, gr_l, bcols_l, brows_l = [], [], [], [], [], [], []
    for bi in seqs:
        ub = ub_ref[bi]
        raw = ub[:, :2 * B_W]
        ext = jnp.concatenate([tail_s[bi], raw], axis=0)
        tail_s[bi] = raw[L - SUBLANES:]
        acc = cb_ref[...] + raw * cw_ref[B_CONV - 1:B_CONV, :]
        for jj in range(B_CONV - 1):
            sh = B_CONV - 1 - jj
            acc = acc + pltpu.roll(ext, sh, axis=0)[SUBLANES:] * cw_ref[jj:jj + 1, :]
        act = _silu(acc)
        ub_l.append(ub)
        q_l.append(act[:, :B_W])
        k_l.append(act[:, B_W:] * (B_HD ** -0.5))
        gc = g_ref[bi] + biasr_ref[...]
        gr = gt_ref[bi] + biasc_ref[...]
        gc_l.append(gc)
        gr_l.append(gr)
        bcols_l.append(_dot_sel_left(tri, _log_sigmoid(gc)))
        brows_l.append(_dot_sel(_log_sigmoid(gr), tri_u))
    m_all = [m_s[bi] for bi in seqs]
    n_all = [n_s[bi] for bi in seqs]

    probs = [(bi, hd) for bi in seqs for hd in heads]
    qb = [q_l[bi][:, sls[hd]].astype(BF16) for bi, hd in probs]
    vb = [ub_l[bi][:, 2 * B_W + hd * B_HD:2 * B_W + (hd + 1) * B_HD].astype(BF16) for bi, hd in probs]
    qk = [_dot_nt(qb[e], k_l[bi][:, sls[hd]].astype(BF16)) for e, (bi, hd) in enumerate(probs)]
    qc = [_dot(qb[e], c_s[bi * B_HEADS + hd].astype(BF16)) for e, (bi, hd) in enumerate(probs)]
    m_prev = [m_all[bi][hd:hd + 1, 0:1] for bi, hd in probs]
    n_prev = [n_all[bi][hd:hd + 1, :] for bi, hd in probs]
    bc = [_lane_col(bcols_l[bi], B_HEADS + hd) for bi, hd in probs]
    ic = [_lane_col(gc_l[bi], hd) for bi, hd in probs]
    dlog, inter, b_last, gi = [], [], [], []
    for e, (bi, hd) in enumerate(probs):
        br = brows_l[bi][B_HEADS + hd:B_HEADS + hd + 1, :]
        ir = gr_l[bi][hd:hd + 1, :]
        dlog.append(jnp.where(causal, bc[e] - br + ir, -jnp.inf))
        inter.append(bc[e] + m_prev[e])
        b_last.append(_row_pick(bc[e], L - 1))
        gi.append(b_last[e] - bc[e] + ic[e])
    row_max = [jnp.max(dlog[e], axis=1, keepdims=True) for e in range(len(probs))]
    gi_max = [jnp.max(gi[e], axis=0, keepdims=True) for e in range(len(probs))]
    s_l, iw, m_t, wc, kw, m_new = [], [], [], [], [], []
    for e, (bi, hd) in enumerate(probs):
        m_t.append(jnp.maximum(inter[e], row_max[e]))
        s_l.append(qk[e] * jnp.exp(dlog[e] - m_t[e]))
        iw.append(jnp.exp(inter[e] - m_t[e]))
        m_new.append(jnp.maximum(b_last[e] + m_prev[e], gi_max[e]))
        wc.append(jnp.exp(b_last[e] + m_prev[e] - m_new[e]))
        kw.append(k_l[bi][:, sls[hd]] * jnp.exp(gi[e] - m_new[e]))
    sv = [_dot(s_l[e].astype(BF16), vb[e]) for e in range(len(probs))]
    kv = [_dot_tn(kw[e].astype(BF16), vb[e]) for e in range(len(probs))]
    s_sum = [jnp.sum(s_l[e], axis=1, keepdims=True) for e in range(len(probs))]
    qn = [jnp.sum(q_l[bi][:, sls[hd]] * n_prev[e], axis=1, keepdims=True) for e, (bi, hd) in enumerate(probs)]
    k_sum = [jnp.sum(kw[e], axis=0, keepdims=True) for e in range(len(probs))]
    hb = []
    for e, (bi, hd) in enumerate(probs):
        den = s_sum[e] + iw[e] * qn[e]
        hh = (sv[e] + iw[e] * qc[e]) / jnp.maximum(jnp.abs(den), jnp.exp(-m_t[e]))
        hb.append(_sigmoid(ub_l[bi][:, 3 * B_W + hd * B_HD:3 * B_W + (hd + 1) * B_HD]) * hh)
        c_s[bi * B_HEADS + hd] = wc[e] * c_s[bi * B_HEADS + hd] + kv[e]
    mean = [jnp.sum(hb[e], axis=-1, keepdims=True) * (1.0 / B_HD) for e in range(len(probs))]
    dev = [hb[e] - mean[e] for e in range(len(probs))]
    var = [jnp.sum(dev[e] * dev[e], axis=-1, keepdims=True) * (1.0 / B_HD) for e in range(len(probs))]
    for e, (bi, hd) in enumerate(probs):
        y_ref[bi, :, sls[hd]] = (dev[e] * lax.rsqrt(var[e] + GN_EPS) * gn_ref[:, sls[hd]]).astype(y_ref.dtype)
    pad_rows = jnp.zeros((SUBLANES - B_HEADS, LANES), F32)
    for bi in seqs:
        es = [e for e, (bj, _) in enumerate(probs) if bj == bi]
        n_s[bi] = jnp.concatenate([wc[e] * n_prev[e] + k_sum[e] for e in es] + [pad_rows], axis=0)
        m_s[bi] = jnp.concatenate([jnp.broadcast_to(m_new[e], (1, LANES)) for e in es] + [pad_rows], axis=0)

    @pl.when(c == pl.num_programs(1) - 1)
    def _():
        for bi in range(nb):
            for hd in heads:
                co_ref[bi, hd] = c_s[bi * B_HEADS + hd]
        no_ref[...] = n_s[...]
        mo_ref[...] = m_s[...]


def _mlstm_chunk(ub, gates, gates_t, cw, cb, bias_r, bias_c, gn, bsz, t_len, nb):
    L = CHUNK
    seq = lambda w: pl.BlockSpec((nb, L, w), lambda b, c: (b, c, 0))
    per_b = lambda shp: pl.BlockSpec((nb,) + shp, lambda b, c: (b,) + (0,) * len(shp))
    return pl.pallas_call(
        _mlstm_chunk_kernel,
        grid=(bsz // nb, t_len // L),
        in_specs=[seq(4 * B_W), seq(LANES),
                  pl.BlockSpec((nb, SUBLANES, L), lambda b, c: (b, 0, c)),
                  _const_spec(cw.shape), _const_spec(cb.shape), _const_spec(bias_r.shape),
                  _const_spec(bias_c.shape), _const_spec(gn.shape)],
        out_specs=[seq(B_W), per_b((B_HEADS, B_HD, B_HD)), per_b((SUBLANES, B_HD)),
                   per_b((SUBLANES, LANES))],
        out_shape=[jax.ShapeDtypeStruct((bsz, t_len, B_W), BF16),
                   jax.ShapeDtypeStruct((bsz, B_HEADS, B_HD, B_HD), F32),
                   jax.ShapeDtypeStruct((bsz, SUBLANES, B_HD), F32),
                   jax.ShapeDtypeStruct((bsz, SUBLANES, LANES), F32)],
        scratch_shapes=[pltpu.VMEM((nb, SUBLANES, 2 * B_W), F32),
                        pltpu.VMEM((nb * B_HEADS, B_HD, B_HD), F32),
                        pltpu.VMEM((nb, SUBLANES, B_HD), F32),
                        pltpu.VMEM((nb, SUBLANES, LANES), F32)],
        compiler_params=_cparams("parallel", "arbitrary"),
        name="mlstm_chunk",
    )(ub, gates, gates_t, cw, cb, bias_r, bias_c, gn)


def _mlstm_step_kernel(ub_ref, g_ref, conv_ref, cw_ref, cb_ref, biasr_ref, gn_ref, c_ref, n_ref, m_ref,
                       y_ref, co_ref, no_ref, mo_ref):
    bb = ub_ref.shape[0]
    ub = ub_ref[...]
    raw = ub[:, :2 * B_W]
    conv = conv_ref[...]
    acc = cb_ref[...] + raw * cw_ref[B_CONV - 1:B_CONV, :]
    for jj in range(B_CONV - 1):
        acc = acc + conv[:, jj * 2 * B_W:(jj + 1) * 2 * B_W] * cw_ref[jj:jj + 1, :]
    qk = _silu(acc)
    q_all = qk[:, :B_W]
    k_all = qk[:, B_W:] * (B_HD ** -0.5)
    gc = g_ref[...] + biasr_ref[...]
    lf = _log_sigmoid(gc)
    m_all = m_ref[...]
    row_pad = lax.broadcasted_iota(jnp.int32, (LANES, LANES), 0)
    pad = jnp.zeros((LANES - bb, B_HD), F32)
    m_out = jnp.zeros((bb, LANES), F32)
    lane_m = lax.broadcasted_iota(jnp.int32, (bb, LANES), 1)
    for hd in range(B_HEADS):
        sl = slice(hd * B_HD, (hd + 1) * B_HD)
        q = q_all[:, sl]
        k = k_all[:, sl]
        v = ub[:, 2 * B_W + hd * B_HD:2 * B_W + (hd + 1) * B_HD]
        o = ub[:, 3 * B_W + hd * B_HD:3 * B_W + (hd + 1) * B_HD]
        ic = _lane_col(gc, hd)
        f = _lane_col(lf, B_HEADS + hd)
        m_prev = _lane_col(m_all, hd)
        n_prev = n_ref[:, sl]
        m_t = jnp.maximum(f + m_prev, ic)
        s = jnp.sum(q * k, axis=1, keepdims=True) * jnp.exp(ic - m_t)
        iw = jnp.exp(f + m_prev - m_t)
        wi = jnp.exp(ic - m_t)
        qb = q.astype(BF16)
        kw_t = jnp.concatenate([k * wi, pad], axis=0).T
        v_pad = jnp.concatenate([v, pad], axis=0)
        qc = jnp.zeros((bb, B_HD), F32)
        row_b = lax.broadcasted_iota(jnp.int32, (bb, B_HD), 0)
        for bi in range(bb):
            c_prev = c_ref[bi, hd]
            qc = qc + jnp.where(row_b == bi, _dot(qb, c_prev.astype(BF16)), 0.0)
            v_one = jnp.where(row_pad == bi, v_pad, 0.0)
            co_ref[bi, hd] = _row_pick(iw, bi) * c_prev + _dot3(kw_t, v_one, _dot)
        num = s * v + iw * qc
        den = s + iw * jnp.sum(q * n_prev, axis=1, keepdims=True)
        hh = num / jnp.maximum(jnp.abs(den), jnp.exp(-m_t))
        no_ref[:, sl] = iw * n_prev + wi * k
        m_out = jnp.where(lane_m == hd, m_t, m_out)
        hb = _sigmoid(o) * hh
        y_ref[:, sl] = _head_norm_lanes(hb, gn_ref[:, sl], GN_EPS).astype(y_ref.dtype)
    mo_ref[...] = m_out


def _mlstm_step(ub, gates, conv, cw, cb, bias_r, gn, c0, n0, m0, bb):
    bsz = ub.shape[0]
    row = lambda w: pl.BlockSpec((bb, w), lambda i: (i, 0))
    c_spec = pl.BlockSpec((bb, B_HEADS, B_HD, B_HD), lambda i: (i, 0, 0, 0))
    return pl.pallas_call(
        _mlstm_step_kernel,
        grid=(bsz // bb,),
        in_specs=[row(4 * B_W), row(LANES), row((B_CONV - 1) * 2 * B_W),
                  _const_spec(cw.shape), _const_spec(cb.shape), _const_spec(bias_r.shape),
                  _const_spec(gn.shape), c_spec, row(B_W), row(LANES)],
        out_specs=[row(B_W), c_spec, row(B_W), row(LANES)],
        out_shape=[jax.ShapeDtypeStruct((bsz, B_W), BF16),
                   jax.ShapeDtypeStruct(c0.shape, F32),
                   jax.ShapeDtypeStruct((bsz, B_W), F32),
                   jax.ShapeDtypeStruct((bsz, LANES), F32)],
        compiler_params=_cparams("parallel"),
        name="mlstm_step",
    )(ub, gates, conv, cw, cb, bias_r, gn, c0, n0, m0)


def _ret_log_gamma(hd):
    return math.log1p(-(2.0 ** (-5.0 - hd)))


def _ret_chunk_kernel(q_ref, qd_ref, k_ref, kd_ref, v_ref, g_ref, y_ref, so_ref, s_s, mask_s):
    nb, L, _ = q_ref.shape
    c = pl.program_id(1)

    @pl.when(c == 0)
    def _():
        s_s[...] = jnp.zeros_like(s_s)
        r_i = lax.broadcasted_iota(jnp.int32, (L, L), 0)
        c_i = lax.broadcasted_iota(jnp.int32, (L, L), 1)
        diff = (r_i - c_i).astype(F32)
        for hd in range(C_HEADS):
            mask_s[hd] = jnp.where(diff >= 0, jnp.exp(jnp.maximum(diff, 0.0) * _ret_log_gamma(hd)), 0.0)

    heads = range(C_HEADS)
    kls = [slice(hd * C_DK, (hd + 1) * C_DK) for hd in heads]
    sls = [slice(hd * C_DV, (hd + 1) * C_DV) for hd in heads]
    probs = [(bi, hd) for bi in range(nb) for hd in heads]
    n = range(len(probs))

    sc = [(_dot_nt(q_ref[bi, :, kls[hd]], k_ref[bi, :, kls[hd]]) * mask_s[hd]).astype(BF16) for bi, hd in probs]
    o_l = []
    for e, (bi, hd) in enumerate(probs):
        sl = sls[hd]
        c_dec = math.exp(L * _ret_log_gamma(hd))
        s_prev = s_s[e]
        o_l.append(_dot(sc[e], v_ref[bi, :, sl]) + _dot(qd_ref[bi, :, kls[hd]], s_prev.astype(BF16)))
        s_s[e] = c_dec * s_prev + _dot_tn(kd_ref[bi, :, kls[hd]], v_ref[bi, :, sl])
    mean = [jnp.sum(o_l[e], axis=-1, keepdims=True) * (1.0 / C_DV) for e in n]
    dev = [o_l[e] - mean[e] for e in n]
    var = [jnp.sum(dev[e] * dev[e], axis=-1, keepdims=True) * (1.0 / C_DV) for e in n]
    for e, (bi, hd) in enumerate(probs):
        sl = sls[hd]
        y_ref[bi, :, sl] = (dev[e] * lax.rsqrt(var[e] + GN_EPS) * g_ref[bi, :, sl]).astype(y_ref.dtype)

    @pl.when(c == pl.num_programs(1) - 1)
    def _():
        for e, (bi, hd) in enumerate(probs):
            so_ref[bi, hd] = s_s[e]


def _ret_chunk(q, qd, k, kd, v, g, bsz, t_len, nb):
    L = RET_CHUNK
    seq = lambda w: pl.BlockSpec((nb, L, w), lambda b, c: (b, c, 0))
    st = pl.BlockSpec((nb, C_HEADS, C_DK, C_DV), lambda b, c: (b, 0, 0, 0))
    return pl.pallas_call(
        _ret_chunk_kernel,
        grid=(bsz // nb, t_len // L),
        in_specs=[seq(C_HEADS * C_DK)] * 4 + [seq(C_OUT_W), seq(C_OUT_W)],
        out_specs=[seq(C_OUT_W), st],
        out_shape=[jax.ShapeDtypeStruct((bsz, t_len, C_OUT_W), BF16),
                   jax.ShapeDtypeStruct((bsz, C_HEADS, C_DK, C_DV), F32)],
        scratch_shapes=[pltpu.VMEM((nb * C_HEADS, C_DK, C_DV), F32), pltpu.VMEM((C_HEADS, L, L), F32)],
        compiler_params=_cparams("parallel", "arbitrary"),
        name="ret_chunk",
    )(q, qd, k, kd, v, g)


def _ret_step_kernel(q_ref, k_ref, v_ref, g_ref, s_ref, y_ref, so_ref):
    bb = q_ref.shape[0]
    hd = pl.program_id(1)
    gam = 1.0 - jnp.exp2(-5.0 - jnp.full((1, 1), hd).astype(F32))
    q = q_ref[...]
    k = k_ref[...]
    v = v_ref[...]
    s = jnp.sum(q * k, axis=1, keepdims=True)
    pad = jnp.zeros((LANES - bb, C_DK), F32)
    k_t = jnp.concatenate([k, pad], axis=0).T
    q_t = jnp.concatenate([q * gam, pad], axis=0).T
    row_b = lax.broadcasted_iota(jnp.int32, (bb, C_DV), 0)
    qs = jnp.zeros((bb, C_DV), F32)
    for bi in range(bb):
        s_prev = s_ref[bi, 0]
        qs = jnp.where(row_b == bi, jnp.sum(_lane_col(q_t, bi) * s_prev, axis=0, keepdims=True), qs)
        so_ref[bi, 0] = gam * s_prev + _lane_col(k_t, bi) * v[bi:bi + 1, :]
    o = s * v + qs
    y = _head_norm_lanes(o, g_ref[...], GN_EPS)
    y_ref[...] = y.astype(y_ref.dtype)


def _ret_step(q, k, v, g, s0, bb):
    bsz = q.shape[0]
    st = pl.BlockSpec((bb, 1, C_DK, C_DV), lambda i, h: (i, h, 0, 0))
    return pl.pallas_call(
        _ret_step_kernel,
        grid=(bsz // bb, C_HEADS),
        in_specs=[pl.BlockSpec((bb, C_DK), lambda i, h: (i, h)),
                  pl.BlockSpec((bb, C_DK), lambda i, h: (i, h)),
                  pl.BlockSpec((bb, C_DV), lambda i, h: (i, h)),
                  pl.BlockSpec((bb, C_DV), lambda i, h: (i, h)), st],
        out_specs=[pl.BlockSpec((bb, C_DV), lambda i, h: (i, h)), st],
        out_shape=[jax.ShapeDtypeStruct((bsz, C_OUT_W), BF16),
                   jax.ShapeDtypeStruct(s0.shape, F32)],
        compiler_params=_cparams("parallel", "parallel"),
        name="ret_step",
    )(q, k, v, g, s0)


def _post_kernel(*refs, n_mix):
    h_ref = refs[0]
    mix_refs = refs[1:1 + n_mix]
    wout_refs = refs[1 + n_mix:1 + 2 * n_mix]
    (p_ref, g_ref, up_ref, down_ref, gate_ref, proj_ref, o_ref, h1_s, xn_s, acc_s) = refs[1 + 2 * n_mix:]
    f = pl.program_id(1)

    @pl.when(f == 0)
    def _():
        mix = _dot(mix_refs[0][...], wout_refs[0][...])
        for m_ref, w_ref in zip(mix_refs[1:], wout_refs[1:]):
            mix = mix + _dot(m_ref[...], w_ref[...])
        h1 = h_ref[...] + _rms(mix, g_ref[1:2, :])
        h1_s[...] = h1
        xn_s[...] = _rms(h1, g_ref[2:3, :]).astype(BF16)
        acc_s[...] = jnp.zeros_like(acc_s)

    hid = jnp.square(jnp.maximum(_dot(xn_s[...], up_ref[...]), 0.0))
    acc_s[...] += _dot(hid.astype(BF16), down_ref[...])

    @pl.when(f == pl.num_programs(1) - 1)
    def _():
        h2 = h1_s[...] + _rms(acc_s[...], g_ref[3:4, :])
        gate = _sigmoid(_dot(h2.astype(BF16), gate_ref[...]))
        emb = _dot(p_ref[...].astype(BF16), proj_ref[...])
        o_ref[...] = h2 + gate * emb


def _post(h, mixes, wouts, p, g, up, down, gate, proj, li, tm, tf):
    n = h.shape[0]
    n_mix = len(mixes)
    row = lambda w: pl.BlockSpec((tm, w), lambda i, f: (i, 0))
    layer = lambda a: pl.BlockSpec((None,) + a.shape[1:], lambda i, f: (li,) + (0,) * (a.ndim - 1))
    return pl.pallas_call(
        functools.partial(_post_kernel, n_mix=n_mix),
        grid=(n // tm, D_FF // tf),
        in_specs=[row(D_MODEL)] + [row(m.shape[1]) for m in mixes]
        + [_const_spec(w.shape) for w in wouts]
        + [pl.BlockSpec((None, tm, D_PLE), lambda i, f: (li, i, 0)), layer(g),
           pl.BlockSpec((None, D_MODEL, tf), lambda i, f: (li, 0, f)),
           pl.BlockSpec((None, tf, D_MODEL), lambda i, f: (li, f, 0)),
           layer(gate), layer(proj)],
        out_specs=row(D_MODEL),
        out_shape=jax.ShapeDtypeStruct((n, D_MODEL), F32),
        scratch_shapes=[pltpu.VMEM((tm, D_MODEL), F32), pltpu.VMEM((tm, D_MODEL), BF16),
                        pltpu.VMEM((tm, D_MODEL), F32)],
        compiler_params=_cparams("parallel", "arbitrary"),
        name="post",
    )(h, *mixes, *wouts, p, g, up, down, gate, proj)


def _prep_weights(W):
    f = {}
    w_in = W['mix0_w_in'][0]
    f['w_a'] = w_in[:, :A_SHIFT_W].astype(BF16)
    f['w_b'] = w_in[:, A_SHIFT_W:A_SHIFT_W + 4 * B_W].astype(BF16)
    f['w_g'] = jnp.pad(w_in[:, A_SHIFT_W + 4 * B_W:], ((0, 0), (0, LANES - 2 * B_HEADS))).astype(BF16)
    wl = jnp.zeros((A_LORA_W, 3 * A_W), F32)
    wl = wl.at[:A_DECAY_LORA, :A_W].set(W['rw_w2'][0])
    wl = wl.at[A_DECAY_LORA:A_DECAY_LORA + A_AAA_LORA, A_W:2 * A_W].set(W['rw_a2'][0])
    wl = wl.at[A_DECAY_LORA + A_AAA_LORA:, 2 * A_W:].set(W['rw_g2'][0])
    f['w_lora'] = wl.astype(BF16)
    f['rw_vec'] = jnp.concatenate(
        [W['rw_w0'], W['rw_a0'], W['rw_kk'], W['rw_ka'], W['rw_rk'][0].reshape(1, A_W),
         jnp.zeros((3, A_W), F32)], axis=0)
    f['rw_mu'] = W['rw_mu']
    f['rw_ln'] = W['rw_ln']
    f['seg'] = (jnp.arange(A_W)[:, None] // A_HD == jnp.arange(A_W)[None, :] // A_HD).astype(BF16)
    f['cw'] = W['ml_conv_w'][0]
    f['cb'] = W['ml_conv_b']
    gate_bias = jnp.concatenate([W['ml_i_bias'][0], W['ml_f_bias'][0]])
    f['bias_r'] = jnp.pad(gate_bias, (0, LANES - 2 * B_HEADS)).reshape(1, LANES)
    f['bias_c'] = gate_bias.reshape(2 * B_HEADS, 1)
    f['ml_gn'] = W['ml_gn']
    w_out0 = W['mix0_w_out'][0].astype(BF16)
    f['w_out_a'] = w_out0[:A_W]
    f['w_out_b'] = w_out0[A_W:]
    w_ret = W['ret_w_in'][0]
    nq = C_HEADS * C_DK
    f['w_rq'] = w_ret[:, :nq].astype(BF16)
    f['w_rk'] = w_ret[:, nq:2 * nq].astype(BF16)
    f['w_rv'] = w_ret[:, 2 * nq:2 * nq + C_OUT_W].astype(BF16)
    f['w_rg'] = w_ret[:, 2 * nq + C_OUT_W:].astype(BF16)
    f['ret_gn'] = W['ret_gn']
    f['w_ret_out'] = W['ret_w_out'][0].astype(BF16)
    for name in ('ffn_up', 'ffn_down', 'ple_gate', 'ple_proj'):
        f[name] = W[name].astype(BF16)
    f['norm_g'] = W['norm_g']
    return f


def _rope_tables(pos):
    half = C_DK // 2
    inv = ROPE_BASE ** (-jnp.arange(half, dtype=F32) / half)
    ang = pos.astype(F32)[:, None] * inv[None, :]
    return jnp.cos(ang), jnp.sin(ang)


def _even_layer(h, p, st, f, bsz, t_len):
    n = bsz * t_len
    seq = st is None
    tm = 512 if seq else n
    ua, ub, gates = _inproj(h, f['norm_g'][0], [f['w_a'], f['w_b'], f['w_g']], [F32, F32, F32], tm)
    qk_raw = ub.reshape(bsz, t_len, 4 * B_W)[:, :, :2 * B_W]

    if seq:
        ya, s_a_new = _rw_chunk(ua, f['rw_mu'], f['w_lora'], f['rw_vec'], f['rw_ln'], bsz, t_len, rt=256)
        conv_new = qk_raw[:, -(B_CONV - 1):]
    else:
        shift, s_a, conv, c0, n0, m0 = st
        conv_new = jnp.concatenate([conv, qk_raw], axis=1)[:, -(B_CONV - 1):]
        outs = _rw_prep(ua, shift, f['rw_mu'], f['w_lora'], f['rw_vec'], f['seg'], n)
        r, w, k, v, a, b, g, bonus = outs
        as3 = lambda x: x.reshape(1, n, A_W)
        y, s_a_new = _rw_rec(as3(r), as3(w), as3(k), as3(v), as3(a), as3(b), s_a, tb=1, bb=SUBLANES)
        ya = _rw_post(y.reshape(n, A_W), bonus, g, f['rw_ln'], f['seg'], n)

    if seq:
        gates3 = gates.reshape(bsz, t_len, LANES)
        gates_t = gates3[:, :, :SUBLANES].transpose(0, 2, 1)
        yb, c_new, n_new, m_new = _mlstm_chunk(ub.reshape(bsz, t_len, 4 * B_W), gates3, gates_t, f['cw'],
                                               f['cb'], f['bias_r'], f['bias_c'], f['ml_gn'], bsz, t_len,
                                               nb=2)
        yb = yb.reshape(n, B_W)
        n_new = n_new[:, :B_HEADS]
        m_new = m_new[:, :B_HEADS, 0]
    else:
        m0p = jnp.pad(m0, ((0, 0), (0, LANES - B_HEADS)))
        yb, c_new, n_new, m_new = _mlstm_step(ub, gates, conv.reshape(bsz, -1), f['cw'], f['cb'],
                                              f['bias_r'], f['ml_gn'], c0, n0.reshape(bsz, B_W), m0p,
                                              bb=SUBLANES)
        n_new = n_new.reshape(bsz, B_HEADS, B_HD)
        m_new = m_new[:, :B_HEADS]

    h_next = _post(h, [ya, yb], [f['w_out_a'], f['w_out_b']], p, f['norm_g'], f['ffn_up'],
                   f['ffn_down'], f['ple_gate'], f['ple_proj'], li=0, tm=tm, tf=POST_TF)

    shift_new = ua.reshape(bsz, t_len, A_SHIFT_W)[:, -1]
    return h_next, (shift_new, s_a_new, conv_new, c_new, n_new, m_new)


def _odd_layer(h, p, s0, pos, f, bsz, t_len):
    n = bsz * t_len
    seq = s0 is None
    cos, sin = _rope_tables(pos)
    tm = 256 if seq else n
    ws = [f['w_rq'], f['w_rk'], f['w_rv'], f['w_rg']]
    if seq:
        q, qd, k, kd, v, g = _inproj_ret(h, f['norm_g'][1], f['ret_gn'], cos, sin, ws, tm, t_len // tm,
                                         [BF16] * 5 + [F32], decayed=True)
        as3 = lambda a: a.reshape(bsz, t_len, a.shape[-1])
        y, s_new = _ret_chunk(as3(q), as3(qd), as3(k), as3(kd), as3(v), as3(g), bsz, t_len, nb=1)
        y = y.reshape(n, C_OUT_W)
    else:
        q, k, v, g = _inproj_ret(h, f['norm_g'][1], f['ret_gn'], cos, sin, ws, tm, 1, [F32] * 4,
                                 decayed=False)
        y, s_new = _ret_step(q, k, v, g, s0, bb=SUBLANES)
    h_next = _post(h, [y], [f['w_ret_out']], p, f['norm_g'], f['ffn_up'], f['ffn_down'],
                   f['ple_gate'], f['ple_proj'], li=1, tm=512 if seq else n, tf=POST_TF)
    return h_next, s_new


def _trunk(x, p, pos, ev_states, od_state, f):
    bsz, t_len, _ = x.shape
    n = bsz * t_len
    h = x.reshape(n, D_MODEL)
    ev_in = None if ev_states is None else tuple(s[0] for s in ev_states)
    od_in = None if od_state is None else od_state[0]
    p = p.reshape(p.shape[0], n, D_PLE)
    h, ev_new = _even_layer(h, p, ev_in, f, bsz, t_len)
    h, od_new = _odd_layer(h, p, od_in, pos, f, bsz, t_len)
    return h.reshape(bsz, t_len, D_MODEL), [s[None] for s in ev_new], od_new[None]


def kernel(x_prompt, x_sample, state_rwkv_shift, state_rwkv_S, state_mlstm_conv, state_mlstm_C,
           state_mlstm_n, state_mlstm_m, state_ret_S, p_prompt, p_sample, norm_g, ffn_up, ffn_down,
           ple_gate, ple_proj, mix0_w_in, rw_mu, rw_w0, rw_w2, rw_a0, rw_a2, rw_g2, rw_kk, rw_ka,
           rw_rk, rw_ln, ml_conv_w, ml_conv_b, ml_i_bias, ml_f_bias, ml_gn, mix0_w_out,
           ret_w_in, ret_gn, ret_w_out):
    W = dict(norm_g=norm_g, ffn_up=ffn_up, ffn_down=ffn_down, ple_gate=ple_gate, ple_proj=ple_proj,
             mix0_w_in=mix0_w_in, rw_mu=rw_mu, rw_w0=rw_w0, rw_w2=rw_w2, rw_a0=rw_a0, rw_a2=rw_a2,
             rw_g2=rw_g2, rw_kk=rw_kk, rw_ka=rw_ka, rw_rk=rw_rk, rw_ln=rw_ln, ml_conv_w=ml_conv_w,
             ml_conv_b=ml_conv_b, ml_i_bias=ml_i_bias, ml_f_bias=ml_f_bias, ml_gn=ml_gn,
             mix0_w_out=mix0_w_out, ret_w_in=ret_w_in, ret_gn=ret_gn, ret_w_out=ret_w_out)
    f = _prep_weights(W)
    bp, tp = x_prompt.shape[:2]
    ts = x_sample.shape[1]
    y_prompt, ev_p, ret_p = _trunk(x_prompt, p_prompt, jnp.arange(tp), None, None, f)
    ev_s_in = (state_rwkv_shift, state_rwkv_S, state_mlstm_conv, state_mlstm_C, state_mlstm_n,
               state_mlstm_m)
    y_sample, ev_s, ret_s = _trunk(x_sample, p_sample, PAST_LEN + jnp.arange(ts), ev_s_in, state_ret_S, f)
    return (y_prompt, y_sample, *ev_p, ret_p, *ev_s, ret_s)
```

```python
import functools
import math

import jax
import jax.numpy as jnp
from jax import lax
from jax.experimental import pallas as pl
from jax.experimental.pallas import tpu as pltpu

F32 = jnp.float32
BF16 = jnp.bfloat16

D_MODEL = 1024
D_PLE = 256
D_FF = 4 * D_MODEL
RMS_EPS = 1e-6
GN_EPS = 1e-5
CHUNK = 128
RET_CHUNK = 256

A_HEADS = 8
A_HD = 64
A_W = A_HEADS * A_HD
A_DECAY_LORA = 64
A_AAA_LORA = 64
A_GATE_LORA = 128
A_LORA_W = A_DECAY_LORA + A_AAA_LORA + A_GATE_LORA
A_SHIFT_W = 3 * A_W + A_LORA_W
A_LN_EPS = 64e-5

B_HEADS = 4
B_HD = 128
B_W = B_HEADS * B_HD
B_CONV = 4

C_HEADS = 4
C_DK = 256
C_DV = 512
C_OUT_W = C_HEADS * C_DV
ROPE_BASE = 10000.0
PAST_LEN = 16384

LANES = 128
SUBLANES = 8
VMEM_LIMIT = 56 * 1024 * 1024
POST_TF = 1024


def _cparams(*sem):
    return pltpu.CompilerParams(dimension_semantics=sem, vmem_limit_bytes=VMEM_LIMIT)


def _const_spec(shape):
    nd = len(shape)
    return pl.BlockSpec(shape, lambda *_: (0,) * nd, pipeline_mode=pl.Buffered(1))


def _rms(x, g):
    return x * lax.rsqrt(jnp.mean(x * x, axis=-1, keepdims=True) + RMS_EPS) * g


def _dot(a, b):
    return jnp.dot(a, b, preferred_element_type=F32)


def _bdot(a, b):
    return jnp.dot(a.astype(BF16), b.astype(BF16), preferred_element_type=F32)


def _dot_nt(a, b):
    return lax.dot_general(a, b, (((1,), (1,)), ((), ())), preferred_element_type=F32)


def _dot_tn(a, b):
    return lax.dot_general(a, b, (((0,), (0,)), ((), ())), preferred_element_type=F32)


def _hilo(x):
    hi = x.astype(BF16)
    lo = (x - hi.astype(F32)).astype(BF16)
    return hi, lo


def _dot_sel(x, sel):
    hi, lo = _hilo(x)
    return _dot(hi, sel) + _dot(lo, sel)


def _dot_sel_left(sel, x):
    hi, lo = _hilo(x)
    return _dot(sel, hi) + _dot(sel, lo)


def _dot3(a, b, dot):
    ah, al = _hilo(a)
    bh, bl = _hilo(b)
    return dot(ah, bh) + dot(ah, bl) + dot(al, bh)


def _sigmoid(x):
    return 1.0 / (1.0 + jnp.exp(-x))


def _softplus(x):
    return jnp.maximum(x, 0.0) + jnp.log1p(jnp.exp(-jnp.abs(x)))


def _log_sigmoid(x):
    return -_softplus(-x)


def _silu(x):
    return x * _sigmoid(x)


def _lane_col(x, idx):
    lane = lax.broadcasted_iota(jnp.int32, x.shape, 1)
    return jnp.sum(jnp.where(lane == idx, x, 0.0), axis=1, keepdims=True)


def _row_pick(x, idx):
    row = lax.broadcasted_iota(jnp.int32, x.shape, 0)
    return jnp.sum(jnp.where(row == idx, x, 0.0), axis=0, keepdims=True)


def _seg_matrix(n, seg):
    sh = seg.bit_length() - 1
    r = jnp.right_shift(lax.broadcasted_iota(jnp.int32, (n, n), 0), sh)
    c = jnp.right_shift(lax.broadcasted_iota(jnp.int32, (n, n), 1), sh)
    return (r == c).astype(BF16)


def _inproj_kernel(x_ref, g_ref, *refs):
    n = len(refs) // 2
    xn = _rms(x_ref[...], g_ref[0:1, :]).astype(BF16)
    for w_ref, o_ref in zip(refs[:n], refs[n:]):
        o_ref[...] = _dot(xn, w_ref[...]).astype(o_ref.dtype)


def _inproj(h, g, ws, dtypes, tm):
    n = h.shape[0]
    return pl.pallas_call(
        _inproj_kernel,
        grid=(n // tm,),
        in_specs=[pl.BlockSpec((tm, D_MODEL), lambda i: (i, 0)), _const_spec(g.shape)]
        + [_const_spec(w.shape) for w in ws],
        out_specs=[pl.BlockSpec((tm, w.shape[1]), lambda i: (i, 0)) for w in ws],
        out_shape=[jax.ShapeDtypeStruct((n, w.shape[1]), dt) for w, dt in zip(ws, dtypes)],
        compiler_params=_cparams("parallel"),
        name="inproj",
    )(h, g, *ws)


def _inproj_ret_kernel(x_ref, g_ref, gn_ref, cos_ref, sin_ref, wq_ref, wk_ref, wv_ref, wg_ref, *out_refs,
                       decayed):
    if decayed:
        q_ref, qd_ref, k_ref, kd_ref, v_ref, gate_ref = out_refs
    else:
        q_ref, k_ref, v_ref, gate_ref = out_refs
        qd_ref = kd_ref = None
    xn = _rms(x_ref[...], g_ref[0:1, :]).astype(BF16)
    cos = cos_ref[...]
    sin = sin_ref[...]
    half = C_DK // 2
    tm = x_ref.shape[0]
    row0 = pl.program_id(0) * tm
    pos = jnp.bitwise_and(lax.broadcasted_iota(jnp.int32, (tm, half), 0) + row0, RET_CHUNK - 1).astype(F32)
    gate_ref[...] = (_silu(_dot(xn, wg_ref[...])) * gn_ref[...]).astype(gate_ref.dtype)
    for w_ref, o_ref, d_ref, scale in ((wq_ref, q_ref, qd_ref, None), (wk_ref, k_ref, kd_ref, C_DK ** -0.5)):
        u = _dot(xn, w_ref[...])
        for hd in range(C_HEADS):
            x1 = u[:, hd * C_DK:hd * C_DK + half]
            x2 = u[:, hd * C_DK + half:(hd + 1) * C_DK]
            y1 = x1 * cos - x2 * sin
            y2 = x1 * sin + x2 * cos
            if scale is not None:
                y1 = y1 * scale
                y2 = y2 * scale
            o_ref[:, hd * C_DK:hd * C_DK + half] = y1.astype(o_ref.dtype)
            o_ref[:, hd * C_DK + half:(hd + 1) * C_DK] = y2.astype(o_ref.dtype)
            if decayed:
                lg = _ret_log_gamma(hd)
                dec = jnp.exp((pos + 1.0) * lg) if d_ref is qd_ref else jnp.exp((RET_CHUNK - 1.0 - pos) * lg)
                d_ref[:, hd * C_DK:hd * C_DK + half] = (y1 * dec).astype(d_ref.dtype)
                d_ref[:, hd * C_DK + half:(hd + 1) * C_DK] = (y2 * dec).astype(d_ref.dtype)
    v_ref[...] = _dot(xn, wv_ref[...]).astype(v_ref.dtype)


def _inproj_ret(h, g, gn, cos, sin, ws, tm, seq_tiles, dtypes, decayed):
    n = h.shape[0]
    if cos.shape[0] == 1:
        cs_spec = _const_spec(cos.shape)
    else:
        cs_spec = pl.BlockSpec((tm, cos.shape[1]), lambda i: (i % seq_tiles, 0))
    widths = [w.shape[1] for w in ws]
    if decayed:
        widths = [widths[0], widths[0], widths[1], widths[1], widths[2], widths[3]]
    return pl.pallas_call(
        functools.partial(_inproj_ret_kernel, decayed=decayed),
        grid=(n // tm,),
        in_specs=[pl.BlockSpec((tm, D_MODEL), lambda i: (i, 0)), _const_spec(g.shape), _const_spec(gn.shape),
                  cs_spec, cs_spec] + [_const_spec(w.shape) for w in ws],
        out_specs=[pl.BlockSpec((tm, wd), lambda i: (i, 0)) for wd in widths],
        out_shape=[jax.ShapeDtypeStruct((n, wd), dt) for wd, dt in zip(widths, dtypes)],
        compiler_params=_cparams("parallel"),
        name="inproj_ret",
    )(h, g, gn, cos, sin, *ws)


def _rw_prep_kernel(x_ref, sh_ref, mu_ref, wl_ref, vec_ref, seg_ref,
                    r_ref, w_ref, k_ref, v_ref, a_ref, b_ref, g_ref, bonus_ref):
    x = x_ref[...]
    xs = x + (sh_ref[...] - x) * mu_ref[...]
    r = xs[:, :A_W]
    k = xs[:, A_W:2 * A_W]
    v = xs[:, 2 * A_W:3 * A_W]
    lo = xs[:, 3 * A_W:]
    lane = lax.broadcasted_iota(jnp.int32, lo.shape, 1)
    act = jnp.where(lane < A_DECAY_LORA, jnp.tanh(lo),
                    jnp.where(lane < A_DECAY_LORA + A_AAA_LORA, lo, _sigmoid(lo)))
    lora = _dot(act.astype(BF16), wl_ref[...])
    w0, a0, kkw, kaw, rkw = (vec_ref[i:i + 1, :] for i in range(5))
    w_log = -_softplus(-(w0 + lora[:, :A_W])) - 0.5
    decay = jnp.exp(-jnp.exp(w_log))
    a = _sigmoid(a0 + lora[:, A_W:2 * A_W])
    g = lora[:, 2 * A_W:]
    seg = seg_ref[...]
    kk = k * kkw
    kk = kk / jnp.maximum(jnp.sqrt(_dot_sel(kk * kk, seg)), 1e-12)
    k2 = k * (1.0 + (a - 1.0) * kaw)
    bonus = _dot_sel(r * k2 * rkw, seg) * v
    r_ref[...] = r.T
    w_ref[...] = decay.T
    k_ref[...] = k2.T
    v_ref[...] = v.T
    a_ref[...] = (-kk).T
    b_ref[...] = (kk * a).T
    g_ref[...] = g
    bonus_ref[...] = bonus


def _rw_prep(ua, shift, mu, wl, vec, seg):
    n = ua.shape[0]
    full = lambda shape: pl.BlockSpec(shape, lambda i: (0,) * len(shape))
    return pl.pallas_call(
        _rw_prep_kernel,
        grid=(1,),
        in_specs=[full(a.shape) for a in (ua, shift, mu, wl, vec, seg)],
        out_specs=[full((A_W, n))] * 6 + [full((n, A_W))] * 2,
        out_shape=[jax.ShapeDtypeStruct((A_W, n), F32)] * 6 + [jax.ShapeDtypeStruct((n, A_W), F32)] * 2,
        compiler_params=_cparams("arbitrary"),
        name="rw_prep",
    )(ua, shift, mu, wl, vec, seg)


def _rw_step_kernel(r_ref, w_ref, k_ref, v_ref, a_ref, b_ref, s_ref, y_ref, so_ref):
    rt, wt, kt, at, bt = r_ref[...], w_ref[...], k_ref[...], a_ref[...], b_ref[...]
    for v in range(A_HD):
        s = s_ref[0, v]
        sa = jnp.sum(s * at, axis=0, keepdims=True)
        sn = s * wt + sa * bt + v_ref[v:v + 1, :] * kt
        so_ref[0, v] = sn
        y_ref[v:v + 1, :] = jnp.sum(sn * rt, axis=0, keepdims=True)


def _rw_step(r, w, k, v, a, b, s_t):
    bsz = r.shape[1]
    vec = pl.BlockSpec((A_HD, bsz), lambda h: (h, 0))
    st = pl.BlockSpec((1, A_HD, A_HD, bsz), lambda h: (h, 0, 0, 0))
    return pl.pallas_call(
        _rw_step_kernel,
        grid=(A_HEADS,),
        in_specs=[vec] * 6 + [st],
        out_specs=[vec, st],
        out_shape=[jax.ShapeDtypeStruct((A_W, bsz), F32), jax.ShapeDtypeStruct(s_t.shape, F32)],
        compiler_params=_cparams("parallel"),
        name="rw_step",
    )(r, w, k, v, a, b, s_t)


RW_C = 64
(F_RT, F_PM1, F_IP, F_DC, F_KH, F_KC, F_V, F_KK, F_ASIG, F_RKK, F_G, F_PC) = range(12)


def _rw_chunk_kernel(x_ref, mu_ref, wl_ref, vec_ref, ln_ref, o_ref, so_ref, carry, s_scr, f_scr):
    rt = x_ref.shape[0]
    i = pl.program_id(1)

    @pl.when(i == 0)
    def _():
        carry[...] = jnp.zeros_like(carry)
        s_scr[...] = jnp.zeros_like(s_scr)

    x = x_ref[...]
    row = lax.broadcasted_iota(jnp.int32, x.shape, 0)
    prev = jnp.where(row == 0, carry[...], pltpu.roll(x, 1, axis=0))
    carry[...] = x[rt - 1:rt, :]
    xs = x + (prev - x) * mu_ref[...]
    r = xs[:, :A_W]
    k = xs[:, A_W:2 * A_W]
    lo = xs[:, 3 * A_W:]
    lane = lax.broadcasted_iota(jnp.int32, lo.shape, 1)
    act = jnp.where(lane < A_DECAY_LORA, jnp.tanh(lo),
                    jnp.where(lane < A_DECAY_LORA + A_AAA_LORA, lo, _sigmoid(lo)))
    lora = _dot(act.astype(BF16), wl_ref[...])
    w0, a0, kkw, kaw, rkw = (vec_ref[n:n + 1, :] for n in range(5))
    lw = -math.exp(-0.5) * _sigmoid(w0 + lora[:, :A_W])
    a_sig = _sigmoid(a0 + lora[:, A_W:2 * A_W])
    k2 = k * (1.0 + (a_sig - 1.0) * kaw)

    ri = lax.broadcasted_iota(jnp.int32, (rt, rt), 0)
    ci = lax.broadcasted_iota(jnp.int32, (rt, rt), 1)
    same = jnp.right_shift(ri, 6) == jnp.right_shift(ci, 6)
    chunk_ones = same.astype(BF16)
    chunk_tri = jnp.where(ci <= ri, chunk_ones, jnp.zeros_like(chunk_ones))
    lp = _dot_sel_left(chunk_tri, lw)
    lpc = _dot_sel_left(chunk_ones, lw)
    ip = jnp.exp(-lp)
    dc = jnp.exp(lpc - lp)
    f_scr[F_RT] = r * jnp.exp(lp)
    f_scr[F_PM1] = jnp.exp(lp - lw)
    f_scr[F_IP] = ip
    f_scr[F_DC] = dc
    f_scr[F_KH] = k2 * ip
    f_scr[F_KC] = k2 * dc
    f_scr[F_V] = xs[:, 2 * A_W:3 * A_W]
    f_scr[F_KK] = k * kkw
    f_scr[F_ASIG] = a_sig
    f_scr[F_RKK] = r * k2 * rkw
    f_scr[F_G] = lora[:, 2 * A_W:]
    f_scr[F_PC] = jnp.exp(lpc)

    C = RW_C
    W2 = 2 * A_HD
    npairs = A_HEADS // 2
    lane_lo = lax.broadcasted_iota(jnp.int32, (C, W2), 1) < A_HD
    own = (lax.broadcasted_iota(jnp.int32, (2 * C, W2), 1) < A_HD) == \
          (lax.broadcasted_iota(jnp.int32, (2 * C, W2), 0) < C)
    r4 = lax.broadcasted_iota(jnp.int32, (4 * C, 4 * C), 0)
    c4 = lax.broadcasted_iota(jnp.int32, (4 * C, 4 * C), 1)
    keep = jnp.bitwise_and(c4, C - 1) < jnp.bitwise_and(r4, C - 1) + jnp.right_shift(r4, 7)
    eye_f = (lax.broadcasted_iota(jnp.int32, (W2, W2), 0) ==
             lax.broadcasted_iota(jnp.int32, (W2, W2), 1)).astype(F32)
    zeros_ww = jnp.zeros((W2, W2), F32)
    bf = lambda z: z.astype(BF16)
    stack = lambda z: jnp.concatenate([jnp.where(lane_lo, z, 0.0), jnp.where(lane_lo, 0.0, z)], axis=0)
    spread = lambda col: jnp.where(lane_lo, col[:C], col[C:])
    chunks_per_body = rt // C
    probs = [(cc, p) for cc in range(chunks_per_body) for p in range(npairs)]

    def body(j, _):
        rows = [pl.ds(pl.multiple_of((j * chunks_per_body + cc) * C, C), C) for cc in range(chunks_per_body)]
        ld = lambda f, q: f_scr[f, rows[q[0]], q[1] * W2:(q[1] + 1) * W2]
        a_l, r_l, bc_l, kc_l, v_l, g_l = [], [], [], [], [], []
        kk_sq = [jnp.sum(jnp.square(stack(ld(F_KK, q))), axis=-1, keepdims=True) for q in probs]
        bonus_l = [jnp.sum(stack(ld(F_RKK, q)), axis=-1, keepdims=True) for q in probs]
        for e, q in enumerate(probs):
            kk = ld(F_KK, q) * spread(1.0 / jnp.maximum(jnp.sqrt(kk_sq[e]), 1e-12))
            b = kk * ld(F_ASIG, q)
            a_l.append(stack(-kk * ld(F_PM1, q)))
            r_l.append(stack(ld(F_RT, q)))
            bc_l.append(stack(b * ld(F_DC, q)))
            kc_l.append(stack(ld(F_KC, q)))
            v_l.append(stack(ld(F_V, q)))
            ar = jnp.concatenate([a_l[-1], r_l[-1]], axis=0)
            bk = jnp.concatenate([stack(b * ld(F_IP, q)), stack(ld(F_KH, q))], axis=0)
            g_l.append(jnp.where(keep, _dot_nt(bf(ar), bf(bk)), 0.0))
        n = range(len(probs))
        x_l = [g_l[e][:W2, :W2] for e in n]
        t_l = [eye_f + x_l[e] for e in n]
        x_l = [_dot(bf(x_l[e]), bf(x_l[e])) for e in n]
        m = 2
        while m < C:
            last = 2 * m >= C
            for e in n:
                rhs = t_l[e] if last else jnp.concatenate([x_l[e], t_l[e]], axis=1)
                prod = _dot(bf(x_l[e]), bf(rhs))
                if last:
                    t_l[e] = t_l[e] + prod
                else:
                    x_l[e] = prod[:, :W2]
                    t_l[e] = t_l[e] + prod[:, W2:]
            m *= 2
        lv_l = [_dot(bf(g_l[e][:W2, W2:]), bf(v_l[e])) for e in n]
        apu_l = [_dot(bf(t_l[e]), bf(jnp.concatenate([a_l[e], lv_l[e]], axis=1))) for e in n]
        qy_l = []
        for e in n:
            low = jnp.concatenate([zeros_ww, v_l[e]], axis=1)
            qy_l.append(_dot(bf(g_l[e][W2:, :]), bf(jnp.concatenate([apu_l[e], low], axis=0))))
        mm_l = [_dot_tn(bf(apu_l[e][:, :W2]), bf(bc_l[e])) for e in n]
        n_l = []
        for e in n:
            uv = jnp.concatenate([apu_l[e][:, W2:], v_l[e]], axis=0)
            n_l.append(_dot_tn(bf(uv), bf(jnp.concatenate([bc_l[e], kc_l[e]], axis=0))))
        y_l = []
        for e, q in enumerate(probs):
            p = q[1]
            s0 = s_scr[p]
            s0b = bf(s0)
            y_l.append(_dot_nt(bf(r_l[e] + qy_l[e][:, :W2]), s0b) + qy_l[e][:, W2:])
            s_scr[p] = s0 * ld(F_PC, q)[0:1, :] + _dot(s0b, bf(mm_l[e])) + n_l[e]
        mu_l = [jnp.sum(y_l[e], axis=-1, keepdims=True) * (1.0 / A_HD) for e in n]
        d_l = [jnp.where(own, y_l[e] - mu_l[e], 0.0) for e in n]
        var_l = [jnp.sum(d_l[e] * d_l[e], axis=-1, keepdims=True) * (1.0 / A_HD) for e in n]
        for e, q in enumerate(probs):
            p = q[1]
            d = d_l[e] * lax.rsqrt(var_l[e] + A_LN_EPS)
            yn = (d[:C] + d[C:]) * ln_ref[:, p * W2:(p + 1) * W2]
            bonus = spread(bonus_l[e]) * ld(F_V, q)
            o_ref[rows[q[0]], p * W2:(p + 1) * W2] = ((yn + bonus) * ld(F_G, q)).astype(o_ref.dtype)
        return 0

    lax.fori_loop(0, rt // (C * chunks_per_body), body, 0)

    @pl.when(i == pl.num_programs(1) - 1)
    def _():
        for p in range(npairs):
            so_ref[0, 2 * p] = s_scr[p, :A_HD, :A_HD]
            so_ref[0, 2 * p + 1] = s_scr[p, A_HD:, A_HD:]


def _rw_chunk(ua, mu, wl, vec, ln, bsz, t_len, rt):
    nt = t_len // rt
    st_spec = pl.BlockSpec((1, A_HEADS, A_HD, A_HD), lambda b, i: (b, 0, 0, 0))
    return pl.pallas_call(
        _rw_chunk_kernel,
        grid=(bsz, nt),
        in_specs=[pl.BlockSpec((rt, A_SHIFT_W), lambda b, i: (b * nt + i, 0)),
                  _const_spec(mu.shape), _const_spec(wl.shape), _const_spec(vec.shape),
                  _const_spec(ln.shape)],
        out_specs=[pl.BlockSpec((rt, A_W), lambda b, i: (b * nt + i, 0)), st_spec],
        out_shape=[jax.ShapeDtypeStruct((bsz * t_len, A_W), BF16),
                   jax.ShapeDtypeStruct((bsz, A_HEADS, A_HD, A_HD), F32)],
        scratch_shapes=[pltpu.VMEM((1, A_SHIFT_W), F32),
                        pltpu.VMEM((A_HEADS // 2, 2 * A_HD, 2 * A_HD), F32),
                        pltpu.VMEM((12, rt, A_W), F32)],
        compiler_params=_cparams("parallel", "arbitrary"),
        name="rw_chunk",
    )(ua, mu, wl, vec, ln)


def _rw_post_kernel(y_ref, bonus_ref, g_ref, ln_ref, seg_ref, o_ref):
    seg = seg_ref[...]
    y = y_ref[...].T
    mu = _dot_sel(y, seg) * (1.0 / A_HD)
    d = y - mu
    var = _dot_sel(d * d, seg) * (1.0 / A_HD)
    yn = d * lax.rsqrt(var + A_LN_EPS) * ln_ref[...]
    o_ref[...] = ((yn + bonus_ref[...]) * g_ref[...]).astype(o_ref.dtype)


def _rw_post(y_t, bonus, g, ln, seg):
    n = bonus.shape[0]
    full = lambda shape: pl.BlockSpec(shape, lambda i: (0,) * len(shape))
    return pl.pallas_call(
        _rw_post_kernel,
        grid=(1,),
        in_specs=[full(a.shape) for a in (y_t, bonus, g, ln, seg)],
        out_specs=full((n, A_W)),
        out_shape=jax.ShapeDtypeStruct((n, A_W), BF16),
        compiler_params=_cparams("arbitrary"),
        name="rw_post",
    )(y_t, bonus, g, ln, seg)


def _head_norm_lanes(x, g, eps):
    mu = jnp.mean(x, axis=-1, keepdims=True)
    d = x - mu
    var = jnp.mean(d * d, axis=-1, keepdims=True)
    return d * lax.rsqrt(var + eps) * g


def _mlstm_chunk_kernel(ub_ref, g_ref, gt_ref, cw_ref, cb_ref, biasr_ref, biasc_ref, gn_ref,
                        y_ref, co_ref, no_ref, mo_ref,
                        tail_s, c_s, n_s, m_s):
    nb, L, _ = ub_ref.shape
    c = pl.program_id(1)

    @pl.when(c == 0)
    def _():
        tail_s[...] = jnp.zeros_like(tail_s)
        c_s[...] = jnp.zeros_like(c_s)
        n_s[...] = jnp.zeros_like(n_s)
        m_s[...] = jnp.zeros_like(m_s)

    r_i = lax.broadcasted_iota(jnp.int32, (L, L), 0)
    c_i = lax.broadcasted_iota(jnp.int32, (L, L), 1)
    causal = c_i <= r_i
    tri = causal.astype(BF16)
    tri_u = (r_i <= c_i).astype(BF16)
    heads = range(B_HEADS)
    sls = [slice(hd * B_HD, (hd + 1) * B_HD) for hd in heads]

    seqs = range(nb)
    ub_l, q_l, k_l, gc_l, gr_l, bcols_l, brows_l = [], [], [], [], [], [], []
    for bi in seqs:
        ub = ub_ref[bi]
        raw = ub[:, :2 * B_W]
        ext = jnp.concatenate([tail_s[bi], raw], axis=0)
        tail_s[bi] = raw[L - SUBLANES:]
        acc = cb_ref[...] + raw * cw_ref[B_CONV - 1:B_CONV, :]
        for jj in range(B_CONV - 1):
            sh = B_CONV - 1 - jj
            acc = acc + pltpu.roll(ext, sh, axis=0)[SUBLANES:] * cw_ref[jj:jj + 1, :]
        act = _silu(acc)
        ub_l.append(ub)
        q_l.append(act[:, :B_W])
        k_l.append(act[:, B_W:] * (B_HD ** -0.5))
        gc = g_ref[bi] + biasr_ref[...]
        gr = gt_ref[bi] + biasc_ref[...]
        gc_l.append(gc)
        gr_l.append(gr)
        bcols_l.append(_dot_sel_left(tri, _log_sigmoid(gc)))
        brows_l.append(_dot_sel(_log_sigmoid(gr), tri_u))
    m_all = [m_s[bi] for bi in seqs]
    n_all = [n_s[bi] for bi in seqs]

    probs = [(bi, hd) for bi in seqs for hd in heads]
    qb = [q_l[bi][:, sls[hd]].astype(BF16) for bi, hd in probs]
    vb = [ub_l[bi][:, 2 * B_W + hd * B_HD:2 * B_W + (hd + 1) * B_HD].astype(BF16) for bi, hd in probs]
    qk = [_dot_nt(qb[e], k_l[bi][:, sls[hd]].astype(BF16)) for e, (bi, hd) in enumerate(probs)]
    qc = [_dot(qb[e], c_s[bi * B_HEADS + hd].astype(BF16)) for e, (bi, hd) in enumerate(probs)]
    m_prev = [m_all[bi][hd:hd + 1, 0:1] for bi, hd in probs]
    n_prev = [n_all[bi][hd:hd + 1, :] for bi, hd in probs]
    bc = [_lane_col(bcols_l[bi], B_HEADS + hd) for bi, hd in probs]
    ic = [_lane_col(gc_l[bi], hd) for bi, hd in probs]
    dlog, inter, b_last, gi = [], [], [], []
    for e, (bi, hd) in enumerate(probs):
        br = brows_l[bi][B_HEADS + hd:B_HEADS + hd + 1, :]
        ir = gr_l[bi][hd:hd + 1, :]
        dlog.append(jnp.where(causal, bc[e] - br + ir, -jnp.inf))
        inter.append(bc[e] + m_prev[e])
        b_last.append(_row_pick(bc[e], L - 1))
        gi.append(b_last[e] - bc[e] + ic[e])
    row_max = [jnp.max(dlog[e], axis=1, keepdims=True) for e in range(len(probs))]
    gi_max = [jnp.max(gi[e], axis=0, keepdims=True) for e in range(len(probs))]
    s_l, iw, m_t, wc, kw, m_new = [], [], [], [], [], []
    for e, (bi, hd) in enumerate(probs):
        m_t.append(jnp.maximum(inter[e], row_max[e]))
        s_l.append(qk[e] * jnp.exp(dlog[e] - m_t[e]))
        iw.append(jnp.exp(inter[e] - m_t[e]))
        m_new.append(jnp.maximum(b_last[e] + m_prev[e], gi_max[e]))
        wc.append(jnp.exp(b_last[e] + m_prev[e] - m_new[e]))
        kw.append(k_l[bi][:, sls[hd]] * jnp.exp(gi[e] - m_new[e]))
    sv = [_dot(s_l[e].astype(BF16), vb[e]) for e in range(len(probs))]
    kv = [_dot_tn(kw[e].astype(BF16), vb[e]) for e in range(len(probs))]
    s_sum = [jnp.sum(s_l[e], axis=1, keepdims=True) for e in range(len(probs))]
    qn = [jnp.sum(q_l[bi][:, sls[hd]] * n_prev[e], axis=1, keepdims=True) for e, (bi, hd) in enumerate(probs)]
    k_sum = [jnp.sum(kw[e], axis=0, keepdims=True) for e in range(len(probs))]
    hb = []
    for e, (bi, hd) in enumerate(probs):
        den = s_sum[e] + iw[e] * qn[e]
        hh = (sv[e] + iw[e] * qc[e]) / jnp.maximum(jnp.abs(den), jnp.exp(-m_t[e]))
        hb.append(_sigmoid(ub_l[bi][:, 3 * B_W + hd * B_HD:3 * B_W + (hd + 1) * B_HD]) * hh)
        c_s[bi * B_HEADS + hd] = wc[e] * c_s[bi * B_HEADS + hd] + kv[e]
    mean = [jnp.sum(hb[e], axis=-1, keepdims=True) * (1.0 / B_HD) for e in range(len(probs))]
    dev = [hb[e] - mean[e] for e in range(len(probs))]
    var = [jnp.sum(dev[e] * dev[e], axis=-1, keepdims=True) * (1.0 / B_HD) for e in range(len(probs))]
    for e, (bi, hd) in enumerate(probs):
        y_ref[bi, :, sls[hd]] = (dev[e] * lax.rsqrt(var[e] + GN_EPS) * gn_ref[:, sls[hd]]).astype(y_ref.dtype)
    pad_rows = jnp.zeros((SUBLANES - B_HEADS, LANES), F32)
    for bi in seqs:
        es = [e for e, (bj, _) in enumerate(probs) if bj == bi]
        n_s[bi] = jnp.concatenate([wc[e] * n_prev[e] + k_sum[e] for e in es] + [pad_rows], axis=0)
        m_s[bi] = jnp.concatenate([jnp.broadcast_to(m_new[e], (1, LANES)) for e in es] + [pad_rows], axis=0)

    @pl.when(c == pl.num_programs(1) - 1)
    def _():
        for bi in range(nb):
            for hd in heads:
                co_ref[bi, hd] = c_s[bi * B_HEADS + hd]
        no_ref[...] = n_s[...]
        mo_ref[...] = m_s[...]


def _mlstm_chunk(ub, gates, gates_t, cw, cb, bias_r, bias_c, gn, bsz, t_len, nb):
    L = CHUNK
    seq = lambda w: pl.BlockSpec((nb, L, w), lambda b, c: (b, c, 0))
    per_b = lambda shp: pl.BlockSpec((nb,) + shp, lambda b, c: (b,) + (0,) * len(shp))
    return pl.pallas_call(
        _mlstm_chunk_kernel,
        grid=(bsz // nb, t_len // L),
        in_specs=[seq(4 * B_W), seq(LANES),
                  pl.BlockSpec((nb, SUBLANES, L), lambda b, c: (b, 0, c)),
                  _const_spec(cw.shape), _const_spec(cb.shape), _const_spec(bias_r.shape),
                  _const_spec(bias_c.shape), _const_spec(gn.shape)],
        out_specs=[seq(B_W), per_b((B_HEADS, B_HD, B_HD)), per_b((SUBLANES, B_HD)),
                   per_b((SUBLANES, LANES))],
        out_shape=[jax.ShapeDtypeStruct((bsz, t_len, B_W), BF16),
                   jax.ShapeDtypeStruct((bsz, B_HEADS, B_HD, B_HD), F32),
                   jax.ShapeDtypeStruct((bsz, SUBLANES, B_HD), F32),
                   jax.ShapeDtypeStruct((bsz, SUBLANES, LANES), F32)],
        scratch_shapes=[pltpu.VMEM((nb, SUBLANES, 2 * B_W), F32),
                        pltpu.VMEM((nb * B_HEADS, B_HD, B_HD), F32),
                        pltpu.VMEM((nb, SUBLANES, B_HD), F32),
                        pltpu.VMEM((nb, SUBLANES, LANES), F32)],
        compiler_params=_cparams("parallel", "arbitrary"),
        name="mlstm_chunk",
    )(ub, gates, gates_t, cw, cb, bias_r, bias_c, gn)


def _mlstm_step_kernel(ub_ref, g_ref, conv_ref, cw_ref, cb_ref, biasr_ref, gn_ref, c_ref, n_ref, m_ref,
                       y_ref, co_ref, no_ref, mo_ref):
    bb = ub_ref.shape[0]
    ub = ub_ref[...]
    raw = ub[:, :2 * B_W]
    conv = conv_ref[...]
    acc = cb_ref[...] + raw * cw_ref[B_CONV - 1:B_CONV, :]
    for jj in range(B_CONV - 1):
        acc = acc + conv[:, jj * 2 * B_W:(jj + 1) * 2 * B_W] * cw_ref[jj:jj + 1, :]
    qk = _silu(acc)
    q_all = qk[:, :B_W]
    k_all = qk[:, B_W:] * (B_HD ** -0.5)
    gc = g_ref[...] + biasr_ref[...]
    lf = _log_sigmoid(gc)
    m_all = m_ref[...]
    row_pad = lax.broadcasted_iota(jnp.int32, (LANES, LANES), 0)
    pad = jnp.zeros((LANES - bb, B_HD), F32)
    m_out = jnp.zeros((bb, LANES), F32)
    lane_m = lax.broadcasted_iota(jnp.int32, (bb, LANES), 1)
    for hd in range(B_HEADS):
        sl = slice(hd * B_HD, (hd + 1) * B_HD)
        q = q_all[:, sl]
        k = k_all[:, sl]
        v = ub[:, 2 * B_W + hd * B_HD:2 * B_W + (hd + 1) * B_HD]
        o = ub[:, 3 * B_W + hd * B_HD:3 * B_W + (hd + 1) * B_HD]
        ic = _lane_col(gc, hd)
        f = _lane_col(lf, B_HEADS + hd)
        m_prev = _lane_col(m_all, hd)
        n_prev = n_ref[:, sl]
        m_t = jnp.maximum(f + m_prev, ic)
        s = jnp.sum(q * k, axis=1, keepdims=True) * jnp.exp(ic - m_t)
        iw = jnp.exp(f + m_prev - m_t)
        wi = jnp.exp(ic - m_t)
        qb = q.astype(BF16)
        kw_t = jnp.concatenate([k * wi, pad], axis=0).T
        v_pad = jnp.concatenate([v, pad], axis=0)
        qc = jnp.zeros((bb, B_HD), F32)
        row_b = lax.broadcasted_iota(jnp.int32, (bb, B_HD), 0)
        for bi in range(bb):
            c_prev = c_ref[bi, hd]
            qc = qc + jnp.where(row_b == bi, _dot(qb, c_prev.astype(BF16)), 0.0)
            v_one = jnp.where(row_pad == bi, v_pad, 0.0)
            co_ref[bi, hd] = _row_pick(iw, bi) * c_prev + _dot3(kw_t, v_one, _dot)
        num = s * v + iw * qc
        den = s + iw * jnp.sum(q * n_prev, axis=1, keepdims=True)
        hh = num / jnp.maximum(jnp.abs(den), jnp.exp(-m_t))
        no_ref[:, sl] = iw * n_prev + wi * k
        m_out = jnp.where(lane_m == hd, m_t, m_out)
        hb = _sigmoid(o) * hh
        y_ref[:, sl] = _head_norm_lanes(hb, gn_ref[:, sl], GN_EPS).astype(y_ref.dtype)
    mo_ref[...] = m_out


def _mlstm_step(ub, gates, conv, cw, cb, bias_r, gn, c0, n0, m0, bb):
    bsz = ub.shape[0]
    row = lambda w: pl.BlockSpec((bb, w), lambda i: (i, 0))
    c_spec = pl.BlockSpec((bb, B_HEADS, B_HD, B_HD), lambda i: (i, 0, 0, 0))
    return pl.pallas_call(
        _mlstm_step_kernel,
        grid=(bsz // bb,),
        in_specs=[row(4 * B_W), row(LANES), row((B_CONV - 1) * 2 * B_W),
                  _const_spec(cw.shape), _const_spec(cb.shape), _const_spec(bias_r.shape),
                  _const_spec(gn.shape), c_spec, row(B_W), row(LANES)],
        out_specs=[row(B_W), c_spec, row(B_W), row(LANES)],
        out_shape=[jax.ShapeDtypeStruct((bsz, B_W), BF16),
                   jax.ShapeDtypeStruct(c0.shape, F32),
                   jax.ShapeDtypeStruct((bsz, B_W), F32),
                   jax.ShapeDtypeStruct((bsz, LANES), F32)],
        compiler_params=_cparams("parallel"),
        name="mlstm_step",
    )(ub, gates, conv, cw, cb, bias_r, gn, c0, n0, m0)


def _ret_log_gamma(hd):
    return math.log1p(-(2.0 ** (-5.0 - hd)))


def _ret_chunk_kernel(q_ref, qd_ref, k_ref, kd_ref, v_ref, g_ref, y_ref, so_ref, s_s, mask_s):
    nb, L, _ = q_ref.shape
    c = pl.program_id(1)

    @pl.when(c == 0)
    def _():
        s_s[...] = jnp.zeros_like(s_s)
        r_i = lax.broadcasted_iota(jnp.int32, (L, L), 0)
        c_i = lax.broadcasted_iota(jnp.int32, (L, L), 1)
        diff = (r_i - c_i).astype(F32)
        for hd in range(C_HEADS):
            mask_s[hd] = jnp.where(diff >= 0, jnp.exp(jnp.maximum(diff, 0.0) * _ret_log_gamma(hd)), 0.0)

    heads = range(C_HEADS)
    kls = [slice(hd * C_DK, (hd + 1) * C_DK) for hd in heads]
    sls = [slice(hd * C_DV, (hd + 1) * C_DV) for hd in heads]
    probs = [(bi, hd) for bi in range(nb) for hd in heads]
    n = range(len(probs))

    sc = [(_dot_nt(q_ref[bi, :, kls[hd]], k_ref[bi, :, kls[hd]]) * mask_s[hd]).astype(BF16) for bi, hd in probs]
    o_l = []
    for e, (bi, hd) in enumerate(probs):
        sl = sls[hd]
        c_dec = math.exp(L * _ret_log_gamma(hd))
        s_prev = s_s[e]
        o_l.append(_dot(sc[e], v_ref[bi, :, sl]) + _dot(qd_ref[bi, :, kls[hd]], s_prev.astype(BF16)))
        s_s[e] = c_dec * s_prev + _dot_tn(kd_ref[bi, :, kls[hd]], v_ref[bi, :, sl])
    mean = [jnp.sum(o_l[e], axis=-1, keepdims=True) * (1.0 / C_DV) for e in n]
    dev = [o_l[e] - mean[e] for e in n]
    var = [jnp.sum(dev[e] * dev[e], axis=-1, keepdims=True) * (1.0 / C_DV) for e in n]
    for e, (bi, hd) in enumerate(probs):
        sl = sls[hd]
        y_ref[bi, :, sl] = (dev[e] * lax.rsqrt(var[e] + GN_EPS) * g_ref[bi, :, sl]).astype(y_ref.dtype)

    @pl.when(c == pl.num_programs(1) - 1)
    def _():
        for e, (bi, hd) in enumerate(probs):
            so_ref[bi, hd] = s_s[e]


def _ret_chunk(q, qd, k, kd, v, g, bsz, t_len, nb):
    L = RET_CHUNK
    seq = lambda w: pl.BlockSpec((nb, L, w), lambda b, c: (b, c, 0))
    st = pl.BlockSpec((nb, C_HEADS, C_DK, C_DV), lambda b, c: (b, 0, 0, 0))
    return pl.pallas_call(
        _ret_chunk_kernel,
        grid=(bsz // nb, t_len // L),
        in_specs=[seq(C_HEADS * C_DK)] * 4 + [seq(C_OUT_W), seq(C_OUT_W)],
        out_specs=[seq(C_OUT_W), st],
        out_shape=[jax.ShapeDtypeStruct((bsz, t_len, C_OUT_W), BF16),
                   jax.ShapeDtypeStruct((bsz, C_HEADS, C_DK, C_DV), F32)],
        scratch_shapes=[pltpu.VMEM((nb * C_HEADS, C_DK, C_DV), F32), pltpu.VMEM((C_HEADS, L, L), F32)],
        compiler_params=_cparams("parallel", "arbitrary"),
        name="ret_chunk",
    )(q, qd, k, kd, v, g)


def _ret_step_kernel(q_ref, k_ref, v_ref, g_ref, s_ref, y_ref, so_ref):
    bb = q_ref.shape[0]
    hd = pl.program_id(1)
    gam = 1.0 - jnp.exp2(-5.0 - jnp.full((1, 1), hd).astype(F32))
    q = q_ref[...]
    k = k_ref[...]
    v = v_ref[...]
    s = jnp.sum(q * k, axis=1, keepdims=True)
    pad = jnp.zeros((LANES - bb, C_DK), F32)
    k_t = jnp.concatenate([k, pad], axis=0).T
    q_t = jnp.concatenate([q * gam, pad], axis=0).T
    row_b = lax.broadcasted_iota(jnp.int32, (bb, C_DV), 0)
    qs = jnp.zeros((bb, C_DV), F32)
    for bi in range(bb):
        s_prev = s_ref[bi, 0]
        qs = jnp.where(row_b == bi, jnp.sum(_lane_col(q_t, bi) * s_prev, axis=0, keepdims=True), qs)
        so_ref[bi, 0] = gam * s_prev + _lane_col(k_t, bi) * v[bi:bi + 1, :]
    o = s * v + qs
    y = _head_norm_lanes(o, g_ref[...], GN_EPS)
    y_ref[...] = y.astype(y_ref.dtype)


def _ret_step(q, k, v, g, s0, bb):
    bsz = q.shape[0]
    st = pl.BlockSpec((bb, 1, C_DK, C_DV), lambda i, h: (i, h, 0, 0))
    return pl.pallas_call(
        _ret_step_kernel,
        grid=(bsz // bb, C_HEADS),
        in_specs=[pl.BlockSpec((bb, C_DK), lambda i, h: (i, h)),
                  pl.BlockSpec((bb, C_DK), lambda i, h: (i, h)),
                  pl.BlockSpec((bb, C_DV), lambda i, h: (i, h)),
                  pl.BlockSpec((bb, C_DV), lambda i, h: (i, h)), st],
        out_specs=[pl.BlockSpec((bb, C_DV), lambda i, h: (i, h)), st],
        out_shape=[jax.ShapeDtypeStruct((bsz, C_OUT_W), BF16),
                   jax.ShapeDtypeStruct(s0.shape, F32)],
        compiler_params=_cparams("parallel", "parallel"),
        name="ret_step",
    )(q, k, v, g, s0)


def _post_kernel(*refs, n_mix):
    h_ref = refs[0]
    mix_refs = refs[1:1 + n_mix]
    wout_refs = refs[1 + n_mix:1 + 2 * n_mix]
    (p_ref, g_ref, up_ref, down_ref, gate_ref, proj_ref, o_ref, h1_s, xn_s, acc_s) = refs[1 + 2 * n_mix:]
    f = pl.program_id(1)

    @pl.when(f == 0)
    def _():
        mix = _dot(mix_refs[0][...], wout_refs[0][...])
        for m_ref, w_ref in zip(mix_refs[1:], wout_refs[1:]):
            mix = mix + _dot(m_ref[...], w_ref[...])
        h1 = h_ref[...] + _rms(mix, g_ref[1:2, :])
        h1_s[...] = h1
        xn_s[...] = _rms(h1, g_ref[2:3, :]).astype(BF16)
        acc_s[...] = jnp.zeros_like(acc_s)

    hid = jnp.square(jnp.maximum(_dot(xn_s[...], up_ref[...]), 0.0))
    acc_s[...] += _dot(hid.astype(BF16), down_ref[...])

    @pl.when(f == pl.num_programs(1) - 1)
    def _():
        h2 = h1_s[...] + _rms(acc_s[...], g_ref[3:4, :])
        gate = _sigmoid(_dot(h2.astype(BF16), gate_ref[...]))
        emb = _dot(p_ref[...].astype(BF16), proj_ref[...])
        o_ref[...] = h2 + gate * emb


def _post(h, mixes, wouts, p, g, up, down, gate, proj, li, tm, tf):
    n = h.shape[0]
    n_mix = len(mixes)
    row = lambda w: pl.BlockSpec((tm, w), lambda i, f: (i, 0))
    layer = lambda a: pl.BlockSpec((None,) + a.shape[1:], lambda i, f: (li,) + (0,) * (a.ndim - 1))
    return pl.pallas_call(
        functools.partial(_post_kernel, n_mix=n_mix),
        grid=(n // tm, D_FF // tf),
        in_specs=[row(D_MODEL)] + [row(m.shape[1]) for m in mixes]
        + [_const_spec(w.shape) for w in wouts]
        + [pl.BlockSpec((None, tm, D_PLE), lambda i, f: (li, i, 0)), layer(g),
           pl.BlockSpec((None, D_MODEL, tf), lambda i, f: (li, 0, f)),
           pl.BlockSpec((None, tf, D_MODEL), lambda i, f: (li, f, 0)),
           layer(gate), layer(proj)],
        out_specs=row(D_MODEL),
        out_shape=jax.ShapeDtypeStruct((n, D_MODEL), F32),
        scratch_shapes=[pltpu.VMEM((tm, D_MODEL), F32), pltpu.VMEM((tm, D_MODEL), BF16),
                        pltpu.VMEM((tm, D_MODEL), F32)],
        compiler_params=_cparams("parallel", "arbitrary"),
        name="post",
    )(h, *mixes, *wouts, p, g, up, down, gate, proj)


def _prep_weights(W):
    f = {}
    w_in = W['mix0_w_in'][0]
    f['w_a'] = w_in[:, :A_SHIFT_W].astype(BF16)
    f['w_b'] = w_in[:, A_SHIFT_W:A_SHIFT_W + 4 * B_W].astype(BF16)
    f['w_g'] = jnp.pad(w_in[:, A_SHIFT_W + 4 * B_W:], ((0, 0), (0, LANES - 2 * B_HEADS))).astype(BF16)
    wl = jnp.zeros((A_LORA_W, 3 * A_W), F32)
    wl = wl.at[:A_DECAY_LORA, :A_W].set(W['rw_w2'][0])
    wl = wl.at[A_DECAY_LORA:A_DECAY_LORA + A_AAA_LORA, A_W:2 * A_W].set(W['rw_a2'][0])
    wl = wl.at[A_DECAY_LORA + A_AAA_LORA:, 2 * A_W:].set(W['rw_g2'][0])
    f['w_lora'] = wl.astype(BF16)
    f['rw_vec'] = jnp.concatenate(
        [W['rw_w0'], W['rw_a0'], W['rw_kk'], W['rw_ka'], W['rw_rk'][0].reshape(1, A_W),
         jnp.zeros((3, A_W), F32)], axis=0)
    f['rw_mu'] = W['rw_mu']
    f['rw_ln'] = W['rw_ln']
    f['seg'] = (jnp.arange(A_W)[:, None] // A_HD == jnp.arange(A_W)[None, :] // A_HD).astype(BF16)
    f['cw'] = W['ml_conv_w'][0]
    f['cb'] = W['ml_conv_b']
    gate_bias = jnp.concatenate([W['ml_i_bias'][0], W['ml_f_bias'][0]])
    f['bias_r'] = jnp.pad(gate_bias, (0, LANES - 2 * B_HEADS)).reshape(1, LANES)
    f['bias_c'] = gate_bias.reshape(2 * B_HEADS, 1)
    f['ml_gn'] = W['ml_gn']
    w_out0 = W['mix0_w_out'][0].astype(BF16)
    f['w_out_a'] = w_out0[:A_W]
    f['w_out_b'] = w_out0[A_W:]
    w_ret = W['ret_w_in'][0]
    nq = C_HEADS * C_DK
    f['w_rq'] = w_ret[:, :nq].astype(BF16)
    f['w_rk'] = w_ret[:, nq:2 * nq].astype(BF16)
    f['w_rv'] = w_ret[:, 2 * nq:2 * nq + C_OUT_W].astype(BF16)
    f['w_rg'] = w_ret[:, 2 * nq + C_OUT_W:].astype(BF16)
    f['ret_gn'] = W['ret_gn']
    f['w_ret_out'] = W['ret_w_out'][0].astype(BF16)
    for name in ('ffn_up', 'ffn_down', 'ple_gate', 'ple_proj'):
        f[name] = W[name].astype(BF16)
    f['norm_g'] = W['norm_g']
    return f


def _rope_tables(pos):
    half = C_DK // 2
    inv = ROPE_BASE ** (-jnp.arange(half, dtype=F32) / half)
    ang = pos.astype(F32)[:, None] * inv[None, :]
    return jnp.cos(ang), jnp.sin(ang)


def _even_layer(h, p, st, f, bsz, t_len):
    n = bsz * t_len
    seq = st is None
    tm = 512 if seq else n
    ua, ub, gates = _inproj(h, f['norm_g'][0], [f['w_a'], f['w_b'], f['w_g']], [F32, F32, F32], tm)
    qk_raw = ub.reshape(bsz, t_len, 4 * B_W)[:, :, :2 * B_W]

    if seq:
        ya, s_a_new = _rw_chunk(ua, f['rw_mu'], f['w_lora'], f['rw_vec'], f['rw_ln'], bsz, t_len, rt=256)
        conv_new = qk_raw[:, -(B_CONV - 1):]
    else:
        shift, s_a, conv, c0, n0, m0 = st
        conv_new = jnp.concatenate([conv, qk_raw], axis=1)[:, -(B_CONV - 1):]
        r, w, k, v, a, b, g, bonus = _rw_prep(ua, shift, f['rw_mu'], f['w_lora'], f['rw_vec'], f['seg'])
        y_t, s_t = _rw_step(r, w, k, v, a, b, jnp.transpose(s_a, (1, 2, 3, 0)))
        s_a_new = jnp.transpose(s_t, (3, 0, 1, 2))
        ya = _rw_post(y_t, bonus, g, f['rw_ln'], f['seg'])

    if seq:
        gates3 = gates.reshape(bsz, t_len, LANES)
        gates_t = gates3[:, :, :SUBLANES].transpose(0, 2, 1)
        yb, c_new, n_new, m_new = _mlstm_chunk(ub.reshape(bsz, t_len, 4 * B_W), gates3, gates_t, f['cw'],
                                               f['cb'], f['bias_r'], f['bias_c'], f['ml_gn'], bsz, t_len,
                                               nb=4)
        yb = yb.reshape(n, B_W)
        n_new = n_new[:, :B_HEADS]
        m_new = m_new[:, :B_HEADS, 0]
    else:
        m0p = jnp.pad(m0, ((0, 0), (0, LANES - B_HEADS)))
        yb, c_new, n_new, m_new = _mlstm_step(ub, gates, conv.reshape(bsz, -1), f['cw'], f['cb'],
                                              f['bias_r'], f['ml_gn'], c0, n0.reshape(bsz, B_W), m0p,
                                              bb=SUBLANES)
        n_new = n_new.reshape(bsz, B_HEADS, B_HD)
        m_new = m_new[:, :B_HEADS]

    h_next = _post(h, [ya, yb], [f['w_out_a'], f['w_out_b']], p, f['norm_g'], f['ffn_up'],
                   f['ffn_down'], f['ple_gate'], f['ple_proj'], li=0, tm=tm, tf=POST_TF)

    shift_new = ua.reshape(bsz, t_len, A_SHIFT_W)[:, -1]
    return h_next, (shift_new, s_a_new, conv_new, c_new, n_new, m_new)


def _odd_layer(h, p, s0, pos, f, bsz, t_len):
    n = bsz * t_len
    seq = s0 is None
    cos, sin = _rope_tables(pos)
    tm = 512 if seq else n
    ws = [f['w_rq'], f['w_rk'], f['w_rv'], f['w_rg']]
    if seq:
        q, qd, k, kd, v, g = _inproj_ret(h, f['norm_g'][1], f['ret_gn'], cos, sin, ws, tm, t_len // tm,
                                         [BF16] * 5 + [F32], decayed=True)
        as3 = lambda a: a.reshape(bsz, t_len, a.shape[-1])
        y, s_new = _ret_chunk(as3(q), as3(qd), as3(k), as3(kd), as3(v), as3(g), bsz, t_len, nb=2)
        y = y.reshape(n, C_OUT_W)
    else:
        q, k, v, g = _inproj_ret(h, f['norm_g'][1], f['ret_gn'], cos, sin, ws, tm, 1, [F32] * 4,
                                 decayed=False)
        y, s_new = _ret_step(q, k, v, g, s0, bb=SUBLANES)
    h_next = _post(h, [y], [f['w_ret_out']], p, f['norm_g'], f['ffn_up'], f['ffn_down'],
                   f['ple_gate'], f['ple_proj'], li=1, tm=512 if seq else n, tf=POST_TF)
    return h_next, s_new


def _trunk(x, p, pos, ev_states, od_state, f):
    bsz, t_len, _ = x.shape
    n = bsz * t_len
    h = x.reshape(n, D_MODEL)
    ev_in = None if ev_states is None else tuple(s[0] for s in ev_states)
    od_in = None if od_state is None else od_state[0]
    p = p.reshape(p.shape[0], n, D_PLE)
    h, ev_new = _even_layer(h, p, ev_in, f, bsz, t_len)
    h, od_new = _odd_layer(h, p, od_in, pos, f, bsz, t_len)
    return h.reshape(bsz, t_len, D_MODEL), [s[None] for s in ev_new], od_new[None]


def kernel(x_prompt, x_sample, state_rwkv_shift, state_rwkv_S, state_mlstm_conv, state_mlstm_C,
           state_mlstm_n, state_mlstm_m, state_ret_S, p_prompt, p_sample, norm_g, ffn_up, ffn_down,
           ple_gate, ple_proj, mix0_w_in, rw_mu, rw_w0, rw_w2, rw_a0, rw_a2, rw_g2, rw_kk, rw_ka,
           rw_rk, rw_ln, ml_conv_w, ml_conv_b, ml_i_bias, ml_f_bias, ml_gn, mix0_w_out,
           ret_w_in, ret_gn, ret_w_out):
    W = dict(norm_g=norm_g, ffn_up=ffn_up, ffn_down=ffn_down, ple_gate=ple_gate, ple_proj=ple_proj,
             mix0_w_in=mix0_w_in, rw_mu=rw_mu, rw_w0=rw_w0, rw_w2=rw_w2, rw_a0=rw_a0, rw_a2=rw_a2,
             rw_g2=rw_g2, rw_kk=rw_kk, rw_ka=rw_ka, rw_rk=rw_rk, rw_ln=rw_ln, ml_conv_w=ml_conv_w,
             ml_conv_b=ml_conv_b, ml_i_bias=ml_i_bias, ml_f_bias=ml_f_bias, ml_gn=ml_gn,
             mix0_w_out=mix0_w_out, ret_w_in=ret_w_in, ret_gn=ret_gn, ret_w_out=ret_w_out)
    f = _prep_weights(W)
    bp, tp = x_prompt.shape[:2]
    ts = x_sample.shape[1]
    y_prompt, ev_p, ret_p = _trunk(x_prompt, p_prompt, jnp.arange(tp), None, None, f)
    ev_s_in = (state_rwkv_shift, state_rwkv_S, state_mlstm_conv, state_mlstm_C, state_mlstm_n,
               state_mlstm_m)
    y_sample, ev_s, ret_s = _trunk(x_sample, p_sample, PAST_LEN + jnp.arange(ts), ev_s_in, state_ret_S, f)
    return (y_prompt, y_sample, *ev_p, ret_p, *ev_s, ret_s)
```

```python
import functools
import math

import jax
import jax.numpy as jnp
from jax import lax
from jax.experimental import pallas as pl
from jax.experimental.pallas import tpu as pltpu

F32 = jnp.float32
BF16 = jnp.bfloat16

D_MODEL = 1024
D_PLE = 256
D_FF = 4 * D_MODEL
RMS_EPS = 1e-6
GN_EPS = 1e-5
CHUNK = 128
RET_CHUNK = 256

A_HEADS = 8
A_HD = 64
A_W = A_HEADS * A_HD
A_DECAY_LORA = 64
A_AAA_LORA = 64
A_GATE_LORA = 128
A_LORA_W = A_DECAY_LORA + A_AAA_LORA + A_GATE_LORA
A_SHIFT_W = 3 * A_W + A_LORA_W
A_LN_EPS = 64e-5

B_HEADS = 4
B_HD = 128
B_W = B_HEADS * B_HD
B_CONV = 4

C_HEADS = 4
C_DK = 256
C_DV = 512
C_OUT_W = C_HEADS * C_DV
ROPE_BASE = 10000.0
PAST_LEN = 16384

LANES = 128
SUBLANES = 8
VMEM_LIMIT = 56 * 1024 * 1024
POST_TF = 1024


def _cparams(*sem):
    return pltpu.CompilerParams(dimension_semantics=sem, vmem_limit_bytes=VMEM_LIMIT)


def _const_spec(shape):
    nd = len(shape)
    return pl.BlockSpec(shape, lambda *_: (0,) * nd, pipeline_mode=pl.Buffered(1))


def _rms(x, g):
    return x * lax.rsqrt(jnp.mean(x * x, axis=-1, keepdims=True) + RMS_EPS) * g


def _dot(a, b):
    return jnp.dot(a, b, preferred_element_type=F32)


def _dot_nt(a, b):
    return lax.dot_general(a, b, (((1,), (1,)), ((), ())), preferred_element_type=F32)


def _dot_tn(a, b):
    return lax.dot_general(a, b, (((0,), (0,)), ((), ())), preferred_element_type=F32)


def _hilo(x):
    hi = x.astype(BF16)
    lo = (x - hi.astype(F32)).astype(BF16)
    return hi, lo


def _dot_sel(x, sel):
    hi, lo = _hilo(x)
    return _dot(hi, sel) + _dot(lo, sel)


def _dot_sel_left(sel, x):
    hi, lo = _hilo(x)
    return _dot(sel, hi) + _dot(sel, lo)


def _dot3(a, b):
    ah, al = _hilo(a)
    bh, bl = _hilo(b)
    return _dot(ah, bh) + _dot(ah, bl) + _dot(al, bh)


def _sigmoid(x):
    return 1.0 / (1.0 + jnp.exp(-x))


def _softplus(x):
    return jnp.maximum(x, 0.0) + jnp.log1p(jnp.exp(-jnp.abs(x)))


def _log_sigmoid(x):
    return -_softplus(-x)


def _silu(x):
    return x * _sigmoid(x)


def _lane_col(x, idx):
    lane = lax.broadcasted_iota(jnp.int32, x.shape, 1)
    return jnp.sum(jnp.where(lane == idx, x, 0.0), axis=1, keepdims=True)


def _row_pick(x, idx):
    row = lax.broadcasted_iota(jnp.int32, x.shape, 0)
    return jnp.sum(jnp.where(row == idx, x, 0.0), axis=0, keepdims=True)


def _inproj_kernel(x_ref, g_ref, *refs):
    n = len(refs) // 2
    xn = _rms(x_ref[...], g_ref[0:1, :]).astype(BF16)
    for w_ref, o_ref in zip(refs[:n], refs[n:]):
        o_ref[...] = _dot_nt(xn, w_ref[...]).astype(o_ref.dtype)


def _inproj(h, g, ws, dtypes, tm):
    n = h.shape[0]
    return pl.pallas_call(
        _inproj_kernel,
        grid=(n // tm,),
        in_specs=[pl.BlockSpec((tm, D_MODEL), lambda i: (i, 0)), _const_spec(g.shape)]
        + [_const_spec(w.shape) for w in ws],
        out_specs=[pl.BlockSpec((tm, w.shape[0]), lambda i: (i, 0)) for w in ws],
        out_shape=[jax.ShapeDtypeStruct((n, w.shape[0]), dt) for w, dt in zip(ws, dtypes)],
        compiler_params=_cparams("parallel"),
        name="inproj",
    )(h, g, *ws)


def _inproj_ret_kernel(x_ref, g_ref, gn_ref, cos_ref, sin_ref, wq_ref, wk_ref, wv_ref, wg_ref, *out_refs,
                       decayed):
    if decayed:
        q_ref, qd_ref, k_ref, kd_ref, v_ref, gate_ref = out_refs
    else:
        q_ref, k_ref, v_ref, gate_ref = out_refs
        qd_ref = kd_ref = None
    xn = _rms(x_ref[...], g_ref[0:1, :]).astype(BF16)
    cos = cos_ref[...]
    sin = sin_ref[...]
    half = C_DK // 2
    tm = x_ref.shape[0]
    row0 = pl.program_id(0) * tm
    pos = jnp.bitwise_and(lax.broadcasted_iota(jnp.int32, (tm, half), 0) + row0, RET_CHUNK - 1).astype(F32)
    gate_ref[...] = (_silu(_dot(xn, wg_ref[...])) * gn_ref[...]).astype(gate_ref.dtype)
    for w_ref, o_ref, d_ref, scale in ((wq_ref, q_ref, qd_ref, None), (wk_ref, k_ref, kd_ref, C_DK ** -0.5)):
        u = _dot(xn, w_ref[...])
        for hd in range(C_HEADS):
            x1 = u[:, hd * C_DK:hd * C_DK + half]
            x2 = u[:, hd * C_DK + half:(hd + 1) * C_DK]
            y1 = x1 * cos - x2 * sin
            y2 = x1 * sin + x2 * cos
            if scale is not None:
                y1 = y1 * scale
                y2 = y2 * scale
            o_ref[:, hd * C_DK:hd * C_DK + half] = y1.astype(o_ref.dtype)
            o_ref[:, hd * C_DK + half:(hd + 1) * C_DK] = y2.astype(o_ref.dtype)
            if decayed:
                lg = _ret_log_gamma(hd)
                dec = jnp.exp((pos + 1.0) * lg) if d_ref is qd_ref else jnp.exp((RET_CHUNK - 1.0 - pos) * lg)
                d_ref[:, hd * C_DK:hd * C_DK + half] = (y1 * dec).astype(d_ref.dtype)
                d_ref[:, hd * C_DK + half:(hd + 1) * C_DK] = (y2 * dec).astype(d_ref.dtype)
    v_ref[...] = _dot(xn, wv_ref[...]).astype(v_ref.dtype)


def _inproj_ret(h, g, gn, cos, sin, ws, tm, seq_tiles, dtypes, decayed):
    n = h.shape[0]
    if cos.shape[0] == 1:
        cs_spec = _const_spec(cos.shape)
    else:
        cs_spec = pl.BlockSpec((tm, cos.shape[1]), lambda i: (i % seq_tiles, 0))
    widths = [w.shape[1] for w in ws]
    if decayed:
        widths = [widths[0], widths[0], widths[1], widths[1], widths[2], widths[3]]
    return pl.pallas_call(
        functools.partial(_inproj_ret_kernel, decayed=decayed),
        grid=(n // tm,),
        in_specs=[pl.BlockSpec((tm, D_MODEL), lambda i: (i, 0)), _const_spec(g.shape), _const_spec(gn.shape),
                  cs_spec, cs_spec] + [_const_spec(w.shape) for w in ws],
        out_specs=[pl.BlockSpec((tm, wd), lambda i: (i, 0)) for wd in widths],
        out_shape=[jax.ShapeDtypeStruct((n, wd), dt) for wd, dt in zip(widths, dtypes)],
        compiler_params=_cparams("parallel"),
        name="inproj_ret",
    )(h, g, gn, cos, sin, *ws)


def _rw_prep_kernel(x_ref, sh_ref, mu_ref, wl_ref, vec_ref, seg_ref,
                    r_ref, w_ref, k_ref, v_ref, a_ref, b_ref, g_ref, bonus_ref):
    x = x_ref[...]
    xs = x + (sh_ref[...] - x) * mu_ref[...]
    r = xs[:, :A_W]
    k = xs[:, A_W:2 * A_W]
    v = xs[:, 2 * A_W:3 * A_W]
    lo = xs[:, 3 * A_W:]
    lane = lax.broadcasted_iota(jnp.int32, lo.shape, 1)
    act = jnp.where(lane < A_DECAY_LORA, jnp.tanh(lo),
                    jnp.where(lane < A_DECAY_LORA + A_AAA_LORA, lo, _sigmoid(lo)))
    lora = _dot(act.astype(BF16), wl_ref[...])
    w0, a0, kkw, kaw, rkw = (vec_ref[i:i + 1, :] for i in range(5))
    w_log = -_softplus(-(w0 + lora[:, :A_W])) - 0.5
    decay = jnp.exp(-jnp.exp(w_log))
    a = _sigmoid(a0 + lora[:, A_W:2 * A_W])
    g = lora[:, 2 * A_W:]
    seg = seg_ref[...]
    kk = k * kkw
    kk = kk / jnp.maximum(jnp.sqrt(_dot_sel(kk * kk, seg)), 1e-12)
    k2 = k * (1.0 + (a - 1.0) * kaw)
    bonus = _dot_sel(r * k2 * rkw, seg) * v
    r_ref[...] = r.T
    w_ref[...] = decay.T
    k_ref[...] = k2.T
    v_ref[...] = v.T
    a_ref[...] = (-kk).T
    b_ref[...] = (kk * a).T
    g_ref[...] = g
    bonus_ref[...] = bonus


def _rw_prep(ua, shift, mu, wl, vec, seg):
    n = ua.shape[0]
    full = lambda shape: pl.BlockSpec(shape, lambda i: (0,) * len(shape))
    return pl.pallas_call(
        _rw_prep_kernel,
        grid=(1,),
        in_specs=[full(a.shape) for a in (ua, shift, mu, wl, vec, seg)],
        out_specs=[full((A_W, n))] * 6 + [full((n, A_W))] * 2,
        out_shape=[jax.ShapeDtypeStruct((A_W, n), F32)] * 6 + [jax.ShapeDtypeStruct((n, A_W), F32)] * 2,
        compiler_params=_cparams("arbitrary"),
        name="rw_prep",
    )(ua, shift, mu, wl, vec, seg)


def _rw_step_kernel(r_ref, w_ref, k_ref, v_ref, a_ref, b_ref, s_ref, y_ref, so_ref):
    rt, wt, kt, at, bt = r_ref[...], w_ref[...], k_ref[...], a_ref[...], b_ref[...]
    for v in range(A_HD):
        s = s_ref[0, v]
        sa = jnp.sum(s * at, axis=0, keepdims=True)
        sn = s * wt + sa * bt + v_ref[v:v + 1, :] * kt
        so_ref[0, v] = sn
        y_ref[v:v + 1, :] = jnp.sum(sn * rt, axis=0, keepdims=True)


def _rw_step(r, w, k, v, a, b, s_t):
    bsz = r.shape[1]
    vec = pl.BlockSpec((A_HD, bsz), lambda h: (h, 0))
    st = pl.BlockSpec((1, A_HD, A_HD, bsz), lambda h: (h, 0, 0, 0))
    return pl.pallas_call(
        _rw_step_kernel,
        grid=(A_HEADS,),
        in_specs=[vec] * 6 + [st],
        out_specs=[vec, st],
        out_shape=[jax.ShapeDtypeStruct((A_W, bsz), F32), jax.ShapeDtypeStruct(s_t.shape, F32)],
        compiler_params=_cparams("parallel"),
        name="rw_step",
    )(r, w, k, v, a, b, s_t)


RW_C = 64
(F_RT, F_PM1, F_IP, F_DC, F_KH, F_KC, F_V, F_KK, F_ASIG, F_RKK, F_G, F_PC) = range(12)


def _rw_chunk_kernel(x_ref, mu_ref, wl_ref, vec_ref, ln_ref, o_ref, so_ref, carry, s_scr, f_scr):
    rt = x_ref.shape[0]
    i = pl.program_id(1)

    @pl.when(i == 0)
    def _():
        carry[...] = jnp.zeros_like(carry)
        s_scr[...] = jnp.zeros_like(s_scr)

    x = x_ref[...]
    row = lax.broadcasted_iota(jnp.int32, x.shape, 0)
    prev = jnp.where(row == 0, carry[...], pltpu.roll(x, 1, axis=0))
    carry[...] = x[rt - 1:rt, :]
    xs = x + (prev - x) * mu_ref[...]
    r = xs[:, :A_W]
    k = xs[:, A_W:2 * A_W]
    lo = xs[:, 3 * A_W:]
    lane = lax.broadcasted_iota(jnp.int32, lo.shape, 1)
    act = jnp.where(lane < A_DECAY_LORA, jnp.tanh(lo),
                    jnp.where(lane < A_DECAY_LORA + A_AAA_LORA, lo, _sigmoid(lo)))
    lora = _dot(act.astype(BF16), wl_ref[...])
    w0, a0, kkw, kaw, rkw = (vec_ref[n:n + 1, :] for n in range(5))
    lw = -math.exp(-0.5) * _sigmoid(w0 + lora[:, :A_W])
    a_sig = _sigmoid(a0 + lora[:, A_W:2 * A_W])
    k2 = k * (1.0 + (a_sig - 1.0) * kaw)

    ri = lax.broadcasted_iota(jnp.int32, (rt, rt), 0)
    ci = lax.broadcasted_iota(jnp.int32, (rt, rt), 1)
    same = jnp.right_shift(ri, 6) == jnp.right_shift(ci, 6)
    chunk_ones = same.astype(BF16)
    chunk_tri = jnp.where(ci <= ri, chunk_ones, jnp.zeros_like(chunk_ones))
    lp = _dot_sel_left(chunk_tri, lw)
    lpc = _dot_sel_left(chunk_ones, lw)
    ip = jnp.exp(-lp)
    dc = jnp.exp(lpc - lp)
    f_scr[F_RT] = r * jnp.exp(lp)
    f_scr[F_PM1] = jnp.exp(lp - lw)
    f_scr[F_IP] = ip
    f_scr[F_DC] = dc
    f_scr[F_KH] = k2 * ip
    f_scr[F_KC] = k2 * dc
    f_scr[F_V] = xs[:, 2 * A_W:3 * A_W]
    f_scr[F_KK] = k * kkw
    f_scr[F_ASIG] = a_sig
    f_scr[F_RKK] = r * k2 * rkw
    f_scr[F_G] = lora[:, 2 * A_W:]
    f_scr[F_PC] = jnp.exp(lpc)

    C = RW_C
    W2 = 2 * A_HD
    npairs = A_HEADS // 2
    lane_lo = lax.broadcasted_iota(jnp.int32, (C, W2), 1) < A_HD
    own = (lax.broadcasted_iota(jnp.int32, (2 * C, W2), 1) < A_HD) == \
          (lax.broadcasted_iota(jnp.int32, (2 * C, W2), 0) < C)
    r4 = lax.broadcasted_iota(jnp.int32, (4 * C, 4 * C), 0)
    c4 = lax.broadcasted_iota(jnp.int32, (4 * C, 4 * C), 1)
    keep = jnp.bitwise_and(c4, C - 1) < jnp.bitwise_and(r4, C - 1) + jnp.right_shift(r4, 7)
    eye_f = (lax.broadcasted_iota(jnp.int32, (W2, W2), 0) ==
             lax.broadcasted_iota(jnp.int32, (W2, W2), 1)).astype(F32)
    zeros_ww = jnp.zeros((W2, W2), F32)
    bf = lambda z: z.astype(BF16)
    stack = lambda z: jnp.concatenate([jnp.where(lane_lo, z, 0.0), jnp.where(lane_lo, 0.0, z)], axis=0)
    spread = lambda col: jnp.where(lane_lo, col[:C], col[C:])
    chunks_per_body = rt // C
    probs = [(cc, p) for cc in range(chunks_per_body) for p in range(npairs)]

    def body(j, _):
        rows = [pl.ds(pl.multiple_of((j * chunks_per_body + cc) * C, C), C) for cc in range(chunks_per_body)]
        ld = lambda f, q: f_scr[f, rows[q[0]], q[1] * W2:(q[1] + 1) * W2]
        a_l, r_l, bc_l, kc_l, v_l, g_l = [], [], [], [], [], []
        kk_sq = [jnp.sum(jnp.square(stack(ld(F_KK, q))), axis=-1, keepdims=True) for q in probs]
        bonus_l = [jnp.sum(stack(ld(F_RKK, q)), axis=-1, keepdims=True) for q in probs]
        for e, q in enumerate(probs):
            kk = ld(F_KK, q) * spread(1.0 / jnp.maximum(jnp.sqrt(kk_sq[e]), 1e-12))
            b = kk * ld(F_ASIG, q)
            a_l.append(stack(-kk * ld(F_PM1, q)))
            r_l.append(stack(ld(F_RT, q)))
            bc_l.append(stack(b * ld(F_DC, q)))
            kc_l.append(stack(ld(F_KC, q)))
            v_l.append(stack(ld(F_V, q)))
            ar = jnp.concatenate([a_l[-1], r_l[-1]], axis=0)
            bk = jnp.concatenate([stack(b * ld(F_IP, q)), stack(ld(F_KH, q))], axis=0)
            g_l.append(jnp.where(keep, _dot_nt(bf(ar), bf(bk)), 0.0))
        n = range(len(probs))
        x_l = [g_l[e][:W2, :W2] for e in n]
        t_l = [eye_f + x_l[e] for e in n]
        x_l = [_dot(bf(x_l[e]), bf(x_l[e])) for e in n]
        m = 2
        while m < C:
            last = 2 * m >= C
            for e in n:
                rhs = t_l[e] if last else jnp.concatenate([x_l[e], t_l[e]], axis=1)
                prod = _dot(bf(x_l[e]), bf(rhs))
                if last:
                    t_l[e] = t_l[e] + prod
                else:
                    x_l[e] = prod[:, :W2]
                    t_l[e] = t_l[e] + prod[:, W2:]
            m *= 2
        lv_l = [_dot(bf(g_l[e][:W2, W2:]), bf(v_l[e])) for e in n]
        apu_l = [_dot(bf(t_l[e]), bf(jnp.concatenate([a_l[e], lv_l[e]], axis=1))) for e in n]
        qy_l = []
        for e in n:
            low = jnp.concatenate([zeros_ww, v_l[e]], axis=1)
            qy_l.append(_dot(bf(g_l[e][W2:, :]), bf(jnp.concatenate([apu_l[e], low], axis=0))))
        mm_l = [_dot_tn(bf(apu_l[e][:, :W2]), bf(bc_l[e])) for e in n]
        n_l = []
        for e in n:
            uv = jnp.concatenate([apu_l[e][:, W2:], v_l[e]], axis=0)
            n_l.append(_dot_tn(bf(uv), bf(jnp.concatenate([bc_l[e], kc_l[e]], axis=0))))
        y_l = []
        for e, q in enumerate(probs):
            p = q[1]
            s0 = s_scr[p]
            s0b = bf(s0)
            y_l.append(_dot_nt(bf(r_l[e] + qy_l[e][:, :W2]), s0b) + qy_l[e][:, W2:])
            s_scr[p] = s0 * ld(F_PC, q)[0:1, :] + _dot(s0b, bf(mm_l[e])) + n_l[e]
        mu_l = [jnp.sum(y_l[e], axis=-1, keepdims=True) * (1.0 / A_HD) for e in n]
        d_l = [jnp.where(own, y_l[e] - mu_l[e], 0.0) for e in n]
        var_l = [jnp.sum(d_l[e] * d_l[e], axis=-1, keepdims=True) * (1.0 / A_HD) for e in n]
        for e, q in enumerate(probs):
            p = q[1]
            d = d_l[e] * lax.rsqrt(var_l[e] + A_LN_EPS)
            yn = (d[:C] + d[C:]) * ln_ref[:, p * W2:(p + 1) * W2]
            bonus = spread(bonus_l[e]) * ld(F_V, q)
            o_ref[rows[q[0]], p * W2:(p + 1) * W2] = ((yn + bonus) * ld(F_G, q)).astype(o_ref.dtype)
        return 0

    lax.fori_loop(0, rt // (C * chunks_per_body), body, 0)

    @pl.when(i == pl.num_programs(1) - 1)
    def _():
        for p in range(npairs):
            so_ref[0, 2 * p] = s_scr[p, :A_HD, :A_HD]
            so_ref[0, 2 * p + 1] = s_scr[p, A_HD:, A_HD:]


def _rw_chunk(ua, mu, wl, vec, ln, bsz, t_len, rt):
    nt = t_len // rt
    st_spec = pl.BlockSpec((1, A_HEADS, A_HD, A_HD), lambda b, i: (b, 0, 0, 0))
    return pl.pallas_call(
        _rw_chunk_kernel,
        grid=(bsz, nt),
        in_specs=[pl.BlockSpec((rt, A_SHIFT_W), lambda b, i: (b * nt + i, 0)),
                  _const_spec(mu.shape), _const_spec(wl.shape), _const_spec(vec.shape),
                  _const_spec(ln.shape)],
        out_specs=[pl.BlockSpec((rt, A_W), lambda b, i: (b * nt + i, 0)), st_spec],
        out_shape=[jax.ShapeDtypeStruct((bsz * t_len, A_W), BF16),
                   jax.ShapeDtypeStruct((bsz, A_HEADS, A_HD, A_HD), F32)],
        scratch_shapes=[pltpu.VMEM((1, A_SHIFT_W), F32),
                        pltpu.VMEM((A_HEADS // 2, 2 * A_HD, 2 * A_HD), F32),
                        pltpu.VMEM((12, rt, A_W), F32)],
        compiler_params=_cparams("parallel", "arbitrary"),
        name="rw_chunk",
    )(ua, mu, wl, vec, ln)


def _rw_post_kernel(y_ref, bonus_ref, g_ref, ln_ref, seg_ref, o_ref):
    seg = seg_ref[...]
    y = y_ref[...].T
    mu = _dot_sel(y, seg) * (1.0 / A_HD)
    d = y - mu
    var = _dot_sel(d * d, seg) * (1.0 / A_HD)
    yn = d * lax.rsqrt(var + A_LN_EPS) * ln_ref[...]
    o_ref[...] = ((yn + bonus_ref[...]) * g_ref[...]).astype(o_ref.dtype)


def _rw_post(y_t, bonus, g, ln, seg):
    n = bonus.shape[0]
    full = lambda shape: pl.BlockSpec(shape, lambda i: (0,) * len(shape))
    return pl.pallas_call(
        _rw_post_kernel,
        grid=(1,),
        in_specs=[full(a.shape) for a in (y_t, bonus, g, ln, seg)],
        out_specs=full((n, A_W)),
        out_shape=jax.ShapeDtypeStruct((n, A_W), BF16),
        compiler_params=_cparams("arbitrary"),
        name="rw_post",
    )(y_t, bonus, g, ln, seg)


def _head_norm_lanes(x, g, eps):
    mu = jnp.mean(x, axis=-1, keepdims=True)
    d = x - mu
    var = jnp.mean(d * d, axis=-1, keepdims=True)
    return d * lax.rsqrt(var + eps) * g


def _mlstm_chunk_kernel(ub_ref, g_ref, gt_ref, cw_ref, cb_ref, biasr_ref, biasc_ref, gn_ref,
                        y_ref, co_ref, no_ref, mo_ref,
                        tail_s, c_s, n_s, m_s):
    nb, L, _ = ub_ref.shape
    c = pl.program_id(1)

    @pl.when(c == 0)
    def _():
        tail_s[...] = jnp.zeros_like(tail_s)
        c_s[...] = jnp.zeros_like(c_s)
        n_s[...] = jnp.zeros_like(n_s)
        m_s[...] = jnp.zeros_like(m_s)

    r_i = lax.broadcasted_iota(jnp.int32, (L, L), 0)
    c_i = lax.broadcasted_iota(jnp.int32, (L, L), 1)
    causal = c_i <= r_i
    tri = causal.astype(BF16)
    tri_u = (r_i <= c_i).astype(BF16)
    heads = range(B_HEADS)
    sls = [slice(hd * B_HD, (hd + 1) * B_HD) for hd in heads]

    seqs = range(nb)
    ub_l, q_l, k_l, gc_l, gr_l, bcols_l, brows_l = [], [], [], [], [], [], []
    for bi in seqs:
        ub = ub_ref[bi]
        raw = ub[:, :2 * B_W]
        ext = jnp.concatenate([tail_s[bi], raw], axis=0)
        tail_s[bi] = raw[L - SUBLANES:]
        acc = cb_ref[...] + raw * cw_ref[B_CONV - 1:B_CONV, :]
        for jj in range(B_CONV - 1):
            sh = B_CONV - 1 - jj
            acc = acc + pltpu.roll(ext, sh, axis=0)[SUBLANES:] * cw_ref[jj:jj + 1, :]
        act = _silu(acc)
        ub_l.append(ub)
        q_l.append(act[:, :B_W])
        k_l.append(act[:, B_W:] * (B_HD ** -0.5))
        gc = g_ref[bi] + biasr_ref[...]
        gr = gt_ref[bi] + biasc_ref[...]
        gc_l.append(gc)
        gr_l.append(gr)
        bcols_l.append(_dot_sel_left(tri, _log_sigmoid(gc)))
        brows_l.append(_dot_sel(_log_sigmoid(gr), tri_u))
    m_all = [m_s[bi] for bi in seqs]
    n_all = [n_s[bi] for bi in seqs]

    probs = [(bi, hd) for bi in seqs for hd in heads]
    qb = [q_l[bi][:, sls[hd]].astype(BF16) for bi, hd in probs]
    vb = [ub_l[bi][:, 2 * B_W + hd * B_HD:2 * B_W + (hd + 1) * B_HD].astype(BF16) for bi, hd in probs]
    qk = [_dot_nt(qb[e], k_l[bi][:, sls[hd]].astype(BF16)) for e, (bi, hd) in enumerate(probs)]
    qc = [_dot(qb[e], c_s[bi * B_HEADS + hd].astype(BF16)) for e, (bi, hd) in enumerate(probs)]
    m_prev = [m_all[bi][hd:hd + 1, 0:1] for bi, hd in probs]
    n_prev = [n_all[bi][hd:hd + 1, :] for bi, hd in probs]
    bc = [_lane_col(bcols_l[bi], B_HEADS + hd) for bi, hd in probs]
    ic = [_lane_col(gc_l[bi], hd) for bi, hd in probs]
    dlog, inter, b_last, gi = [], [], [], []
    for e, (bi, hd) in enumerate(probs):
        br = brows_l[bi][B_HEADS + hd:B_HEADS + hd + 1, :]
        ir = gr_l[bi][hd:hd + 1, :]
        dlog.append(jnp.where(causal, bc[e] - br + ir, -jnp.inf))
        inter.append(bc[e] + m_prev[e])
        b_last.append(_row_pick(bc[e], L - 1))
        gi.append(b_last[e] - bc[e] + ic[e])
    row_max = [jnp.max(dlog[e], axis=1, keepdims=True) for e in range(len(probs))]
    gi_max = [jnp.max(gi[e], axis=0, keepdims=True) for e in range(len(probs))]
    s_l, iw, m_t, wc, kw, m_new = [], [], [], [], [], []
    for e, (bi, hd) in enumerate(probs):
        m_t.append(jnp.maximum(inter[e], row_max[e]))
        s_l.append(qk[e] * jnp.exp(dlog[e] - m_t[e]))
        iw.append(jnp.exp(inter[e] - m_t[e]))
        m_new.append(jnp.maximum(b_last[e] + m_prev[e], gi_max[e]))
        wc.append(jnp.exp(b_last[e] + m_prev[e] - m_new[e]))
        kw.append(k_l[bi][:, sls[hd]] * jnp.exp(gi[e] - m_new[e]))
    sv = [_dot(s_l[e].astype(BF16), vb[e]) for e in range(len(probs))]
    kv = [_dot_tn(kw[e].astype(BF16), vb[e]) for e in range(len(probs))]
    s_sum = [jnp.sum(s_l[e], axis=1, keepdims=True) for e in range(len(probs))]
    qn = [jnp.sum(q_l[bi][:, sls[hd]] * n_prev[e], axis=1, keepdims=True) for e, (bi, hd) in enumerate(probs)]
    k_sum = [jnp.sum(kw[e], axis=0, keepdims=True) for e in range(len(probs))]
    hb = []
    for e, (bi, hd) in enumerate(probs):
        den = s_sum[e] + iw[e] * qn[e]
        hh = (sv[e] + iw[e] * qc[e]) / jnp.maximum(jnp.abs(den), jnp.exp(-m_t[e]))
        hb.append(_sigmoid(ub_l[bi][:, 3 * B_W + hd * B_HD:3 * B_W + (hd + 1) * B_HD]) * hh)
        c_s[bi * B_HEADS + hd] = wc[e] * c_s[bi * B_HEADS + hd] + kv[e]
    mean = [jnp.sum(hb[e], axis=-1, keepdims=True) * (1.0 / B_HD) for e in range(len(probs))]
    dev = [hb[e] - mean[e] for e in range(len(probs))]
    var = [jnp.sum(dev[e] * dev[e], axis=-1, keepdims=True) * (1.0 / B_HD) for e in range(len(probs))]
    for e, (bi, hd) in enumerate(probs):
        y_ref[bi, :, sls[hd]] = (dev[e] * lax.rsqrt(var[e] + GN_EPS) * gn_ref[:, sls[hd]]).astype(y_ref.dtype)
    pad_rows = jnp.zeros((SUBLANES - B_HEADS, LANES), F32)
    for bi in seqs:
        es = [e for e, (bj, _) in enumerate(probs) if bj == bi]
        n_s[bi] = jnp.concatenate([wc[e] * n_prev[e] + k_sum[e] for e in es] + [pad_rows], axis=0)
        m_s[bi] = jnp.concatenate([jnp.broadcast_to(m_new[e], (1, LANES)) for e in es] + [pad_rows], axis=0)

    @pl.when(c == pl.num_programs(1) - 1)
    def _():
        for bi in range(nb):
            for hd in heads:
                co_ref[bi, hd] = c_s[bi * B_HEADS + hd]
        no_ref[...] = n_s[...]
        mo_ref[...] = m_s[...]


def _mlstm_chunk(ub, gates, gates_t, cw, cb, bias_r, bias_c, gn, bsz, t_len, nb):
    L = CHUNK
    seq = lambda w: pl.BlockSpec((nb, L, w), lambda b, c: (b, c, 0))
    per_b = lambda shp: pl.BlockSpec((nb,) + shp, lambda b, c: (b,) + (0,) * len(shp))
    return pl.pallas_call(
        _mlstm_chunk_kernel,
        grid=(bsz // nb, t_len // L),
        in_specs=[seq(4 * B_W), seq(LANES),
                  pl.BlockSpec((nb, SUBLANES, L), lambda b, c: (b, 0, c)),
                  _const_spec(cw.shape), _const_spec(cb.shape), _const_spec(bias_r.shape),
                  _const_spec(bias_c.shape), _const_spec(gn.shape)],
        out_specs=[seq(B_W), per_b((B_HEADS, B_HD, B_HD)), per_b((SUBLANES, B_HD)),
                   per_b((SUBLANES, LANES))],
        out_shape=[jax.ShapeDtypeStruct((bsz, t_len, B_W), BF16),
                   jax.ShapeDtypeStruct((bsz, B_HEADS, B_HD, B_HD), F32),
                   jax.ShapeDtypeStruct((bsz, SUBLANES, B_HD), F32),
                   jax.ShapeDtypeStruct((bsz, SUBLANES, LANES), F32)],
        scratch_shapes=[pltpu.VMEM((nb, SUBLANES, 2 * B_W), F32),
                        pltpu.VMEM((nb * B_HEADS, B_HD, B_HD), F32),
                        pltpu.VMEM((nb, SUBLANES, B_HD), F32),
                        pltpu.VMEM((nb, SUBLANES, LANES), F32)],
        compiler_params=_cparams("parallel", "arbitrary"),
        name="mlstm_chunk",
    )(ub, gates, gates_t, cw, cb, bias_r, bias_c, gn)


def _mlstm_step_kernel(ub_ref, g_ref, conv_ref, cw_ref, cb_ref, biasr_ref, gn_ref, c_ref, n_ref, m_ref,
                       y_ref, co_ref, no_ref, mo_ref):
    bb = ub_ref.shape[0]
    ub = ub_ref[...]
    raw = ub[:, :2 * B_W]
    conv = conv_ref[...]
    acc = cb_ref[...] + raw * cw_ref[B_CONV - 1:B_CONV, :]
    for jj in range(B_CONV - 1):
        acc = acc + conv[:, jj * 2 * B_W:(jj + 1) * 2 * B_W] * cw_ref[jj:jj + 1, :]
    qk = _silu(acc)
    q_all = qk[:, :B_W]
    k_all = qk[:, B_W:] * (B_HD ** -0.5)
    gc = g_ref[...] + biasr_ref[...]
    lf = _log_sigmoid(gc)
    m_all = m_ref[...]
    row_pad = lax.broadcasted_iota(jnp.int32, (LANES, LANES), 0)
    pad = jnp.zeros((LANES - bb, B_HD), F32)
    m_out = jnp.zeros((bb, LANES), F32)
    lane_m = lax.broadcasted_iota(jnp.int32, (bb, LANES), 1)
    for hd in range(B_HEADS):
        sl = slice(hd * B_HD, (hd + 1) * B_HD)
        q = q_all[:, sl]
        k = k_all[:, sl]
        v = ub[:, 2 * B_W + hd * B_HD:2 * B_W + (hd + 1) * B_HD]
        o = ub[:, 3 * B_W + hd * B_HD:3 * B_W + (hd + 1) * B_HD]
        ic = _lane_col(gc, hd)
        f = _lane_col(lf, B_HEADS + hd)
        m_prev = _lane_col(m_all, hd)
        n_prev = n_ref[:, sl]
        m_t = jnp.maximum(f + m_prev, ic)
        s = jnp.sum(q * k, axis=1, keepdims=True) * jnp.exp(ic - m_t)
        iw = jnp.exp(f + m_prev - m_t)
        wi = jnp.exp(ic - m_t)
        qb = q.astype(BF16)
        kw_t = jnp.concatenate([k * wi, pad], axis=0).T
        v_pad = jnp.concatenate([v, pad], axis=0)
        qc = jnp.zeros((bb, B_HD), F32)
        row_b = lax.broadcasted_iota(jnp.int32, (bb, B_HD), 0)
        for bi in range(bb):
            c_prev = c_ref[bi, hd]
            qc = qc + jnp.where(row_b == bi, _dot(qb, c_prev.astype(BF16)), 0.0)
            v_one = jnp.where(row_pad == bi, v_pad, 0.0)
            co_ref[bi, hd] = _row_pick(iw, bi) * c_prev + _dot3(kw_t, v_one)
        num = s * v + iw * qc
        den = s + iw * jnp.sum(q * n_prev, axis=1, keepdims=True)
        hh = num / jnp.maximum(jnp.abs(den), jnp.exp(-m_t))
        no_ref[:, sl] = iw * n_prev + wi * k
        m_out = jnp.where(lane_m == hd, m_t, m_out)
        hb = _sigmoid(o) * hh
        y_ref[:, sl] = _head_norm_lanes(hb, gn_ref[:, sl], GN_EPS).astype(y_ref.dtype)
    mo_ref[...] = m_out


def _mlstm_step(ub, gates, conv, cw, cb, bias_r, gn, c0, n0, m0, bb):
    bsz = ub.shape[0]
    row = lambda w: pl.BlockSpec((bb, w), lambda i: (i, 0))
    c_spec = pl.BlockSpec((bb, B_HEADS, B_HD, B_HD), lambda i: (i, 0, 0, 0))
    return pl.pallas_call(
        _mlstm_step_kernel,
        grid=(bsz // bb,),
        in_specs=[row(4 * B_W), row(LANES), row((B_CONV - 1) * 2 * B_W),
                  _const_spec(cw.shape), _const_spec(cb.shape), _const_spec(bias_r.shape),
                  _const_spec(gn.shape), c_spec, row(B_W), row(LANES)],
        out_specs=[row(B_W), c_spec, row(B_W), row(LANES)],
        out_shape=[jax.ShapeDtypeStruct((bsz, B_W), BF16),
                   jax.ShapeDtypeStruct(c0.shape, F32),
                   jax.ShapeDtypeStruct((bsz, B_W), F32),
                   jax.ShapeDtypeStruct((bsz, LANES), F32)],
        compiler_params=_cparams("parallel"),
        name="mlstm_step",
    )(ub, gates, conv, cw, cb, bias_r, gn, c0, n0, m0)


def _ret_log_gamma(hd):
    return math.log1p(-(2.0 ** (-5.0 - hd)))


def _ret_chunk_kernel(q_ref, qd_ref, k_ref, kd_ref, v_ref, g_ref, y_ref, so_ref, s_s, mask_s):
    nb, L, _ = q_ref.shape
    c = pl.program_id(1)

    @pl.when(c == 0)
    def _():
        s_s[...] = jnp.zeros_like(s_s)
        r_i = lax.broadcasted_iota(jnp.int32, (L, L), 0)
        c_i = lax.broadcasted_iota(jnp.int32, (L, L), 1)
        diff = (r_i - c_i).astype(F32)
        for hd in range(C_HEADS):
            mask_s[hd] = jnp.where(diff >= 0, jnp.exp(jnp.maximum(diff, 0.0) * _ret_log_gamma(hd)), 0.0)

    heads = range(C_HEADS)
    kls = [slice(hd * C_DK, (hd + 1) * C_DK) for hd in heads]
    sls = [slice(hd * C_DV, (hd + 1) * C_DV) for hd in heads]
    probs = [(bi, hd) for bi in range(nb) for hd in heads]
    n = range(len(probs))

    sc = [(_dot_nt(q_ref[bi, :, kls[hd]], k_ref[bi, :, kls[hd]]) * mask_s[hd]).astype(BF16) for bi, hd in probs]
    o_l = []
    for e, (bi, hd) in enumerate(probs):
        sl = sls[hd]
        c_dec = math.exp(L * _ret_log_gamma(hd))
        s_prev = s_s[e]
        o_l.append(_dot(sc[e], v_ref[bi, :, sl]) + _dot(qd_ref[bi, :, kls[hd]], s_prev.astype(BF16)))
        s_s[e] = c_dec * s_prev + _dot_tn(kd_ref[bi, :, kls[hd]], v_ref[bi, :, sl])
    mean = [jnp.sum(o_l[e], axis=-1, keepdims=True) * (1.0 / C_DV) for e in n]
    dev = [o_l[e] - mean[e] for e in n]
    var = [jnp.sum(dev[e] * dev[e], axis=-1, keepdims=True) * (1.0 / C_DV) for e in n]
    for e, (bi, hd) in enumerate(probs):
        sl = sls[hd]
        y_ref[bi, :, sl] = (dev[e] * lax.rsqrt(var[e] + GN_EPS) * g_ref[bi, :, sl]).astype(y_ref.dtype)

    @pl.when(c == pl.num_programs(1) - 1)
    def _():
        for e, (bi, hd) in enumerate(probs):
            so_ref[bi, hd] = s_s[e]


def _ret_chunk(q, qd, k, kd, v, g, bsz, t_len, nb):
    L = RET_CHUNK
    seq = lambda w: pl.BlockSpec((nb, L, w), lambda b, c: (b, c, 0))
    st = pl.BlockSpec((nb, C_HEADS, C_DK, C_DV), lambda b, c: (b, 0, 0, 0))
    return pl.pallas_call(
        _ret_chunk_kernel,
        grid=(bsz // nb, t_len // L),
        in_specs=[seq(C_HEADS * C_DK)] * 4 + [seq(C_OUT_W), seq(C_OUT_W)],
        out_specs=[seq(C_OUT_W), st],
        out_shape=[jax.ShapeDtypeStruct((bsz, t_len, C_OUT_W), BF16),
                   jax.ShapeDtypeStruct((bsz, C_HEADS, C_DK, C_DV), F32)],
        scratch_shapes=[pltpu.VMEM((nb * C_HEADS, C_DK, C_DV), F32), pltpu.VMEM((C_HEADS, L, L), F32)],
        compiler_params=_cparams("parallel", "arbitrary"),
        name="ret_chunk",
    )(q, qd, k, kd, v, g)


def _ret_step_kernel(q_ref, k_ref, v_ref, g_ref, s_ref, y_ref, so_ref):
    bb = q_ref.shape[0]
    hd = pl.program_id(1)
    gam = 1.0 - jnp.exp2(-5.0 - jnp.full((1, 1), hd).astype(F32))
    q = q_ref[...]
    k = k_ref[...]
    v = v_ref[...]
    s = jnp.sum(q * k, axis=1, keepdims=True)
    pad = jnp.zeros((LANES - bb, C_DK), F32)
    k_t = jnp.concatenate([k, pad], axis=0).T
    q_t = jnp.concatenate([q * gam, pad], axis=0).T
    row_b = lax.broadcasted_iota(jnp.int32, (bb, C_DV), 0)
    qs = jnp.zeros((bb, C_DV), F32)
    for bi in range(bb):
        s_prev = s_ref[bi, 0]
        qs = jnp.where(row_b == bi, jnp.sum(_lane_col(q_t, bi) * s_prev, axis=0, keepdims=True), qs)
        so_ref[bi, 0] = gam * s_prev + _lane_col(k_t, bi) * v[bi:bi + 1, :]
    o = s * v + qs
    y = _head_norm_lanes(o, g_ref[...], GN_EPS)
    y_ref[...] = y.astype(y_ref.dtype)


def _ret_step(q, k, v, g, s0, bb):
    bsz = q.shape[0]
    st = pl.BlockSpec((bb, 1, C_DK, C_DV), lambda i, h: (i, h, 0, 0))
    return pl.pallas_call(
        _ret_step_kernel,
        grid=(bsz // bb, C_HEADS),
        in_specs=[pl.BlockSpec((bb, C_DK), lambda i, h: (i, h)),
                  pl.BlockSpec((bb, C_DK), lambda i, h: (i, h)),
                  pl.BlockSpec((bb, C_DV), lambda i, h: (i, h)),
                  pl.BlockSpec((bb, C_DV), lambda i, h: (i, h)), st],
        out_specs=[pl.BlockSpec((bb, C_DV), lambda i, h: (i, h)), st],
        out_shape=[jax.ShapeDtypeStruct((bsz, C_OUT_W), BF16),
                   jax.ShapeDtypeStruct(s0.shape, F32)],
        compiler_params=_cparams("parallel", "parallel"),
        name="ret_step",
    )(q, k, v, g, s0)


def _post_kernel(*refs, n_mix):
    h_ref = refs[0]
    mix_refs = refs[1:1 + n_mix]
    wout_refs = refs[1 + n_mix:1 + 2 * n_mix]
    (p_ref, g_ref, up_ref, down_ref, gate_ref, proj_ref, o_ref, h1_s, xn_s, acc_s) = refs[1 + 2 * n_mix:]
    f = pl.program_id(1)

    tm = h_ref.shape[0]
    halves = [slice(0, tm // 2), slice(tm // 2, tm)] if tm >= 2 * LANES else [slice(0, tm)]

    @pl.when(f == 0)
    def _():
        for rows in halves:
            mix = _dot(mix_refs[0][rows, :], wout_refs[0][...])
            for m_ref, w_ref in zip(mix_refs[1:], wout_refs[1:]):
                mix = mix + _dot(m_ref[rows, :], w_ref[...])
            h1 = h_ref[rows, :] + _rms(mix, g_ref[1:2, :])
            h1_s[rows, :] = h1
            xn_s[rows, :] = _rms(h1, g_ref[2:3, :]).astype(BF16)
        acc_s[...] = jnp.zeros_like(acc_s)

    hid = jnp.square(jnp.maximum(_dot(xn_s[...], up_ref[...]), 0.0))
    acc_s[...] += _dot(hid.astype(BF16), down_ref[...])

    @pl.when(f == pl.num_programs(1) - 1)
    def _():
        for rows in halves:
            h2 = h1_s[rows, :] + _rms(acc_s[rows, :], g_ref[3:4, :])
            gate = _sigmoid(_dot(h2.astype(BF16), gate_ref[...]))
            emb = _dot(p_ref[rows, :].astype(BF16), proj_ref[...])
            o_ref[rows, :] = h2 + gate * emb


def _post(h, mixes, wouts, p, g, up, down, gate, proj, li, tm, tf):
    n = h.shape[0]
    n_mix = len(mixes)
    row = lambda w: pl.BlockSpec((tm, w), lambda i, f: (i, 0))
    layer = lambda a: pl.BlockSpec((None,) + a.shape[1:], lambda i, f: (li,) + (0,) * (a.ndim - 1))
    return pl.pallas_call(
        functools.partial(_post_kernel, n_mix=n_mix),
        grid=(n // tm, D_FF // tf),
        in_specs=[row(D_MODEL)] + [row(m.shape[1]) for m in mixes]
        + [_const_spec(w.shape) for w in wouts]
        + [pl.BlockSpec((None, tm, D_PLE), lambda i, f: (li, i, 0)), layer(g),
           pl.BlockSpec((None, D_MODEL, tf), lambda i, f: (li, 0, f)),
           pl.BlockSpec((None, tf, D_MODEL), lambda i, f: (li, f, 0)),
           layer(gate), layer(proj)],
        out_specs=row(D_MODEL),
        out_shape=jax.ShapeDtypeStruct((n, D_MODEL), F32),
        scratch_shapes=[pltpu.VMEM((tm, D_MODEL), F32), pltpu.VMEM((tm, D_MODEL), BF16),
                        pltpu.VMEM((tm, D_MODEL), F32)],
        compiler_params=_cparams("parallel", "arbitrary"),
        name="post",
    )(h, *mixes, *wouts, p, g, up, down, gate, proj)


def _prep_weights(W):
    f = {}
    w_in = W['mix0_w_in'][0]
    w_in_t = w_in.T
    f['w_a'] = w_in_t[:A_SHIFT_W].astype(BF16)
    f['w_b'] = w_in_t[A_SHIFT_W:A_SHIFT_W + 4 * B_W].astype(BF16)
    f['w_g'] = jnp.pad(w_in_t[A_SHIFT_W + 4 * B_W:], ((0, LANES - 2 * B_HEADS), (0, 0))).astype(BF16)
    wl = jnp.zeros((A_LORA_W, 3 * A_W), F32)
    wl = wl.at[:A_DECAY_LORA, :A_W].set(W['rw_w2'][0])
    wl = wl.at[A_DECAY_LORA:A_DECAY_LORA + A_AAA_LORA, A_W:2 * A_W].set(W['rw_a2'][0])
    wl = wl.at[A_DECAY_LORA + A_AAA_LORA:, 2 * A_W:].set(W['rw_g2'][0])
    f['w_lora'] = wl.astype(BF16)
    f['rw_vec'] = jnp.concatenate(
        [W['rw_w0'], W['rw_a0'], W['rw_kk'], W['rw_ka'], W['rw_rk'][0].reshape(1, A_W),
         jnp.zeros((3, A_W), F32)], axis=0)
    f['rw_mu'] = W['rw_mu']
    f['rw_ln'] = W['rw_ln']
    f['seg'] = (jnp.arange(A_W)[:, None] // A_HD == jnp.arange(A_W)[None, :] // A_HD).astype(BF16)
    f['cw'] = W['ml_conv_w'][0]
    f['cb'] = W['ml_conv_b']
    gate_bias = jnp.concatenate([W['ml_i_bias'][0], W['ml_f_bias'][0]])
    f['bias_r'] = jnp.pad(gate_bias, (0, LANES - 2 * B_HEADS)).reshape(1, LANES)
    f['bias_c'] = gate_bias.reshape(2 * B_HEADS, 1)
    f['ml_gn'] = W['ml_gn']
    w_out0 = W['mix0_w_out'][0].astype(BF16)
    f['w_out_a'] = w_out0[:A_W]
    f['w_out_b'] = w_out0[A_W:]
    w_ret = W['ret_w_in'][0]
    nq = C_HEADS * C_DK
    f['w_rq'] = w_ret[:, :nq].astype(BF16)
    f['w_rk'] = w_ret[:, nq:2 * nq].astype(BF16)
    f['w_rv'] = w_ret[:, 2 * nq:2 * nq + C_OUT_W].astype(BF16)
    f['w_rg'] = w_ret[:, 2 * nq + C_OUT_W:].astype(BF16)
    f['ret_gn'] = W['ret_gn']
    f['w_ret_out'] = W['ret_w_out'][0].astype(BF16)
    for name in ('ffn_up', 'ffn_down', 'ple_gate', 'ple_proj'):
        f[name] = W[name].astype(BF16)
    f['norm_g'] = W['norm_g']
    return f


def _rope_tables(pos):
    half = C_DK // 2
    inv = ROPE_BASE ** (-jnp.arange(half, dtype=F32) / half)
    ang = pos.astype(F32)[:, None] * inv[None, :]
    return jnp.cos(ang), jnp.sin(ang)


def _even_layer(h, p, st, f, bsz, t_len):
    n = bsz * t_len
    seq = st is None
    tm = 512 if seq else n
    ua, ub, gates = _inproj(h, f['norm_g'][0], [f['w_a'], f['w_b'], f['w_g']], [F32, F32, F32], tm)
    qk_raw = ub.reshape(bsz, t_len, 4 * B_W)[:, :, :2 * B_W]

    if seq:
        ya, s_a_new = _rw_chunk(ua, f['rw_mu'], f['w_lora'], f['rw_vec'], f['rw_ln'], bsz, t_len, rt=256)
        conv_new = qk_raw[:, -(B_CONV - 1):]
    else:
        shift, s_a, conv, c0, n0, m0 = st
        conv_new = jnp.concatenate([conv, qk_raw], axis=1)[:, -(B_CONV - 1):]
        r, w, k, v, a, b, g, bonus = _rw_prep(ua, shift, f['rw_mu'], f['w_lora'], f['rw_vec'], f['seg'])
        y_t, s_t = _rw_step(r, w, k, v, a, b, jnp.transpose(s_a, (1, 2, 3, 0)))
        s_a_new = jnp.transpose(s_t, (3, 0, 1, 2))
        ya = _rw_post(y_t, bonus, g, f['rw_ln'], f['seg'])

    if seq:
        gates3 = gates.reshape(bsz, t_len, LANES)
        gates_t = gates3[:, :, :SUBLANES].transpose(0, 2, 1)
        yb, c_new, n_new, m_new = _mlstm_chunk(ub.reshape(bsz, t_len, 4 * B_W), gates3, gates_t, f['cw'],
                                               f['cb'], f['bias_r'], f['bias_c'], f['ml_gn'], bsz, t_len,
                                               nb=4)
        yb = yb.reshape(n, B_W)
        n_new = n_new[:, :B_HEADS]
        m_new = m_new[:, :B_HEADS, 0]
    else:
        m0p = jnp.pad(m0, ((0, 0), (0, LANES - B_HEADS)))
        yb, c_new, n_new, m_new = _mlstm_step(ub, gates, conv.reshape(bsz, -1), f['cw'], f['cb'],
                                              f['bias_r'], f['ml_gn'], c0, n0.reshape(bsz, B_W), m0p,
                                              bb=SUBLANES)
        n_new = n_new.reshape(bsz, B_HEADS, B_HD)
        m_new = m_new[:, :B_HEADS]

    h_next = _post(h, [ya, yb], [f['w_out_a'], f['w_out_b']], p, f['norm_g'], f['ffn_up'],
                   f['ffn_down'], f['ple_gate'], f['ple_proj'], li=0, tm=tm, tf=POST_TF)

    shift_new = ua.reshape(bsz, t_len, A_SHIFT_W)[:, -1]
    return h_next, (shift_new, s_a_new, conv_new, c_new, n_new, m_new)


def _odd_layer(h, p, s0, pos, f, bsz, t_len):
    n = bsz * t_len
    seq = s0 is None
    cos, sin = _rope_tables(pos)
    tm = 512 if seq else n
    ws = [f['w_rq'], f['w_rk'], f['w_rv'], f['w_rg']]
    if seq:
        q, qd, k, kd, v, g = _inproj_ret(h, f['norm_g'][1], f['ret_gn'], cos, sin, ws, tm, t_len // tm,
                                         [BF16] * 5 + [F32], decayed=True)
        as3 = lambda a: a.reshape(bsz, t_len, a.shape[-1])
        y, s_new = _ret_chunk(as3(q), as3(qd), as3(k), as3(kd), as3(v), as3(g), bsz, t_len, nb=2)
        y = y.reshape(n, C_OUT_W)
    else:
        q, k, v, g = _inproj_ret(h, f['norm_g'][1], f['ret_gn'], cos, sin, ws, tm, 1, [F32] * 4,
                                 decayed=False)
        y, s_new = _ret_step(q, k, v, g, s0, bb=2 * SUBLANES)
    h_next = _post(h, [y], [f['w_ret_out']], p, f['norm_g'], f['ffn_up'], f['ffn_down'],
                   f['ple_gate'], f['ple_proj'], li=1, tm=512 if seq else n, tf=POST_TF)
    return h_next, s_new


def _trunk(x, p, pos, ev_states, od_state, f):
    bsz, t_len, _ = x.shape
    n = bsz * t_len
    h = x.reshape(n, D_MODEL)
    ev_in = None if ev_states is None else tuple(s[0] for s in ev_states)
    od_in = None if od_state is None else od_state[0]
    p = p.reshape(p.shape[0], n, D_PLE)
    h, ev_new = _even_layer(h, p, ev_in, f, bsz, t_len)
    h, od_new = _odd_layer(h, p, od_in, pos, f, bsz, t_len)
    return h.reshape(bsz, t_len, D_MODEL), [s[None] for s in ev_new], od_new[None]


def kernel(x_prompt, x_sample, state_rwkv_shift, state_rwkv_S, state_mlstm_conv, state_mlstm_C,
           state_mlstm_n, state_mlstm_m, state_ret_S, p_prompt, p_sample, norm_g, ffn_up, ffn_down,
           ple_gate, ple_proj, mix0_w_in, rw_mu, rw_w0, rw_w2, rw_a0, rw_a2, rw_g2, rw_kk, rw_ka,
           rw_rk, rw_ln, ml_conv_w, ml_conv_b, ml_i_bias, ml_f_bias, ml_gn, mix0_w_out,
           ret_w_in, ret_gn, ret_w_out):
    W = dict(norm_g=norm_g, ffn_up=ffn_up, ffn_down=ffn_down, ple_gate=ple_gate, ple_proj=ple_proj,
             mix0_w_in=mix0_w_in, rw_mu=rw_mu, rw_w0=rw_w0, rw_w2=rw_w2, rw_a0=rw_a0, rw_a2=rw_a2,
             rw_g2=rw_g2, rw_kk=rw_kk, rw_ka=rw_ka, rw_rk=rw_rk, rw_ln=rw_ln, ml_conv_w=ml_conv_w,
             ml_conv_b=ml_conv_b, ml_i_bias=ml_i_bias, ml_f_bias=ml_f_bias, ml_gn=ml_gn,
             mix0_w_out=mix0_w_out, ret_w_in=ret_w_in, ret_gn=ret_gn, ret_w_out=ret_w_out)
    f = _prep_weights(W)
    bp, tp = x_prompt.shape[:2]
    ts = x_sample.shape[1]
    y_prompt, ev_p, ret_p = _trunk(x_prompt, p_prompt, jnp.arange(tp), None, None, f)
    ev_s_in = (state_rwkv_shift, state_rwkv_S, state_mlstm_conv, state_mlstm_C, state_mlstm_n,
               state_mlstm_m)
    y_sample, ev_s, ret_s = _trunk(x_sample, p_sample, PAST_LEN + jnp.arange(ts), ev_s_in, state_ret_S, f)
    return (y_prompt, y_sample, *ev_p, ret_p, *ev_s, ret_s)
```

```python
import functools
import math

import jax
import jax.numpy as jnp
from jax import lax
from jax.experimental import pallas as pl
from jax.experimental.pallas import tpu as pltpu

F32 = jnp.float32
BF16 = jnp.bfloat16

D_MODEL = 1024
D_PLE = 256
D_FF = 4 * D_MODEL
RMS_EPS = 1e-6
GN_EPS = 1e-5
CHUNK = 128
RET_CHUNK = 256

A_HEADS = 8
A_HD = 64
A_W = A_HEADS * A_HD
A_DECAY_LORA = 64
A_AAA_LORA = 64
A_GATE_LORA = 128
A_LORA_W = A_DECAY_LORA + A_AAA_LORA + A_GATE_LORA
A_SHIFT_W = 3 * A_W + A_LORA_W
A_LN_EPS = 64e-5

B_HEADS = 4
B_HD = 128
B_W = B_HEADS * B_HD
B_CONV = 4

C_HEADS = 4
C_DK = 256
C_DV = 512
C_OUT_W = C_HEADS * C_DV
ROPE_BASE = 10000.0
PAST_LEN = 16384

LANES = 128
SUBLANES = 8
VMEM_LIMIT = 56 * 1024 * 1024
POST_TF = 2048


def _cparams(*sem):
    return pltpu.CompilerParams(dimension_semantics=sem, vmem_limit_bytes=VMEM_LIMIT)


def _const_spec(shape):
    nd = len(shape)
    return pl.BlockSpec(shape, lambda *_: (0,) * nd, pipeline_mode=pl.Buffered(1))


def _rms(x, g):
    return x * lax.rsqrt(jnp.mean(x * x, axis=-1, keepdims=True) + RMS_EPS) * g


def _dot(a, b):
    return jnp.dot(a, b, preferred_element_type=F32)


def _dot_nt(a, b):
    return lax.dot_general(a, b, (((1,), (1,)), ((), ())), preferred_element_type=F32)


def _dot_tn(a, b):
    return lax.dot_general(a, b, (((0,), (0,)), ((), ())), preferred_element_type=F32)


def _hilo(x):
    hi = x.astype(BF16)
    lo = (x - hi.astype(F32)).astype(BF16)
    return hi, lo


def _dot_sel(x, sel):
    hi, lo = _hilo(x)
    return _dot(hi, sel) + _dot(lo, sel)


def _dot_sel_left(sel, x):
    hi, lo = _hilo(x)
    return _dot(sel, hi) + _dot(sel, lo)


def _dot3(a, b):
    ah, al = _hilo(a)
    bh, bl = _hilo(b)
    return _dot(ah, bh) + _dot(ah, bl) + _dot(al, bh)


def _sigmoid(x):
    return 1.0 / (1.0 + jnp.exp(-x))


def _softplus(x):
    return jnp.maximum(x, 0.0) + jnp.log1p(jnp.exp(-jnp.abs(x)))


def _log_sigmoid(x):
    return -_softplus(-x)


def _silu(x):
    return x * _sigmoid(x)


def _lane_col(x, idx):
    lane = lax.broadcasted_iota(jnp.int32, x.shape, 1)
    return jnp.sum(jnp.where(lane == idx, x, 0.0), axis=1, keepdims=True)


def _row_pick(x, idx):
    row = lax.broadcasted_iota(jnp.int32, x.shape, 0)
    return jnp.sum(jnp.where(row == idx, x, 0.0), axis=0, keepdims=True)


def _inproj_kernel(x_ref, g_ref, *refs):
    n = len(refs) // 2
    xn = _rms(x_ref[...], g_ref[0:1, :]).astype(BF16)
    for w_ref, o_ref in zip(refs[:n], refs[n:]):
        o_ref[...] = _dot_nt(xn, w_ref[...]).astype(o_ref.dtype)


def _inproj(h, g, ws, dtypes, tm):
    n = h.shape[0]
    return pl.pallas_call(
        _inproj_kernel,
        grid=(n // tm,),
        in_specs=[pl.BlockSpec((tm, D_MODEL), lambda i: (i, 0)), _const_spec(g.shape)]
        + [_const_spec(w.shape) for w in ws],
        out_specs=[pl.BlockSpec((tm, w.shape[0]), lambda i: (i, 0)) for w in ws],
        out_shape=[jax.ShapeDtypeStruct((n, w.shape[0]), dt) for w, dt in zip(ws, dtypes)],
        compiler_params=_cparams("parallel"),
        name="inproj",
    )(h, g, *ws)


def _inproj_ret_kernel(x_ref, g_ref, gn_ref, cos_ref, sin_ref, wq_ref, wk_ref, wv_ref, wg_ref, *out_refs,
                       decayed):
    if decayed:
        q_ref, qd_ref, k_ref, kd_ref, v_ref, gate_ref = out_refs
    else:
        q_ref, k_ref, v_ref, gate_ref = out_refs
        qd_ref = kd_ref = None
    xn = _rms(x_ref[...], g_ref[0:1, :]).astype(BF16)
    cos = cos_ref[...]
    sin = sin_ref[...]
    half = C_DK // 2
    tm = x_ref.shape[0]
    row0 = pl.program_id(0) * tm
    pos = jnp.bitwise_and(lax.broadcasted_iota(jnp.int32, (tm, half), 0) + row0, RET_CHUNK - 1).astype(F32)
    gate_ref[...] = (_silu(_dot(xn, wg_ref[...])) * gn_ref[...]).astype(gate_ref.dtype)
    for w_ref, o_ref, d_ref, scale in ((wq_ref, q_ref, qd_ref, None), (wk_ref, k_ref, kd_ref, C_DK ** -0.5)):
        u = _dot(xn, w_ref[...])
        for hd in range(C_HEADS):
            x1 = u[:, hd * C_DK:hd * C_DK + half]
            x2 = u[:, hd * C_DK + half:(hd + 1) * C_DK]
            y1 = x1 * cos - x2 * sin
            y2 = x1 * sin + x2 * cos
            if scale is not None:
                y1 = y1 * scale
                y2 = y2 * scale
            o_ref[:, hd * C_DK:hd * C_DK + half] = y1.astype(o_ref.dtype)
            o_ref[:, hd * C_DK + half:(hd + 1) * C_DK] = y2.astype(o_ref.dtype)
            if decayed:
                lg = _ret_log_gamma(hd)
                dec = jnp.exp((pos + 1.0) * lg) if d_ref is qd_ref else jnp.exp((RET_CHUNK - 1.0 - pos) * lg)
                d_ref[:, hd * C_DK:hd * C_DK + half] = (y1 * dec).astype(d_ref.dtype)
                d_ref[:, hd * C_DK + half:(hd + 1) * C_DK] = (y2 * dec).astype(d_ref.dtype)
    v_ref[...] = _dot(xn, wv_ref[...]).astype(v_ref.dtype)


def _inproj_ret(h, g, gn, cos, sin, ws, tm, seq_tiles, dtypes, decayed):
    n = h.shape[0]
    if cos.shape[0] == 1:
        cs_spec = _const_spec(cos.shape)
    else:
        cs_spec = pl.BlockSpec((tm, cos.shape[1]), lambda i: (i % seq_tiles, 0))
    widths = [w.shape[1] for w in ws]
    if decayed:
        widths = [widths[0], widths[0], widths[1], widths[1], widths[2], widths[3]]
    return pl.pallas_call(
        functools.partial(_inproj_ret_kernel, decayed=decayed),
        grid=(n // tm,),
        in_specs=[pl.BlockSpec((tm, D_MODEL), lambda i: (i, 0)), _const_spec(g.shape), _const_spec(gn.shape),
                  cs_spec, cs_spec] + [_const_spec(w.shape) for w in ws],
        out_specs=[pl.BlockSpec((tm, wd), lambda i: (i, 0)) for wd in widths],
        out_shape=[jax.ShapeDtypeStruct((n, wd), dt) for wd, dt in zip(widths, dtypes)],
        compiler_params=_cparams("parallel"),
        name="inproj_ret",
    )(h, g, gn, cos, sin, *ws)


def _rw_prep_kernel(x_ref, sh_ref, mu_ref, wl_ref, vec_ref, seg_ref,
                    r_ref, w_ref, k_ref, v_ref, a_ref, b_ref, g_ref, bonus_ref):
    x = x_ref[...]
    xs = x + (sh_ref[...] - x) * mu_ref[...]
    r = xs[:, :A_W]
    k = xs[:, A_W:2 * A_W]
    v = xs[:, 2 * A_W:3 * A_W]
    lo = xs[:, 3 * A_W:]
    lane = lax.broadcasted_iota(jnp.int32, lo.shape, 1)
    act = jnp.where(lane < A_DECAY_LORA, jnp.tanh(lo),
                    jnp.where(lane < A_DECAY_LORA + A_AAA_LORA, lo, _sigmoid(lo)))
    lora = _dot(act.astype(BF16), wl_ref[...])
    w0, a0, kkw, kaw, rkw = (vec_ref[i:i + 1, :] for i in range(5))
    w_log = -_softplus(-(w0 + lora[:, :A_W])) - 0.5
    decay = jnp.exp(-jnp.exp(w_log))
    a = _sigmoid(a0 + lora[:, A_W:2 * A_W])
    g = lora[:, 2 * A_W:]
    seg = seg_ref[...]
    kk = k * kkw
    kk = kk / jnp.maximum(jnp.sqrt(_dot_sel(kk * kk, seg)), 1e-12)
    k2 = k * (1.0 + (a - 1.0) * kaw)
    bonus = _dot_sel(r * k2 * rkw, seg) * v
    r_ref[...] = r.T
    w_ref[...] = decay.T
    k_ref[...] = k2.T
    v_ref[...] = v.T
    a_ref[...] = (-kk).T
    b_ref[...] = (kk * a).T
    g_ref[...] = g
    bonus_ref[...] = bonus


def _rw_prep(ua, shift, mu, wl, vec, seg):
    n = ua.shape[0]
    full = lambda shape: pl.BlockSpec(shape, lambda i: (0,) * len(shape))
    return pl.pallas_call(
        _rw_prep_kernel,
        grid=(1,),
        in_specs=[full(a.shape) for a in (ua, shift, mu, wl, vec, seg)],
        out_specs=[full((A_W, n))] * 6 + [full((n, A_W))] * 2,
        out_shape=[jax.ShapeDtypeStruct((A_W, n), F32)] * 6 + [jax.ShapeDtypeStruct((n, A_W), F32)] * 2,
        compiler_params=_cparams("arbitrary"),
        name="rw_prep",
    )(ua, shift, mu, wl, vec, seg)


def _rw_step_kernel(r_ref, w_ref, k_ref, v_ref, a_ref, b_ref, s_ref, y_ref, so_ref):
    rt, wt, kt, at, bt = r_ref[...], w_ref[...], k_ref[...], a_ref[...], b_ref[...]
    for v in range(A_HD):
        s = s_ref[0, v]
        sa = jnp.sum(s * at, axis=0, keepdims=True)
        sn = s * wt + sa * bt + v_ref[v:v + 1, :] * kt
        so_ref[0, v] = sn
        y_ref[v:v + 1, :] = jnp.sum(sn * rt, axis=0, keepdims=True)


def _rw_step(r, w, k, v, a, b, s_t):
    bsz = r.shape[1]
    vec = pl.BlockSpec((A_HD, bsz), lambda h: (h, 0))
    st = pl.BlockSpec((1, A_HD, A_HD, bsz), lambda h: (h, 0, 0, 0))
    return pl.pallas_call(
        _rw_step_kernel,
        grid=(A_HEADS,),
        in_specs=[vec] * 6 + [st],
        out_specs=[vec, st],
        out_shape=[jax.ShapeDtypeStruct((A_W, bsz), F32), jax.ShapeDtypeStruct(s_t.shape, F32)],
        compiler_params=_cparams("parallel"),
        name="rw_step",
    )(r, w, k, v, a, b, s_t)


RW_C = 64
(F_RT, F_PM1, F_IP, F_DC, F_KH, F_KC, F_V, F_KK, F_ASIG, F_RKK, F_G, F_PC) = range(12)


def _rw_chunk_kernel(x_ref, mu_ref, wl_ref, vec_ref, ln_ref, o_ref, so_ref, carry, s_scr, f_scr):
    rt = x_ref.shape[0]
    i = pl.program_id(1)

    @pl.when(i == 0)
    def _():
        carry[...] = jnp.zeros_like(carry)
        s_scr[...] = jnp.zeros_like(s_scr)

    def prepare():
        x = x_ref[...]
        row = lax.broadcasted_iota(jnp.int32, x.shape, 0)
        prev = jnp.where(row == 0, carry[...], pltpu.roll(x, 1, axis=0))
        carry[...] = x[rt - 1:rt, :]
        xs = x + (prev - x) * mu_ref[...]
        r = xs[:, :A_W]
        k = xs[:, A_W:2 * A_W]
        lo = xs[:, 3 * A_W:]
        lane = lax.broadcasted_iota(jnp.int32, lo.shape, 1)
        act = jnp.where(lane < A_DECAY_LORA, jnp.tanh(lo),
                        jnp.where(lane < A_DECAY_LORA + A_AAA_LORA, lo, _sigmoid(lo)))
        lora = _dot(act.astype(BF16), wl_ref[...])
        w0, a0, kkw, kaw, rkw = (vec_ref[n:n + 1, :] for n in range(5))
        lw = -math.exp(-0.5) * _sigmoid(w0 + lora[:, :A_W])
        a_sig = _sigmoid(a0 + lora[:, A_W:2 * A_W])
        k2 = k * (1.0 + (a_sig - 1.0) * kaw)

        ri = lax.broadcasted_iota(jnp.int32, (rt, rt), 0)
        ci = lax.broadcasted_iota(jnp.int32, (rt, rt), 1)
        same = jnp.right_shift(ri, 6) == jnp.right_shift(ci, 6)
        chunk_ones = same.astype(BF16)
        chunk_tri = jnp.where(ci <= ri, chunk_ones, jnp.zeros_like(chunk_ones))
        lp = _dot_sel_left(chunk_tri, lw)
        lpc = _dot_sel_left(chunk_ones, lw)
        ip = jnp.exp(-lp)
        dc = jnp.exp(lpc - lp)
        f_scr[F_RT] = r * jnp.exp(lp)
        f_scr[F_PM1] = jnp.exp(lp - lw)
        f_scr[F_IP] = ip
        f_scr[F_DC] = dc
        f_scr[F_KH] = k2 * ip
        f_scr[F_KC] = k2 * dc
        f_scr[F_V] = xs[:, 2 * A_W:3 * A_W]
        f_scr[F_KK] = k * kkw
        f_scr[F_ASIG] = a_sig
        f_scr[F_RKK] = r * k2 * rkw
        f_scr[F_G] = lora[:, 2 * A_W:]
        f_scr[F_PC] = jnp.exp(lpc)

    C = RW_C
    W2 = 2 * A_HD
    npairs = A_HEADS // 2
    lane_lo = lax.broadcasted_iota(jnp.int32, (C, W2), 1) < A_HD
    own = (lax.broadcasted_iota(jnp.int32, (2 * C, W2), 1) < A_HD) == \
          (lax.broadcasted_iota(jnp.int32, (2 * C, W2), 0) < C)
    r4 = lax.broadcasted_iota(jnp.int32, (4 * C, 4 * C), 0)
    c4 = lax.broadcasted_iota(jnp.int32, (4 * C, 4 * C), 1)
    keep = jnp.bitwise_and(c4, C - 1) < jnp.bitwise_and(r4, C - 1) + jnp.right_shift(r4, 7)
    eye_f = (lax.broadcasted_iota(jnp.int32, (W2, W2), 0) ==
             lax.broadcasted_iota(jnp.int32, (W2, W2), 1)).astype(F32)
    zeros_ww = jnp.zeros((W2, W2), F32)
    bf = lambda z: z.astype(BF16)
    stack = lambda z: jnp.concatenate([jnp.where(lane_lo, z, 0.0), jnp.where(lane_lo, 0.0, z)], axis=0)
    spread = lambda col: jnp.where(lane_lo, col[:C], col[C:])
    chunks_per_body = rt // C
    probs = [(cc, p) for cc in range(chunks_per_body) for p in range(npairs)]

    def body():
        rows = [slice(cc * C, (cc + 1) * C) for cc in range(chunks_per_body)]
        ld = lambda f, q: f_scr[f, rows[q[0]], q[1] * W2:(q[1] + 1) * W2]
        a_l, r_l, bc_l, kc_l, v_l, g_l = [], [], [], [], [], []
        pc_l = [ld(F_PC, q)[0:1, :] for q in probs]
        vn_l = [ld(F_V, q) for q in probs]
        gate_l = [ld(F_G, q) for q in probs]
        kk_sq = [jnp.sum(jnp.square(stack(ld(F_KK, q))), axis=-1, keepdims=True) for q in probs]
        bonus_l = [jnp.sum(stack(ld(F_RKK, q)), axis=-1, keepdims=True) for q in probs]
        for e, q in enumerate(probs):
            kk = ld(F_KK, q) * spread(1.0 / jnp.maximum(jnp.sqrt(kk_sq[e]), 1e-12))
            b = kk * ld(F_ASIG, q)
            a_l.append(stack(-kk * ld(F_PM1, q)))
            r_l.append(stack(ld(F_RT, q)))
            bc_l.append(stack(b * ld(F_DC, q)))
            kc_l.append(stack(ld(F_KC, q)))
            v_l.append(stack(vn_l[e]))
            ar = jnp.concatenate([a_l[-1], r_l[-1]], axis=0)
            bk = jnp.concatenate([stack(b * ld(F_IP, q)), stack(ld(F_KH, q))], axis=0)
            g_l.append(jnp.where(keep, _dot_nt(bf(ar), bf(bk)), 0.0))
        n = range(len(probs))
        x_l = [g_l[e][:W2, :W2] for e in n]
        t_l = [eye_f + x_l[e] for e in n]
        x_l = [_dot(bf(x_l[e]), bf(x_l[e])) for e in n]
        m = 2
        while m < C:
            last = 2 * m >= C
            for e in n:
                rhs = t_l[e] if last else jnp.concatenate([x_l[e], t_l[e]], axis=1)
                prod = _dot(bf(x_l[e]), bf(rhs))
                if last:
                    t_l[e] = t_l[e] + prod
                else:
                    x_l[e] = prod[:, :W2]
                    t_l[e] = t_l[e] + prod[:, W2:]
            m *= 2
        lv_l = [_dot(bf(g_l[e][:W2, W2:]), bf(v_l[e])) for e in n]
        apu_l = [_dot(bf(t_l[e]), bf(jnp.concatenate([a_l[e], lv_l[e]], axis=1))) for e in n]
        qy_l = []
        for e in n:
            low = jnp.concatenate([zeros_ww, v_l[e]], axis=1)
            qy_l.append(_dot(bf(g_l[e][W2:, :]), bf(jnp.concatenate([apu_l[e], low], axis=0))))
        mm_l = [_dot_tn(bf(apu_l[e][:, :W2]), bf(bc_l[e])) for e in n]
        n_l = []
        for e in n:
            uv = jnp.concatenate([apu_l[e][:, W2:], v_l[e]], axis=0)
            n_l.append(_dot_tn(bf(uv), bf(jnp.concatenate([bc_l[e], kc_l[e]], axis=0))))
        y_l = []
        for e, q in enumerate(probs):
            p = q[1]
            s0 = s_scr[p]
            s0b = bf(s0)
            y_l.append(_dot_nt(bf(r_l[e] + qy_l[e][:, :W2]), s0b) + qy_l[e][:, W2:])
            s_scr[p] = s0 * pc_l[e] + _dot(s0b, bf(mm_l[e])) + n_l[e]
        mu_l = [jnp.sum(y_l[e], axis=-1, keepdims=True) * (1.0 / A_HD) for e in n]
        d_l = [jnp.where(own, y_l[e] - mu_l[e], 0.0) for e in n]
        var_l = [jnp.sum(d_l[e] * d_l[e], axis=-1, keepdims=True) * (1.0 / A_HD) for e in n]
        for e, q in enumerate(probs):
            p = q[1]
            d = d_l[e] * lax.rsqrt(var_l[e] + A_LN_EPS)
            yn = (d[:C] + d[C:]) * ln_ref[:, p * W2:(p + 1) * W2]
            bonus = spread(bonus_l[e]) * vn_l[e]
            o_ref[rows[q[0]], p * W2:(p + 1) * W2] = ((yn + bonus) * gate_l[e]).astype(o_ref.dtype)

    prepare()
    body()

    @pl.when(i == pl.num_programs(1) - 1)
    def _():
        for p in range(npairs):
            so_ref[0, 2 * p] = s_scr[p, :A_HD, :A_HD]
            so_ref[0, 2 * p + 1] = s_scr[p, A_HD:, A_HD:]


def _rw_chunk(ua, mu, wl, vec, ln, bsz, t_len, rt):
    nt = t_len // rt
    st_spec = pl.BlockSpec((1, A_HEADS, A_HD, A_HD), lambda b, i: (b, 0, 0, 0))
    return pl.pallas_call(
        _rw_chunk_kernel,
        grid=(bsz, nt),
        in_specs=[pl.BlockSpec((rt, A_SHIFT_W), lambda b, i: (b * nt + i, 0)),
                  _const_spec(mu.shape), _const_spec(wl.shape), _const_spec(vec.shape),
                  _const_spec(ln.shape)],
        out_specs=[pl.BlockSpec((rt, A_W), lambda b, i: (b * nt + i, 0)), st_spec],
        out_shape=[jax.ShapeDtypeStruct((bsz * t_len, A_W), BF16),
                   jax.ShapeDtypeStruct((bsz, A_HEADS, A_HD, A_HD), F32)],
        scratch_shapes=[pltpu.VMEM((1, A_SHIFT_W), F32),
                        pltpu.VMEM((A_HEADS // 2, 2 * A_HD, 2 * A_HD), F32),
                        pltpu.VMEM((12, rt, A_W), F32)],
        compiler_params=_cparams("parallel", "arbitrary"),
        name="rw_chunk",
    )(ua, mu, wl, vec, ln)


def _rw_post_kernel(y_ref, bonus_ref, g_ref, ln_ref, seg_ref, o_ref):
    seg = seg_ref[...]
    y = y_ref[...].T
    mu = _dot_sel(y, seg) * (1.0 / A_HD)
    d = y - mu
    var = _dot_sel(d * d, seg) * (1.0 / A_HD)
    yn = d * lax.rsqrt(var + A_LN_EPS) * ln_ref[...]
    o_ref[...] = ((yn + bonus_ref[...]) * g_ref[...]).astype(o_ref.dtype)


def _rw_post(y_t, bonus, g, ln, seg):
    n = bonus.shape[0]
    full = lambda shape: pl.BlockSpec(shape, lambda i: (0,) * len(shape))
    return pl.pallas_call(
        _rw_post_kernel,
        grid=(1,),
        in_specs=[full(a.shape) for a in (y_t, bonus, g, ln, seg)],
        out_specs=full((n, A_W)),
        out_shape=jax.ShapeDtypeStruct((n, A_W), BF16),
        compiler_params=_cparams("arbitrary"),
        name="rw_post",
    )(y_t, bonus, g, ln, seg)


def _head_norm_lanes(x, g, eps):
    mu = jnp.mean(x, axis=-1, keepdims=True)
    d = x - mu
    var = jnp.mean(d * d, axis=-1, keepdims=True)
    return d * lax.rsqrt(var + eps) * g


def _mlstm_chunk_kernel(ub_ref, g_ref, gt_ref, cw_ref, cb_ref, biasr_ref, biasc_ref, gn_ref,
                        y_ref, co_ref, no_ref, mo_ref,
                        tail_s, c_s, n_s, m_s):
    nb, L, _ = ub_ref.shape
    c = pl.program_id(1)

    @pl.when(c == 0)
    def _():
        tail_s[...] = jnp.zeros_like(tail_s)
        c_s[...] = jnp.zeros_like(c_s)
        n_s[...] = jnp.zeros_like(n_s)
        m_s[...] = jnp.zeros_like(m_s)

    r_i = lax.broadcasted_iota(jnp.int32, (L, L), 0)
    c_i = lax.broadcasted_iota(jnp.int32, (L, L), 1)
    causal = c_i <= r_i
    tri = causal.astype(BF16)
    tri_u = (r_i <= c_i).astype(BF16)
    heads = range(B_HEADS)
    sls = [slice(hd * B_HD, (hd + 1) * B_HD) for hd in heads]

    seqs = range(nb)
    ub_l, q_l, k_l, gc_l, gr_l, bcols_l, brows_l = [], [], [], [], [], [], []
    for bi in seqs:
        ub = ub_ref[bi]
        raw = ub[:, :2 * B_W]
        ext = jnp.concatenate([tail_s[bi], raw], axis=0)
        tail_s[bi] = raw[L - SUBLANES:]
        acc = cb_ref[...] + raw * cw_ref[B_CONV - 1:B_CONV, :]
        for jj in range(B_CONV - 1):
            sh = B_CONV - 1 - jj
            acc = acc + pltpu.roll(ext, sh, axis=0)[SUBLANES:] * cw_ref[jj:jj + 1, :]
        act = _silu(acc)
        ub_l.append(ub)
        q_l.append(act[:, :B_W])
        k_l.append(act[:, B_W:] * (B_HD ** -0.5))
        gc = g_ref[bi] + biasr_ref[...]
        gr = gt_ref[bi] + biasc_ref[...]
        gc_l.append(gc)
        gr_l.append(gr)
        bcols_l.append(_dot_sel_left(tri, _log_sigmoid(gc)))
        brows_l.append(_dot_sel(_log_sigmoid(gr), tri_u))
    m_all = [m_s[bi] for bi in seqs]
    n_all = [n_s[bi] for bi in seqs]

    probs = [(bi, hd) for bi in seqs for hd in heads]
    qb = [q_l[bi][:, sls[hd]].astype(BF16) for bi, hd in probs]
    vb = [ub_l[bi][:, 2 * B_W + hd * B_HD:2 * B_W + (hd + 1) * B_HD].astype(BF16) for bi, hd in probs]
    qk = [_dot_nt(qb[e], k_l[bi][:, sls[hd]].astype(BF16)) for e, (bi, hd) in enumerate(probs)]
    qc = [_dot(qb[e], c_s[bi * B_HEADS + hd].astype(BF16)) for e, (bi, hd) in enumerate(probs)]
    m_prev = [m_all[bi][hd:hd + 1, 0:1] for bi, hd in probs]
    n_prev = [n_all[bi][hd:hd + 1, :] for bi, hd in probs]
    bc = [_lane_col(bcols_l[bi], B_HEADS + hd) for bi, hd in probs]
    ic = [_lane_col(gc_l[bi], hd) for bi, hd in probs]
    dlog, inter, b_last, gi = [], [], [], []
    for e, (bi, hd) in enumerate(probs):
        br = brows_l[bi][B_HEADS + hd:B_HEADS + hd + 1, :]
        ir = gr_l[bi][hd:hd + 1, :]
        dlog.append(jnp.where(causal, bc[e] - br + ir, -jnp.inf))
        inter.append(bc[e] + m_prev[e])
        b_last.append(_row_pick(bc[e], L - 1))
        gi.append(b_last[e] - bc[e] + ic[e])
    row_max = [jnp.max(dlog[e], axis=1, keepdims=True) for e in range(len(probs))]
    gi_max = [jnp.max(gi[e], axis=0, keepdims=True) for e in range(len(probs))]
    s_l, iw, m_t, wc, kw, m_new = [], [], [], [], [], []
    for e, (bi, hd) in enumerate(probs):
        m_t.append(jnp.maximum(inter[e], row_max[e]))
        s_l.append(qk[e] * jnp.exp(dlog[e] - m_t[e]))
        iw.append(jnp.exp(inter[e] - m_t[e]))
        m_new.append(jnp.maximum(b_last[e] + m_prev[e], gi_max[e]))
        wc.append(jnp.exp(b_last[e] + m_prev[e] - m_new[e]))
        kw.append(k_l[bi][:, sls[hd]] * jnp.exp(gi[e] - m_new[e]))
    sv = [_dot(s_l[e].astype(BF16), vb[e]) for e in range(len(probs))]
    kv = [_dot_tn(kw[e].astype(BF16), vb[e]) for e in range(len(probs))]
    s_sum = [jnp.sum(s_l[e], axis=1, keepdims=True) for e in range(len(probs))]
    qn = [jnp.sum(q_l[bi][:, sls[hd]] * n_prev[e], axis=1, keepdims=True) for e, (bi, hd) in enumerate(probs)]
    k_sum = [jnp.sum(kw[e], axis=0, keepdims=True) for e in range(len(probs))]
    hb = []
    for e, (bi, hd) in enumerate(probs):
        den = s_sum[e] + iw[e] * qn[e]
        hh = (sv[e] + iw[e] * qc[e]) / jnp.maximum(jnp.abs(den), jnp.exp(-m_t[e]))
        hb.append(_sigmoid(ub_l[bi][:, 3 * B_W + hd * B_HD:3 * B_W + (hd + 1) * B_HD]) * hh)
        c_s[bi * B_HEADS + hd] = wc[e] * c_s[bi * B_HEADS + hd] + kv[e]
    mean = [jnp.sum(hb[e], axis=-1, keepdims=True) * (1.0 / B_HD) for e in range(len(probs))]
    dev = [hb[e] - mean[e] for e in range(len(probs))]
    var = [jnp.sum(dev[e] * dev[e], axis=-1, keepdims=True) * (1.0 / B_HD) for e in range(len(probs))]
    for e, (bi, hd) in enumerate(probs):
        y_ref[bi, :, sls[hd]] = (dev[e] * lax.rsqrt(var[e] + GN_EPS) * gn_ref[:, sls[hd]]).astype(y_ref.dtype)
    pad_rows = jnp.zeros((SUBLANES - B_HEADS, LANES), F32)
    for bi in seqs:
        es = [e for e, (bj, _) in enumerate(probs) if bj == bi]
        n_s[bi] = jnp.concatenate([wc[e] * n_prev[e] + k_sum[e] for e in es] + [pad_rows], axis=0)
        m_s[bi] = jnp.concatenate([jnp.broadcast_to(m_new[e], (1, LANES)) for e in es] + [pad_rows], axis=0)

    @pl.when(c == pl.num_programs(1) - 1)
    def _():
        for bi in range(nb):
            for hd in heads:
                co_ref[bi, hd] = c_s[bi * B_HEADS + hd]
        no_ref[...] = n_s[...]
        mo_ref[...] = m_s[...]


def _mlstm_chunk(ub, gates, gates_t, cw, cb, bias_r, bias_c, gn, bsz, t_len, nb):
    L = CHUNK
    seq = lambda w: pl.BlockSpec((nb, L, w), lambda b, c: (b, c, 0))
    per_b = lambda shp: pl.BlockSpec((nb,) + shp, lambda b, c: (b,) + (0,) * len(shp))
    return pl.pallas_call(
        _mlstm_chunk_kernel,
        grid=(bsz // nb, t_len // L),
        in_specs=[seq(4 * B_W), seq(LANES),
                  pl.BlockSpec((nb, SUBLANES, L), lambda b, c: (b, 0, c)),
                  _const_spec(cw.shape), _const_spec(cb.shape), _const_spec(bias_r.shape),
                  _const_spec(bias_c.shape), _const_spec(gn.shape)],
        out_specs=[seq(B_W), per_b((B_HEADS, B_HD, B_HD)), per_b((SUBLANES, B_HD)),
                   per_b((SUBLANES, LANES))],
        out_shape=[jax.ShapeDtypeStruct((bsz, t_len, B_W), BF16),
                   jax.ShapeDtypeStruct((bsz, B_HEADS, B_HD, B_HD), F32),
                   jax.ShapeDtypeStruct((bsz, SUBLANES, B_HD), F32),
                   jax.ShapeDtypeStruct((bsz, SUBLANES, LANES), F32)],
        scratch_shapes=[pltpu.VMEM((nb, SUBLANES, 2 * B_W), F32),
                        pltpu.VMEM((nb * B_HEADS, B_HD, B_HD), F32),
                        pltpu.VMEM((nb, SUBLANES, B_HD), F32),
                        pltpu.VMEM((nb, SUBLANES, LANES), F32)],
        compiler_params=_cparams("parallel", "arbitrary"),
        name="mlstm_chunk",
    )(ub, gates, gates_t, cw, cb, bias_r, bias_c, gn)


def _mlstm_step_kernel(ub_ref, g_ref, conv_ref, cw_ref, cb_ref, biasr_ref, gn_ref, c_ref, n_ref, m_ref,
                       y_ref, co_ref, no_ref, mo_ref):
    bb = ub_ref.shape[0]
    ub = ub_ref[...]
    raw = ub[:, :2 * B_W]
    conv = conv_ref[...]
    acc = cb_ref[...] + raw * cw_ref[B_CONV - 1:B_CONV, :]
    for jj in range(B_CONV - 1):
        acc = acc + conv[:, jj * 2 * B_W:(jj + 1) * 2 * B_W] * cw_ref[jj:jj + 1, :]
    qk = _silu(acc)
    q_all = qk[:, :B_W]
    k_all = qk[:, B_W:] * (B_HD ** -0.5)
    gc = g_ref[...] + biasr_ref[...]
    lf = _log_sigmoid(gc)
    m_all = m_ref[...]
    row_pad = lax.broadcasted_iota(jnp.int32, (LANES, LANES), 0)
    pad = jnp.zeros((LANES - bb, B_HD), F32)
    m_out = jnp.zeros((bb, LANES), F32)
    lane_m = lax.broadcasted_iota(jnp.int32, (bb, LANES), 1)
    for hd in range(B_HEADS):
        sl = slice(hd * B_HD, (hd + 1) * B_HD)
        q = q_all[:, sl]
        k = k_all[:, sl]
        v = ub[:, 2 * B_W + hd * B_HD:2 * B_W + (hd + 1) * B_HD]
        o = ub[:, 3 * B_W + hd * B_HD:3 * B_W + (hd + 1) * B_HD]
        ic = _lane_col(gc, hd)
        f = _lane_col(lf, B_HEADS + hd)
        m_prev = _lane_col(m_all, hd)
        n_prev = n_ref[:, sl]
        m_t = jnp.maximum(f + m_prev, ic)
        s = jnp.sum(q * k, axis=1, keepdims=True) * jnp.exp(ic - m_t)
        iw = jnp.exp(f + m_prev - m_t)
        wi = jnp.exp(ic - m_t)
        qb = q.astype(BF16)
        kw_t = jnp.concatenate([k * wi, pad], axis=0).T
        v_pad = jnp.concatenate([v, pad], axis=0)
        qc = jnp.zeros((bb, B_HD), F32)
        row_b = lax.broadcasted_iota(jnp.int32, (bb, B_HD), 0)
        for bi in range(bb):
            c_prev = c_ref[bi, hd]
            qc = qc + jnp.where(row_b == bi, _dot(qb, c_prev.astype(BF16)), 0.0)
            v_one = jnp.where(row_pad == bi, v_pad, 0.0)
            co_ref[bi, hd] = _row_pick(iw, bi) * c_prev + _dot3(kw_t, v_one)
        num = s * v + iw * qc
        den = s + iw * jnp.sum(q * n_prev, axis=1, keepdims=True)
        hh = num / jnp.maximum(jnp.abs(den), jnp.exp(-m_t))
        no_ref[:, sl] = iw * n_prev + wi * k
        m_out = jnp.where(lane_m == hd, m_t, m_out)
        hb = _sigmoid(o) * hh
        y_ref[:, sl] = _head_norm_lanes(hb, gn_ref[:, sl], GN_EPS).astype(y_ref.dtype)
    mo_ref[...] = m_out


def _mlstm_step(ub, gates, conv, cw, cb, bias_r, gn, c0, n0, m0, bb):
    bsz = ub.shape[0]
    row = lambda w: pl.BlockSpec((bb, w), lambda i: (i, 0))
    c_spec = pl.BlockSpec((bb, B_HEADS, B_HD, B_HD), lambda i: (i, 0, 0, 0))
    return pl.pallas_call(
        _mlstm_step_kernel,
        grid=(bsz // bb,),
        in_specs=[row(4 * B_W), row(LANES), row((B_CONV - 1) * 2 * B_W),
                  _const_spec(cw.shape), _const_spec(cb.shape), _const_spec(bias_r.shape),
                  _const_spec(gn.shape), c_spec, row(B_W), row(LANES)],
        out_specs=[row(B_W), c_spec, row(B_W), row(LANES)],
        out_shape=[jax.ShapeDtypeStruct((bsz, B_W), BF16),
                   jax.ShapeDtypeStruct(c0.shape, F32),
                   jax.ShapeDtypeStruct((bsz, B_W), F32),
                   jax.ShapeDtypeStruct((bsz, LANES), F32)],
        compiler_params=_cparams("parallel"),
        name="mlstm_step",
    )(ub, gates, conv, cw, cb, bias_r, gn, c0, n0, m0)


def _ret_log_gamma(hd):
    return math.log1p(-(2.0 ** (-5.0 - hd)))


def _ret_chunk_kernel(q_ref, qd_ref, k_ref, kd_ref, v_ref, g_ref, y_ref, so_ref, s_s, mask_s):
    nb, L, _ = q_ref.shape
    c = pl.program_id(1)

    @pl.when(c == 0)
    def _():
        s_s[...] = jnp.zeros_like(s_s)
        r_i = lax.broadcasted_iota(jnp.int32, (L, L), 0)
        c_i = lax.broadcasted_iota(jnp.int32, (L, L), 1)
        diff = (r_i - c_i).astype(F32)
        for hd in range(C_HEADS):
            mask_s[hd] = jnp.where(diff >= 0, jnp.exp(jnp.maximum(diff, 0.0) * _ret_log_gamma(hd)), 0.0)

    heads = range(C_HEADS)
    kls = [slice(hd * C_DK, (hd + 1) * C_DK) for hd in heads]
    sls = [slice(hd * C_DV, (hd + 1) * C_DV) for hd in heads]
    probs = [(bi, hd) for bi in range(nb) for hd in heads]
    n = range(len(probs))

    sc = [(_dot_nt(q_ref[bi, :, kls[hd]], k_ref[bi, :, kls[hd]]) * mask_s[hd]).astype(BF16) for bi, hd in probs]
    o_l = []
    for e, (bi, hd) in enumerate(probs):
        sl = sls[hd]
        c_dec = math.exp(L * _ret_log_gamma(hd))
        s_prev = s_s[e]
        o_l.append(_dot(sc[e], v_ref[bi, :, sl]) + _dot(qd_ref[bi, :, kls[hd]], s_prev.astype(BF16)))
        s_s[e] = c_dec * s_prev + _dot_tn(kd_ref[bi, :, kls[hd]], v_ref[bi, :, sl])
    mean = [jnp.sum(o_l[e], axis=-1, keepdims=True) * (1.0 / C_DV) for e in n]
    dev = [o_l[e] - mean[e] for e in n]
    var = [jnp.sum(dev[e] * dev[e], axis=-1, keepdims=True) * (1.0 / C_DV) for e in n]
    for e, (bi, hd) in enumerate(probs):
        sl = sls[hd]
        y_ref[bi, :, sl] = (dev[e] * lax.rsqrt(var[e] + GN_EPS) * g_ref[bi, :, sl]).astype(y_ref.dtype)

    @pl.when(c == pl.num_programs(1) - 1)
    def _():
        for e, (bi, hd) in enumerate(probs):
            so_ref[bi, hd] = s_s[e]


def _ret_chunk(q, qd, k, kd, v, g, bsz, t_len, nb):
    L = RET_CHUNK
    seq = lambda w: pl.BlockSpec((nb, L, w), lambda b, c: (b, c, 0))
    st = pl.BlockSpec((nb, C_HEADS, C_DK, C_DV), lambda b, c: (b, 0, 0, 0))
    return pl.pallas_call(
        _ret_chunk_kernel,
        grid=(bsz // nb, t_len // L),
        in_specs=[seq(C_HEADS * C_DK)] * 4 + [seq(C_OUT_W), seq(C_OUT_W)],
        out_specs=[seq(C_OUT_W), st],
        out_shape=[jax.ShapeDtypeStruct((bsz, t_len, C_OUT_W), BF16),
                   jax.ShapeDtypeStruct((bsz, C_HEADS, C_DK, C_DV), F32)],
        scratch_shapes=[pltpu.VMEM((nb * C_HEADS, C_DK, C_DV), F32), pltpu.VMEM((C_HEADS, L, L), F32)],
        compiler_params=_cparams("parallel", "arbitrary"),
        name="ret_chunk",
    )(q, qd, k, kd, v, g)


def _ret_step_kernel(q_ref, k_ref, v_ref, g_ref, s_ref, y_ref, so_ref):
    bb = q_ref.shape[0]
    hd = pl.program_id(1)
    gam = 1.0 - jnp.exp2(-5.0 - jnp.full((1, 1), hd).astype(F32))
    q = q_ref[...]
    k = k_ref[...]
    v = v_ref[...]
    s = jnp.sum(q * k, axis=1, keepdims=True)
    pad = jnp.zeros((LANES - bb, C_DK), F32)
    k_t = jnp.concatenate([k, pad], axis=0).T
    q_t = jnp.concatenate([q * gam, pad], axis=0).T
    row_b = lax.broadcasted_iota(jnp.int32, (bb, C_DV), 0)
    qs = jnp.zeros((bb, C_DV), F32)
    for bi in range(bb):
        s_prev = s_ref[bi, 0]
        qs = jnp.where(row_b == bi, jnp.sum(_lane_col(q_t, bi) * s_prev, axis=0, keepdims=True), qs)
        so_ref[bi, 0] = gam * s_prev + _lane_col(k_t, bi) * v[bi:bi + 1, :]
    o = s * v + qs
    y = _head_norm_lanes(o, g_ref[...], GN_EPS)
    y_ref[...] = y.astype(y_ref.dtype)


def _ret_step(q, k, v, g, s0, bb):
    bsz = q.shape[0]
    st = pl.BlockSpec((bb, 1, C_DK, C_DV), lambda i, h: (i, h, 0, 0))
    return pl.pallas_call(
        _ret_step_kernel,
        grid=(bsz // bb, C_HEADS),
        in_specs=[pl.BlockSpec((bb, C_DK), lambda i, h: (i, h)),
                  pl.BlockSpec((bb, C_DK), lambda i, h: (i, h)),
                  pl.BlockSpec((bb, C_DV), lambda i, h: (i, h)),
                  pl.BlockSpec((bb, C_DV), lambda i, h: (i, h)), st],
        out_specs=[pl.BlockSpec((bb, C_DV), lambda i, h: (i, h)), st],
        out_shape=[jax.ShapeDtypeStruct((bsz, C_OUT_W), BF16),
                   jax.ShapeDtypeStruct(s0.shape, F32)],
        compiler_params=_cparams("parallel", "parallel"),
        name="ret_step",
    )(q, k, v, g, s0)


def _post_kernel(*refs, n_mix):
    h_ref = refs[0]
    mix_refs = refs[1:1 + n_mix]
    wout_refs = refs[1 + n_mix:1 + 2 * n_mix]
    (p_ref, g_ref, up_ref, down_ref, gate_ref, proj_ref, o_ref, h1_s, xn_s, acc_s) = refs[1 + 2 * n_mix:]
    f = pl.program_id(1)

    tm = h_ref.shape[0]
    halves = [slice(0, tm // 2), slice(tm // 2, tm)] if tm >= 2 * LANES else [slice(0, tm)]

    @pl.when(f == 0)
    def _():
        for rows in halves:
            mix = _dot(mix_refs[0][rows, :], wout_refs[0][...])
            for m_ref, w_ref in zip(mix_refs[1:], wout_refs[1:]):
                mix = mix + _dot(m_ref[rows, :], w_ref[...])
            h1 = h_ref[rows, :] + _rms(mix, g_ref[1:2, :])
            h1_s[rows, :] = h1
            xn_s[rows, :] = _rms(h1, g_ref[2:3, :]).astype(BF16)
        acc_s[...] = jnp.zeros_like(acc_s)

    hid = jnp.square(jnp.maximum(_dot(xn_s[...], up_ref[...]), 0.0))
    acc_s[...] += _dot(hid.astype(BF16), down_ref[...])

    @pl.when(f == pl.num_programs(1) - 1)
    def _():
        for rows in halves:
            h2 = h1_s[rows, :] + _rms(acc_s[rows, :], g_ref[3:4, :])
            gate = _sigmoid(_dot(h2.astype(BF16), gate_ref[...]))
            emb = _dot(p_ref[rows, :].astype(BF16), proj_ref[...])
            o_ref[rows, :] = h2 + gate * emb


def _post(h, mixes, wouts, p, g, up, down, gate, proj, li, tm, tf):
    n = h.shape[0]
    n_mix = len(mixes)
    row = lambda w: pl.BlockSpec((tm, w), lambda i, f: (i, 0))
    layer = lambda a: pl.BlockSpec((None,) + a.shape[1:], lambda i, f: (li,) + (0,) * (a.ndim - 1),
                                   pipeline_mode=pl.Buffered(1))
    return pl.pallas_call(
        functools.partial(_post_kernel, n_mix=n_mix),
        grid=(n // tm, D_FF // tf),
        in_specs=[row(D_MODEL)] + [row(m.shape[1]) for m in mixes]
        + [_const_spec(w.shape) for w in wouts]
        + [pl.BlockSpec((None, tm, D_PLE), lambda i, f: (li, i, 0)), layer(g),
           pl.BlockSpec((None, D_MODEL, tf), lambda i, f: (li, 0, f)),
           pl.BlockSpec((None, tf, D_MODEL), lambda i, f: (li, f, 0)),
           layer(gate), layer(proj)],
        out_specs=row(D_MODEL),
        out_shape=jax.ShapeDtypeStruct((n, D_MODEL), F32),
        scratch_shapes=[pltpu.VMEM((tm, D_MODEL), F32), pltpu.VMEM((tm, D_MODEL), BF16),
                        pltpu.VMEM((tm, D_MODEL), F32)],
        compiler_params=_cparams("parallel", "arbitrary"),
        name="post",
    )(h, *mixes, *wouts, p, g, up, down, gate, proj)


def _prep_weights(W):
    f = {}
    w_in = W['mix0_w_in'][0]
    w_in_t = w_in.T
    f['w_a'] = w_in_t[:A_SHIFT_W].astype(BF16)
    f['w_b'] = w_in_t[A_SHIFT_W:A_SHIFT_W + 4 * B_W].astype(BF16)
    f['w_g'] = jnp.pad(w_in_t[A_SHIFT_W + 4 * B_W:], ((0, LANES - 2 * B_HEADS), (0, 0))).astype(BF16)
    wl = jnp.zeros((A_LORA_W, 3 * A_W), F32)
    wl = wl.at[:A_DECAY_LORA, :A_W].set(W['rw_w2'][0])
    wl = wl.at[A_DECAY_LORA:A_DECAY_LORA + A_AAA_LORA, A_W:2 * A_W].set(W['rw_a2'][0])
    wl = wl.at[A_DECAY_LORA + A_AAA_LORA:, 2 * A_W:].set(W['rw_g2'][0])
    f['w_lora'] = wl.astype(BF16)
    f['rw_vec'] = jnp.concatenate(
        [W['rw_w0'], W['rw_a0'], W['rw_kk'], W['rw_ka'], W['rw_rk'][0].reshape(1, A_W),
         jnp.zeros((3, A_W), F32)], axis=0)
    f['rw_mu'] = W['rw_mu']
    f['rw_ln'] = W['rw_ln']
    f['seg'] = (jnp.arange(A_W)[:, None] // A_HD == jnp.arange(A_W)[None, :] // A_HD).astype(BF16)
    f['cw'] = W['ml_conv_w'][0]
    f['cb'] = W['ml_conv_b']
    gate_bias = jnp.concatenate([W['ml_i_bias'][0], W['ml_f_bias'][0]])
    f['bias_r'] = jnp.pad(gate_bias, (0, LANES - 2 * B_HEADS)).reshape(1, LANES)
    f['bias_c'] = gate_bias.reshape(2 * B_HEADS, 1)
    f['ml_gn'] = W['ml_gn']
    w_out0 = W['mix0_w_out'][0].astype(BF16)
    f['w_out_a'] = w_out0[:A_W]
    f['w_out_b'] = w_out0[A_W:]
    w_ret = W['ret_w_in'][0]
    nq = C_HEADS * C_DK
    f['w_rq'] = w_ret[:, :nq].astype(BF16)
    f['w_rk'] = w_ret[:, nq:2 * nq].astype(BF16)
    f['w_rv'] = w_ret[:, 2 * nq:2 * nq + C_OUT_W].astype(BF16)
    f['w_rg'] = w_ret[:, 2 * nq + C_OUT_W:].astype(BF16)
    f['ret_gn'] = W['ret_gn']
    f['w_ret_out'] = W['ret_w_out'][0].astype(BF16)
    for name in ('ffn_up', 'ffn_down', 'ple_gate', 'ple_proj'):
        f[name] = W[name].astype(BF16)
    f['norm_g'] = W['norm_g']
    return f


def _rope_tables(pos):
    half = C_DK // 2
    inv = ROPE_BASE ** (-jnp.arange(half, dtype=F32) / half)
    ang = pos.astype(F32)[:, None] * inv[None, :]
    return jnp.cos(ang), jnp.sin(ang)


def _even_layer(h, p, st, f, bsz, t_len):
    n = bsz * t_len
    seq = st is None
    tm = 512 if seq else n
    ua, ub, gates = _inproj(h, f['norm_g'][0], [f['w_a'], f['w_b'], f['w_g']], [F32, F32, F32], tm)
    qk_raw = ub.reshape(bsz, t_len, 4 * B_W)[:, :, :2 * B_W]

    if seq:
        ya, s_a_new = _rw_chunk(ua, f['rw_mu'], f['w_lora'], f['rw_vec'], f['rw_ln'], bsz, t_len, rt=256)
        conv_new = qk_raw[:, -(B_CONV - 1):]
    else:
        shift, s_a, conv, c0, n0, m0 = st
        conv_new = jnp.concatenate([conv, qk_raw], axis=1)[:, -(B_CONV - 1):]
        r, w, k, v, a, b, g, bonus = _rw_prep(ua, shift, f['rw_mu'], f['w_lora'], f['rw_vec'], f['seg'])
        y_t, s_t = _rw_step(r, w, k, v, a, b, jnp.transpose(s_a, (1, 2, 3, 0)))
        s_a_new = jnp.transpose(s_t, (3, 0, 1, 2))
        ya = _rw_post(y_t, bonus, g, f['rw_ln'], f['seg'])

    if seq:
        gates3 = gates.reshape(bsz, t_len, LANES)
        gates_t = gates3[:, :, :SUBLANES].transpose(0, 2, 1)
        yb, c_new, n_new, m_new = _mlstm_chunk(ub.reshape(bsz, t_len, 4 * B_W), gates3, gates_t, f['cw'],
                                               f['cb'], f['bias_r'], f['bias_c'], f['ml_gn'], bsz, t_len,
                                               nb=4)
        yb = yb.reshape(n, B_W)
        n_new = n_new[:, :B_HEADS]
        m_new = m_new[:, :B_HEADS, 0]
    else:
        m0p = jnp.pad(m0, ((0, 0), (0, LANES - B_HEADS)))
        yb, c_new, n_new, m_new = _mlstm_step(ub, gates, conv.reshape(bsz, -1), f['cw'], f['cb'],
                                              f['bias_r'], f['ml_gn'], c0, n0.reshape(bsz, B_W), m0p,
                                              bb=SUBLANES)
        n_new = n_new.reshape(bsz, B_HEADS, B_HD)
        m_new = m_new[:, :B_HEADS]

    h_next = _post(h, [ya, yb], [f['w_out_a'], f['w_out_b']], p, f['norm_g'], f['ffn_up'],
                   f['ffn_down'], f['ple_gate'], f['ple_proj'], li=0, tm=tm, tf=POST_TF)

    shift_new = ua.reshape(bsz, t_len, A_SHIFT_W)[:, -1]
    return h_next, (shift_new, s_a_new, conv_new, c_new, n_new, m_new)


def _odd_layer(h, p, s0, pos, f, bsz, t_len):
    n = bsz * t_len
    seq = s0 is None
    cos, sin = _rope_tables(pos)
    tm = 512 if seq else n
    ws = [f['w_rq'], f['w_rk'], f['w_rv'], f['w_rg']]
    if seq:
        q, qd, k, kd, v, g = _inproj_ret(h, f['norm_g'][1], f['ret_gn'], cos, sin, ws, tm, t_len // tm,
                                         [BF16] * 5 + [F32], decayed=True)
        as3 = lambda a: a.reshape(bsz, t_len, a.shape[-1])
        y, s_new = _ret_chunk(as3(q), as3(qd), as3(k), as3(kd), as3(v), as3(g), bsz, t_len, nb=2)
        y = y.reshape(n, C_OUT_W)
    else:
        q, k, v, g = _inproj_ret(h, f['norm_g'][1], f['ret_gn'], cos, sin, ws, tm, 1, [F32] * 4,
                                 decayed=False)
        y, s_new = _ret_step(q, k, v, g, s0, bb=2 * SUBLANES)
    h_next = _post(h, [y], [f['w_ret_out']], p, f['norm_g'], f['ffn_up'], f['ffn_down'],
                   f['ple_gate'], f['ple_proj'], li=1, tm=512 if seq else n, tf=POST_TF)
    return h_next, s_new


def _trunk(x, p, pos, ev_states, od_state, f):
    bsz, t_len, _ = x.shape
    n = bsz * t_len
    h = x.reshape(n, D_MODEL)
    ev_in = None if ev_states is None else tuple(s[0] for s in ev_states)
    od_in = None if od_state is None else od_state[0]
    p = p.reshape(p.shape[0], n, D_PLE)
    h, ev_new = _even_layer(h, p, ev_in, f, bsz, t_len)
    h, od_new = _odd_layer(h, p, od_in, pos, f, bsz, t_len)
    return h.reshape(bsz, t_len, D_MODEL), [s[None] for s in ev_new], od_new[None]


def kernel(x_prompt, x_sample, state_rwkv_shift, state_rwkv_S, state_mlstm_conv, state_mlstm_C,
           state_mlstm_n, state_mlstm_m, state_ret_S, p_prompt, p_sample, norm_g, ffn_up, ffn_down,
           ple_gate, ple_proj, mix0_w_in, rw_mu, rw_w0, rw_w2, rw_a0, rw_a2, rw_g2, rw_kk, rw_ka,
           rw_rk, rw_ln, ml_conv_w, ml_conv_b, ml_i_bias, ml_f_bias, ml_gn, mix0_w_out,
           ret_w_in, ret_gn, ret_w_out):
    W = dict(norm_g=norm_g, ffn_up=ffn_up, ffn_down=ffn_down, ple_gate=ple_gate, ple_proj=ple_proj,
             mix0_w_in=mix0_w_in, rw_mu=rw_mu, rw_w0=rw_w0, rw_w2=rw_w2, rw_a0=rw_a0, rw_a2=rw_a2,
             rw_g2=rw_g2, rw_kk=rw_kk, rw_ka=rw_ka, rw_rk=rw_rk, rw_ln=rw_ln, ml_conv_w=ml_conv_w,
             ml_conv_b=ml_conv_b, ml_i_bias=ml_i_bias, ml_f_bias=ml_f_bias, ml_gn=ml_gn,
             mix0_w_out=mix0_w_out, ret_w_in=ret_w_in, ret_gn=ret_gn, ret_w_out=ret_w_out)
    f = _prep_weights(W)
    bp, tp = x_prompt.shape[:2]
    ts = x_sample.shape[1]
    y_prompt, ev_p, ret_p = _trunk(x_prompt, p_prompt, jnp.arange(tp), None, None, f)
    ev_s_in = (state_rwkv_shift, state_rwkv_S, state_mlstm_conv, state_mlstm_C, state_mlstm_n,
               state_mlstm_m)
    y_sample, ev_s, ret_s = _trunk(x_sample, p_sample, PAST_LEN + jnp.arange(ts), ev_s_in, state_ret_S, f)
    return (y_prompt, y_sample, *ev_p, ret_p, *ev_s, ret_s)
```

```python
import functools
import math

import jax
import jax.numpy as jnp
from jax import lax
from jax.experimental import pallas as pl
from jax.experimental.pallas import tpu as pltpu

F32 = jnp.float32
BF16 = jnp.bfloat16

D_MODEL = 1024
D_PLE = 256
D_FF = 4 * D_MODEL
RMS_EPS = 1e-6
GN_EPS = 1e-5
CHUNK = 128
RET_CHUNK = 256

A_HEADS = 8
A_HD = 64
A_W = A_HEADS * A_HD
A_DECAY_LORA = 64
A_AAA_LORA = 64
A_GATE_LORA = 128
A_LORA_W = A_DECAY_LORA + A_AAA_LORA + A_GATE_LORA
A_SHIFT_W = 3 * A_W + A_LORA_W
A_LN_EPS = 64e-5

B_HEADS = 4
B_HD = 128
B_W = B_HEADS * B_HD
B_CONV = 4

C_HEADS = 4
C_DK = 256
C_DV = 512
C_OUT_W = C_HEADS * C_DV
ROPE_BASE = 10000.0
PAST_LEN = 16384

LANES = 128
SUBLANES = 8
VMEM_LIMIT = 56 * 1024 * 1024
POST_TF = 4096


def _cparams(*sem):
    return pltpu.CompilerParams(dimension_semantics=sem, vmem_limit_bytes=VMEM_LIMIT)


def _const_spec(shape):
    nd = len(shape)
    return pl.BlockSpec(shape, lambda *_: (0,) * nd, pipeline_mode=pl.Buffered(1))


def _rms(x, g):
    return x * lax.rsqrt(jnp.mean(x * x, axis=-1, keepdims=True) + RMS_EPS) * g


def _dot(a, b):
    return jnp.dot(a, b, preferred_element_type=F32)


def _dot_nt(a, b):
    return lax.dot_general(a, b, (((1,), (1,)), ((), ())), preferred_element_type=F32)


def _dot_tn(a, b):
    return lax.dot_general(a, b, (((0,), (0,)), ((), ())), preferred_element_type=F32)


def _hilo(x):
    hi = x.astype(BF16)
    lo = (x - hi.astype(F32)).astype(BF16)
    return hi, lo


def _dot_sel(x, sel):
    hi, lo = _hilo(x)
    return _dot(hi, sel) + _dot(lo, sel)


def _dot_sel_left(sel, x):
    hi, lo = _hilo(x)
    return _dot(sel, hi) + _dot(sel, lo)


def _dot3(a, b):
    ah, al = _hilo(a)
    bh, bl = _hilo(b)
    return _dot(ah, bh) + _dot(ah, bl) + _dot(al, bh)


def _sigmoid(x):
    return 1.0 / (1.0 + jnp.exp(-x))


def _softplus(x):
    return jnp.maximum(x, 0.0) + jnp.log1p(jnp.exp(-jnp.abs(x)))


def _log_sigmoid(x):
    return -_softplus(-x)


def _silu(x):
    return x * _sigmoid(x)


def _lane_col(x, idx):
    lane = lax.broadcasted_iota(jnp.int32, x.shape, 1)
    return jnp.sum(jnp.where(lane == idx, x, 0.0), axis=1, keepdims=True)


def _row_pick(x, idx):
    row = lax.broadcasted_iota(jnp.int32, x.shape, 0)
    return jnp.sum(jnp.where(row == idx, x, 0.0), axis=0, keepdims=True)


def _inproj_kernel(x_ref, g_ref, *refs):
    n = len(refs) // 2
    xn = _rms(x_ref[...], g_ref[0:1, :]).astype(BF16)
    for w_ref, o_ref in zip(refs[:n], refs[n:]):
        o_ref[...] = _dot_nt(xn, w_ref[...]).astype(o_ref.dtype)


def _inproj(h, g, ws, dtypes, tm):
    n = h.shape[0]
    return pl.pallas_call(
        _inproj_kernel,
        grid=(n // tm,),
        in_specs=[pl.BlockSpec((tm, D_MODEL), lambda i: (i, 0)), _const_spec(g.shape)]
        + [_const_spec(w.shape) for w in ws],
        out_specs=[pl.BlockSpec((tm, w.shape[0]), lambda i: (i, 0)) for w in ws],
        out_shape=[jax.ShapeDtypeStruct((n, w.shape[0]), dt) for w, dt in zip(ws, dtypes)],
        compiler_params=_cparams("parallel"),
        name="inproj",
    )(h, g, *ws)


def _inproj_ret_kernel(x_ref, g_ref, gn_ref, cos_ref, sin_ref, wq_ref, wk_ref, wv_ref, wg_ref, *out_refs,
                       decayed):
    if decayed:
        q_ref, qd_ref, k_ref, kd_ref, v_ref, gate_ref = out_refs
    else:
        q_ref, k_ref, v_ref, gate_ref = out_refs
        qd_ref = kd_ref = None
    xn = _rms(x_ref[...], g_ref[0:1, :]).astype(BF16)
    cos = cos_ref[...]
    sin = sin_ref[...]
    half = C_DK // 2
    tm = x_ref.shape[0]
    row0 = pl.program_id(0) * tm
    pos = jnp.bitwise_and(lax.broadcasted_iota(jnp.int32, (tm, half), 0) + row0, RET_CHUNK - 1).astype(F32)
    gate_ref[...] = (_silu(_dot(xn, wg_ref[...])) * gn_ref[...]).astype(gate_ref.dtype)
    for w_ref, o_ref, d_ref, scale in ((wq_ref, q_ref, qd_ref, None), (wk_ref, k_ref, kd_ref, C_DK ** -0.5)):
        u = _dot(xn, w_ref[...])
        for hd in range(C_HEADS):
            x1 = u[:, hd * C_DK:hd * C_DK + half]
            x2 = u[:, hd * C_DK + half:(hd + 1) * C_DK]
            y1 = x1 * cos - x2 * sin
            y2 = x1 * sin + x2 * cos
            if scale is not None:
                y1 = y1 * scale
                y2 = y2 * scale
            o_ref[:, hd * C_DK:hd * C_DK + half] = y1.astype(o_ref.dtype)
            o_ref[:, hd * C_DK + half:(hd + 1) * C_DK] = y2.astype(o_ref.dtype)
            if decayed:
                lg = _ret_log_gamma(hd)
                dec = jnp.exp((pos + 1.0) * lg) if d_ref is qd_ref else jnp.exp((RET_CHUNK - 1.0 - pos) * lg)
                d_ref[:, hd * C_DK:hd * C_DK + half] = (y1 * dec).astype(d_ref.dtype)
                d_ref[:, hd * C_DK + half:(hd + 1) * C_DK] = (y2 * dec).astype(d_ref.dtype)
    v_ref[...] = _dot(xn, wv_ref[...]).astype(v_ref.dtype)


def _inproj_ret(h, g, gn, cos, sin, ws, tm, seq_tiles, dtypes, decayed):
    n = h.shape[0]
    if cos.shape[0] == 1:
        cs_spec = _const_spec(cos.shape)
    else:
        cs_spec = pl.BlockSpec((tm, cos.shape[1]), lambda i: (i % seq_tiles, 0))
    widths = [w.shape[1] for w in ws]
    if decayed:
        widths = [widths[0], widths[0], widths[1], widths[1], widths[2], widths[3]]
    return pl.pallas_call(
        functools.partial(_inproj_ret_kernel, decayed=decayed),
        grid=(n // tm,),
        in_specs=[pl.BlockSpec((tm, D_MODEL), lambda i: (i, 0)), _const_spec(g.shape), _const_spec(gn.shape),
                  cs_spec, cs_spec] + [_const_spec(w.shape) for w in ws],
        out_specs=[pl.BlockSpec((tm, wd), lambda i: (i, 0)) for wd in widths],
        out_shape=[jax.ShapeDtypeStruct((n, wd), dt) for wd, dt in zip(widths, dtypes)],
        compiler_params=_cparams("parallel"),
        name="inproj_ret",
    )(h, g, gn, cos, sin, *ws)


def _rw_prep_kernel(x_ref, sh_ref, mu_ref, wl_ref, vec_ref, seg_ref,
                    r_ref, w_ref, k_ref, v_ref, a_ref, b_ref, g_ref, bonus_ref):
    x = x_ref[...]
    xs = x + (sh_ref[...] - x) * mu_ref[...]
    r = xs[:, :A_W]
    k = xs[:, A_W:2 * A_W]
    v = xs[:, 2 * A_W:3 * A_W]
    lo = xs[:, 3 * A_W:]
    lane = lax.broadcasted_iota(jnp.int32, lo.shape, 1)
    act = jnp.where(lane < A_DECAY_LORA, jnp.tanh(lo),
                    jnp.where(lane < A_DECAY_LORA + A_AAA_LORA, lo, _sigmoid(lo)))
    lora = _dot(act.astype(BF16), wl_ref[...])
    w0, a0, kkw, kaw, rkw = (vec_ref[i:i + 1, :] for i in range(5))
    w_log = -_softplus(-(w0 + lora[:, :A_W])) - 0.5
    decay = jnp.exp(-jnp.exp(w_log))
    a = _sigmoid(a0 + lora[:, A_W:2 * A_W])
    g = lora[:, 2 * A_W:]
    seg = seg_ref[...]
    kk = k * kkw
    kk = kk / jnp.maximum(jnp.sqrt(_dot_sel(kk * kk, seg)), 1e-12)
    k2 = k * (1.0 + (a - 1.0) * kaw)
    bonus = _dot_sel(r * k2 * rkw, seg) * v
    r_ref[...] = r.T
    w_ref[...] = decay.T
    k_ref[...] = k2.T
    v_ref[...] = v.T
    a_ref[...] = (-kk).T
    b_ref[...] = (kk * a).T
    g_ref[...] = g
    bonus_ref[...] = bonus


def _rw_prep(ua, shift, mu, wl, vec, seg):
    n = ua.shape[0]
    full = lambda shape: pl.BlockSpec(shape, lambda i: (0,) * len(shape))
    return pl.pallas_call(
        _rw_prep_kernel,
        grid=(1,),
        in_specs=[full(a.shape) for a in (ua, shift, mu, wl, vec, seg)],
        out_specs=[full((A_W, n))] * 6 + [full((n, A_W))] * 2,
        out_shape=[jax.ShapeDtypeStruct((A_W, n), F32)] * 6 + [jax.ShapeDtypeStruct((n, A_W), F32)] * 2,
        compiler_params=_cparams("arbitrary"),
        name="rw_prep",
    )(ua, shift, mu, wl, vec, seg)


def _rw_step_kernel(r_ref, w_ref, k_ref, v_ref, a_ref, b_ref, s_ref, y_ref, so_ref):
    rt, wt, kt, at, bt = r_ref[...], w_ref[...], k_ref[...], a_ref[...], b_ref[...]
    for v in range(A_HD):
        s = s_ref[0, v]
        sa = jnp.sum(s * at, axis=0, keepdims=True)
        sn = s * wt + sa * bt + v_ref[v:v + 1, :] * kt
        so_ref[0, v] = sn
        y_ref[v:v + 1, :] = jnp.sum(sn * rt, axis=0, keepdims=True)


def _rw_step(r, w, k, v, a, b, s_t):
    bsz = r.shape[1]
    vec = pl.BlockSpec((A_HD, bsz), lambda h: (h, 0))
    st = pl.BlockSpec((1, A_HD, A_HD, bsz), lambda h: (h, 0, 0, 0))
    return pl.pallas_call(
        _rw_step_kernel,
        grid=(A_HEADS,),
        in_specs=[vec] * 6 + [st],
        out_specs=[vec, st],
        out_shape=[jax.ShapeDtypeStruct((A_W, bsz), F32), jax.ShapeDtypeStruct(s_t.shape, F32)],
        compiler_params=_cparams("parallel"),
        name="rw_step",
    )(r, w, k, v, a, b, s_t)


RW_C = 64
(F_RT, F_PM1, F_IP, F_DC, F_KH, F_KC, F_V, F_KK, F_ASIG, F_RKK, F_G, F_PC) = range(12)


def _rw_chunk_kernel(x_ref, mu_ref, wl_ref, vec_ref, ln_ref, o_ref, so_ref, carry, s_scr, f_scr):
    rt = x_ref.shape[0]
    i = pl.program_id(1)

    @pl.when(i == 0)
    def _():
        carry[...] = jnp.zeros_like(carry)
        s_scr[...] = jnp.zeros_like(s_scr)

    def prepare():
        x = x_ref[...]
        row = lax.broadcasted_iota(jnp.int32, x.shape, 0)
        prev = jnp.where(row == 0, carry[...], pltpu.roll(x, 1, axis=0))
        carry[...] = x[rt - 1:rt, :]
        xs = x + (prev - x) * mu_ref[...]
        r = xs[:, :A_W]
        k = xs[:, A_W:2 * A_W]
        lo = xs[:, 3 * A_W:]
        lane = lax.broadcasted_iota(jnp.int32, lo.shape, 1)
        act = jnp.where(lane < A_DECAY_LORA, jnp.tanh(lo),
                        jnp.where(lane < A_DECAY_LORA + A_AAA_LORA, lo, _sigmoid(lo)))
        lora = _dot(act.astype(BF16), wl_ref[...])
        w0, a0, kkw, kaw, rkw = (vec_ref[n:n + 1, :] for n in range(5))
        lw = -math.exp(-0.5) * _sigmoid(w0 + lora[:, :A_W])
        a_sig = _sigmoid(a0 + lora[:, A_W:2 * A_W])
        k2 = k * (1.0 + (a_sig - 1.0) * kaw)

        ri = lax.broadcasted_iota(jnp.int32, (rt, rt), 0)
        ci = lax.broadcasted_iota(jnp.int32, (rt, rt), 1)
        same = jnp.right_shift(ri, 6) == jnp.right_shift(ci, 6)
        chunk_ones = same.astype(BF16)
        chunk_tri = jnp.where(ci <= ri, chunk_ones, jnp.zeros_like(chunk_ones))
        lp = _dot_sel_left(chunk_tri, lw)
        lpc = _dot_sel_left(chunk_ones, lw)
        ip = jnp.exp(-lp)
        dc = jnp.exp(lpc - lp)
        f_scr[F_RT] = r * jnp.exp(lp)
        f_scr[F_PM1] = jnp.exp(lp - lw)
        f_scr[F_IP] = ip
        f_scr[F_DC] = dc
        f_scr[F_KH] = k2 * ip
        f_scr[F_KC] = k2 * dc
        f_scr[F_V] = xs[:, 2 * A_W:3 * A_W]
        f_scr[F_KK] = k * kkw
        f_scr[F_ASIG] = a_sig
        f_scr[F_RKK] = r * k2 * rkw
        f_scr[F_G] = lora[:, 2 * A_W:]
        f_scr[F_PC] = jnp.exp(lpc)

    C = RW_C
    W2 = 2 * A_HD
    npairs = A_HEADS // 2
    lane_lo = lax.broadcasted_iota(jnp.int32, (C, W2), 1) < A_HD
    own = (lax.broadcasted_iota(jnp.int32, (2 * C, W2), 1) < A_HD) == \
          (lax.broadcasted_iota(jnp.int32, (2 * C, W2), 0) < C)
    r4 = lax.broadcasted_iota(jnp.int32, (4 * C, 4 * C), 0)
    c4 = lax.broadcasted_iota(jnp.int32, (4 * C, 4 * C), 1)
    keep = jnp.bitwise_and(c4, C - 1) < jnp.bitwise_and(r4, C - 1) + jnp.right_shift(r4, 7)
    eye_f = (lax.broadcasted_iota(jnp.int32, (W2, W2), 0) ==
             lax.broadcasted_iota(jnp.int32, (W2, W2), 1)).astype(F32)
    zeros_ww = jnp.zeros((W2, W2), F32)
    bf = lambda z: z.astype(BF16)
    stack = lambda z: jnp.concatenate([jnp.where(lane_lo, z, 0.0), jnp.where(lane_lo, 0.0, z)], axis=0)
    spread = lambda col: jnp.where(lane_lo, col[:C], col[C:])
    chunks_per_body = rt // C
    probs = [(cc, p) for cc in range(chunks_per_body) for p in range(npairs)]

    def body():
        rows = [slice(cc * C, (cc + 1) * C) for cc in range(chunks_per_body)]
        ld = lambda f, q: f_scr[f, rows[q[0]], q[1] * W2:(q[1] + 1) * W2]
        a_l, r_l, bc_l, kc_l, v_l, g_l = [], [], [], [], [], []
        pc_l = [ld(F_PC, q)[0:1, :] for q in probs]
        vn_l = [ld(F_V, q) for q in probs]
        gate_l = [ld(F_G, q) for q in probs]
        kk_sq = [jnp.sum(jnp.square(stack(ld(F_KK, q))), axis=-1, keepdims=True) for q in probs]
        bonus_l = [jnp.sum(stack(ld(F_RKK, q)), axis=-1, keepdims=True) for q in probs]
        for e, q in enumerate(probs):
            kk = ld(F_KK, q) * spread(1.0 / jnp.maximum(jnp.sqrt(kk_sq[e]), 1e-12))
            b = kk * ld(F_ASIG, q)
            a_l.append(stack(-kk * ld(F_PM1, q)))
            r_l.append(stack(ld(F_RT, q)))
            bc_l.append(stack(b * ld(F_DC, q)))
            kc_l.append(stack(ld(F_KC, q)))
            v_l.append(stack(vn_l[e]))
            ar = jnp.concatenate([a_l[-1], r_l[-1]], axis=0)
            bk = jnp.concatenate([stack(b * ld(F_IP, q)), stack(ld(F_KH, q))], axis=0)
            g_l.append(jnp.where(keep, _dot_nt(bf(ar), bf(bk)), 0.0))
        n = range(len(probs))
        x_l = [g_l[e][:W2, :W2] for e in n]
        t_l = [eye_f + x_l[e] for e in n]
        x_l = [_dot(bf(x_l[e]), bf(x_l[e])) for e in n]
        m = 2
        while m < C:
            last = 2 * m >= C
            for e in n:
                rhs = t_l[e] if last else jnp.concatenate([x_l[e], t_l[e]], axis=1)
                prod = _dot(bf(x_l[e]), bf(rhs))
                if last:
                    t_l[e] = t_l[e] + prod
                else:
                    x_l[e] = prod[:, :W2]
                    t_l[e] = t_l[e] + prod[:, W2:]
            m *= 2
        lv_l = [_dot(bf(g_l[e][:W2, W2:]), bf(v_l[e])) for e in n]
        apu_l = [_dot(bf(t_l[e]), bf(jnp.concatenate([a_l[e], lv_l[e]], axis=1))) for e in n]
        qy_l = []
        for e in n:
            low = jnp.concatenate([zeros_ww, v_l[e]], axis=1)
            qy_l.append(_dot(bf(g_l[e][W2:, :]), bf(jnp.concatenate([apu_l[e], low], axis=0))))
        mm_l = [_dot_tn(bf(apu_l[e][:, :W2]), bf(bc_l[e])) for e in n]
        n_l = []
        for e in n:
            uv = jnp.concatenate([apu_l[e][:, W2:], v_l[e]], axis=0)
            n_l.append(_dot_tn(bf(uv), bf(jnp.concatenate([bc_l[e], kc_l[e]], axis=0))))
        y_l = []
        for e, q in enumerate(probs):
            p = q[1]
            s0 = s_scr[p]
            s0b = bf(s0)
            y_l.append(_dot_nt(bf(r_l[e] + qy_l[e][:, :W2]), s0b) + qy_l[e][:, W2:])
            s_scr[p] = s0 * pc_l[e] + _dot(s0b, bf(mm_l[e])) + n_l[e]
        mu_l = [jnp.sum(y_l[e], axis=-1, keepdims=True) * (1.0 / A_HD) for e in n]
        d_l = [jnp.where(own, y_l[e] - mu_l[e], 0.0) for e in n]
        var_l = [jnp.sum(d_l[e] * d_l[e], axis=-1, keepdims=True) * (1.0 / A_HD) for e in n]
        for e, q in enumerate(probs):
            p = q[1]
            d = d_l[e] * lax.rsqrt(var_l[e] + A_LN_EPS)
            yn = (d[:C] + d[C:]) * ln_ref[:, p * W2:(p + 1) * W2]
            bonus = spread(bonus_l[e]) * vn_l[e]
            o_ref[rows[q[0]], p * W2:(p + 1) * W2] = ((yn + bonus) * gate_l[e]).astype(o_ref.dtype)

    prepare()
    body()

    @pl.when(i == pl.num_programs(1) - 1)
    def _():
        for p in range(npairs):
            so_ref[0, 2 * p] = s_scr[p, :A_HD, :A_HD]
            so_ref[0, 2 * p + 1] = s_scr[p, A_HD:, A_HD:]


def _rw_chunk(ua, mu, wl, vec, ln, bsz, t_len, rt):
    nt = t_len // rt
    st_spec = pl.BlockSpec((1, A_HEADS, A_HD, A_HD), lambda b, i: (b, 0, 0, 0))
    return pl.pallas_call(
        _rw_chunk_kernel,
        grid=(bsz, nt),
        in_specs=[pl.BlockSpec((rt, A_SHIFT_W), lambda b, i: (b * nt + i, 0)),
                  _const_spec(mu.shape), _const_spec(wl.shape), _const_spec(vec.shape),
                  _const_spec(ln.shape)],
        out_specs=[pl.BlockSpec((rt, A_W), lambda b, i: (b * nt + i, 0)), st_spec],
        out_shape=[jax.ShapeDtypeStruct((bsz * t_len, A_W), BF16),
                   jax.ShapeDtypeStruct((bsz, A_HEADS, A_HD, A_HD), F32)],
        scratch_shapes=[pltpu.VMEM((1, A_SHIFT_W), F32),
                        pltpu.VMEM((A_HEADS // 2, 2 * A_HD, 2 * A_HD), F32),
                        pltpu.VMEM((12, rt, A_W), F32)],
        compiler_params=_cparams("parallel", "arbitrary"),
        name="rw_chunk",
    )(ua, mu, wl, vec, ln)


def _rw_post_kernel(y_ref, bonus_ref, g_ref, ln_ref, seg_ref, o_ref):
    seg = seg_ref[...]
    y = y_ref[...].T
    mu = _dot_sel(y, seg) * (1.0 / A_HD)
    d = y - mu
    var = _dot_sel(d * d, seg) * (1.0 / A_HD)
    yn = d * lax.rsqrt(var + A_LN_EPS) * ln_ref[...]
    o_ref[...] = ((yn + bonus_ref[...]) * g_ref[...]).astype(o_ref.dtype)


def _rw_post(y_t, bonus, g, ln, seg):
    n = bonus.shape[0]
    full = lambda shape: pl.BlockSpec(shape, lambda i: (0,) * len(shape))
    return pl.pallas_call(
        _rw_post_kernel,
        grid=(1,),
        in_specs=[full(a.shape) for a in (y_t, bonus, g, ln, seg)],
        out_specs=full((n, A_W)),
        out_shape=jax.ShapeDtypeStruct((n, A_W), BF16),
        compiler_params=_cparams("arbitrary"),
        name="rw_post",
    )(y_t, bonus, g, ln, seg)


def _head_norm_lanes(x, g, eps):
    mu = jnp.mean(x, axis=-1, keepdims=True)
    d = x - mu
    var = jnp.mean(d * d, axis=-1, keepdims=True)
    return d * lax.rsqrt(var + eps) * g


def _mlstm_chunk_kernel(ub_ref, g_ref, gt_ref, cw_ref, cb_ref, biasr_ref, biasc_ref, gn_ref,
                        y_ref, co_ref, no_ref, mo_ref,
                        tail_s, c_s, n_s, m_s):
    nb, L, _ = ub_ref.shape
    c = pl.program_id(1)

    @pl.when(c == 0)
    def _():
        tail_s[...] = jnp.zeros_like(tail_s)
        c_s[...] = jnp.zeros_like(c_s)
        n_s[...] = jnp.zeros_like(n_s)
        m_s[...] = jnp.zeros_like(m_s)

    r_i = lax.broadcasted_iota(jnp.int32, (L, L), 0)
    c_i = lax.broadcasted_iota(jnp.int32, (L, L), 1)
    causal = c_i <= r_i
    tri = causal.astype(BF16)
    tri_u = (r_i <= c_i).astype(BF16)
    heads = range(B_HEADS)
    sls = [slice(hd * B_HD, (hd + 1) * B_HD) for hd in heads]

    seqs = range(nb)
    ub_l, q_l, k_l, gc_l, gr_l, bcols_l, brows_l = [], [], [], [], [], [], []
    for bi in seqs:
        ub = ub_ref[bi]
        raw = ub[:, :2 * B_W]
        ext = jnp.concatenate([tail_s[bi], raw], axis=0)
        tail_s[bi] = raw[L - SUBLANES:]
        acc = cb_ref[...] + raw * cw_ref[B_CONV - 1:B_CONV, :]
        for jj in range(B_CONV - 1):
            sh = B_CONV - 1 - jj
            acc = acc + pltpu.roll(ext, sh, axis=0)[SUBLANES:] * cw_ref[jj:jj + 1, :]
        act = _silu(acc)
        ub_l.append(ub)
        q_l.append(act[:, :B_W])
        k_l.append(act[:, B_W:] * (B_HD ** -0.5))
        gc = g_ref[bi] + biasr_ref[...]
        gr = gt_ref[bi] + biasc_ref[...]
        gc_l.append(gc)
        gr_l.append(gr)
        bcols_l.append(_dot_sel_left(tri, _log_sigmoid(gc)))
        brows_l.append(_dot_sel(_log_sigmoid(gr), tri_u))
    m_all = [m_s[bi] for bi in seqs]
    n_all = [n_s[bi] for bi in seqs]

    probs = [(bi, hd) for bi in seqs for hd in heads]
    qb = [q_l[bi][:, sls[hd]].astype(BF16) for bi, hd in probs]
    vb = [ub_l[bi][:, 2 * B_W + hd * B_HD:2 * B_W + (hd + 1) * B_HD].astype(BF16) for bi, hd in probs]
    qk = [_dot_nt(qb[e], k_l[bi][:, sls[hd]].astype(BF16)) for e, (bi, hd) in enumerate(probs)]
    qc = [_dot(qb[e], c_s[bi * B_HEADS + hd].astype(BF16)) for e, (bi, hd) in enumerate(probs)]
    m_prev = [m_all[bi][hd:hd + 1, 0:1] for bi, hd in probs]
    n_prev = [n_all[bi][hd:hd + 1, :] for bi, hd in probs]
    bc = [_lane_col(bcols_l[bi], B_HEADS + hd) for bi, hd in probs]
    ic = [_lane_col(gc_l[bi], hd) for bi, hd in probs]
    dlog, inter, b_last, gi = [], [], [], []
    for e, (bi, hd) in enumerate(probs):
        br = brows_l[bi][B_HEADS + hd:B_HEADS + hd + 1, :]
        ir = gr_l[bi][hd:hd + 1, :]
        dlog.append(jnp.where(causal, bc[e] - br + ir, -jnp.inf))
        inter.append(bc[e] + m_prev[e])
        b_last.append(_row_pick(bc[e], L - 1))
        gi.append(b_last[e] - bc[e] + ic[e])
    row_max = [jnp.max(dlog[e], axis=1, keepdims=True) for e in range(len(probs))]
    gi_max = [jnp.max(gi[e], axis=0, keepdims=True) for e in range(len(probs))]
    s_l, iw, m_t, wc, kw, m_new = [], [], [], [], [], []
    for e, (bi, hd) in enumerate(probs):
        m_t.append(jnp.maximum(inter[e], row_max[e]))
        s_l.append(qk[e] * jnp.exp(dlog[e] - m_t[e]))
        iw.append(jnp.exp(inter[e] - m_t[e]))
        m_new.append(jnp.maximum(b_last[e] + m_prev[e], gi_max[e]))
        wc.append(jnp.exp(b_last[e] + m_prev[e] - m_new[e]))
        kw.append(k_l[bi][:, sls[hd]] * jnp.exp(gi[e] - m_new[e]))
    sv = [_dot(s_l[e].astype(BF16), vb[e]) for e in range(len(probs))]
    kv = [_dot_tn(kw[e].astype(BF16), vb[e]) for e in range(len(probs))]
    s_sum = [jnp.sum(s_l[e], axis=1, keepdims=True) for e in range(len(probs))]
    qn = [jnp.sum(q_l[bi][:, sls[hd]] * n_prev[e], axis=1, keepdims=True) for e, (bi, hd) in enumerate(probs)]
    k_sum = [jnp.sum(kw[e], axis=0, keepdims=True) for e in range(len(probs))]
    hb = []
    for e, (bi, hd) in enumerate(probs):
        den = s_sum[e] + iw[e] * qn[e]
        hh = (sv[e] + iw[e] * qc[e]) / jnp.maximum(jnp.abs(den), jnp.exp(-m_t[e]))
        hb.append(_sigmoid(ub_l[bi][:, 3 * B_W + hd * B_HD:3 * B_W + (hd + 1) * B_HD]) * hh)
        c_s[bi * B_HEADS + hd] = wc[e] * c_s[bi * B_HEADS + hd] + kv[e]
    mean = [jnp.sum(hb[e], axis=-1, keepdims=True) * (1.0 / B_HD) for e in range(len(probs))]
    dev = [hb[e] - mean[e] for e in range(len(probs))]
    var = [jnp.sum(dev[e] * dev[e], axis=-1, keepdims=True) * (1.0 / B_HD) for e in range(len(probs))]
    for e, (bi, hd) in enumerate(probs):
        y_ref[bi, :, sls[hd]] = (dev[e] * lax.rsqrt(var[e] + GN_EPS) * gn_ref[:, sls[hd]]).astype(y_ref.dtype)
    pad_rows = jnp.zeros((SUBLANES - B_HEADS, LANES), F32)
    for bi in seqs:
        es = [e for e, (bj, _) in enumerate(probs) if bj == bi]
        n_s[bi] = jnp.concatenate([wc[e] * n_prev[e] + k_sum[e] for e in es] + [pad_rows], axis=0)
        m_s[bi] = jnp.concatenate([jnp.broadcast_to(m_new[e], (1, LANES)) for e in es] + [pad_rows], axis=0)

    @pl.when(c == pl.num_programs(1) - 1)
    def _():
        for bi in range(nb):
            for hd in heads:
                co_ref[bi, hd] = c_s[bi * B_HEADS + hd]
        no_ref[...] = n_s[...]
        mo_ref[...] = m_s[...]


def _mlstm_chunk(ub, gates, gates_t, cw, cb, bias_r, bias_c, gn, bsz, t_len, nb):
    L = CHUNK
    seq = lambda w: pl.BlockSpec((nb, L, w), lambda b, c: (b, c, 0))
    per_b = lambda shp: pl.BlockSpec((nb,) + shp, lambda b, c: (b,) + (0,) * len(shp))
    return pl.pallas_call(
        _mlstm_chunk_kernel,
        grid=(bsz // nb, t_len // L),
        in_specs=[seq(4 * B_W), seq(LANES),
                  pl.BlockSpec((nb, SUBLANES, L), lambda b, c: (b, 0, c)),
                  _const_spec(cw.shape), _const_spec(cb.shape), _const_spec(bias_r.shape),
                  _const_spec(bias_c.shape), _const_spec(gn.shape)],
        out_specs=[seq(B_W), per_b((B_HEADS, B_HD, B_HD)), per_b((SUBLANES, B_HD)),
                   per_b((SUBLANES, LANES))],
        out_shape=[jax.ShapeDtypeStruct((bsz, t_len, B_W), BF16),
                   jax.ShapeDtypeStruct((bsz, B_HEADS, B_HD, B_HD), F32),
                   jax.ShapeDtypeStruct((bsz, SUBLANES, B_HD), F32),
                   jax.ShapeDtypeStruct((bsz, SUBLANES, LANES), F32)],
        scratch_shapes=[pltpu.VMEM((nb, SUBLANES, 2 * B_W), F32),
                        pltpu.VMEM((nb * B_HEADS, B_HD, B_HD), F32),
                        pltpu.VMEM((nb, SUBLANES, B_HD), F32),
                        pltpu.VMEM((nb, SUBLANES, LANES), F32)],
        compiler_params=_cparams("parallel", "arbitrary"),
        name="mlstm_chunk",
    )(ub, gates, gates_t, cw, cb, bias_r, bias_c, gn)


def _mlstm_step_kernel(ub_ref, g_ref, conv_ref, cw_ref, cb_ref, biasr_ref, gn_ref, c_ref, n_ref, m_ref,
                       y_ref, co_ref, no_ref, mo_ref):
    bb = ub_ref.shape[0]
    ub = ub_ref[...]
    raw = ub[:, :2 * B_W]
    conv = conv_ref[...]
    acc = cb_ref[...] + raw * cw_ref[B_CONV - 1:B_CONV, :]
    for jj in range(B_CONV - 1):
        acc = acc + conv[:, jj * 2 * B_W:(jj + 1) * 2 * B_W] * cw_ref[jj:jj + 1, :]
    qk = _silu(acc)
    q_all = qk[:, :B_W]
    k_all = qk[:, B_W:] * (B_HD ** -0.5)
    gc = g_ref[...] + biasr_ref[...]
    lf = _log_sigmoid(gc)
    m_all = m_ref[...]
    row_pad = lax.broadcasted_iota(jnp.int32, (LANES, LANES), 0)
    pad = jnp.zeros((LANES - bb, B_HD), F32)
    m_out = jnp.zeros((bb, LANES), F32)
    lane_m = lax.broadcasted_iota(jnp.int32, (bb, LANES), 1)
    for hd in range(B_HEADS):
        sl = slice(hd * B_HD, (hd + 1) * B_HD)
        q = q_all[:, sl]
        k = k_all[:, sl]
        v = ub[:, 2 * B_W + hd * B_HD:2 * B_W + (hd + 1) * B_HD]
        o = ub[:, 3 * B_W + hd * B_HD:3 * B_W + (hd + 1) * B_HD]
        ic = _lane_col(gc, hd)
        f = _lane_col(lf, B_HEADS + hd)
        m_prev = _lane_col(m_all, hd)
        n_prev = n_ref[:, sl]
        m_t = jnp.maximum(f + m_prev, ic)
        s = jnp.sum(q * k, axis=1, keepdims=True) * jnp.exp(ic - m_t)
        iw = jnp.exp(f + m_prev - m_t)
        wi = jnp.exp(ic - m_t)
        qb = q.astype(BF16)
        kw_t = jnp.concatenate([k * wi, pad], axis=0).T
        v_pad = jnp.concatenate([v, pad], axis=0)
        qc = jnp.zeros((bb, B_HD), F32)
        row_b = lax.broadcasted_iota(jnp.int32, (bb, B_HD), 0)
        for bi in range(bb):
            c_prev = c_ref[bi, hd]
            qc = qc + jnp.where(row_b == bi, _dot(qb, c_prev.astype(BF16)), 0.0)
            v_one = jnp.where(row_pad == bi, v_pad, 0.0)
            co_ref[bi, hd] = _row_pick(iw, bi) * c_prev + _dot3(kw_t, v_one)
        num = s * v + iw * qc
        den = s + iw * jnp.sum(q * n_prev, axis=1, keepdims=True)
        hh = num / jnp.maximum(jnp.abs(den), jnp.exp(-m_t))
        no_ref[:, sl] = iw * n_prev + wi * k
        m_out = jnp.where(lane_m == hd, m_t, m_out)
        hb = _sigmoid(o) * hh
        y_ref[:, sl] = _head_norm_lanes(hb, gn_ref[:, sl], GN_EPS).astype(y_ref.dtype)
    mo_ref[...] = m_out


def _mlstm_step(ub, gates, conv, cw, cb, bias_r, gn, c0, n0, m0, bb):
    bsz = ub.shape[0]
    row = lambda w: pl.BlockSpec((bb, w), lambda i: (i, 0))
    c_spec = pl.BlockSpec((bb, B_HEADS, B_HD, B_HD), lambda i: (i, 0, 0, 0))
    return pl.pallas_call(
        _mlstm_step_kernel,
        grid=(bsz // bb,),
        in_specs=[row(4 * B_W), row(LANES), row((B_CONV - 1) * 2 * B_W),
                  _const_spec(cw.shape), _const_spec(cb.shape), _const_spec(bias_r.shape),
                  _const_spec(gn.shape), c_spec, row(B_W), row(LANES)],
        out_specs=[row(B_W), c_spec, row(B_W), row(LANES)],
        out_shape=[jax.ShapeDtypeStruct((bsz, B_W), BF16),
                   jax.ShapeDtypeStruct(c0.shape, F32),
                   jax.ShapeDtypeStruct((bsz, B_W), F32),
                   jax.ShapeDtypeStruct((bsz, LANES), F32)],
        compiler_params=_cparams("parallel"),
        name="mlstm_step",
    )(ub, gates, conv, cw, cb, bias_r, gn, c0, n0, m0)


def _ret_log_gamma(hd):
    return math.log1p(-(2.0 ** (-5.0 - hd)))


def _ret_chunk_kernel(q_ref, qd_ref, k_ref, kd_ref, v_ref, g_ref, y_ref, so_ref, s_s, mask_s):
    nb, L, _ = q_ref.shape
    c = pl.program_id(1)

    @pl.when(c == 0)
    def _():
        s_s[...] = jnp.zeros_like(s_s)
        r_i = lax.broadcasted_iota(jnp.int32, (L, L), 0)
        c_i = lax.broadcasted_iota(jnp.int32, (L, L), 1)
        diff = (r_i - c_i).astype(F32)
        for hd in range(C_HEADS):
            mask_s[hd] = jnp.where(diff >= 0, jnp.exp(jnp.maximum(diff, 0.0) * _ret_log_gamma(hd)), 0.0)

    heads = range(C_HEADS)
    kls = [slice(hd * C_DK, (hd + 1) * C_DK) for hd in heads]
    sls = [slice(hd * C_DV, (hd + 1) * C_DV) for hd in heads]
    probs = [(bi, hd) for bi in range(nb) for hd in heads]
    n = range(len(probs))

    sc = [(_dot_nt(q_ref[bi, :, kls[hd]], k_ref[bi, :, kls[hd]]) * mask_s[hd]).astype(BF16) for bi, hd in probs]
    o_l = []
    for e, (bi, hd) in enumerate(probs):
        sl = sls[hd]
        c_dec = math.exp(L * _ret_log_gamma(hd))
        s_prev = s_s[e]
        o_l.append(_dot(sc[e], v_ref[bi, :, sl]) + _dot(qd_ref[bi, :, kls[hd]], s_prev.astype(BF16)))
        s_s[e] = c_dec * s_prev + _dot_tn(kd_ref[bi, :, kls[hd]], v_ref[bi, :, sl])
    mean = [jnp.sum(o_l[e], axis=-1, keepdims=True) * (1.0 / C_DV) for e in n]
    dev = [o_l[e] - mean[e] for e in n]
    var = [jnp.sum(dev[e] * dev[e], axis=-1, keepdims=True) * (1.0 / C_DV) for e in n]
    for e, (bi, hd) in enumerate(probs):
        sl = sls[hd]
        y_ref[bi, :, sl] = (dev[e] * lax.rsqrt(var[e] + GN_EPS) * g_ref[bi, :, sl]).astype(y_ref.dtype)

    @pl.when(c == pl.num_programs(1) - 1)
    def _():
        for e, (bi, hd) in enumerate(probs):
            so_ref[bi, hd] = s_s[e]


def _ret_chunk(q, qd, k, kd, v, g, bsz, t_len, nb):
    L = RET_CHUNK
    seq = lambda w: pl.BlockSpec((nb, L, w), lambda b, c: (b, c, 0))
    st = pl.BlockSpec((nb, C_HEADS, C_DK, C_DV), lambda b, c: (b, 0, 0, 0))
    return pl.pallas_call(
        _ret_chunk_kernel,
        grid=(bsz // nb, t_len // L),
        in_specs=[seq(C_HEADS * C_DK)] * 4 + [seq(C_OUT_W), seq(C_OUT_W)],
        out_specs=[seq(C_OUT_W), st],
        out_shape=[jax.ShapeDtypeStruct((bsz, t_len, C_OUT_W), BF16),
                   jax.ShapeDtypeStruct((bsz, C_HEADS, C_DK, C_DV), F32)],
        scratch_shapes=[pltpu.VMEM((nb * C_HEADS, C_DK, C_DV), F32), pltpu.VMEM((C_HEADS, L, L), F32)],
        compiler_params=_cparams("parallel", "arbitrary"),
        name="ret_chunk",
    )(q, qd, k, kd, v, g)


def _ret_step_kernel(q_ref, k_ref, v_ref, g_ref, s_ref, y_ref, so_ref):
    bb = q_ref.shape[0]
    hd = pl.program_id(1)
    gam = 1.0 - jnp.exp2(-5.0 - jnp.full((1, 1), hd).astype(F32))
    q = q_ref[...]
    k = k_ref[...]
    v = v_ref[...]
    s = jnp.sum(q * k, axis=1, keepdims=True)
    pad = jnp.zeros((LANES - bb, C_DK), F32)
    k_t = jnp.concatenate([k, pad], axis=0).T
    q_t = jnp.concatenate([q * gam, pad], axis=0).T
    row_b = lax.broadcasted_iota(jnp.int32, (bb, C_DV), 0)
    qs = jnp.zeros((bb, C_DV), F32)
    for bi in range(bb):
        s_prev = s_ref[bi, 0]
        qs = jnp.where(row_b == bi, jnp.sum(_lane_col(q_t, bi) * s_prev, axis=0, keepdims=True), qs)
        so_ref[bi, 0] = gam * s_prev + _lane_col(k_t, bi) * v[bi:bi + 1, :]
    o = s * v + qs
    y = _head_norm_lanes(o, g_ref[...], GN_EPS)
    y_ref[...] = y.astype(y_ref.dtype)


def _ret_step(q, k, v, g, s0, bb):
    bsz = q.shape[0]
    st = pl.BlockSpec((bb, 1, C_DK, C_DV), lambda i, h: (i, h, 0, 0))
    return pl.pallas_call(
        _ret_step_kernel,
        grid=(bsz // bb, C_HEADS),
        in_specs=[pl.BlockSpec((bb, C_DK), lambda i, h: (i, h)),
                  pl.BlockSpec((bb, C_DK), lambda i, h: (i, h)),
                  pl.BlockSpec((bb, C_DV), lambda i, h: (i, h)),
                  pl.BlockSpec((bb, C_DV), lambda i, h: (i, h)), st],
        out_specs=[pl.BlockSpec((bb, C_DV), lambda i, h: (i, h)), st],
        out_shape=[jax.ShapeDtypeStruct((bsz, C_OUT_W), BF16),
                   jax.ShapeDtypeStruct(s0.shape, F32)],
        compiler_params=_cparams("parallel", "parallel"),
        name="ret_step",
    )(q, k, v, g, s0)


def _post_kernel(*refs, n_mix):
    h_ref = refs[0]
    mix_refs = refs[1:1 + n_mix]
    wout_refs = refs[1 + n_mix:1 + 2 * n_mix]
    (p_ref, g_ref, up_ref, down_ref, gate_ref, proj_ref, o_ref, h1_s, xn_s, acc_s) = refs[1 + 2 * n_mix:]
    f = pl.program_id(1)

    tm = h_ref.shape[0]
    halves = [slice(0, tm // 2), slice(tm // 2, tm)] if tm >= 2 * LANES else [slice(0, tm)]

    @pl.when(f == 0)
    def _():
        for rows in halves:
            mix = _dot(mix_refs[0][rows, :], wout_refs[0][...])
            for m_ref, w_ref in zip(mix_refs[1:], wout_refs[1:]):
                mix = mix + _dot(m_ref[rows, :], w_ref[...])
            h1 = h_ref[rows, :] + _rms(mix, g_ref[1:2, :])
            h1_s[rows, :] = h1
            xn_s[rows, :] = _rms(h1, g_ref[2:3, :]).astype(BF16)
        acc_s[...] = jnp.zeros_like(acc_s)

    hid = jnp.square(jnp.maximum(_dot(xn_s[...], up_ref[...]), 0.0))
    acc_s[...] += _dot(hid.astype(BF16), down_ref[...])

    @pl.when(f == pl.num_programs(1) - 1)
    def _():
        for rows in halves:
            h2 = h1_s[rows, :] + _rms(acc_s[rows, :], g_ref[3:4, :])
            gate = _sigmoid(_dot(h2.astype(BF16), gate_ref[...]))
            emb = _dot(p_ref[rows, :].astype(BF16), proj_ref[...])
            o_ref[rows, :] = h2 + gate * emb


def _post(h, mixes, wouts, p, g, up, down, gate, proj, li, tm, tf):
    n = h.shape[0]
    n_mix = len(mixes)
    row = lambda w: pl.BlockSpec((tm, w), lambda i, f: (i, 0))
    layer = lambda a: pl.BlockSpec((None,) + a.shape[1:], lambda i, f: (li,) + (0,) * (a.ndim - 1),
                                   pipeline_mode=pl.Buffered(1))
    ff_mode = pl.Buffered(1) if tf == D_FF else None
    return pl.pallas_call(
        functools.partial(_post_kernel, n_mix=n_mix),
        grid=(n // tm, D_FF // tf),
        in_specs=[row(D_MODEL)] + [row(m.shape[1]) for m in mixes]
        + [_const_spec(w.shape) for w in wouts]
        + [pl.BlockSpec((None, tm, D_PLE), lambda i, f: (li, i, 0)), layer(g),
           pl.BlockSpec((None, D_MODEL, tf), lambda i, f: (li, 0, f), pipeline_mode=ff_mode),
           pl.BlockSpec((None, tf, D_MODEL), lambda i, f: (li, f, 0), pipeline_mode=ff_mode),
           layer(gate), layer(proj)],
        out_specs=row(D_MODEL),
        out_shape=jax.ShapeDtypeStruct((n, D_MODEL), F32),
        scratch_shapes=[pltpu.VMEM((tm, D_MODEL), F32), pltpu.VMEM((tm, D_MODEL), BF16),
                        pltpu.VMEM((tm, D_MODEL), F32)],
        compiler_params=_cparams("parallel", "arbitrary"),
        name="post",
    )(h, *mixes, *wouts, p, g, up, down, gate, proj)


def _prep_weights(W):
    f = {}
    w_in = W['mix0_w_in'][0]
    w_in_t = w_in.T
    f['w_a'] = w_in_t[:A_SHIFT_W].astype(BF16)
    f['w_b'] = w_in_t[A_SHIFT_W:A_SHIFT_W + 4 * B_W].astype(BF16)
    f['w_g'] = jnp.pad(w_in_t[A_SHIFT_W + 4 * B_W:], ((0, LANES - 2 * B_HEADS), (0, 0))).astype(BF16)
    wl = jnp.zeros((A_LORA_W, 3 * A_W), F32)
    wl = wl.at[:A_DECAY_LORA, :A_W].set(W['rw_w2'][0])
    wl = wl.at[A_DECAY_LORA:A_DECAY_LORA + A_AAA_LORA, A_W:2 * A_W].set(W['rw_a2'][0])
    wl = wl.at[A_DECAY_LORA + A_AAA_LORA:, 2 * A_W:].set(W['rw_g2'][0])
    f['w_lora'] = wl.astype(BF16)
    f['rw_vec'] = jnp.concatenate(
        [W['rw_w0'], W['rw_a0'], W['rw_kk'], W['rw_ka'], W['rw_rk'][0].reshape(1, A_W),
         jnp.zeros((3, A_W), F32)], axis=0)
    f['rw_mu'] = W['rw_mu']
    f['rw_ln'] = W['rw_ln']
    f['seg'] = (jnp.arange(A_W)[:, None] // A_HD == jnp.arange(A_W)[None, :] // A_HD).astype(BF16)
    f['cw'] = W['ml_conv_w'][0]
    f['cb'] = W['ml_conv_b']
    gate_bias = jnp.concatenate([W['ml_i_bias'][0], W['ml_f_bias'][0]])
    f['bias_r'] = jnp.pad(gate_bias, (0, LANES - 2 * B_HEADS)).reshape(1, LANES)
    f['bias_c'] = gate_bias.reshape(2 * B_HEADS, 1)
    f['ml_gn'] = W['ml_gn']
    w_out0 = W['mix0_w_out'][0].astype(BF16)
    f['w_out_a'] = w_out0[:A_W]
    f['w_out_b'] = w_out0[A_W:]
    w_ret = W['ret_w_in'][0]
    nq = C_HEADS * C_DK
    f['w_rq'] = w_ret[:, :nq].astype(BF16)
    f['w_rk'] = w_ret[:, nq:2 * nq].astype(BF16)
    f['w_rv'] = w_ret[:, 2 * nq:2 * nq + C_OUT_W].astype(BF16)
    f['w_rg'] = w_ret[:, 2 * nq + C_OUT_W:].astype(BF16)
    f['ret_gn'] = W['ret_gn']
    f['w_ret_out'] = W['ret_w_out'][0].astype(BF16)
    for name in ('ffn_up', 'ffn_down', 'ple_gate', 'ple_proj'):
        f[name] = W[name].astype(BF16)
    f['norm_g'] = W['norm_g']
    return f


def _rope_tables(pos):
    half = C_DK // 2
    inv = ROPE_BASE ** (-jnp.arange(half, dtype=F32) / half)
    ang = pos.astype(F32)[:, None] * inv[None, :]
    return jnp.cos(ang), jnp.sin(ang)


def _even_layer(h, p, st, f, bsz, t_len):
    n = bsz * t_len
    seq = st is None
    tm = 512 if seq else n
    ua, ub, gates = _inproj(h, f['norm_g'][0], [f['w_a'], f['w_b'], f['w_g']], [F32, F32, F32], tm)
    qk_raw = ub.reshape(bsz, t_len, 4 * B_W)[:, :, :2 * B_W]

    if seq:
        ya, s_a_new = _rw_chunk(ua, f['rw_mu'], f['w_lora'], f['rw_vec'], f['rw_ln'], bsz, t_len, rt=256)
        conv_new = qk_raw[:, -(B_CONV - 1):]
    else:
        shift, s_a, conv, c0, n0, m0 = st
        conv_new = jnp.concatenate([conv, qk_raw], axis=1)[:, -(B_CONV - 1):]
        r, w, k, v, a, b, g, bonus = _rw_prep(ua, shift, f['rw_mu'], f['w_lora'], f['rw_vec'], f['seg'])
        y_t, s_t = _rw_step(r, w, k, v, a, b, jnp.transpose(s_a, (1, 2, 3, 0)))
        s_a_new = jnp.transpose(s_t, (3, 0, 1, 2))
        ya = _rw_post(y_t, bonus, g, f['rw_ln'], f['seg'])

    if seq:
        gates3 = gates.reshape(bsz, t_len, LANES)
        gates_t = gates3[:, :, :SUBLANES].transpose(0, 2, 1)
        yb, c_new, n_new, m_new = _mlstm_chunk(ub.reshape(bsz, t_len, 4 * B_W), gates3, gates_t, f['cw'],
                                               f['cb'], f['bias_r'], f['bias_c'], f['ml_gn'], bsz, t_len,
                                               nb=4)
        yb = yb.reshape(n, B_W)
        n_new = n_new[:, :B_HEADS]
        m_new = m_new[:, :B_HEADS, 0]
    else:
        m0p = jnp.pad(m0, ((0, 0), (0, LANES - B_HEADS)))
        yb, c_new, n_new, m_new = _mlstm_step(ub, gates, conv.reshape(bsz, -1), f['cw'], f['cb'],
                                              f['bias_r'], f['ml_gn'], c0, n0.reshape(bsz, B_W), m0p,
                                              bb=SUBLANES)
        n_new = n_new.reshape(bsz, B_HEADS, B_HD)
        m_new = m_new[:, :B_HEADS]

    h_next = _post(h, [ya, yb], [f['w_out_a'], f['w_out_b']], p, f['norm_g'], f['ffn_up'],
                   f['ffn_down'], f['ple_gate'], f['ple_proj'], li=0, tm=tm, tf=POST_TF)

    shift_new = ua.reshape(bsz, t_len, A_SHIFT_W)[:, -1]
    return h_next, (shift_new, s_a_new, conv_new, c_new, n_new, m_new)


def _odd_layer(h, p, s0, pos, f, bsz, t_len):
    n = bsz * t_len
    seq = s0 is None
    cos, sin = _rope_tables(pos)
    tm = 512 if seq else n
    ws = [f['w_rq'], f['w_rk'], f['w_rv'], f['w_rg']]
    if seq:
        q, qd, k, kd, v, g = _inproj_ret(h, f['norm_g'][1], f['ret_gn'], cos, sin, ws, tm, t_len // tm,
                                         [BF16] * 5 + [F32], decayed=True)
        as3 = lambda a: a.reshape(bsz, t_len, a.shape[-1])
        y, s_new = _ret_chunk(as3(q), as3(qd), as3(k), as3(kd), as3(v), as3(g), bsz, t_len, nb=2)
        y = y.reshape(n, C_OUT_W)
    else:
        q, k, v, g = _inproj_ret(h, f['norm_g'][1], f['ret_gn'], cos, sin, ws, tm, 1, [F32] * 4,
                                 decayed=False)
        y, s_new = _ret_step(q, k, v, g, s0, bb=2 * SUBLANES)
    h_next = _post(h, [y], [f['w_ret_out']], p, f['norm_g'], f['ffn_up'], f['ffn_down'],
                   f['ple_gate'], f['ple_proj'], li=1, tm=512 if seq else n, tf=POST_TF)
    return h_next, s_new


def _trunk(x, p, pos, ev_states, od_state, f):
    bsz, t_len, _ = x.shape
    n = bsz * t_len
    h = x.reshape(n, D_MODEL)
    ev_in = None if ev_states is None else tuple(s[0] for s in ev_states)
    od_in = None if od_state is None else od_state[0]
    p = p.reshape(p.shape[0], n, D_PLE)
    h, ev_new = _even_layer(h, p, ev_in, f, bsz, t_len)
    h, od_new = _odd_layer(h, p, od_in, pos, f, bsz, t_len)
    return h.reshape(bsz, t_len, D_MODEL), [s[None] for s in ev_new], od_new[None]


def kernel(x_prompt, x_sample, state_rwkv_shift, state_rwkv_S, state_mlstm_conv, state_mlstm_C,
           state_mlstm_n, state_mlstm_m, state_ret_S, p_prompt, p_sample, norm_g, ffn_up, ffn_down,
           ple_gate, ple_proj, mix0_w_in, rw_mu, rw_w0, rw_w2, rw_a0, rw_a2, rw_g2, rw_kk, rw_ka,
           rw_rk, rw_ln, ml_conv_w, ml_conv_b, ml_i_bias, ml_f_bias, ml_gn, mix0_w_out,
           ret_w_in, ret_gn, ret_w_out):
    W = dict(norm_g=norm_g, ffn_up=ffn_up, ffn_down=ffn_down, ple_gate=ple_gate, ple_proj=ple_proj,
             mix0_w_in=mix0_w_in, rw_mu=rw_mu, rw_w0=rw_w0, rw_w2=rw_w2, rw_a0=rw_a0, rw_a2=rw_a2,
             rw_g2=rw_g2, rw_kk=rw_kk, rw_ka=rw_ka, rw_rk=rw_rk, rw_ln=rw_ln, ml_conv_w=ml_conv_w,
             ml_conv_b=ml_conv_b, ml_i_bias=ml_i_bias, ml_f_bias=ml_f_bias, ml_gn=ml_gn,
             mix0_w_out=mix0_w_out, ret_w_in=ret_w_in, ret_gn=ret_gn, ret_w_out=ret_w_out)
    f = _prep_weights(W)
    bp, tp = x_prompt.shape[:2]
    ts = x_sample.shape[1]
    y_prompt, ev_p, ret_p = _trunk(x_prompt, p_prompt, jnp.arange(tp), None, None, f)
    ev_s_in = (state_rwkv_shift, state_rwkv_S, state_mlstm_conv, state_mlstm_C, state_mlstm_n,
               state_mlstm_m)
    y_sample, ev_s, ret_s = _trunk(x_sample, p_sample, PAST_LEN + jnp.arange(ts), ev_s_in, state_ret_S, f)
    return (y_prompt, y_sample, *ev_p, ret_p, *ev_s, ret_s)
```

```python
import functools
import math

import jax
import jax.numpy as jnp
from jax import lax
from jax.experimental import pallas as pl
from jax.experimental.pallas import tpu as pltpu

F32 = jnp.float32
BF16 = jnp.bfloat16

D_MODEL = 1024
D_PLE = 256
D_FF = 4 * D_MODEL
RMS_EPS = 1e-6
GN_EPS = 1e-5
CHUNK = 128
RET_CHUNK = 256

A_HEADS = 8
A_HD = 64
A_W = A_HEADS * A_HD
A_DECAY_LORA = 64
A_AAA_LORA = 64
A_GATE_LORA = 128
A_LORA_W = A_DECAY_LORA + A_AAA_LORA + A_GATE_LORA
A_SHIFT_W = 3 * A_W + A_LORA_W
A_LN_EPS = 64e-5

B_HEADS = 4
B_HD = 128
B_W = B_HEADS * B_HD
B_CONV = 4

C_HEADS = 4
C_DK = 256
C_DV = 512
C_OUT_W = C_HEADS * C_DV
ROPE_BASE = 10000.0
PAST_LEN = 16384

LANES = 128
SUBLANES = 8
VMEM_LIMIT = 56 * 1024 * 1024
POST_TF = 4096


def _cparams(*sem):
    return pltpu.CompilerParams(dimension_semantics=sem, vmem_limit_bytes=VMEM_LIMIT)


def _const_spec(shape):
    nd = len(shape)
    return pl.BlockSpec(shape, lambda *_: (0,) * nd, pipeline_mode=pl.Buffered(1))


def _rms(x, g):
    return x * lax.rsqrt(jnp.mean(x * x, axis=-1, keepdims=True) + RMS_EPS) * g


def _dot(a, b):
    return jnp.dot(a, b, preferred_element_type=F32)


def _dot_nt(a, b):
    return lax.dot_general(a, b, (((1,), (1,)), ((), ())), preferred_element_type=F32)


def _dot_tn(a, b):
    return lax.dot_general(a, b, (((0,), (0,)), ((), ())), preferred_element_type=F32)


def _hilo(x):
    hi = x.astype(BF16)
    lo = (x - hi.astype(F32)).astype(BF16)
    return hi, lo


def _dot_sel(x, sel):
    hi, lo = _hilo(x)
    return _dot(hi, sel) + _dot(lo, sel)


def _dot_sel_left(sel, x):
    hi, lo = _hilo(x)
    return _dot(sel, hi) + _dot(sel, lo)


def _dot3(a, b):
    ah, al = _hilo(a)
    bh, bl = _hilo(b)
    return _dot(ah, bh) + _dot(ah, bl) + _dot(al, bh)


def _sigmoid(x):
    return 1.0 / (1.0 + jnp.exp(-x))


def _softplus(x):
    return jnp.maximum(x, 0.0) + jnp.log1p(jnp.exp(-jnp.abs(x)))


def _log_sigmoid(x):
    return -_softplus(-x)


def _silu(x):
    return x * _sigmoid(x)


def _lane_col(x, idx):
    lane = lax.broadcasted_iota(jnp.int32, x.shape, 1)
    return jnp.sum(jnp.where(lane == idx, x, 0.0), axis=1, keepdims=True)


def _row_pick(x, idx):
    row = lax.broadcasted_iota(jnp.int32, x.shape, 0)
    return jnp.sum(jnp.where(row == idx, x, 0.0), axis=0, keepdims=True)


def _inproj_kernel(x_ref, g_ref, *refs):
    n = len(refs) // 2
    xn = _rms(x_ref[...], g_ref[0:1, :]).astype(BF16)
    for w_ref, o_ref in zip(refs[:n], refs[n:]):
        o_ref[...] = _dot_nt(xn, w_ref[...]).astype(o_ref.dtype)


def _inproj(h, g, ws, dtypes, tm):
    n = h.shape[0]
    return pl.pallas_call(
        _inproj_kernel,
        grid=(n // tm,),
        in_specs=[pl.BlockSpec((tm, D_MODEL), lambda i: (i, 0)), _const_spec(g.shape)]
        + [_const_spec(w.shape) for w in ws],
        out_specs=[pl.BlockSpec((tm, w.shape[0]), lambda i: (i, 0)) for w in ws],
        out_shape=[jax.ShapeDtypeStruct((n, w.shape[0]), dt) for w, dt in zip(ws, dtypes)],
        compiler_params=_cparams("parallel"),
        name="inproj",
    )(h, g, *ws)


def _inproj_ret_kernel(x_ref, g_ref, gn_ref, cos_ref, sin_ref, wq_ref, wk_ref, wv_ref, wg_ref, *out_refs,
                       decayed):
    if decayed:
        q_ref, qd_ref, k_ref, kd_ref, v_ref, gate_ref = out_refs
    else:
        q_ref, k_ref, v_ref, gate_ref = out_refs
        qd_ref = kd_ref = None
    half = C_DK // 2
    tm = x_ref.shape[0]
    parts = [slice(0, tm // 2), slice(tm // 2, tm)] if tm >= 2 * LANES else [slice(0, tm)]
    xn = [_rms(x_ref[rows, :], g_ref[0:1, :]).astype(BF16) for rows in parts]
    tabs = [(cos_ref[rows, :], sin_ref[rows, :]) if cos_ref.shape[0] == tm else (cos_ref[...], sin_ref[...])
            for rows in parts]
    row0 = pl.program_id(0) * tm
    row_id = lax.broadcasted_iota(jnp.int32, (tm, half), 0) + row0
    pos = [jnp.bitwise_and(row_id[rows, :], RET_CHUNK - 1).astype(F32) for rows in parts]
    for pi, rows in enumerate(parts):
        gate_ref[rows, :] = (_silu(_dot(xn[pi], wg_ref[...])) * gn_ref[...]).astype(gate_ref.dtype)
    for w_ref, o_ref, d_ref, scale in ((wq_ref, q_ref, qd_ref, None), (wk_ref, k_ref, kd_ref, C_DK ** -0.5)):
        for pi, rows in enumerate(parts):
            u = _dot(xn[pi], w_ref[...])
            cos, sin = tabs[pi]
            for hd in range(C_HEADS):
                x1 = u[:, hd * C_DK:hd * C_DK + half]
                x2 = u[:, hd * C_DK + half:(hd + 1) * C_DK]
                y1 = x1 * cos - x2 * sin
                y2 = x1 * sin + x2 * cos
                if scale is not None:
                    y1 = y1 * scale
                    y2 = y2 * scale
                o_ref[rows, hd * C_DK:hd * C_DK + half] = y1.astype(o_ref.dtype)
                o_ref[rows, hd * C_DK + half:(hd + 1) * C_DK] = y2.astype(o_ref.dtype)
                if decayed:
                    lg = _ret_log_gamma(hd)
                    dec = jnp.exp((pos[pi] + 1.0) * lg) if d_ref is qd_ref else \
                        jnp.exp((RET_CHUNK - 1.0 - pos[pi]) * lg)
                    d_ref[rows, hd * C_DK:hd * C_DK + half] = (y1 * dec).astype(d_ref.dtype)
                    d_ref[rows, hd * C_DK + half:(hd + 1) * C_DK] = (y2 * dec).astype(d_ref.dtype)
    for pi, rows in enumerate(parts):
        v_ref[rows, :] = _dot(xn[pi], wv_ref[...]).astype(v_ref.dtype)


def _inproj_ret(h, g, gn, cos, sin, ws, tm, seq_tiles, dtypes, decayed):
    n = h.shape[0]
    if cos.shape[0] == 1:
        cs_spec = _const_spec(cos.shape)
    else:
        cs_spec = pl.BlockSpec((tm, cos.shape[1]), lambda i: (i % seq_tiles, 0))
    widths = [w.shape[1] for w in ws]
    if decayed:
        widths = [widths[0], widths[0], widths[1], widths[1], widths[2], widths[3]]
    return pl.pallas_call(
        functools.partial(_inproj_ret_kernel, decayed=decayed),
        grid=(n // tm,),
        in_specs=[pl.BlockSpec((tm, D_MODEL), lambda i: (i, 0)), _const_spec(g.shape), _const_spec(gn.shape),
                  cs_spec, cs_spec] + [_const_spec(w.shape) for w in ws],
        out_specs=[pl.BlockSpec((tm, wd), lambda i: (i, 0)) for wd in widths],
        out_shape=[jax.ShapeDtypeStruct((n, wd), dt) for wd, dt in zip(widths, dtypes)],
        compiler_params=_cparams("parallel"),
        name="inproj_ret",
    )(h, g, gn, cos, sin, *ws)


def _rw_prep_kernel(x_ref, sh_ref, mu_ref, wl_ref, vec_ref, seg_ref,
                    r_ref, w_ref, k_ref, v_ref, a_ref, b_ref, g_ref, bonus_ref):
    x = x_ref[...]
    xs = x + (sh_ref[...] - x) * mu_ref[...]
    r = xs[:, :A_W]
    k = xs[:, A_W:2 * A_W]
    v = xs[:, 2 * A_W:3 * A_W]
    lo = xs[:, 3 * A_W:]
    lane = lax.broadcasted_iota(jnp.int32, lo.shape, 1)
    act = jnp.where(lane < A_DECAY_LORA, jnp.tanh(lo),
                    jnp.where(lane < A_DECAY_LORA + A_AAA_LORA, lo, _sigmoid(lo)))
    lora = _dot(act.astype(BF16), wl_ref[...])
    w0, a0, kkw, kaw, rkw = (vec_ref[i:i + 1, :] for i in range(5))
    w_log = -_softplus(-(w0 + lora[:, :A_W])) - 0.5
    decay = jnp.exp(-jnp.exp(w_log))
    a = _sigmoid(a0 + lora[:, A_W:2 * A_W])
    g = lora[:, 2 * A_W:]
    seg = seg_ref[...]
    kk = k * kkw
    kk = kk / jnp.maximum(jnp.sqrt(_dot_sel(kk * kk, seg)), 1e-12)
    k2 = k * (1.0 + (a - 1.0) * kaw)
    bonus = _dot_sel(r * k2 * rkw, seg) * v
    r_ref[...] = r.T
    w_ref[...] = decay.T
    k_ref[...] = k2.T
    v_ref[...] = v.T
    a_ref[...] = (-kk).T
    b_ref[...] = (kk * a).T
    g_ref[...] = g
    bonus_ref[...] = bonus


def _rw_prep(ua, shift, mu, wl, vec, seg):
    n = ua.shape[0]
    full = lambda shape: pl.BlockSpec(shape, lambda i: (0,) * len(shape))
    return pl.pallas_call(
        _rw_prep_kernel,
        grid=(1,),
        in_specs=[full(a.shape) for a in (ua, shift, mu, wl, vec, seg)],
        out_specs=[full((A_W, n))] * 6 + [full((n, A_W))] * 2,
        out_shape=[jax.ShapeDtypeStruct((A_W, n), F32)] * 6 + [jax.ShapeDtypeStruct((n, A_W), F32)] * 2,
        compiler_params=_cparams("arbitrary"),
        name="rw_prep",
    )(ua, shift, mu, wl, vec, seg)


def _rw_step_kernel(r_ref, w_ref, k_ref, v_ref, a_ref, b_ref, s_ref, y_ref, so_ref):
    rt, wt, kt, at, bt = r_ref[...], w_ref[...], k_ref[...], a_ref[...], b_ref[...]
    for v in range(A_HD):
        s = s_ref[0, v]
        sa = jnp.sum(s * at, axis=0, keepdims=True)
        sn = s * wt + sa * bt + v_ref[v:v + 1, :] * kt
        so_ref[0, v] = sn
        y_ref[v:v + 1, :] = jnp.sum(sn * rt, axis=0, keepdims=True)


def _rw_step(r, w, k, v, a, b, s_t):
    bsz = r.shape[1]
    vec = pl.BlockSpec((A_HD, bsz), lambda h: (h, 0))
    st = pl.BlockSpec((1, A_HD, A_HD, bsz), lambda h: (h, 0, 0, 0))
    return pl.pallas_call(
        _rw_step_kernel,
        grid=(A_HEADS,),
        in_specs=[vec] * 6 + [st],
        out_specs=[vec, st],
        out_shape=[jax.ShapeDtypeStruct((A_W, bsz), F32), jax.ShapeDtypeStruct(s_t.shape, F32)],
        compiler_params=_cparams("parallel"),
        name="rw_step",
    )(r, w, k, v, a, b, s_t)


RW_C = 64
(F_RT, F_PM1, F_IP, F_DC, F_KH, F_KC, F_V, F_KK, F_ASIG, F_RKK, F_G, F_PC) = range(12)


def _rw_chunk_kernel(x_ref, mu_ref, wl_ref, vec_ref, ln_ref, o_ref, so_ref, carry, s_scr, f_scr):
    rt = x_ref.shape[0]
    i = pl.program_id(1)

    @pl.when(i == 0)
    def _():
        carry[...] = jnp.zeros_like(carry)
        s_scr[...] = jnp.zeros_like(s_scr)

    def prepare():
        x = x_ref[...]
        row = lax.broadcasted_iota(jnp.int32, x.shape, 0)
        prev = jnp.where(row == 0, carry[...], pltpu.roll(x, 1, axis=0))
        carry[...] = x[rt - 1:rt, :]
        xs = x + (prev - x) * mu_ref[...]
        r = xs[:, :A_W]
        k = xs[:, A_W:2 * A_W]
        lo = xs[:, 3 * A_W:]
        lane = lax.broadcasted_iota(jnp.int32, lo.shape, 1)
        act = jnp.where(lane < A_DECAY_LORA, jnp.tanh(lo),
                        jnp.where(lane < A_DECAY_LORA + A_AAA_LORA, lo, _sigmoid(lo)))
        lora = _dot(act.astype(BF16), wl_ref[...])
        w0, a0, kkw, kaw, rkw = (vec_ref[n:n + 1, :] for n in range(5))
        lw = -math.exp(-0.5) * _sigmoid(w0 + lora[:, :A_W])
        a_sig = _sigmoid(a0 + lora[:, A_W:2 * A_W])
        k2 = k * (1.0 + (a_sig - 1.0) * kaw)

        ri = lax.broadcasted_iota(jnp.int32, (rt, rt), 0)
        ci = lax.broadcasted_iota(jnp.int32, (rt, rt), 1)
        same = jnp.right_shift(ri, 6) == jnp.right_shift(ci, 6)
        chunk_ones = same.astype(BF16)
        chunk_tri = jnp.where(ci <= ri, chunk_ones, jnp.zeros_like(chunk_ones))
        lp = _dot_sel_left(chunk_tri, lw)
        lpc = _dot_sel_left(chunk_ones, lw)
        ip = jnp.exp(-lp)
        dc = jnp.exp(lpc - lp)
        f_scr[F_RT] = r * jnp.exp(lp)
        f_scr[F_PM1] = jnp.exp(lp - lw)
        f_scr[F_IP] = ip
        f_scr[F_DC] = dc
        f_scr[F_KH] = k2 * ip
        f_scr[F_KC] = k2 * dc
        f_scr[F_V] = xs[:, 2 * A_W:3 * A_W]
        f_scr[F_KK] = k * kkw
        f_scr[F_ASIG] = a_sig
        f_scr[F_RKK] = r * k2 * rkw
        f_scr[F_G] = lora[:, 2 * A_W:]
        f_scr[F_PC] = jnp.exp(lpc)

    C = RW_C
    W2 = 2 * A_HD
    npairs = A_HEADS // 2
    lane_lo = lax.broadcasted_iota(jnp.int32, (C, W2), 1) < A_HD
    own = (lax.broadcasted_iota(jnp.int32, (2 * C, W2), 1) < A_HD) == \
          (lax.broadcasted_iota(jnp.int32, (2 * C, W2), 0) < C)
    r4 = lax.broadcasted_iota(jnp.int32, (4 * C, 4 * C), 0)
    c4 = lax.broadcasted_iota(jnp.int32, (4 * C, 4 * C), 1)
    keep = jnp.bitwise_and(c4, C - 1) < jnp.bitwise_and(r4, C - 1) + jnp.right_shift(r4, 7)
    eye_f = (lax.broadcasted_iota(jnp.int32, (W2, W2), 0) ==
             lax.broadcasted_iota(jnp.int32, (W2, W2), 1)).astype(F32)
    zeros_ww = jnp.zeros((W2, W2), F32)
    bf = lambda z: z.astype(BF16)
    stack = lambda z: jnp.concatenate([jnp.where(lane_lo, z, 0.0), jnp.where(lane_lo, 0.0, z)], axis=0)
    spread = lambda col: jnp.where(lane_lo, col[:C], col[C:])
    chunks_per_body = rt // C
    probs = [(cc, p) for cc in range(chunks_per_body) for p in range(npairs)]

    def body():
        rows = [slice(cc * C, (cc + 1) * C) for cc in range(chunks_per_body)]
        ld = lambda f, q: f_scr[f, rows[q[0]], q[1] * W2:(q[1] + 1) * W2]
        a_l, r_l, bc_l, kc_l, v_l, g_l = [], [], [], [], [], []
        pc_l = [ld(F_PC, q)[0:1, :] for q in probs]
        vn_l = [ld(F_V, q) for q in probs]
        gate_l = [ld(F_G, q) for q in probs]
        kk_sq = [jnp.sum(jnp.square(stack(ld(F_KK, q))), axis=-1, keepdims=True) for q in probs]
        bonus_l = [jnp.sum(stack(ld(F_RKK, q)), axis=-1, keepdims=True) for q in probs]
        for e, q in enumerate(probs):
            kk = ld(F_KK, q) * spread(1.0 / jnp.maximum(jnp.sqrt(kk_sq[e]), 1e-12))
            b = kk * ld(F_ASIG, q)
            a_l.append(stack(-kk * ld(F_PM1, q)))
            r_l.append(stack(ld(F_RT, q)))
            bc_l.append(stack(b * ld(F_DC, q)))
            kc_l.append(stack(ld(F_KC, q)))
            v_l.append(stack(vn_l[e]))
            ar = jnp.concatenate([a_l[-1], r_l[-1]], axis=0)
            bk = jnp.concatenate([stack(b * ld(F_IP, q)), stack(ld(F_KH, q))], axis=0)
            g_l.append(jnp.where(keep, _dot_nt(bf(ar), bf(bk)), 0.0))
        n = range(len(probs))
        x_l = [g_l[e][:W2, :W2] for e in n]
        t_l = [eye_f + x_l[e] for e in n]
        x_l = [_dot(bf(x_l[e]), bf(x_l[e])) for e in n]
        m = 2
        while m < C:
            last = 2 * m >= C
            for e in n:
                rhs = t_l[e] if last else jnp.concatenate([x_l[e], t_l[e]], axis=1)
                prod = _dot(bf(x_l[e]), bf(rhs))
                if last:
                    t_l[e] = t_l[e] + prod
                else:
                    x_l[e] = prod[:, :W2]
                    t_l[e] = t_l[e] + prod[:, W2:]
            m *= 2
        lv_l = [_dot(bf(g_l[e][:W2, W2:]), bf(v_l[e])) for e in n]
        apu_l = [_dot(bf(t_l[e]), bf(jnp.concatenate([a_l[e], lv_l[e]], axis=1))) for e in n]
        qy_l = []
        for e in n:
            low = jnp.concatenate([zeros_ww, v_l[e]], axis=1)
            qy_l.append(_dot(bf(g_l[e][W2:, :]), bf(jnp.concatenate([apu_l[e], low], axis=0))))
        mm_l = [_dot_tn(bf(apu_l[e][:, :W2]), bf(bc_l[e])) for e in n]
        n_l = []
        for e in n:
            uv = jnp.concatenate([apu_l[e][:, W2:], v_l[e]], axis=0)
            n_l.append(_dot_tn(bf(uv), bf(jnp.concatenate([bc_l[e], kc_l[e]], axis=0))))
        y_l = []
        for e, q in enumerate(probs):
            p = q[1]
            s0 = s_scr[p]
            s0b = bf(s0)
            y_l.append(_dot_nt(bf(r_l[e] + qy_l[e][:, :W2]), s0b) + qy_l[e][:, W2:])
            s_scr[p] = s0 * pc_l[e] + _dot(s0b, bf(mm_l[e])) + n_l[e]
        mu_l = [jnp.sum(y_l[e], axis=-1, keepdims=True) * (1.0 / A_HD) for e in n]
        d_l = [jnp.where(own, y_l[e] - mu_l[e], 0.0) for e in n]
        var_l = [jnp.sum(d_l[e] * d_l[e], axis=-1, keepdims=True) * (1.0 / A_HD) for e in n]
        for e, q in enumerate(probs):
            p = q[1]
            d = d_l[e] * lax.rsqrt(var_l[e] + A_LN_EPS)
            yn = (d[:C] + d[C:]) * ln_ref[:, p * W2:(p + 1) * W2]
            bonus = spread(bonus_l[e]) * vn_l[e]
            o_ref[rows[q[0]], p * W2:(p + 1) * W2] = ((yn + bonus) * gate_l[e]).astype(o_ref.dtype)

    prepare()
    body()

    @pl.when(i == pl.num_programs(1) - 1)
    def _():
        for p in range(npairs):
            so_ref[0, 2 * p] = s_scr[p, :A_HD, :A_HD]
            so_ref[0, 2 * p + 1] = s_scr[p, A_HD:, A_HD:]


def _rw_chunk(ua, mu, wl, vec, ln, bsz, t_len, rt):
    nt = t_len // rt
    st_spec = pl.BlockSpec((1, A_HEADS, A_HD, A_HD), lambda b, i: (b, 0, 0, 0))
    return pl.pallas_call(
        _rw_chunk_kernel,
        grid=(bsz, nt),
        in_specs=[pl.BlockSpec((rt, A_SHIFT_W), lambda b, i: (b * nt + i, 0)),
                  _const_spec(mu.shape), _const_spec(wl.shape), _const_spec(vec.shape),
                  _const_spec(ln.shape)],
        out_specs=[pl.BlockSpec((rt, A_W), lambda b, i: (b * nt + i, 0)), st_spec],
        out_shape=[jax.ShapeDtypeStruct((bsz * t_len, A_W), BF16),
                   jax.ShapeDtypeStruct((bsz, A_HEADS, A_HD, A_HD), F32)],
        scratch_shapes=[pltpu.VMEM((1, A_SHIFT_W), F32),
                        pltpu.VMEM((A_HEADS // 2, 2 * A_HD, 2 * A_HD), F32),
                        pltpu.VMEM((12, rt, A_W), F32)],
        compiler_params=_cparams("parallel", "arbitrary"),
        name="rw_chunk",
    )(ua, mu, wl, vec, ln)


def _rw_post_kernel(y_ref, bonus_ref, g_ref, ln_ref, seg_ref, o_ref):
    seg = seg_ref[...]
    y = y_ref[...].T
    mu = _dot_sel(y, seg) * (1.0 / A_HD)
    d = y - mu
    var = _dot_sel(d * d, seg) * (1.0 / A_HD)
    yn = d * lax.rsqrt(var + A_LN_EPS) * ln_ref[...]
    o_ref[...] = ((yn + bonus_ref[...]) * g_ref[...]).astype(o_ref.dtype)


def _rw_post(y_t, bonus, g, ln, seg):
    n = bonus.shape[0]
    full = lambda shape: pl.BlockSpec(shape, lambda i: (0,) * len(shape))
    return pl.pallas_call(
        _rw_post_kernel,
        grid=(1,),
        in_specs=[full(a.shape) for a in (y_t, bonus, g, ln, seg)],
        out_specs=full((n, A_W)),
        out_shape=jax.ShapeDtypeStruct((n, A_W), BF16),
        compiler_params=_cparams("arbitrary"),
        name="rw_post",
    )(y_t, bonus, g, ln, seg)


def _head_norm_lanes(x, g, eps):
    mu = jnp.mean(x, axis=-1, keepdims=True)
    d = x - mu
    var = jnp.mean(d * d, axis=-1, keepdims=True)
    return d * lax.rsqrt(var + eps) * g


def _mlstm_chunk_kernel(ub_ref, g_ref, gt_ref, cw_ref, cb_ref, biasr_ref, biasc_ref, gn_ref,
                        y_ref, co_ref, no_ref, mo_ref,
                        tail_s, c_s, n_s, m_s):
    nb, L, _ = ub_ref.shape
    c = pl.program_id(1)

    @pl.when(c == 0)
    def _():
        tail_s[...] = jnp.zeros_like(tail_s)
        c_s[...] = jnp.zeros_like(c_s)
        n_s[...] = jnp.zeros_like(n_s)
        m_s[...] = jnp.zeros_like(m_s)

    r_i = lax.broadcasted_iota(jnp.int32, (L, L), 0)
    c_i = lax.broadcasted_iota(jnp.int32, (L, L), 1)
    causal = c_i <= r_i
    tri = causal.astype(BF16)
    tri_u = (r_i <= c_i).astype(BF16)
    heads = range(B_HEADS)
    sls = [slice(hd * B_HD, (hd + 1) * B_HD) for hd in heads]

    seqs = range(nb)
    ub_l, q_l, k_l, gc_l, gr_l, bcols_l, brows_l = [], [], [], [], [], [], []
    for bi in seqs:
        ub = ub_ref[bi]
        raw = ub[:, :2 * B_W]
        ext = jnp.concatenate([tail_s[bi], raw], axis=0)
        tail_s[bi] = raw[L - SUBLANES:]
        acc = cb_ref[...] + raw * cw_ref[B_CONV - 1:B_CONV, :]
        for jj in range(B_CONV - 1):
            sh = B_CONV - 1 - jj
            acc = acc + pltpu.roll(ext, sh, axis=0)[SUBLANES:] * cw_ref[jj:jj + 1, :]
        act = _silu(acc)
        ub_l.append(ub)
        q_l.append(act[:, :B_W])
        k_l.append(act[:, B_W:] * (B_HD ** -0.5))
        gc = g_ref[bi] + biasr_ref[...]
        gr = gt_ref[bi] + biasc_ref[...]
        gc_l.append(gc)
        gr_l.append(gr)
        bcols_l.append(_dot_sel_left(tri, _log_sigmoid(gc)))
        brows_l.append(_dot_sel(_log_sigmoid(gr), tri_u))
    m_all = [m_s[bi] for bi in seqs]
    n_all = [n_s[bi] for bi in seqs]

    probs = [(bi, hd) for bi in seqs for hd in heads]
    qb = [q_l[bi][:, sls[hd]].astype(BF16) for bi, hd in probs]
    vb = [ub_l[bi][:, 2 * B_W + hd * B_HD:2 * B_W + (hd + 1) * B_HD].astype(BF16) for bi, hd in probs]
    qk = [_dot_nt(qb[e], k_l[bi][:, sls[hd]].astype(BF16)) for e, (bi, hd) in enumerate(probs)]
    qc = [_dot(qb[e], c_s[bi * B_HEADS + hd].astype(BF16)) for e, (bi, hd) in enumerate(probs)]
    m_prev = [m_all[bi][hd:hd + 1, 0:1] for bi, hd in probs]
    n_prev = [n_all[bi][hd:hd + 1, :] for bi, hd in probs]
    bc = [_lane_col(bcols_l[bi], B_HEADS + hd) for bi, hd in probs]
    ic = [_lane_col(gc_l[bi], hd) for bi, hd in probs]
    dlog, inter, b_last, gi = [], [], [], []
    for e, (bi, hd) in enumerate(probs):
        br = brows_l[bi][B_HEADS + hd:B_HEADS + hd + 1, :]
        ir = gr_l[bi][hd:hd + 1, :]
        dlog.append(jnp.where(causal, bc[e] - br + ir, -jnp.inf))
        inter.append(bc[e] + m_prev[e])
        b_last.append(_row_pick(bc[e], L - 1))
        gi.append(b_last[e] - bc[e] + ic[e])
    row_max = [jnp.max(dlog[e], axis=1, keepdims=True) for e in range(len(probs))]
    gi_max = [jnp.max(gi[e], axis=0, keepdims=True) for e in range(len(probs))]
    s_l, iw, m_t, wc, kw, m_new = [], [], [], [], [], []
    for e, (bi, hd) in enumerate(probs):
        m_t.append(jnp.maximum(inter[e], row_max[e]))
        s_l.append(qk[e] * jnp.exp(dlog[e] - m_t[e]))
        iw.append(jnp.exp(inter[e] - m_t[e]))
        m_new.append(jnp.maximum(b_last[e] + m_prev[e], gi_max[e]))
        wc.append(jnp.exp(b_last[e] + m_prev[e] - m_new[e]))
        kw.append(k_l[bi][:, sls[hd]] * jnp.exp(gi[e] - m_new[e]))
    sv = [_dot(s_l[e].astype(BF16), vb[e]) for e in range(len(probs))]
    kv = [_dot_tn(kw[e].astype(BF16), vb[e]) for e in range(len(probs))]
    s_sum = [jnp.sum(s_l[e], axis=1, keepdims=True) for e in range(len(probs))]
    qn = [jnp.sum(q_l[bi][:, sls[hd]] * n_prev[e], axis=1, keepdims=True) for e, (bi, hd) in enumerate(probs)]
    k_sum = [jnp.sum(kw[e], axis=0, keepdims=True) for e in range(len(probs))]
    hb = []
    for e, (bi, hd) in enumerate(probs):
        den = s_sum[e] + iw[e] * qn[e]
        hh = (sv[e] + iw[e] * qc[e]) / jnp.maximum(jnp.abs(den), jnp.exp(-m_t[e]))
        hb.append(_sigmoid(ub_l[bi][:, 3 * B_W + hd * B_HD:3 * B_W + (hd + 1) * B_HD]) * hh)
        c_s[bi * B_HEADS + hd] = wc[e] * c_s[bi * B_HEADS + hd] + kv[e]
    mean = [jnp.sum(hb[e], axis=-1, keepdims=True) * (1.0 / B_HD) for e in range(len(probs))]
    dev = [hb[e] - mean[e] for e in range(len(probs))]
    var = [jnp.sum(dev[e] * dev[e], axis=-1, keepdims=True) * (1.0 / B_HD) for e in range(len(probs))]
    for e, (bi, hd) in enumerate(probs):
        y_ref[bi, :, sls[hd]] = (dev[e] * lax.rsqrt(var[e] + GN_EPS) * gn_ref[:, sls[hd]]).astype(y_ref.dtype)
    pad_rows = jnp.zeros((SUBLANES - B_HEADS, LANES), F32)
    for bi in seqs:
        es = [e for e, (bj, _) in enumerate(probs) if bj == bi]
        n_s[bi] = jnp.concatenate([wc[e] * n_prev[e] + k_sum[e] for e in es] + [pad_rows], axis=0)
        m_s[bi] = jnp.concatenate([jnp.broadcast_to(m_new[e], (1, LANES)) for e in es] + [pad_rows], axis=0)

    @pl.when(c == pl.num_programs(1) - 1)
    def _():
        for bi in range(nb):
            for hd in heads:
                co_ref[bi, hd] = c_s[bi * B_HEADS + hd]
        no_ref[...] = n_s[...]
        mo_ref[...] = m_s[...]


def _mlstm_chunk(ub, gates, gates_t, cw, cb, bias_r, bias_c, gn, bsz, t_len, nb):
    L = CHUNK
    seq = lambda w: pl.BlockSpec((nb, L, w), lambda b, c: (b, c, 0))
    per_b = lambda shp: pl.BlockSpec((nb,) + shp, lambda b, c: (b,) + (0,) * len(shp))
    return pl.pallas_call(
        _mlstm_chunk_kernel,
        grid=(bsz // nb, t_len // L),
        in_specs=[seq(4 * B_W), seq(LANES),
                  pl.BlockSpec((nb, SUBLANES, L), lambda b, c: (b, 0, c)),
                  _const_spec(cw.shape), _const_spec(cb.shape), _const_spec(bias_r.shape),
                  _const_spec(bias_c.shape), _const_spec(gn.shape)],
        out_specs=[seq(B_W), per_b((B_HEADS, B_HD, B_HD)), per_b((SUBLANES, B_HD)),
                   per_b((SUBLANES, LANES))],
        out_shape=[jax.ShapeDtypeStruct((bsz, t_len, B_W), BF16),
                   jax.ShapeDtypeStruct((bsz, B_HEADS, B_HD, B_HD), F32),
                   jax.ShapeDtypeStruct((bsz, SUBLANES, B_HD), F32),
                   jax.ShapeDtypeStruct((bsz, SUBLANES, LANES), F32)],
        scratch_shapes=[pltpu.VMEM((nb, SUBLANES, 2 * B_W), F32),
                        pltpu.VMEM((nb * B_HEADS, B_HD, B_HD), F32),
                        pltpu.VMEM((nb, SUBLANES, B_HD), F32),
                        pltpu.VMEM((nb, SUBLANES, LANES), F32)],
        compiler_params=_cparams("parallel", "arbitrary"),
        name="mlstm_chunk",
    )(ub, gates, gates_t, cw, cb, bias_r, bias_c, gn)


def _mlstm_step_kernel(ub_ref, g_ref, conv_ref, cw_ref, cb_ref, biasr_ref, gn_ref, c_ref, n_ref, m_ref,
                       y_ref, co_ref, no_ref, mo_ref):
    bb = ub_ref.shape[0]
    ub = ub_ref[...]
    raw = ub[:, :2 * B_W]
    conv = conv_ref[...]
    acc = cb_ref[...] + raw * cw_ref[B_CONV - 1:B_CONV, :]
    for jj in range(B_CONV - 1):
        acc = acc + conv[:, jj * 2 * B_W:(jj + 1) * 2 * B_W] * cw_ref[jj:jj + 1, :]
    qk = _silu(acc)
    q_all = qk[:, :B_W]
    k_all = qk[:, B_W:] * (B_HD ** -0.5)
    gc = g_ref[...] + biasr_ref[...]
    lf = _log_sigmoid(gc)
    m_all = m_ref[...]
    row_pad = lax.broadcasted_iota(jnp.int32, (LANES, LANES), 0)
    pad = jnp.zeros((LANES - bb, B_HD), F32)
    m_out = jnp.zeros((bb, LANES), F32)
    lane_m = lax.broadcasted_iota(jnp.int32, (bb, LANES), 1)
    for hd in range(B_HEADS):
        sl = slice(hd * B_HD, (hd + 1) * B_HD)
        q = q_all[:, sl]
        k = k_all[:, sl]
        v = ub[:, 2 * B_W + hd * B_HD:2 * B_W + (hd + 1) * B_HD]
        o = ub[:, 3 * B_W + hd * B_HD:3 * B_W + (hd + 1) * B_HD]
        ic = _lane_col(gc, hd)
        f = _lane_col(lf, B_HEADS + hd)
        m_prev = _lane_col(m_all, hd)
        n_prev = n_ref[:, sl]
        m_t = jnp.maximum(f + m_prev, ic)
        s = jnp.sum(q * k, axis=1, keepdims=True) * jnp.exp(ic - m_t)
        iw = jnp.exp(f + m_prev - m_t)
        wi = jnp.exp(ic - m_t)
        qb = q.astype(BF16)
        kw_t = jnp.concatenate([k * wi, pad], axis=0).T
        v_pad = jnp.concatenate([v, pad], axis=0)
        qc = jnp.zeros((bb, B_HD), F32)
        row_b = lax.broadcasted_iota(jnp.int32, (bb, B_HD), 0)
        for bi in range(bb):
            c_prev = c_ref[bi, hd]
            qc = qc + jnp.where(row_b == bi, _dot(qb, c_prev.astype(BF16)), 0.0)
            v_one = jnp.where(row_pad == bi, v_pad, 0.0)
            co_ref[bi, hd] = _row_pick(iw, bi) * c_prev + _dot3(kw_t, v_one)
        num = s * v + iw * qc
        den = s + iw * jnp.sum(q * n_prev, axis=1, keepdims=True)
        hh = num / jnp.maximum(jnp.abs(den), jnp.exp(-m_t))
        no_ref[:, sl] = iw * n_prev + wi * k
        m_out = jnp.where(lane_m == hd, m_t, m_out)
        hb = _sigmoid(o) * hh
        y_ref[:, sl] = _head_norm_lanes(hb, gn_ref[:, sl], GN_EPS).astype(y_ref.dtype)
    mo_ref[...] = m_out


def _mlstm_step(ub, gates, conv, cw, cb, bias_r, gn, c0, n0, m0, bb):
    bsz = ub.shape[0]
    row = lambda w: pl.BlockSpec((bb, w), lambda i: (i, 0))
    c_spec = pl.BlockSpec((bb, B_HEADS, B_HD, B_HD), lambda i: (i, 0, 0, 0))
    return pl.pallas_call(
        _mlstm_step_kernel,
        grid=(bsz // bb,),
        in_specs=[row(4 * B_W), row(LANES), row((B_CONV - 1) * 2 * B_W),
                  _const_spec(cw.shape), _const_spec(cb.shape), _const_spec(bias_r.shape),
                  _const_spec(gn.shape), c_spec, row(B_W), row(LANES)],
        out_specs=[row(B_W), c_spec, row(B_W), row(LANES)],
        out_shape=[jax.ShapeDtypeStruct((bsz, B_W), BF16),
                   jax.ShapeDtypeStruct(c0.shape, F32),
                   jax.ShapeDtypeStruct((bsz, B_W), F32),
                   jax.ShapeDtypeStruct((bsz, LANES), F32)],
        compiler_params=_cparams("parallel"),
        name="mlstm_step",
    )(ub, gates, conv, cw, cb, bias_r, gn, c0, n0, m0)


def _ret_log_gamma(hd):
    return math.log1p(-(2.0 ** (-5.0 - hd)))


def _ret_chunk_kernel(q_ref, qd_ref, k_ref, kd_ref, v_ref, g_ref, y_ref, so_ref, s_s, mask_s):
    nb, L, _ = q_ref.shape
    c = pl.program_id(1)

    @pl.when(c == 0)
    def _():
        s_s[...] = jnp.zeros_like(s_s)
        r_i = lax.broadcasted_iota(jnp.int32, (L, L), 0)
        c_i = lax.broadcasted_iota(jnp.int32, (L, L), 1)
        diff = (r_i - c_i).astype(F32)
        for hd in range(C_HEADS):
            mask_s[hd] = jnp.where(diff >= 0, jnp.exp(jnp.maximum(diff, 0.0) * _ret_log_gamma(hd)), 0.0)

    heads = range(C_HEADS)
    kls = [slice(hd * C_DK, (hd + 1) * C_DK) for hd in heads]
    sls = [slice(hd * C_DV, (hd + 1) * C_DV) for hd in heads]
    probs = [(bi, hd) for bi in range(nb) for hd in heads]
    n = range(len(probs))

    sc = [(_dot_nt(q_ref[bi, :, kls[hd]], k_ref[bi, :, kls[hd]]) * mask_s[hd]).astype(BF16) for bi, hd in probs]
    o_l = []
    for e, (bi, hd) in enumerate(probs):
        sl = sls[hd]
        c_dec = math.exp(L * _ret_log_gamma(hd))
        s_prev = s_s[e]
        o_l.append(_dot(sc[e], v_ref[bi, :, sl]) + _dot(qd_ref[bi, :, kls[hd]], s_prev.astype(BF16)))
        s_s[e] = c_dec * s_prev + _dot_tn(kd_ref[bi, :, kls[hd]], v_ref[bi, :, sl])
    mean = [jnp.sum(o_l[e], axis=-1, keepdims=True) * (1.0 / C_DV) for e in n]
    dev = [o_l[e] - mean[e] for e in n]
    var = [jnp.sum(dev[e] * dev[e], axis=-1, keepdims=True) * (1.0 / C_DV) for e in n]
    for e, (bi, hd) in enumerate(probs):
        sl = sls[hd]
        y_ref[bi, :, sl] = (dev[e] * lax.rsqrt(var[e] + GN_EPS) * g_ref[bi, :, sl]).astype(y_ref.dtype)

    @pl.when(c == pl.num_programs(1) - 1)
    def _():
        for e, (bi, hd) in enumerate(probs):
            so_ref[bi, hd] = s_s[e]


def _ret_chunk(q, qd, k, kd, v, g, bsz, t_len, nb):
    L = RET_CHUNK
    seq = lambda w: pl.BlockSpec((nb, L, w), lambda b, c: (b, c, 0))
    st = pl.BlockSpec((nb, C_HEADS, C_DK, C_DV), lambda b, c: (b, 0, 0, 0))
    return pl.pallas_call(
        _ret_chunk_kernel,
        grid=(bsz // nb, t_len // L),
        in_specs=[seq(C_HEADS * C_DK)] * 4 + [seq(C_OUT_W), seq(C_OUT_W)],
        out_specs=[seq(C_OUT_W), st],
        out_shape=[jax.ShapeDtypeStruct((bsz, t_len, C_OUT_W), BF16),
                   jax.ShapeDtypeStruct((bsz, C_HEADS, C_DK, C_DV), F32)],
        scratch_shapes=[pltpu.VMEM((nb * C_HEADS, C_DK, C_DV), F32), pltpu.VMEM((C_HEADS, L, L), F32)],
        compiler_params=_cparams("parallel", "arbitrary"),
        name="ret_chunk",
    )(q, qd, k, kd, v, g)


def _ret_step_kernel(q_ref, k_ref, v_ref, g_ref, s_ref, y_ref, so_ref):
    bb = q_ref.shape[0]
    hd = pl.program_id(1)
    gam = 1.0 - jnp.exp2(-5.0 - jnp.full((1, 1), hd).astype(F32))
    q = q_ref[...]
    k = k_ref[...]
    v = v_ref[...]
    s = jnp.sum(q * k, axis=1, keepdims=True)
    pad = jnp.zeros((LANES - bb, C_DK), F32)
    k_t = jnp.concatenate([k, pad], axis=0).T
    q_t = jnp.concatenate([q * gam, pad], axis=0).T
    row_b = lax.broadcasted_iota(jnp.int32, (bb, C_DV), 0)
    qs = jnp.zeros((bb, C_DV), F32)
    for bi in range(bb):
        s_prev = s_ref[bi, 0]
        qs = jnp.where(row_b == bi, jnp.sum(_lane_col(q_t, bi) * s_prev, axis=0, keepdims=True), qs)
        so_ref[bi, 0] = gam * s_prev + _lane_col(k_t, bi) * v[bi:bi + 1, :]
    o = s * v + qs
    y = _head_norm_lanes(o, g_ref[...], GN_EPS)
    y_ref[...] = y.astype(y_ref.dtype)


def _ret_step(q, k, v, g, s0, bb):
    bsz = q.shape[0]
    st = pl.BlockSpec((bb, 1, C_DK, C_DV), lambda i, h: (i, h, 0, 0))
    return pl.pallas_call(
        _ret_step_kernel,
        grid=(bsz // bb, C_HEADS),
        in_specs=[pl.BlockSpec((bb, C_DK), lambda i, h: (i, h)),
                  pl.BlockSpec((bb, C_DK), lambda i, h: (i, h)),
                  pl.BlockSpec((bb, C_DV), lambda i, h: (i, h)),
                  pl.BlockSpec((bb, C_DV), lambda i, h: (i, h)), st],
        out_specs=[pl.BlockSpec((bb, C_DV), lambda i, h: (i, h)), st],
        out_shape=[jax.ShapeDtypeStruct((bsz, C_OUT_W), BF16),
                   jax.ShapeDtypeStruct(s0.shape, F32)],
        compiler_params=_cparams("parallel", "parallel"),
        name="ret_step",
    )(q, k, v, g, s0)


def _post_kernel(*refs, n_mix):
    h_ref = refs[0]
    mix_refs = refs[1:1 + n_mix]
    wout_refs = refs[1 + n_mix:1 + 2 * n_mix]
    (p_ref, g_ref, up_ref, down_ref, gate_ref, proj_ref, o_ref, h1_s, xn_s, acc_s) = refs[1 + 2 * n_mix:]
    f = pl.program_id(1)

    tm = h_ref.shape[0]
    halves = [slice(0, tm // 2), slice(tm // 2, tm)] if tm >= 2 * LANES else [slice(0, tm)]

    @pl.when(f == 0)
    def _():
        for rows in halves:
            mix = _dot(mix_refs[0][rows, :], wout_refs[0][...])
            for m_ref, w_ref in zip(mix_refs[1:], wout_refs[1:]):
                mix = mix + _dot(m_ref[rows, :], w_ref[...])
            h1 = h_ref[rows, :] + _rms(mix, g_ref[1:2, :])
            h1_s[rows, :] = h1
            xn_s[rows, :] = _rms(h1, g_ref[2:3, :]).astype(BF16)
        acc_s[...] = jnp.zeros_like(acc_s)

    hid = jnp.square(jnp.maximum(_dot(xn_s[...], up_ref[...]), 0.0))
    acc_s[...] += _dot(hid.astype(BF16), down_ref[...])

    @pl.when(f == pl.num_programs(1) - 1)
    def _():
        for rows in halves:
            h2 = h1_s[rows, :] + _rms(acc_s[rows, :], g_ref[3:4, :])
            gate = _sigmoid(_dot(h2.astype(BF16), gate_ref[...]))
            emb = _dot(p_ref[rows, :].astype(BF16), proj_ref[...])
            o_ref[rows, :] = h2 + gate * emb


def _post(h, mixes, wouts, p, g, up, down, gate, proj, li, tm, tf):
    n = h.shape[0]
    n_mix = len(mixes)
    row = lambda w: pl.BlockSpec((tm, w), lambda i, f: (i, 0))
    layer = lambda a: pl.BlockSpec((None,) + a.shape[1:], lambda i, f: (li,) + (0,) * (a.ndim - 1),
                                   pipeline_mode=pl.Buffered(1))
    ff_mode = pl.Buffered(1) if tf == D_FF else None
    return pl.pallas_call(
        functools.partial(_post_kernel, n_mix=n_mix),
        grid=(n // tm, D_FF // tf),
        in_specs=[row(D_MODEL)] + [row(m.shape[1]) for m in mixes]
        + [_const_spec(w.shape) for w in wouts]
        + [pl.BlockSpec((None, tm, D_PLE), lambda i, f: (li, i, 0)), layer(g),
           pl.BlockSpec((None, D_MODEL, tf), lambda i, f: (li, 0, f), pipeline_mode=ff_mode),
           pl.BlockSpec((None, tf, D_MODEL), lambda i, f: (li, f, 0), pipeline_mode=ff_mode),
           layer(gate), layer(proj)],
        out_specs=row(D_MODEL),
        out_shape=jax.ShapeDtypeStruct((n, D_MODEL), F32),
        scratch_shapes=[pltpu.VMEM((tm, D_MODEL), F32), pltpu.VMEM((tm, D_MODEL), BF16),
                        pltpu.VMEM((tm, D_MODEL), F32)],
        compiler_params=_cparams("parallel", "arbitrary"),
        name="post",
    )(h, *mixes, *wouts, p, g, up, down, gate, proj)


def _prep_weights(W):
    f = {}
    w_in = W['mix0_w_in'][0]
    w_in_t = w_in.T
    f['w_a'] = w_in_t[:A_SHIFT_W].astype(BF16)
    f['w_b'] = w_in_t[A_SHIFT_W:A_SHIFT_W + 4 * B_W].astype(BF16)
    f['w_g'] = jnp.pad(w_in_t[A_SHIFT_W + 4 * B_W:], ((0, LANES - 2 * B_HEADS), (0, 0))).astype(BF16)
    wl = jnp.zeros((A_LORA_W, 3 * A_W), F32)
    wl = wl.at[:A_DECAY_LORA, :A_W].set(W['rw_w2'][0])
    wl = wl.at[A_DECAY_LORA:A_DECAY_LORA + A_AAA_LORA, A_W:2 * A_W].set(W['rw_a2'][0])
    wl = wl.at[A_DECAY_LORA + A_AAA_LORA:, 2 * A_W:].set(W['rw_g2'][0])
    f['w_lora'] = wl.astype(BF16)
    f['rw_vec'] = jnp.concatenate(
        [W['rw_w0'], W['rw_a0'], W['rw_kk'], W['rw_ka'], W['rw_rk'][0].reshape(1, A_W),
         jnp.zeros((3, A_W), F32)], axis=0)
    f['rw_mu'] = W['rw_mu']
    f['rw_ln'] = W['rw_ln']
    f['seg'] = (jnp.arange(A_W)[:, None] // A_HD == jnp.arange(A_W)[None, :] // A_HD).astype(BF16)
    f['cw'] = W['ml_conv_w'][0]
    f['cb'] = W['ml_conv_b']
    gate_bias = jnp.concatenate([W['ml_i_bias'][0], W['ml_f_bias'][0]])
    f['bias_r'] = jnp.pad(gate_bias, (0, LANES - 2 * B_HEADS)).reshape(1, LANES)
    f['bias_c'] = gate_bias.reshape(2 * B_HEADS, 1)
    f['ml_gn'] = W['ml_gn']
    w_out0 = W['mix0_w_out'][0].astype(BF16)
    f['w_out_a'] = w_out0[:A_W]
    f['w_out_b'] = w_out0[A_W:]
    w_ret = W['ret_w_in'][0]
    nq = C_HEADS * C_DK
    f['w_rq'] = w_ret[:, :nq].astype(BF16)
    f['w_rk'] = w_ret[:, nq:2 * nq].astype(BF16)
    f['w_rv'] = w_ret[:, 2 * nq:2 * nq + C_OUT_W].astype(BF16)
    f['w_rg'] = w_ret[:, 2 * nq + C_OUT_W:].astype(BF16)
    f['ret_gn'] = W['ret_gn']
    f['w_ret_out'] = W['ret_w_out'][0].astype(BF16)
    for name in ('ffn_up', 'ffn_down', 'ple_gate', 'ple_proj'):
        f[name] = W[name].astype(BF16)
    f['norm_g'] = W['norm_g']
    return f


def _rope_tables(pos):
    half = C_DK // 2
    inv = ROPE_BASE ** (-jnp.arange(half, dtype=F32) / half)
    ang = pos.astype(F32)[:, None] * inv[None, :]
    return jnp.cos(ang), jnp.sin(ang)


def _even_layer(h, p, st, f, bsz, t_len):
    n = bsz * t_len
    seq = st is None
    tm = 512 if seq else n
    ua, ub, gates = _inproj(h, f['norm_g'][0], [f['w_a'], f['w_b'], f['w_g']], [F32, F32, F32], tm)
    qk_raw = ub.reshape(bsz, t_len, 4 * B_W)[:, :, :2 * B_W]

    if seq:
        ya, s_a_new = _rw_chunk(ua, f['rw_mu'], f['w_lora'], f['rw_vec'], f['rw_ln'], bsz, t_len, rt=256)
        conv_new = qk_raw[:, -(B_CONV - 1):]
    else:
        shift, s_a, conv, c0, n0, m0 = st
        conv_new = jnp.concatenate([conv, qk_raw], axis=1)[:, -(B_CONV - 1):]
        r, w, k, v, a, b, g, bonus = _rw_prep(ua, shift, f['rw_mu'], f['w_lora'], f['rw_vec'], f['seg'])
        y_t, s_t = _rw_step(r, w, k, v, a, b, jnp.transpose(s_a, (1, 2, 3, 0)))
        s_a_new = jnp.transpose(s_t, (3, 0, 1, 2))
        ya = _rw_post(y_t, bonus, g, f['rw_ln'], f['seg'])

    if seq:
        gates3 = gates.reshape(bsz, t_len, LANES)
        gates_t = gates3[:, :, :SUBLANES].transpose(0, 2, 1)
        yb, c_new, n_new, m_new = _mlstm_chunk(ub.reshape(bsz, t_len, 4 * B_W), gates3, gates_t, f['cw'],
                                               f['cb'], f['bias_r'], f['bias_c'], f['ml_gn'], bsz, t_len,
                                               nb=4)
        yb = yb.reshape(n, B_W)
        n_new = n_new[:, :B_HEADS]
        m_new = m_new[:, :B_HEADS, 0]
    else:
        m0p = jnp.pad(m0, ((0, 0), (0, LANES - B_HEADS)))
        yb, c_new, n_new, m_new = _mlstm_step(ub, gates, conv.reshape(bsz, -1), f['cw'], f['cb'],
                                              f['bias_r'], f['ml_gn'], c0, n0.reshape(bsz, B_W), m0p,
                                              bb=SUBLANES)
        n_new = n_new.reshape(bsz, B_HEADS, B_HD)
        m_new = m_new[:, :B_HEADS]

    h_next = _post(h, [ya, yb], [f['w_out_a'], f['w_out_b']], p, f['norm_g'], f['ffn_up'],
                   f['ffn_down'], f['ple_gate'], f['ple_proj'], li=0, tm=tm, tf=POST_TF)

    shift_new = ua.reshape(bsz, t_len, A_SHIFT_W)[:, -1]
    return h_next, (shift_new, s_a_new, conv_new, c_new, n_new, m_new)


def _odd_layer(h, p, s0, pos, f, bsz, t_len):
    n = bsz * t_len
    seq = s0 is None
    cos, sin = _rope_tables(pos)
    tm = 512 if seq else n
    ws = [f['w_rq'], f['w_rk'], f['w_rv'], f['w_rg']]
    if seq:
        q, qd, k, kd, v, g = _inproj_ret(h, f['norm_g'][1], f['ret_gn'], cos, sin, ws, tm, t_len // tm,
                                         [BF16] * 5 + [F32], decayed=True)
        as3 = lambda a: a.reshape(bsz, t_len, a.shape[-1])
        y, s_new = _ret_chunk(as3(q), as3(qd), as3(k), as3(kd), as3(v), as3(g), bsz, t_len, nb=2)
        y = y.reshape(n, C_OUT_W)
    else:
        q, k, v, g = _inproj_ret(h, f['norm_g'][1], f['ret_gn'], cos, sin, ws, tm, 1, [F32] * 4,
                                 decayed=False)
        y, s_new = _ret_step(q, k, v, g, s0, bb=2 * SUBLANES)
    h_next = _post(h, [y], [f['w_ret_out']], p, f['norm_g'], f['ffn_up'], f['ffn_down'],
                   f['ple_gate'], f['ple_proj'], li=1, tm=512 if seq else n, tf=POST_TF)
    return h_next, s_new


def _trunk(x, p, pos, ev_states, od_state, f):
    bsz, t_len, _ = x.shape
    n = bsz * t_len
    h = x.reshape(n, D_MODEL)
    ev_in = None if ev_states is None else tuple(s[0] for s in ev_states)
    od_in = None if od_state is None else od_state[0]
    p = p.reshape(p.shape[0], n, D_PLE)
    h, ev_new = _even_layer(h, p, ev_in, f, bsz, t_len)
    h, od_new = _odd_layer(h, p, od_in, pos, f, bsz, t_len)
    return h.reshape(bsz, t_len, D_MODEL), [s[None] for s in ev_new], od_new[None]


def kernel(x_prompt, x_sample, state_rwkv_shift, state_rwkv_S, state_mlstm_conv, state_mlstm_C,
           state_mlstm_n, state_mlstm_m, state_ret_S, p_prompt, p_sample, norm_g, ffn_up, ffn_down,
           ple_gate, ple_proj, mix0_w_in, rw_mu, rw_w0, rw_w2, rw_a0, rw_a2, rw_g2, rw_kk, rw_ka,
           rw_rk, rw_ln, ml_conv_w, ml_conv_b, ml_i_bias, ml_f_bias, ml_gn, mix0_w_out,
           ret_w_in, ret_gn, ret_w_out):
    W = dict(norm_g=norm_g, ffn_up=ffn_up, ffn_down=ffn_down, ple_gate=ple_gate, ple_proj=ple_proj,
             mix0_w_in=mix0_w_in, rw_mu=rw_mu, rw_w0=rw_w0, rw_w2=rw_w2, rw_a0=rw_a0, rw_a2=rw_a2,
             rw_g2=rw_g2, rw_kk=rw_kk, rw_ka=rw_ka, rw_rk=rw_rk, rw_ln=rw_ln, ml_conv_w=ml_conv_w,
             ml_conv_b=ml_conv_b, ml_i_bias=ml_i_bias, ml_f_bias=ml_f_bias, ml_gn=ml_gn,
             mix0_w_out=mix0_w_out, ret_w_in=ret_w_in, ret_gn=ret_gn, ret_w_out=ret_w_out)
    f = _prep_weights(W)
    bp, tp = x_prompt.shape[:2]
    ts = x_sample.shape[1]
    y_prompt, ev_p, ret_p = _trunk(x_prompt, p_prompt, jnp.arange(tp), None, None, f)
    ev_s_in = (state_rwkv_shift, state_rwkv_S, state_mlstm_conv, state_mlstm_C, state_mlstm_n,
               state_mlstm_m)
    y_sample, ev_s, ret_s = _trunk(x_sample, p_sample, PAST_LEN + jnp.arange(ts), ev_s_in, state_ret_S, f)
    return (y_prompt, y_sample, *ev_p, ret_p, *ev_s, ret_s)
```

```python
import functools
import math

import jax
import jax.numpy as jnp
from jax import lax
from jax.experimental import pallas as pl
from jax.experimental.pallas import tpu as pltpu

F32 = jnp.float32
BF16 = jnp.bfloat16

D_MODEL = 1024
D_PLE = 256
D_FF = 4 * D_MODEL
RMS_EPS = 1e-6
GN_EPS = 1e-5
CHUNK = 128
RET_CHUNK = 256

A_HEADS = 8
A_HD = 64
A_W = A_HEADS * A_HD
A_DECAY_LORA = 64
A_AAA_LORA = 64
A_GATE_LORA = 128
A_LORA_W = A_DECAY_LORA + A_AAA_LORA + A_GATE_LORA
A_SHIFT_W = 3 * A_W + A_LORA_W
A_LN_EPS = 64e-5

B_HEADS = 4
B_HD = 128
B_W = B_HEADS * B_HD
B_CONV = 4

C_HEADS = 4
C_DK = 256
C_DV = 512
C_OUT_W = C_HEADS * C_DV
ROPE_BASE = 10000.0
PAST_LEN = 16384

LANES = 128
SUBLANES = 8
VMEM_LIMIT = 56 * 1024 * 1024
POST_TF = 4096


def _cparams(*sem):
    return pltpu.CompilerParams(dimension_semantics=sem, vmem_limit_bytes=VMEM_LIMIT)


def _const_spec(shape):
    nd = len(shape)
    return pl.BlockSpec(shape, lambda *_: (0,) * nd, pipeline_mode=pl.Buffered(1))


def _rms(x, g):
    return x * lax.rsqrt(jnp.mean(x * x, axis=-1, keepdims=True) + RMS_EPS) * g


def _dot(a, b):
    return jnp.dot(a, b, preferred_element_type=F32)


def _dot_nt(a, b):
    return lax.dot_general(a, b, (((1,), (1,)), ((), ())), preferred_element_type=F32)


def _dot_tn(a, b):
    return lax.dot_general(a, b, (((0,), (0,)), ((), ())), preferred_element_type=F32)


def _hilo(x):
    hi = x.astype(BF16)
    lo = (x - hi.astype(F32)).astype(BF16)
    return hi, lo


def _dot_sel(x, sel):
    hi, lo = _hilo(x)
    return _dot(hi, sel) + _dot(lo, sel)


def _dot_sel_left(sel, x):
    hi, lo = _hilo(x)
    return _dot(sel, hi) + _dot(sel, lo)


def _dot3(a, b):
    ah, al = _hilo(a)
    bh, bl = _hilo(b)
    return _dot(ah, bh) + _dot(ah, bl) + _dot(al, bh)


def _sigmoid(x):
    return 1.0 / (1.0 + jnp.exp(-x))


def _softplus(x):
    return jnp.maximum(x, 0.0) + jnp.log1p(jnp.exp(-jnp.abs(x)))


def _log_sigmoid(x):
    return -_softplus(-x)


def _silu(x):
    return x * _sigmoid(x)


def _lane_col(x, idx):
    lane = lax.broadcasted_iota(jnp.int32, x.shape, 1)
    return jnp.sum(jnp.where(lane == idx, x, 0.0), axis=1, keepdims=True)


def _row_pick(x, idx):
    row = lax.broadcasted_iota(jnp.int32, x.shape, 0)
    return jnp.sum(jnp.where(row == idx, x, 0.0), axis=0, keepdims=True)


def _inproj_kernel(x_ref, g_ref, *refs):
    n = len(refs) // 2
    xn = _rms(x_ref[...], g_ref[0:1, :]).astype(BF16)
    for w_ref, o_ref in zip(refs[:n], refs[n:]):
        o_ref[...] = _dot_nt(xn, w_ref[...]).astype(o_ref.dtype)


def _inproj(h, g, ws, dtypes, tm):
    n = h.shape[0]
    return pl.pallas_call(
        _inproj_kernel,
        grid=(n // tm,),
        in_specs=[pl.BlockSpec((tm, D_MODEL), lambda i: (i, 0)), _const_spec(g.shape)]
        + [_const_spec(w.shape) for w in ws],
        out_specs=[pl.BlockSpec((tm, w.shape[0]), lambda i: (i, 0)) for w in ws],
        out_shape=[jax.ShapeDtypeStruct((n, w.shape[0]), dt) for w, dt in zip(ws, dtypes)],
        compiler_params=_cparams("parallel"),
        name="inproj",
    )(h, g, *ws)


def _inproj_ret_kernel(x_ref, g_ref, gn_ref, cos_ref, sin_ref, wq_ref, wk_ref, wv_ref, wg_ref, *out_refs,
                       decayed):
    if decayed:
        q_ref, qd_ref, k_ref, kd_ref, v_ref, gate_ref = out_refs
    else:
        q_ref, k_ref, v_ref, gate_ref = out_refs
        qd_ref = kd_ref = None
    xn = _rms(x_ref[...], g_ref[0:1, :]).astype(BF16)
    cos = cos_ref[...]
    sin = sin_ref[...]
    half = C_DK // 2
    tm = x_ref.shape[0]
    row0 = pl.program_id(0) * tm
    pos = jnp.bitwise_and(lax.broadcasted_iota(jnp.int32, (tm, half), 0) + row0, RET_CHUNK - 1).astype(F32)
    gate_ref[...] = (_silu(_dot(xn, wg_ref[...])) * gn_ref[...]).astype(gate_ref.dtype)
    for w_ref, o_ref, d_ref, scale in ((wq_ref, q_ref, qd_ref, None), (wk_ref, k_ref, kd_ref, C_DK ** -0.5)):
        u = _dot(xn, w_ref[...])
        for hd in range(C_HEADS):
            x1 = u[:, hd * C_DK:hd * C_DK + half]
            x2 = u[:, hd * C_DK + half:(hd + 1) * C_DK]
            y1 = x1 * cos - x2 * sin
            y2 = x1 * sin + x2 * cos
            if scale is not None:
                y1 = y1 * scale
                y2 = y2 * scale
            o_ref[:, hd * C_DK:hd * C_DK + half] = y1.astype(o_ref.dtype)
            o_ref[:, hd * C_DK + half:(hd + 1) * C_DK] = y2.astype(o_ref.dtype)
            if decayed:
                lg = _ret_log_gamma(hd)
                dec = jnp.exp((pos + 1.0) * lg) if d_ref is qd_ref else jnp.exp((RET_CHUNK - 1.0 - pos) * lg)
                d_ref[:, hd * C_DK:hd * C_DK + half] = (y1 * dec).astype(d_ref.dtype)
                d_ref[:, hd * C_DK + half:(hd + 1) * C_DK] = (y2 * dec).astype(d_ref.dtype)
    v_ref[...] = _dot(xn, wv_ref[...]).astype(v_ref.dtype)


def _inproj_ret(h, g, gn, cos, sin, ws, tm, seq_tiles, dtypes, decayed):
    n = h.shape[0]
    if cos.shape[0] == 1:
        cs_spec = _const_spec(cos.shape)
    else:
        cs_spec = pl.BlockSpec((tm, cos.shape[1]), lambda i: (i % seq_tiles, 0))
    widths = [w.shape[1] for w in ws]
    if decayed:
        widths = [widths[0], widths[0], widths[1], widths[1], widths[2], widths[3]]
    return pl.pallas_call(
        functools.partial(_inproj_ret_kernel, decayed=decayed),
        grid=(n // tm,),
        in_specs=[pl.BlockSpec((tm, D_MODEL), lambda i: (i, 0)), _const_spec(g.shape), _const_spec(gn.shape),
                  cs_spec, cs_spec] + [_const_spec(w.shape) for w in ws],
        out_specs=[pl.BlockSpec((tm, wd), lambda i: (i, 0)) for wd in widths],
        out_shape=[jax.ShapeDtypeStruct((n, wd), dt) for wd, dt in zip(widths, dtypes)],
        compiler_params=_cparams("parallel"),
        name="inproj_ret",
    )(h, g, gn, cos, sin, *ws)


def _rw_prep_kernel(x_ref, sh_ref, mu_ref, wl_ref, vec_ref, seg_ref,
                    r_ref, w_ref, k_ref, v_ref, a_ref, b_ref, g_ref, bonus_ref):
    x = x_ref[...]
    xs = x + (sh_ref[...] - x) * mu_ref[...]
    r = xs[:, :A_W]
    k = xs[:, A_W:2 * A_W]
    v = xs[:, 2 * A_W:3 * A_W]
    lo = xs[:, 3 * A_W:]
    lane = lax.broadcasted_iota(jnp.int32, lo.shape, 1)
    act = jnp.where(lane < A_DECAY_LORA, jnp.tanh(lo),
                    jnp.where(lane < A_DECAY_LORA + A_AAA_LORA, lo, _sigmoid(lo)))
    lora = _dot(act.astype(BF16), wl_ref[...])
    w0, a0, kkw, kaw, rkw = (vec_ref[i:i + 1, :] for i in range(5))
    w_log = -_softplus(-(w0 + lora[:, :A_W])) - 0.5
    decay = jnp.exp(-jnp.exp(w_log))
    a = _sigmoid(a0 + lora[:, A_W:2 * A_W])
    g = lora[:, 2 * A_W:]
    seg = seg_ref[...]
    kk = k * kkw
    kk = kk / jnp.maximum(jnp.sqrt(_dot_sel(kk * kk, seg)), 1e-12)
    k2 = k * (1.0 + (a - 1.0) * kaw)
    bonus = _dot_sel(r * k2 * rkw, seg) * v
    r_ref[...] = r.T
    w_ref[...] = decay.T
    k_ref[...] = k2.T
    v_ref[...] = v.T
    a_ref[...] = (-kk).T
    b_ref[...] = (kk * a).T
    g_ref[...] = g
    bonus_ref[...] = bonus


def _rw_prep(ua, shift, mu, wl, vec, seg):
    n = ua.shape[0]
    full = lambda shape: pl.BlockSpec(shape, lambda i: (0,) * len(shape))
    return pl.pallas_call(
        _rw_prep_kernel,
        grid=(1,),
        in_specs=[full(a.shape) for a in (ua, shift, mu, wl, vec, seg)],
        out_specs=[full((A_W, n))] * 6 + [full((n, A_W))] * 2,
        out_shape=[jax.ShapeDtypeStruct((A_W, n), F32)] * 6 + [jax.ShapeDtypeStruct((n, A_W), F32)] * 2,
        compiler_params=_cparams("arbitrary"),
        name="rw_prep",
    )(ua, shift, mu, wl, vec, seg)


def _rw_step_kernel(r_ref, w_ref, k_ref, v_ref, a_ref, b_ref, s_ref, y_ref, so_ref):
    rt, wt, kt, at, bt = r_ref[...], w_ref[...], k_ref[...], a_ref[...], b_ref[...]
    for v in range(A_HD):
        s = s_ref[0, v]
        sa = jnp.sum(s * at, axis=0, keepdims=True)
        sn = s * wt + sa * bt + v_ref[v:v + 1, :] * kt
        so_ref[0, v] = sn
        y_ref[v:v + 1, :] = jnp.sum(sn * rt, axis=0, keepdims=True)


def _rw_step(r, w, k, v, a, b, s_t):
    bsz = r.shape[1]
    vec = pl.BlockSpec((A_HD, bsz), lambda h: (h, 0))
    st = pl.BlockSpec((1, A_HD, A_HD, bsz), lambda h: (h, 0, 0, 0))
    return pl.pallas_call(
        _rw_step_kernel,
        grid=(A_HEADS,),
        in_specs=[vec] * 6 + [st],
        out_specs=[vec, st],
        out_shape=[jax.ShapeDtypeStruct((A_W, bsz), F32), jax.ShapeDtypeStruct(s_t.shape, F32)],
        compiler_params=_cparams("parallel"),
        name="rw_step",
    )(r, w, k, v, a, b, s_t)


RW_C = 64
(F_RT, F_PM1, F_IP, F_DC, F_KH, F_KC, F_V, F_KK, F_ASIG, F_RKK, F_G, F_PC) = range(12)


def _rw_chunk_kernel(x_ref, mu_ref, wl_ref, vec_ref, ln_ref, o_ref, so_ref, carry, s_scr, f_scr):
    rt = x_ref.shape[0]
    i = pl.program_id(1)

    @pl.when(i == 0)
    def _():
        carry[...] = jnp.zeros_like(carry)
        s_scr[...] = jnp.zeros_like(s_scr)

    def prepare():
        x = x_ref[...]
        row = lax.broadcasted_iota(jnp.int32, x.shape, 0)
        prev = jnp.where(row == 0, carry[...], pltpu.roll(x, 1, axis=0))
        carry[...] = x[rt - 1:rt, :]
        xs = x + (prev - x) * mu_ref[...]
        r = xs[:, :A_W]
        k = xs[:, A_W:2 * A_W]
        lo = xs[:, 3 * A_W:]
        lane = lax.broadcasted_iota(jnp.int32, lo.shape, 1)
        act = jnp.where(lane < A_DECAY_LORA, jnp.tanh(lo),
                        jnp.where(lane < A_DECAY_LORA + A_AAA_LORA, lo, _sigmoid(lo)))
        lora = _dot(act.astype(BF16), wl_ref[...])
        w0, a0, kkw, kaw, rkw = (vec_ref[n:n + 1, :] for n in range(5))
        lw = -math.exp(-0.5) * _sigmoid(w0 + lora[:, :A_W])
        a_sig = _sigmoid(a0 + lora[:, A_W:2 * A_W])
        k2 = k * (1.0 + (a_sig - 1.0) * kaw)

        ri = lax.broadcasted_iota(jnp.int32, (rt, rt), 0)
        ci = lax.broadcasted_iota(jnp.int32, (rt, rt), 1)
        same = jnp.right_shift(ri, 6) == jnp.right_shift(ci, 6)
        chunk_ones = same.astype(BF16)
        chunk_tri = jnp.where(ci <= ri, chunk_ones, jnp.zeros_like(chunk_ones))
        lp = _dot_sel_left(chunk_tri, lw)
        lpc = _dot_sel_left(chunk_ones, lw)
        ip = jnp.exp(-lp)
        dc = jnp.exp(lpc - lp)
        f_scr[F_RT] = r * jnp.exp(lp)
        f_scr[F_PM1] = jnp.exp(lp - lw)
        f_scr[F_IP] = ip
        f_scr[F_DC] = dc
        f_scr[F_KH] = k2 * ip
        f_scr[F_KC] = k2 * dc
        f_scr[F_V] = xs[:, 2 * A_W:3 * A_W]
        f_scr[F_KK] = k * kkw
        f_scr[F_ASIG] = a_sig
        f_scr[F_RKK] = r * k2 * rkw
        f_scr[F_G] = lora[:, 2 * A_W:]
        f_scr[F_PC] = jnp.exp(lpc)

    C = RW_C
    W2 = 2 * A_HD
    npairs = A_HEADS // 2
    lane_lo = lax.broadcasted_iota(jnp.int32, (C, W2), 1) < A_HD
    own = (lax.broadcasted_iota(jnp.int32, (2 * C, W2), 1) < A_HD) == \
          (lax.broadcasted_iota(jnp.int32, (2 * C, W2), 0) < C)
    r4 = lax.broadcasted_iota(jnp.int32, (4 * C, 4 * C), 0)
    c4 = lax.broadcasted_iota(jnp.int32, (4 * C, 4 * C), 1)
    keep = jnp.bitwise_and(c4, C - 1) < jnp.bitwise_and(r4, C - 1) + jnp.right_shift(r4, 7)
    eye_f = (lax.broadcasted_iota(jnp.int32, (W2, W2), 0) ==
             lax.broadcasted_iota(jnp.int32, (W2, W2), 1)).astype(F32)
    zeros_ww = jnp.zeros((W2, W2), F32)
    bf = lambda z: z.astype(BF16)
    stack = lambda z: jnp.concatenate([jnp.where(lane_lo, z, 0.0), jnp.where(lane_lo, 0.0, z)], axis=0)
    spread = lambda col: jnp.where(lane_lo, col[:C], col[C:])
    chunks_per_body = rt // C
    probs = [(cc, p) for cc in range(chunks_per_body) for p in range(npairs)]

    def body():
        rows = [slice(cc * C, (cc + 1) * C) for cc in range(chunks_per_body)]
        ld = lambda f, q: f_scr[f, rows[q[0]], q[1] * W2:(q[1] + 1) * W2]
        a_l, r_l, bc_l, kc_l, v_l, g_l = [], [], [], [], [], []
        pc_l = [ld(F_PC, q)[0:1, :] for q in probs]
        vn_l = [ld(F_V, q) for q in probs]
        gate_l = [ld(F_G, q) for q in probs]
        kk_sq = [jnp.sum(jnp.square(stack(ld(F_KK, q))), axis=-1, keepdims=True) for q in probs]
        bonus_l = [jnp.sum(stack(ld(F_RKK, q)), axis=-1, keepdims=True) for q in probs]
        for e, q in enumerate(probs):
            kk = ld(F_KK, q) * spread(1.0 / jnp.maximum(jnp.sqrt(kk_sq[e]), 1e-12))
            b = kk * ld(F_ASIG, q)
            a_l.append(stack(-kk * ld(F_PM1, q)))
            r_l.append(stack(ld(F_RT, q)))
            bc_l.append(stack(b * ld(F_DC, q)))
            kc_l.append(stack(ld(F_KC, q)))
            v_l.append(stack(vn_l[e]))
            ar = jnp.concatenate([a_l[-1], r_l[-1]], axis=0)
            bk = jnp.concatenate([stack(b * ld(F_IP, q)), stack(ld(F_KH, q))], axis=0)
            g_l.append(jnp.where(keep, _dot_nt(bf(ar), bf(bk)), 0.0))
        n = range(len(probs))
        x_l = [g_l[e][:W2, :W2] for e in n]
        t_l = [eye_f + x_l[e] for e in n]
        x_l = [_dot(bf(x_l[e]), bf(x_l[e])) for e in n]
        m = 2
        while m < C:
            last = 2 * m >= C
            for e in n:
                rhs = t_l[e] if last else jnp.concatenate([x_l[e], t_l[e]], axis=1)
                prod = _dot(bf(x_l[e]), bf(rhs))
                if last:
                    t_l[e] = t_l[e] + prod
                else:
                    x_l[e] = prod[:, :W2]
                    t_l[e] = t_l[e] + prod[:, W2:]
            m *= 2
        lv_l = [_dot(bf(g_l[e][:W2, W2:]), bf(v_l[e])) for e in n]
        apu_l = [_dot(bf(t_l[e]), bf(jnp.concatenate([a_l[e], lv_l[e]], axis=1))) for e in n]
        qy_l = []
        for e in n:
            low = jnp.concatenate([zeros_ww, v_l[e]], axis=1)
            qy_l.append(_dot(bf(g_l[e][W2:, :]), bf(jnp.concatenate([apu_l[e], low], axis=0))))
        mm_l = [_dot_tn(bf(apu_l[e][:, :W2]), bf(bc_l[e])) for e in n]
        n_l = []
        for e in n:
            uv = jnp.concatenate([apu_l[e][:, W2:], v_l[e]], axis=0)
            n_l.append(_dot_tn(bf(uv), bf(jnp.concatenate([bc_l[e], kc_l[e]], axis=0))))
        y_l = []
        for e, q in enumerate(probs):
            p = q[1]
            s0 = s_scr[p]
            s0b = bf(s0)
            y_l.append(_dot_nt(bf(r_l[e] + qy_l[e][:, :W2]), s0b) + qy_l[e][:, W2:])
            s_scr[p] = s0 * pc_l[e] + _dot(s0b, bf(mm_l[e])) + n_l[e]
        mu_l = [jnp.sum(y_l[e], axis=-1, keepdims=True) * (1.0 / A_HD) for e in n]
        d_l = [jnp.where(own, y_l[e] - mu_l[e], 0.0) for e in n]
        var_l = [jnp.sum(d_l[e] * d_l[e], axis=-1, keepdims=True) * (1.0 / A_HD) for e in n]
        for e, q in enumerate(probs):
            p = q[1]
            d = d_l[e] * lax.rsqrt(var_l[e] + A_LN_EPS)
            yn = (d[:C] + d[C:]) * ln_ref[:, p * W2:(p + 1) * W2]
            bonus = spread(bonus_l[e]) * vn_l[e]
            o_ref[rows[q[0]], p * W2:(p + 1) * W2] = ((yn + bonus) * gate_l[e]).astype(o_ref.dtype)

    prepare()
    body()

    @pl.when(i == pl.num_programs(1) - 1)
    def _():
        for p in range(npairs):
            so_ref[0, 2 * p] = s_scr[p, :A_HD, :A_HD]
            so_ref[0, 2 * p + 1] = s_scr[p, A_HD:, A_HD:]


def _rw_chunk(ua, mu, wl, vec, ln, bsz, t_len, rt):
    nt = t_len // rt
    st_spec = pl.BlockSpec((1, A_HEADS, A_HD, A_HD), lambda b, i: (b, 0, 0, 0))
    return pl.pallas_call(
        _rw_chunk_kernel,
        grid=(bsz, nt),
        in_specs=[pl.BlockSpec((rt, A_SHIFT_W), lambda b, i: (b * nt + i, 0)),
                  _const_spec(mu.shape), _const_spec(wl.shape), _const_spec(vec.shape),
                  _const_spec(ln.shape)],
        out_specs=[pl.BlockSpec((rt, A_W), lambda b, i: (b * nt + i, 0)), st_spec],
        out_shape=[jax.ShapeDtypeStruct((bsz * t_len, A_W), BF16),
                   jax.ShapeDtypeStruct((bsz, A_HEADS, A_HD, A_HD), F32)],
        scratch_shapes=[pltpu.VMEM((1, A_SHIFT_W), F32),
                        pltpu.VMEM((A_HEADS // 2, 2 * A_HD, 2 * A_HD), F32),
                        pltpu.VMEM((12, rt, A_W), F32)],
        compiler_params=_cparams("parallel", "arbitrary"),
        name="rw_chunk",
    )(ua, mu, wl, vec, ln)


def _rw_post_kernel(y_ref, bonus_ref, g_ref, ln_ref, seg_ref, o_ref):
    seg = seg_ref[...]
    y = y_ref[...].T
    mu = _dot_sel(y, seg) * (1.0 / A_HD)
    d = y - mu
    var = _dot_sel(d * d, seg) * (1.0 / A_HD)
    yn = d * lax.rsqrt(var + A_LN_EPS) * ln_ref[...]
    o_ref[...] = ((yn + bonus_ref[...]) * g_ref[...]).astype(o_ref.dtype)


def _rw_post(y_t, bonus, g, ln, seg):
    n = bonus.shape[0]
    full = lambda shape: pl.BlockSpec(shape, lambda i: (0,) * len(shape))
    return pl.pallas_call(
        _rw_post_kernel,
        grid=(1,),
        in_specs=[full(a.shape) for a in (y_t, bonus, g, ln, seg)],
        out_specs=full((n, A_W)),
        out_shape=jax.ShapeDtypeStruct((n, A_W), BF16),
        compiler_params=_cparams("arbitrary"),
        name="rw_post",
    )(y_t, bonus, g, ln, seg)


def _head_norm_lanes(x, g, eps):
    mu = jnp.mean(x, axis=-1, keepdims=True)
    d = x - mu
    var = jnp.mean(d * d, axis=-1, keepdims=True)
    return d * lax.rsqrt(var + eps) * g


def _mlstm_chunk_kernel(ub_ref, g_ref, gt_ref, cw_ref, cb_ref, biasr_ref, biasc_ref, gn_ref,
                        y_ref, co_ref, no_ref, mo_ref,
                        tail_s, c_s, n_s, m_s):
    nb, L, _ = ub_ref.shape
    c = pl.program_id(1)

    @pl.when(c == 0)
    def _():
        tail_s[...] = jnp.zeros_like(tail_s)
        c_s[...] = jnp.zeros_like(c_s)
        n_s[...] = jnp.zeros_like(n_s)
        m_s[...] = jnp.zeros_like(m_s)

    r_i = lax.broadcasted_iota(jnp.int32, (L, L), 0)
    c_i = lax.broadcasted_iota(jnp.int32, (L, L), 1)
    causal = c_i <= r_i
    tri = causal.astype(BF16)
    tri_u = (r_i <= c_i).astype(BF16)
    heads = range(B_HEADS)
    sls = [slice(hd * B_HD, (hd + 1) * B_HD) for hd in heads]

    seqs = range(nb)
    ub_l, q_l, k_l, gc_l, gr_l, bcols_l, brows_l = [], [], [], [], [], [], []
    for bi in seqs:
        ub = ub_ref[bi]
        raw = ub[:, :2 * B_W]
        ext = jnp.concatenate([tail_s[bi], raw], axis=0)
        tail_s[bi] = raw[L - SUBLANES:]
        acc = cb_ref[...] + raw * cw_ref[B_CONV - 1:B_CONV, :]
        for jj in range(B_CONV - 1):
            sh = B_CONV - 1 - jj
            acc = acc + pltpu.roll(ext, sh, axis=0)[SUBLANES:] * cw_ref[jj:jj + 1, :]
        act = _silu(acc)
        ub_l.append(ub)
        q_l.append(act[:, :B_W])
        k_l.append(act[:, B_W:] * (B_HD ** -0.5))
        gc = g_ref[bi] + biasr_ref[...]
        gr = gt_ref[bi] + biasc_ref[...]
        gc_l.append(gc)
        gr_l.append(gr)
        bcols_l.append(_dot_sel_left(tri, _log_sigmoid(gc)))
        brows_l.append(_dot_sel(_log_sigmoid(gr), tri_u))
    m_all = [m_s[bi] for bi in seqs]
    n_all = [n_s[bi] for bi in seqs]

    probs = [(bi, hd) for bi in seqs for hd in heads]
    qb = [q_l[bi][:, sls[hd]].astype(BF16) for bi, hd in probs]
    vb = [ub_l[bi][:, 2 * B_W + hd * B_HD:2 * B_W + (hd + 1) * B_HD].astype(BF16) for bi, hd in probs]
    qk = [_dot_nt(qb[e], k_l[bi][:, sls[hd]].astype(BF16)) for e, (bi, hd) in enumerate(probs)]
    qc = [_dot(qb[e], c_s[bi * B_HEADS + hd].astype(BF16)) for e, (bi, hd) in enumerate(probs)]
    m_prev = [m_all[bi][hd:hd + 1, 0:1] for bi, hd in probs]
    n_prev = [n_all[bi][hd:hd + 1, :] for bi, hd in probs]
    bc = [_lane_col(bcols_l[bi], B_HEADS + hd) for bi, hd in probs]
    ic = [_lane_col(gc_l[bi], hd) for bi, hd in probs]
    dlog, inter, b_last, gi = [], [], [], []
    for e, (bi, hd) in enumerate(probs):
        br = brows_l[bi][B_HEADS + hd:B_HEADS + hd + 1, :]
        ir = gr_l[bi][hd:hd + 1, :]
        dlog.append(jnp.where(causal, bc[e] - br + ir, -jnp.inf))
        inter.append(bc[e] + m_prev[e])
        b_last.append(_row_pick(bc[e], L - 1))
        gi.append(b_last[e] - bc[e] + ic[e])
    row_max = [jnp.max(dlog[e], axis=1, keepdims=True) for e in range(len(probs))]
    gi_max = [jnp.max(gi[e], axis=0, keepdims=True) for e in range(len(probs))]
    s_l, iw, m_t, wc, kw, m_new = [], [], [], [], [], []
    for e, (bi, hd) in enumerate(probs):
        m_t.append(jnp.maximum(inter[e], row_max[e]))
        s_l.append(qk[e] * jnp.exp(dlog[e] - m_t[e]))
        iw.append(jnp.exp(inter[e] - m_t[e]))
        m_new.append(jnp.maximum(b_last[e] + m_prev[e], gi_max[e]))
        wc.append(jnp.exp(b_last[e] + m_prev[e] - m_new[e]))
        kw.append(k_l[bi][:, sls[hd]] * jnp.exp(gi[e] - m_new[e]))
    sv = [_dot(s_l[e].astype(BF16), vb[e]) for e in range(len(probs))]
    kv = [_dot_tn(kw[e].astype(BF16), vb[e]) for e in range(len(probs))]
    s_sum = [jnp.sum(s_l[e], axis=1, keepdims=True) for e in range(len(probs))]
    qn = [jnp.sum(q_l[bi][:, sls[hd]] * n_prev[e], axis=1, keepdims=True) for e, (bi, hd) in enumerate(probs)]
    k_sum = [jnp.sum(kw[e], axis=0, keepdims=True) for e in range(len(probs))]
    hb = []
    for e, (bi, hd) in enumerate(probs):
        den = s_sum[e] + iw[e] * qn[e]
        hh = (sv[e] + iw[e] * qc[e]) / jnp.maximum(jnp.abs(den), jnp.exp(-m_t[e]))
        hb.append(_sigmoid(ub_l[bi][:, 3 * B_W + hd * B_HD:3 * B_W + (hd + 1) * B_HD]) * hh)
        c_s[bi * B_HEADS + hd] = wc[e] * c_s[bi * B_HEADS + hd] + kv[e]
    mean = [jnp.sum(hb[e], axis=-1, keepdims=True) * (1.0 / B_HD) for e in range(len(probs))]
    dev = [hb[e] - mean[e] for e in range(len(probs))]
    var = [jnp.sum(dev[e] * dev[e], axis=-1, keepdims=True) * (1.0 / B_HD) for e in range(len(probs))]
    for e, (bi, hd) in enumerate(probs):
        y_ref[bi, :, sls[hd]] = (dev[e] * lax.rsqrt(var[e] + GN_EPS) * gn_ref[:, sls[hd]]).astype(y_ref.dtype)
    pad_rows = jnp.zeros((SUBLANES - B_HEADS, LANES), F32)
    for bi in seqs:
        es = [e for e, (bj, _) in enumerate(probs) if bj == bi]
        n_s[bi] = jnp.concatenate([wc[e] * n_prev[e] + k_sum[e] for e in es] + [pad_rows], axis=0)
        m_s[bi] = jnp.concatenate([jnp.broadcast_to(m_new[e], (1, LANES)) for e in es] + [pad_rows], axis=0)

    @pl.when(c == pl.num_programs(1) - 1)
    def _():
        for bi in range(nb):
            for hd in heads:
                co_ref[bi, hd] = c_s[bi * B_HEADS + hd]
        no_ref[...] = n_s[...]
        mo_ref[...] = m_s[...]


def _mlstm_chunk(ub, gates, gates_t, cw, cb, bias_r, bias_c, gn, bsz, t_len, nb):
    L = CHUNK
    seq = lambda w: pl.BlockSpec((nb, L, w), lambda b, c: (b, c, 0))
    per_b = lambda shp: pl.BlockSpec((nb,) + shp, lambda b, c: (b,) + (0,) * len(shp))
    return pl.pallas_call(
        _mlstm_chunk_kernel,
        grid=(bsz // nb, t_len // L),
        in_specs=[seq(4 * B_W), seq(LANES),
                  pl.BlockSpec((nb, SUBLANES, L), lambda b, c: (b, 0, c)),
                  _const_spec(cw.shape), _const_spec(cb.shape), _const_spec(bias_r.shape),
                  _const_spec(bias_c.shape), _const_spec(gn.shape)],
        out_specs=[seq(B_W), per_b((B_HEADS, B_HD, B_HD)), per_b((SUBLANES, B_HD)),
                   per_b((SUBLANES, LANES))],
        out_shape=[jax.ShapeDtypeStruct((bsz, t_len, B_W), BF16),
                   jax.ShapeDtypeStruct((bsz, B_HEADS, B_HD, B_HD), F32),
                   jax.ShapeDtypeStruct((bsz, SUBLANES, B_HD), F32),
                   jax.ShapeDtypeStruct((bsz, SUBLANES, LANES), F32)],
        scratch_shapes=[pltpu.VMEM((nb, SUBLANES, 2 * B_W), F32),
                        pltpu.VMEM((nb * B_HEADS, B_HD, B_HD), F32),
                        pltpu.VMEM((nb, SUBLANES, B_HD), F32),
                        pltpu.VMEM((nb, SUBLANES, LANES), F32)],
        compiler_params=_cparams("parallel", "arbitrary"),
        name="mlstm_chunk",
    )(ub, gates, gates_t, cw, cb, bias_r, bias_c, gn)


def _mlstm_step_kernel(ub_ref, g_ref, conv_ref, cw_ref, cb_ref, biasr_ref, gn_ref, c_ref, n_ref, m_ref,
                       y_ref, co_ref, no_ref, mo_ref):
    bb = ub_ref.shape[0]
    ub = ub_ref[...]
    raw = ub[:, :2 * B_W]
    conv = conv_ref[...]
    acc = cb_ref[...] + raw * cw_ref[B_CONV - 1:B_CONV, :]
    for jj in range(B_CONV - 1):
        acc = acc + conv[:, jj * 2 * B_W:(jj + 1) * 2 * B_W] * cw_ref[jj:jj + 1, :]
    qk = _silu(acc)
    q_all = qk[:, :B_W]
    k_all = qk[:, B_W:] * (B_HD ** -0.5)
    gc = g_ref[...] + biasr_ref[...]
    lf = _log_sigmoid(gc)
    m_all = m_ref[...]
    row_pad = lax.broadcasted_iota(jnp.int32, (LANES, LANES), 0)
    pad = jnp.zeros((LANES - bb, B_HD), F32)
    m_out = jnp.zeros((bb, LANES), F32)
    lane_m = lax.broadcasted_iota(jnp.int32, (bb, LANES), 1)
    for hd in range(B_HEADS):
        sl = slice(hd * B_HD, (hd + 1) * B_HD)
        q = q_all[:, sl]
        k = k_all[:, sl]
        v = ub[:, 2 * B_W + hd * B_HD:2 * B_W + (hd + 1) * B_HD]
        o = ub[:, 3 * B_W + hd * B_HD:3 * B_W + (hd + 1) * B_HD]
        ic = _lane_col(gc, hd)
        f = _lane_col(lf, B_HEADS + hd)
        m_prev = _lane_col(m_all, hd)
        n_prev = n_ref[:, sl]
        m_t = jnp.maximum(f + m_prev, ic)
        s = jnp.sum(q * k, axis=1, keepdims=True) * jnp.exp(ic - m_t)
        iw = jnp.exp(f + m_prev - m_t)
        wi = jnp.exp(ic - m_t)
        qb = q.astype(BF16)
        kw_t = jnp.concatenate([k * wi, pad], axis=0).T
        v_pad = jnp.concatenate([v, pad], axis=0)
        qc = jnp.zeros((bb, B_HD), F32)
        row_b = lax.broadcasted_iota(jnp.int32, (bb, B_HD), 0)
        for bi in range(bb):
            c_prev = c_ref[bi, hd]
            qc = qc + jnp.where(row_b == bi, _dot(qb, c_prev.astype(BF16)), 0.0)
            v_one = jnp.where(row_pad == bi, v_pad, 0.0)
            co_ref[bi, hd] = _row_pick(iw, bi) * c_prev + _dot3(kw_t, v_one)
        num = s * v + iw * qc
        den = s + iw * jnp.sum(q * n_prev, axis=1, keepdims=True)
        hh = num / jnp.maximum(jnp.abs(den), jnp.exp(-m_t))
        no_ref[:, sl] = iw * n_prev + wi * k
        m_out = jnp.where(lane_m == hd, m_t, m_out)
        hb = _sigmoid(o) * hh
        y_ref[:, sl] = _head_norm_lanes(hb, gn_ref[:, sl], GN_EPS).astype(y_ref.dtype)
    mo_ref[...] = m_out


def _mlstm_step(ub, gates, conv, cw, cb, bias_r, gn, c0, n0, m0, bb):
    bsz = ub.shape[0]
    row = lambda w: pl.BlockSpec((bb, w), lambda i: (i, 0))
    c_spec = pl.BlockSpec((bb, B_HEADS, B_HD, B_HD), lambda i: (i, 0, 0, 0))
    return pl.pallas_call(
        _mlstm_step_kernel,
        grid=(bsz // bb,),
        in_specs=[row(4 * B_W), row(LANES), row((B_CONV - 1) * 2 * B_W),
                  _const_spec(cw.shape), _const_spec(cb.shape), _const_spec(bias_r.shape),
                  _const_spec(gn.shape), c_spec, row(B_W), row(LANES)],
        out_specs=[row(B_W), c_spec, row(B_W), row(LANES)],
        out_shape=[jax.ShapeDtypeStruct((bsz, B_W), BF16),
                   jax.ShapeDtypeStruct(c0.shape, F32),
                   jax.ShapeDtypeStruct((bsz, B_W), F32),
                   jax.ShapeDtypeStruct((bsz, LANES), F32)],
        compiler_params=_cparams("parallel"),
        name="mlstm_step",
    )(ub, gates, conv, cw, cb, bias_r, gn, c0, n0, m0)


def _ret_log_gamma(hd):
    return math.log1p(-(2.0 ** (-5.0 - hd)))


def _ret_chunk_kernel(q_ref, qd_ref, k_ref, kd_ref, v_ref, g_ref, y_ref, so_ref, s_s, mask_s):
    nb, L, _ = q_ref.shape
    c = pl.program_id(1)

    @pl.when(c == 0)
    def _():
        s_s[...] = jnp.zeros_like(s_s)
        r_i = lax.broadcasted_iota(jnp.int32, (L, L), 0)
        c_i = lax.broadcasted_iota(jnp.int32, (L, L), 1)
        diff = (r_i - c_i).astype(F32)
        for hd in range(C_HEADS):
            mask_s[hd] = jnp.where(diff >= 0, jnp.exp(jnp.maximum(diff, 0.0) * _ret_log_gamma(hd)), 0.0)

    heads = range(C_HEADS)
    kls = [slice(hd * C_DK, (hd + 1) * C_DK) for hd in heads]
    sls = [slice(hd * C_DV, (hd + 1) * C_DV) for hd in heads]
    probs = [(bi, hd) for bi in range(nb) for hd in heads]
    n = range(len(probs))

    sc = [(_dot_nt(q_ref[bi, :, kls[hd]], k_ref[bi, :, kls[hd]]) * mask_s[hd]).astype(BF16) for bi, hd in probs]
    o_l = []
    for e, (bi, hd) in enumerate(probs):
        sl = sls[hd]
        c_dec = math.exp(L * _ret_log_gamma(hd))
        s_prev = s_s[e]
        o_l.append(_dot(sc[e], v_ref[bi, :, sl]) + _dot(qd_ref[bi, :, kls[hd]], s_prev.astype(BF16)))
        s_s[e] = c_dec * s_prev + _dot_tn(kd_ref[bi, :, kls[hd]], v_ref[bi, :, sl])
    mean = [jnp.sum(o_l[e], axis=-1, keepdims=True) * (1.0 / C_DV) for e in n]
    dev = [o_l[e] - mean[e] for e in n]
    var = [jnp.sum(dev[e] * dev[e], axis=-1, keepdims=True) * (1.0 / C_DV) for e in n]
    for e, (bi, hd) in enumerate(probs):
        sl = sls[hd]
        y_ref[bi, :, sl] = (dev[e] * lax.rsqrt(var[e] + GN_EPS) * g_ref[bi, :, sl]).astype(y_ref.dtype)

    @pl.when(c == pl.num_programs(1) - 1)
    def _():
        for e, (bi, hd) in enumerate(probs):
            so_ref[bi, hd] = s_s[e]


def _ret_chunk(q, qd, k, kd, v, g, bsz, t_len, nb):
    L = RET_CHUNK
    seq = lambda w: pl.BlockSpec((nb, L, w), lambda b, c: (b, c, 0))
    st = pl.BlockSpec((nb, C_HEADS, C_DK, C_DV), lambda b, c: (b, 0, 0, 0))
    return pl.pallas_call(
        _ret_chunk_kernel,
        grid=(bsz // nb, t_len // L),
        in_specs=[seq(C_HEADS * C_DK)] * 4 + [seq(C_OUT_W), seq(C_OUT_W)],
        out_specs=[seq(C_OUT_W), st],
        out_shape=[jax.ShapeDtypeStruct((bsz, t_len, C_OUT_W), BF16),
                   jax.ShapeDtypeStruct((bsz, C_HEADS, C_DK, C_DV), F32)],
        scratch_shapes=[pltpu.VMEM((nb * C_HEADS, C_DK, C_DV), F32), pltpu.VMEM((C_HEADS, L, L), F32)],
        compiler_params=_cparams("parallel", "arbitrary"),
        name="ret_chunk",
    )(q, qd, k, kd, v, g)


def _ret_step_kernel(q_ref, k_ref, v_ref, g_ref, s_ref, y_ref, so_ref):
    bb = q_ref.shape[0]
    hd = pl.program_id(1)
    gam = 1.0 - jnp.exp2(-5.0 - jnp.full((1, 1), hd).astype(F32))
    q = q_ref[...]
    k = k_ref[...]
    v = v_ref[...]
    s = jnp.sum(q * k, axis=1, keepdims=True)
    pad = jnp.zeros((LANES - bb, C_DK), F32)
    k_t = jnp.concatenate([k, pad], axis=0).T
    q_t = jnp.concatenate([q * gam, pad], axis=0).T
    row_b = lax.broadcasted_iota(jnp.int32, (bb, C_DV), 0)
    qs = jnp.zeros((bb, C_DV), F32)
    for bi in range(bb):
        s_prev = s_ref[bi, 0]
        qs = jnp.where(row_b == bi, jnp.sum(_lane_col(q_t, bi) * s_prev, axis=0, keepdims=True), qs)
        so_ref[bi, 0] = gam * s_prev + _lane_col(k_t, bi) * v[bi:bi + 1, :]
    o = s * v + qs
    y = _head_norm_lanes(o, g_ref[...], GN_EPS)
    y_ref[...] = y.astype(y_ref.dtype)


def _ret_step(q, k, v, g, s0, bb):
    bsz = q.shape[0]
    st = pl.BlockSpec((bb, 1, C_DK, C_DV), lambda i, h: (i, h, 0, 0))
    return pl.pallas_call(
        _ret_step_kernel,
        grid=(bsz // bb, C_HEADS),
        in_specs=[pl.BlockSpec((bb, C_DK), lambda i, h: (i, h)),
                  pl.BlockSpec((bb, C_DK), lambda i, h: (i, h)),
                  pl.BlockSpec((bb, C_DV), lambda i, h: (i, h)),
                  pl.BlockSpec((bb, C_DV), lambda i, h: (i, h)), st],
        out_specs=[pl.BlockSpec((bb, C_DV), lambda i, h: (i, h)), st],
        out_shape=[jax.ShapeDtypeStruct((bsz, C_OUT_W), BF16),
                   jax.ShapeDtypeStruct(s0.shape, F32)],
        compiler_params=_cparams("parallel", "parallel"),
        name="ret_step",
    )(q, k, v, g, s0)


def _post_kernel(*refs, n_mix):
    h_ref = refs[0]
    mix_refs = refs[1:1 + n_mix]
    wout_refs = refs[1 + n_mix:1 + 2 * n_mix]
    (p_ref, g_ref, up_ref, down_ref, gate_ref, proj_ref, o_ref, h1_s, xn_s, acc_s) = refs[1 + 2 * n_mix:]
    f = pl.program_id(1)

    tm = h_ref.shape[0]
    halves = [slice(0, tm // 2), slice(tm // 2, tm)] if tm >= 2 * LANES else [slice(0, tm)]

    @pl.when(f == 0)
    def _():
        for rows in halves:
            mix = _dot(mix_refs[0][rows, :], wout_refs[0][...])
            for m_ref, w_ref in zip(mix_refs[1:], wout_refs[1:]):
                mix = mix + _dot(m_ref[rows, :], w_ref[...])
            h1 = h_ref[rows, :] + _rms(mix, g_ref[1:2, :])
            h1_s[rows, :] = h1
            xn_s[rows, :] = _rms(h1, g_ref[2:3, :]).astype(BF16)
        acc_s[...] = jnp.zeros_like(acc_s)

    hid = jnp.square(jnp.maximum(_dot(xn_s[...], up_ref[...]), 0.0))
    acc_s[...] += _dot(hid.astype(BF16), down_ref[...])

    @pl.when(f == pl.num_programs(1) - 1)
    def _():
        for rows in halves:
            h2 = h1_s[rows, :] + _rms(acc_s[rows, :], g_ref[3:4, :])
            gate = _sigmoid(_dot(h2.astype(BF16), gate_ref[...]))
            emb = _dot(p_ref[rows, :].astype(BF16), proj_ref[...])
            o_ref[rows, :] = h2 + gate * emb


def _post(h, mixes, wouts, p, g, up, down, gate, proj, li, tm, tf):
    n = h.shape[0]
    n_mix = len(mixes)
    row = lambda w: pl.BlockSpec((tm, w), lambda i, f: (i, 0))
    layer = lambda a: pl.BlockSpec((None,) + a.shape[1:], lambda i, f: (li,) + (0,) * (a.ndim - 1),
                                   pipeline_mode=pl.Buffered(1))
    ff_mode = pl.Buffered(1) if tf == D_FF else None
    return pl.pallas_call(
        functools.partial(_post_kernel, n_mix=n_mix),
        grid=(n // tm, D_FF // tf),
        in_specs=[row(D_MODEL)] + [row(m.shape[1]) for m in mixes]
        + [_const_spec(w.shape) for w in wouts]
        + [pl.BlockSpec((None, tm, D_PLE), lambda i, f: (li, i, 0)), layer(g),
           pl.BlockSpec((None, D_MODEL, tf), lambda i, f: (li, 0, f), pipeline_mode=ff_mode),
           pl.BlockSpec((None, tf, D_MODEL), lambda i, f: (li, f, 0), pipeline_mode=ff_mode),
           layer(gate), layer(proj)],
        out_specs=row(D_MODEL),
        out_shape=jax.ShapeDtypeStruct((n, D_MODEL), F32),
        scratch_shapes=[pltpu.VMEM((tm, D_MODEL), F32), pltpu.VMEM((tm, D_MODEL), BF16),
                        pltpu.VMEM((tm, D_MODEL), F32)],
        compiler_params=_cparams("parallel", "arbitrary"),
        name="post",
    )(h, *mixes, *wouts, p, g, up, down, gate, proj)


def _prep_weights(W):
    f = {}
    w_in = W['mix0_w_in'][0]
    w_in_t = w_in.T
    f['w_a'] = w_in_t[:A_SHIFT_W].astype(BF16)
    f['w_b'] = w_in_t[A_SHIFT_W:A_SHIFT_W + 4 * B_W].astype(BF16)
    f['w_g'] = jnp.pad(w_in_t[A_SHIFT_W + 4 * B_W:], ((0, LANES - 2 * B_HEADS), (0, 0))).astype(BF16)
    wl = jnp.zeros((A_LORA_W, 3 * A_W), F32)
    wl = wl.at[:A_DECAY_LORA, :A_W].set(W['rw_w2'][0])
    wl = wl.at[A_DECAY_LORA:A_DECAY_LORA + A_AAA_LORA, A_W:2 * A_W].set(W['rw_a2'][0])
    wl = wl.at[A_DECAY_LORA + A_AAA_LORA:, 2 * A_W:].set(W['rw_g2'][0])
    f['w_lora'] = wl.astype(BF16)
    f['rw_vec'] = jnp.concatenate(
        [W['rw_w0'], W['rw_a0'], W['rw_kk'], W['rw_ka'], W['rw_rk'][0].reshape(1, A_W),
         jnp.zeros((3, A_W), F32)], axis=0)
    f['rw_mu'] = W['rw_mu']
    f['rw_ln'] = W['rw_ln']
    f['seg'] = (jnp.arange(A_W)[:, None] // A_HD == jnp.arange(A_W)[None, :] // A_HD).astype(BF16)
    f['cw'] = W['ml_conv_w'][0]
    f['cb'] = W['ml_conv_b']
    gate_bias = jnp.concatenate([W['ml_i_bias'][0], W['ml_f_bias'][0]])
    f['bias_r'] = jnp.pad(gate_bias, (0, LANES - 2 * B_HEADS)).reshape(1, LANES)
    f['bias_c'] = gate_bias.reshape(2 * B_HEADS, 1)
    f['ml_gn'] = W['ml_gn']
    w_out0 = W['mix0_w_out'][0].astype(BF16)
    f['w_out_a'] = w_out0[:A_W]
    f['w_out_b'] = w_out0[A_W:]
    w_ret = W['ret_w_in'][0]
    nq = C_HEADS * C_DK
    f['w_rq'] = w_ret[:, :nq].astype(BF16)
    f['w_rk'] = w_ret[:, nq:2 * nq].astype(BF16)
    f['w_rv'] = w_ret[:, 2 * nq:2 * nq + C_OUT_W].astype(BF16)
    f['w_rg'] = w_ret[:, 2 * nq + C_OUT_W:].astype(BF16)
    f['ret_gn'] = W['ret_gn']
    f['w_ret_out'] = W['ret_w_out'][0].astype(BF16)
    for name in ('ffn_up', 'ffn_down', 'ple_gate', 'ple_proj'):
        f[name] = W[name].astype(BF16)
    f['norm_g'] = W['norm_g']
    return f


def _rope_tables(pos):
    half = C_DK // 2
    inv = ROPE_BASE ** (-jnp.arange(half, dtype=F32) / half)
    ang = pos.astype(F32)[:, None] * inv[None, :]
    return jnp.cos(ang), jnp.sin(ang)


def _even_layer(h, p, st, f, bsz, t_len):
    n = bsz * t_len
    seq = st is None
    tm = 512 if seq else n
    ua, ub, gates = _inproj(h, f['norm_g'][0], [f['w_a'], f['w_b'], f['w_g']], [F32, F32, F32], tm)
    qk_raw = ub.reshape(bsz, t_len, 4 * B_W)[:, :, :2 * B_W]

    if seq:
        ya, s_a_new = _rw_chunk(ua, f['rw_mu'], f['w_lora'], f['rw_vec'], f['rw_ln'], bsz, t_len, rt=256)
        conv_new = qk_raw[:, -(B_CONV - 1):]
    else:
        shift, s_a, conv, c0, n0, m0 = st
        conv_new = jnp.concatenate([conv, qk_raw], axis=1)[:, -(B_CONV - 1):]
        r, w, k, v, a, b, g, bonus = _rw_prep(ua, shift, f['rw_mu'], f['w_lora'], f['rw_vec'], f['seg'])
        y_t, s_t = _rw_step(r, w, k, v, a, b, jnp.transpose(s_a, (1, 2, 3, 0)))
        s_a_new = jnp.transpose(s_t, (3, 0, 1, 2))
        ya = _rw_post(y_t, bonus, g, f['rw_ln'], f['seg'])

    if seq:
        gates3 = gates.reshape(bsz, t_len, LANES)
        gates_t = gates3[:, :, :SUBLANES].transpose(0, 2, 1)
        yb, c_new, n_new, m_new = _mlstm_chunk(ub.reshape(bsz, t_len, 4 * B_W), gates3, gates_t, f['cw'],
                                               f['cb'], f['bias_r'], f['bias_c'], f['ml_gn'], bsz, t_len,
                                               nb=4)
        yb = yb.reshape(n, B_W)
        n_new = n_new[:, :B_HEADS]
        m_new = m_new[:, :B_HEADS, 0]
    else:
        m0p = jnp.pad(m0, ((0, 0), (0, LANES - B_HEADS)))
        yb, c_new, n_new, m_new = _mlstm_step(ub, gates, conv.reshape(bsz, -1), f['cw'], f['cb'],
                                              f['bias_r'], f['ml_gn'], c0, n0.reshape(bsz, B_W), m0p,
                                              bb=SUBLANES)
        n_new = n_new.reshape(bsz, B_HEADS, B_HD)
        m_new = m_new[:, :B_HEADS]

    h_next = _post(h, [ya, yb], [f['w_out_a'], f['w_out_b']], p, f['norm_g'], f['ffn_up'],
                   f['ffn_down'], f['ple_gate'], f['ple_proj'], li=0, tm=tm, tf=POST_TF)

    shift_new = ua.reshape(bsz, t_len, A_SHIFT_W)[:, -1]
    return h_next, (shift_new, s_a_new, conv_new, c_new, n_new, m_new)


def _odd_layer(h, p, s0, pos, f, bsz, t_len):
    n = bsz * t_len
    seq = s0 is None
    cos, sin = _rope_tables(pos)
    tm = 512 if seq else n
    ws = [f['w_rq'], f['w_rk'], f['w_rv'], f['w_rg']]
    if seq:
        q, qd, k, kd, v, g = _inproj_ret(h, f['norm_g'][1], f['ret_gn'], cos, sin, ws, tm, t_len // tm,
                                         [BF16] * 5 + [F32], decayed=True)
        as3 = lambda a: a.reshape(bsz, t_len, a.shape[-1])
        y, s_new = _ret_chunk(as3(q), as3(qd), as3(k), as3(kd), as3(v), as3(g), bsz, t_len, nb=2)
        y = y.reshape(n, C_OUT_W)
    else:
        q, k, v, g = _inproj_ret(h, f['norm_g'][1], f['ret_gn'], cos, sin, ws, tm, 1, [F32] * 4,
                                 decayed=False)
        y, s_new = _ret_step(q, k, v, g, s0, bb=2 * SUBLANES)
    h_next = _post(h, [y], [f['w_ret_out']], p, f['norm_g'], f['ffn_up'], f['ffn_down'],
                   f['ple_gate'], f['ple_proj'], li=1, tm=512 if seq else n, tf=POST_TF)
    return h_next, s_new


def _trunk(x, p, pos, ev_states, od_state, f):
    bsz, t_len, _ = x.shape
    n = bsz * t_len
    h = x.reshape(n, D_MODEL)
    ev_in = None if ev_states is None else tuple(s[0] for s in ev_states)
    od_in = None if od_state is None else od_state[0]
    p = p.reshape(p.shape[0], n, D_PLE)
    h, ev_new = _even_layer(h, p, ev_in, f, bsz, t_len)
    h, od_new = _odd_layer(h, p, od_in, pos, f, bsz, t_len)
    return h.reshape(bsz, t_len, D_MODEL), [s[None] for s in ev_new], od_new[None]


def kernel(x_prompt, x_sample, state_rwkv_shift, state_rwkv_S, state_mlstm_conv, state_mlstm_C,
           state_mlstm_n, state_mlstm_m, state_ret_S, p_prompt, p_sample, norm_g, ffn_up, ffn_down,
           ple_gate, ple_proj, mix0_w_in, rw_mu, rw_w0, rw_w2, rw_a0, rw_a2, rw_g2, rw_kk, rw_ka,
           rw_rk, rw_ln, ml_conv_w, ml_conv_b, ml_i_bias, ml_f_bias, ml_gn, mix0_w_out,
           ret_w_in, ret_gn, ret_w_out):
    W = dict(norm_g=norm_g, ffn_up=ffn_up, ffn_down=ffn_down, ple_gate=ple_gate, ple_proj=ple_proj,
             mix0_w_in=mix0_w_in, rw_mu=rw_mu, rw_w0=rw_w0, rw_w2=rw_w2, rw_a0=rw_a0, rw_a2=rw_a2,
             rw_g2=rw_g2, rw_kk=rw_kk, rw_ka=rw_ka, rw_rk=rw_rk, rw_ln=rw_ln, ml_conv_w=ml_conv_w,
             ml_conv_b=ml_conv_b, ml_i_bias=ml_i_bias, ml_f_bias=ml_f_bias, ml_gn=ml_gn,
             mix0_w_out=mix0_w_out, ret_w_in=ret_w_in, ret_gn=ret_gn, ret_w_out=ret_w_out)
    f = _prep_weights(W)
    bp, tp = x_prompt.shape[:2]
    ts = x_sample.shape[1]
    y_prompt, ev_p, ret_p = _trunk(x_prompt, p_prompt, jnp.arange(tp), None, None, f)
    ev_s_in = (state_rwkv_shift, state_rwkv_S, state_mlstm_conv, state_mlstm_C, state_mlstm_n,
               state_mlstm_m)
    y_sample, ev_s, ret_s = _trunk(x_sample, p_sample, PAST_LEN + jnp.arange(ts), ev_s_in, state_ret_S, f)
    return (y_prompt, y_sample, *ev_p, ret_p, *ev_s, ret_s)
```

```python
import functools
import math

import jax
import jax.numpy as jnp
from jax import lax
from jax.experimental import pallas as pl
from jax.experimental.pallas import tpu as pltpu

F32 = jnp.float32
BF16 = jnp.bfloat16

D_MODEL = 1024
D_PLE = 256
D_FF = 4 * D_MODEL
RMS_EPS = 1e-6
GN_EPS = 1e-5
CHUNK = 128
RET_CHUNK = 256

A_HEADS = 8
A_HD = 64
A_W = A_HEADS * A_HD
A_DECAY_LORA = 64
A_AAA_LORA = 64
A_GATE_LORA = 128
A_LORA_W = A_DECAY_LORA + A_AAA_LORA + A_GATE_LORA
A_SHIFT_W = 3 * A_W + A_LORA_W
A_LN_EPS = 64e-5

B_HEADS = 4
B_HD = 128
B_W = B_HEADS * B_HD
B_CONV = 4

C_HEADS = 4
C_DK = 256
C_DV = 512
C_OUT_W = C_HEADS * C_DV
ROPE_BASE = 10000.0
PAST_LEN = 16384

LANES = 128
SUBLANES = 8
VMEM_LIMIT = 56 * 1024 * 1024
POST_TF = 4096


def _cparams(*sem):
    return pltpu.CompilerParams(dimension_semantics=sem, vmem_limit_bytes=VMEM_LIMIT)


def _const_spec(shape):
    nd = len(shape)
    return pl.BlockSpec(shape, lambda *_: (0,) * nd, pipeline_mode=pl.Buffered(1))


def _rms(x, g):
    return x * lax.rsqrt(jnp.mean(x * x, axis=-1, keepdims=True) + RMS_EPS) * g


def _dot(a, b):
    return jnp.dot(a, b, preferred_element_type=F32)


def _dot_nt(a, b):
    return lax.dot_general(a, b, (((1,), (1,)), ((), ())), preferred_element_type=F32)


def _dot_tn(a, b):
    return lax.dot_general(a, b, (((0,), (0,)), ((), ())), preferred_element_type=F32)


def _hilo(x):
    hi = x.astype(BF16)
    lo = (x - hi.astype(F32)).astype(BF16)
    return hi, lo


def _dot_sel(x, sel):
    hi, lo = _hilo(x)
    return _dot(hi, sel) + _dot(lo, sel)


def _dot_sel_left(sel, x):
    hi, lo = _hilo(x)
    return _dot(sel, hi) + _dot(sel, lo)


def _dot3(a, b):
    ah, al = _hilo(a)
    bh, bl = _hilo(b)
    return _dot(ah, bh) + _dot(ah, bl) + _dot(al, bh)


def _sigmoid(x):
    return 1.0 / (1.0 + jnp.exp(-x))


def _softplus(x):
    return jnp.maximum(x, 0.0) + jnp.log1p(jnp.exp(-jnp.abs(x)))


def _log_sigmoid(x):
    return -_softplus(-x)


def _silu(x):
    return x * _sigmoid(x)


def _lane_col(x, idx):
    lane = lax.broadcasted_iota(jnp.int32, x.shape, 1)
    return jnp.sum(jnp.where(lane == idx, x, 0.0), axis=1, keepdims=True)


def _row_pick(x, idx):
    row = lax.broadcasted_iota(jnp.int32, x.shape, 0)
    return jnp.sum(jnp.where(row == idx, x, 0.0), axis=0, keepdims=True)


def _inproj_kernel(x_ref, g_ref, *refs):
    n = len(refs) // 2
    xn = _rms(x_ref[...], g_ref[0:1, :]).astype(BF16)
    for w_ref, o_ref in zip(refs[:n], refs[n:]):
        o_ref[...] = _dot_nt(xn, w_ref[...]).astype(o_ref.dtype)


def _inproj(h, g, ws, dtypes, tm):
    n = h.shape[0]
    return pl.pallas_call(
        _inproj_kernel,
        grid=(n // tm,),
        in_specs=[pl.BlockSpec((tm, D_MODEL), lambda i: (i, 0)), _const_spec(g.shape)]
        + [_const_spec(w.shape) for w in ws],
        out_specs=[pl.BlockSpec((tm, w.shape[0]), lambda i: (i, 0)) for w in ws],
        out_shape=[jax.ShapeDtypeStruct((n, w.shape[0]), dt) for w, dt in zip(ws, dtypes)],
        compiler_params=_cparams("parallel"),
        name="inproj",
    )(h, g, *ws)


def _inproj_ret_kernel(x_ref, g_ref, gn_ref, cos_ref, sin_ref, wq_ref, wk_ref, wv_ref, wg_ref, *out_refs,
                       decayed):
    if decayed:
        q_ref, qd_ref, k_ref, kd_ref, v_ref, gate_ref = out_refs
    else:
        q_ref, k_ref, v_ref, gate_ref = out_refs
        qd_ref = kd_ref = None
    xn = _rms(x_ref[...], g_ref[0:1, :]).astype(BF16)
    cos = cos_ref[...]
    sin = sin_ref[...]
    half = C_DK // 2
    tm = x_ref.shape[0]
    row0 = pl.program_id(0) * tm
    pos = jnp.bitwise_and(lax.broadcasted_iota(jnp.int32, (tm, half), 0) + row0, RET_CHUNK - 1).astype(F32)
    gate_ref[...] = (_silu(_dot(xn, wg_ref[...])) * gn_ref[...]).astype(gate_ref.dtype)
    for w_ref, o_ref, d_ref, scale in ((wq_ref, q_ref, qd_ref, None), (wk_ref, k_ref, kd_ref, C_DK ** -0.5)):
        u = _dot(xn, w_ref[...])
        for hd in range(C_HEADS):
            x1 = u[:, hd * C_DK:hd * C_DK + half]
            x2 = u[:, hd * C_DK + half:(hd + 1) * C_DK]
            y1 = x1 * cos - x2 * sin
            y2 = x1 * sin + x2 * cos
            if scale is not None:
                y1 = y1 * scale
                y2 = y2 * scale
            o_ref[:, hd * C_DK:hd * C_DK + half] = y1.astype(o_ref.dtype)
            o_ref[:, hd * C_DK + half:(hd + 1) * C_DK] = y2.astype(o_ref.dtype)
            if decayed:
                lg = _ret_log_gamma(hd)
                dec = jnp.exp((pos + 1.0) * lg) if d_ref is qd_ref else jnp.exp((RET_CHUNK - 1.0 - pos) * lg)
                d_ref[:, hd * C_DK:hd * C_DK + half] = (y1 * dec).astype(d_ref.dtype)
                d_ref[:, hd * C_DK + half:(hd + 1) * C_DK] = (y2 * dec).astype(d_ref.dtype)
    v_ref[...] = _dot(xn, wv_ref[...]).astype(v_ref.dtype)


def _inproj_ret(h, g, gn, cos, sin, ws, tm, seq_tiles, dtypes, decayed):
    n = h.shape[0]
    if cos.shape[0] == 1:
        cs_spec = _const_spec(cos.shape)
    else:
        cs_spec = pl.BlockSpec((tm, cos.shape[1]), lambda i: (i % seq_tiles, 0))
    widths = [w.shape[1] for w in ws]
    if decayed:
        widths = [widths[0], widths[0], widths[1], widths[1], widths[2], widths[3]]
    return pl.pallas_call(
        functools.partial(_inproj_ret_kernel, decayed=decayed),
        grid=(n // tm,),
        in_specs=[pl.BlockSpec((tm, D_MODEL), lambda i: (i, 0)), _const_spec(g.shape), _const_spec(gn.shape),
                  cs_spec, cs_spec] + [_const_spec(w.shape) for w in ws],
        out_specs=[pl.BlockSpec((tm, wd), lambda i: (i, 0)) for wd in widths],
        out_shape=[jax.ShapeDtypeStruct((n, wd), dt) for wd, dt in zip(widths, dtypes)],
        compiler_params=_cparams("parallel"),
        name="inproj_ret",
    )(h, g, gn, cos, sin, *ws)


def _rw_prep_kernel(x_ref, sh_ref, mu_ref, wl_ref, vec_ref, seg_ref,
                    r_ref, w_ref, k_ref, v_ref, a_ref, b_ref, g_ref, bonus_ref):
    x = x_ref[...]
    xs = x + (sh_ref[...] - x) * mu_ref[...]
    r = xs[:, :A_W]
    k = xs[:, A_W:2 * A_W]
    v = xs[:, 2 * A_W:3 * A_W]
    lo = xs[:, 3 * A_W:]
    lane = lax.broadcasted_iota(jnp.int32, lo.shape, 1)
    act = jnp.where(lane < A_DECAY_LORA, jnp.tanh(lo),
                    jnp.where(lane < A_DECAY_LORA + A_AAA_LORA, lo, _sigmoid(lo)))
    lora = _dot(act.astype(BF16), wl_ref[...])
    w0, a0, kkw, kaw, rkw = (vec_ref[i:i + 1, :] for i in range(5))
    w_log = -_softplus(-(w0 + lora[:, :A_W])) - 0.5
    decay = jnp.exp(-jnp.exp(w_log))
    a = _sigmoid(a0 + lora[:, A_W:2 * A_W])
    g = lora[:, 2 * A_W:]
    seg = seg_ref[...]
    kk = k * kkw
    kk = kk / jnp.maximum(jnp.sqrt(_dot_sel(kk * kk, seg)), 1e-12)
    k2 = k * (1.0 + (a - 1.0) * kaw)
    bonus = _dot_sel(r * k2 * rkw, seg) * v
    r_ref[...] = r.T
    w_ref[...] = decay.T
    k_ref[...] = k2.T
    v_ref[...] = v.T
    a_ref[...] = (-kk).T
    b_ref[...] = (kk * a).T
    g_ref[...] = g
    bonus_ref[...] = bonus


def _rw_prep(ua, shift, mu, wl, vec, seg):
    n = ua.shape[0]
    full = lambda shape: pl.BlockSpec(shape, lambda i: (0,) * len(shape))
    return pl.pallas_call(
        _rw_prep_kernel,
        grid=(1,),
        in_specs=[full(a.shape) for a in (ua, shift, mu, wl, vec, seg)],
        out_specs=[full((A_W, n))] * 6 + [full((n, A_W))] * 2,
        out_shape=[jax.ShapeDtypeStruct((A_W, n), F32)] * 6 + [jax.ShapeDtypeStruct((n, A_W), F32)] * 2,
        compiler_params=_cparams("arbitrary"),
        name="rw_prep",
    )(ua, shift, mu, wl, vec, seg)


def _rw_step_kernel(r_ref, w_ref, k_ref, v_ref, a_ref, b_ref, s_ref, y_ref, so_ref):
    rt, wt, kt, at, bt = r_ref[...], w_ref[...], k_ref[...], a_ref[...], b_ref[...]
    for v in range(A_HD):
        s = s_ref[0, v]
        sa = jnp.sum(s * at, axis=0, keepdims=True)
        sn = s * wt + sa * bt + v_ref[v:v + 1, :] * kt
        so_ref[0, v] = sn
        y_ref[v:v + 1, :] = jnp.sum(sn * rt, axis=0, keepdims=True)


def _rw_step(r, w, k, v, a, b, s_t):
    bsz = r.shape[1]
    vec = pl.BlockSpec((A_HD, bsz), lambda h: (h, 0))
    st = pl.BlockSpec((1, A_HD, A_HD, bsz), lambda h: (h, 0, 0, 0))
    return pl.pallas_call(
        _rw_step_kernel,
        grid=(A_HEADS,),
        in_specs=[vec] * 6 + [st],
        out_specs=[vec, st],
        out_shape=[jax.ShapeDtypeStruct((A_W, bsz), F32), jax.ShapeDtypeStruct(s_t.shape, F32)],
        compiler_params=_cparams("parallel"),
        name="rw_step",
    )(r, w, k, v, a, b, s_t)


RW_C = 64
(F_RT, F_PM1, F_IP, F_DC, F_KH, F_KC, F_V, F_KK, F_ASIG, F_RKK, F_G, F_PC) = range(12)


def _rw_chunk_kernel(x_ref, mu_ref, wl_ref, vec_ref, ln_ref, o_ref, so_ref, carry, s_scr, f_scr):
    rt = x_ref.shape[0]
    i = pl.program_id(1)

    @pl.when(i == 0)
    def _():
        carry[...] = jnp.zeros_like(carry)
        s_scr[...] = jnp.zeros_like(s_scr)

    def prepare():
        x = x_ref[...]
        row = lax.broadcasted_iota(jnp.int32, x.shape, 0)
        prev = jnp.where(row == 0, carry[...], pltpu.roll(x, 1, axis=0))
        carry[...] = x[rt - 1:rt, :]
        xs = x + (prev - x) * mu_ref[...]
        r = xs[:, :A_W]
        k = xs[:, A_W:2 * A_W]
        lo = xs[:, 3 * A_W:]
        lane = lax.broadcasted_iota(jnp.int32, lo.shape, 1)
        act = jnp.where(lane < A_DECAY_LORA, jnp.tanh(lo),
                        jnp.where(lane < A_DECAY_LORA + A_AAA_LORA, lo, _sigmoid(lo)))
        lora = _dot(act.astype(BF16), wl_ref[...])
        w0, a0, kkw, kaw, rkw = (vec_ref[n:n + 1, :] for n in range(5))
        lw = -math.exp(-0.5) * _sigmoid(w0 + lora[:, :A_W])
        a_sig = _sigmoid(a0 + lora[:, A_W:2 * A_W])
        k2 = k * (1.0 + (a_sig - 1.0) * kaw)

        ri = lax.broadcasted_iota(jnp.int32, (rt, rt), 0)
        ci = lax.broadcasted_iota(jnp.int32, (rt, rt), 1)
        same = jnp.right_shift(ri, 6) == jnp.right_shift(ci, 6)
        chunk_ones = same.astype(BF16)
        chunk_tri = jnp.where(ci <= ri, chunk_ones, jnp.zeros_like(chunk_ones))
        lp = _dot_sel_left(chunk_tri, lw)
        lpc = _dot_sel_left(chunk_ones, lw)
        ip = jnp.exp(-lp)
        dc = jnp.exp(lpc - lp)
        f_scr[F_RT] = r * jnp.exp(lp)
        f_scr[F_PM1] = jnp.exp(lp - lw)
        f_scr[F_IP] = ip
        f_scr[F_DC] = dc
        f_scr[F_KH] = k2 * ip
        f_scr[F_KC] = k2 * dc
        f_scr[F_V] = xs[:, 2 * A_W:3 * A_W]
        f_scr[F_KK] = k * kkw
        f_scr[F_ASIG] = a_sig
        f_scr[F_RKK] = r * k2 * rkw
        f_scr[F_G] = lora[:, 2 * A_W:]
        f_scr[F_PC] = jnp.exp(lpc)

    C = RW_C
    W2 = 2 * A_HD
    npairs = A_HEADS // 2
    lane_lo = lax.broadcasted_iota(jnp.int32, (C, W2), 1) < A_HD
    own = (lax.broadcasted_iota(jnp.int32, (2 * C, W2), 1) < A_HD) == \
          (lax.broadcasted_iota(jnp.int32, (2 * C, W2), 0) < C)
    r4 = lax.broadcasted_iota(jnp.int32, (4 * C, 4 * C), 0)
    c4 = lax.broadcasted_iota(jnp.int32, (4 * C, 4 * C), 1)
    keep = jnp.bitwise_and(c4, C - 1) < jnp.bitwise_and(r4, C - 1) + jnp.right_shift(r4, 7)
    eye_f = (lax.broadcasted_iota(jnp.int32, (W2, W2), 0) ==
             lax.broadcasted_iota(jnp.int32, (W2, W2), 1)).astype(F32)
    zeros_ww = jnp.zeros((W2, W2), F32)
    rw_i = lax.broadcasted_iota(jnp.int32, (W2, W2), 0)
    cw_i = lax.broadcasted_iota(jnp.int32, (W2, W2), 1)
    same_blk = {b: jnp.right_shift(rw_i, b) == jnp.right_shift(cw_i, b) for b in (3, 4, 5, 6)}
    bf = lambda z: z.astype(BF16)
    stack = lambda z: jnp.concatenate([jnp.where(lane_lo, z, 0.0), jnp.where(lane_lo, 0.0, z)], axis=0)
    spread = lambda col: jnp.where(lane_lo, col[:C], col[C:])
    chunks_per_body = rt // C
    probs = [(cc, p) for cc in range(chunks_per_body) for p in range(npairs)]

    def body():
        rows = [slice(cc * C, (cc + 1) * C) for cc in range(chunks_per_body)]
        ld = lambda f, q: f_scr[f, rows[q[0]], q[1] * W2:(q[1] + 1) * W2]
        a_l, r_l, bc_l, kc_l, v_l, g_l = [], [], [], [], [], []
        pc_l = [ld(F_PC, q)[0:1, :] for q in probs]
        vn_l = [ld(F_V, q) for q in probs]
        gate_l = [ld(F_G, q) for q in probs]
        kk_sq = [jnp.sum(jnp.square(stack(ld(F_KK, q))), axis=-1, keepdims=True) for q in probs]
        bonus_l = [jnp.sum(stack(ld(F_RKK, q)), axis=-1, keepdims=True) for q in probs]
        for e, q in enumerate(probs):
            kk = ld(F_KK, q) * spread(1.0 / jnp.maximum(jnp.sqrt(kk_sq[e]), 1e-12))
            b = kk * ld(F_ASIG, q)
            a_l.append(stack(-kk * ld(F_PM1, q)))
            r_l.append(stack(ld(F_RT, q)))
            bc_l.append(stack(b * ld(F_DC, q)))
            kc_l.append(stack(ld(F_KC, q)))
            v_l.append(stack(vn_l[e]))
            ar = jnp.concatenate([a_l[-1], r_l[-1]], axis=0)
            bk = jnp.concatenate([stack(b * ld(F_IP, q)), stack(ld(F_KH, q))], axis=0)
            g_l.append(jnp.where(keep, _dot_nt(bf(ar), bf(bk)), 0.0))
        n = range(len(probs))
        xs_l = [g_l[e][:W2, :W2] for e in n]
        d_l = [jnp.where(same_blk[3], xs_l[e], 0.0) for e in n]
        t_l = [eye_f + d_l[e] for e in n]
        d_l = [_dot(bf(d_l[e]), bf(d_l[e])) for e in n]
        for e in n:
            prod = _dot(bf(d_l[e]), bf(jnp.concatenate([d_l[e], t_l[e]], axis=1)))
            d_l[e] = prod[:, :W2]
            t_l[e] = t_l[e] + prod[:, W2:]
        t_l = [t_l[e] + _dot(bf(d_l[e]), bf(t_l[e])) for e in n]
        for bits in (3, 4, 5):
            off_l = [jnp.where(same_blk[bits + 1], jnp.where(same_blk[bits], 0.0, xs_l[e]), 0.0) for e in n]
            ot_l = [_dot(bf(off_l[e]), bf(t_l[e])) for e in n]
            t_l = [t_l[e] + _dot(bf(t_l[e]), bf(ot_l[e])) for e in n]
        lv_l = [_dot(bf(g_l[e][:W2, W2:]), bf(v_l[e])) for e in n]
        apu_l = [_dot(bf(t_l[e]), bf(jnp.concatenate([a_l[e], lv_l[e]], axis=1))) for e in n]
        qy_l = []
        for e in n:
            low = jnp.concatenate([zeros_ww, v_l[e]], axis=1)
            qy_l.append(_dot(bf(g_l[e][W2:, :]), bf(jnp.concatenate([apu_l[e], low], axis=0))))
        mm_l = [_dot_tn(bf(apu_l[e][:, :W2]), bf(bc_l[e])) for e in n]
        n_l = []
        for e in n:
            uv = jnp.concatenate([apu_l[e][:, W2:], v_l[e]], axis=0)
            n_l.append(_dot_tn(bf(uv), bf(jnp.concatenate([bc_l[e], kc_l[e]], axis=0))))
        y_l = []
        for e, q in enumerate(probs):
            p = q[1]
            s0 = s_scr[p]
            s0b = bf(s0)
            y_l.append(_dot_nt(bf(r_l[e] + qy_l[e][:, :W2]), s0b) + qy_l[e][:, W2:])
            s_scr[p] = s0 * pc_l[e] + _dot(s0b, bf(mm_l[e])) + n_l[e]
        mu_l = [jnp.sum(y_l[e], axis=-1, keepdims=True) * (1.0 / A_HD) for e in n]
        d_l = [jnp.where(own, y_l[e] - mu_l[e], 0.0) for e in n]
        var_l = [jnp.sum(d_l[e] * d_l[e], axis=-1, keepdims=True) * (1.0 / A_HD) for e in n]
        for e, q in enumerate(probs):
            p = q[1]
            d = d_l[e] * lax.rsqrt(var_l[e] + A_LN_EPS)
            yn = (d[:C] + d[C:]) * ln_ref[:, p * W2:(p + 1) * W2]
            bonus = spread(bonus_l[e]) * vn_l[e]
            o_ref[rows[q[0]], p * W2:(p + 1) * W2] = ((yn + bonus) * gate_l[e]).astype(o_ref.dtype)

    prepare()
    body()

    @pl.when(i == pl.num_programs(1) - 1)
    def _():
        for p in range(npairs):
            so_ref[0, 2 * p] = s_scr[p, :A_HD, :A_HD]
            so_ref[0, 2 * p + 1] = s_scr[p, A_HD:, A_HD:]


def _rw_chunk(ua, mu, wl, vec, ln, bsz, t_len, rt):
    nt = t_len // rt
    st_spec = pl.BlockSpec((1, A_HEADS, A_HD, A_HD), lambda b, i: (b, 0, 0, 0))
    return pl.pallas_call(
        _rw_chunk_kernel,
        grid=(bsz, nt),
        in_specs=[pl.BlockSpec((rt, A_SHIFT_W), lambda b, i: (b * nt + i, 0)),
                  _const_spec(mu.shape), _const_spec(wl.shape), _const_spec(vec.shape),
                  _const_spec(ln.shape)],
        out_specs=[pl.BlockSpec((rt, A_W), lambda b, i: (b * nt + i, 0)), st_spec],
        out_shape=[jax.ShapeDtypeStruct((bsz * t_len, A_W), BF16),
                   jax.ShapeDtypeStruct((bsz, A_HEADS, A_HD, A_HD), F32)],
        scratch_shapes=[pltpu.VMEM((1, A_SHIFT_W), F32),
                        pltpu.VMEM((A_HEADS // 2, 2 * A_HD, 2 * A_HD), F32),
                        pltpu.VMEM((12, rt, A_W), F32)],
        compiler_params=_cparams("parallel", "arbitrary"),
        name="rw_chunk",
    )(ua, mu, wl, vec, ln)


def _rw_post_kernel(y_ref, bonus_ref, g_ref, ln_ref, seg_ref, o_ref):
    seg = seg_ref[...]
    y = y_ref[...].T
    mu = _dot_sel(y, seg) * (1.0 / A_HD)
    d = y - mu
    var = _dot_sel(d * d, seg) * (1.0 / A_HD)
    yn = d * lax.rsqrt(var + A_LN_EPS) * ln_ref[...]
    o_ref[...] = ((yn + bonus_ref[...]) * g_ref[...]).astype(o_ref.dtype)


def _rw_post(y_t, bonus, g, ln, seg):
    n = bonus.shape[0]
    full = lambda shape: pl.BlockSpec(shape, lambda i: (0,) * len(shape))
    return pl.pallas_call(
        _rw_post_kernel,
        grid=(1,),
        in_specs=[full(a.shape) for a in (y_t, bonus, g, ln, seg)],
        out_specs=full((n, A_W)),
        out_shape=jax.ShapeDtypeStruct((n, A_W), BF16),
        compiler_params=_cparams("arbitrary"),
        name="rw_post",
    )(y_t, bonus, g, ln, seg)


def _head_norm_lanes(x, g, eps):
    mu = jnp.mean(x, axis=-1, keepdims=True)
    d = x - mu
    var = jnp.mean(d * d, axis=-1, keepdims=True)
    return d * lax.rsqrt(var + eps) * g


def _mlstm_chunk_kernel(ub_ref, g_ref, gt_ref, cw_ref, cb_ref, biasr_ref, biasc_ref, gn_ref,
                        y_ref, co_ref, no_ref, mo_ref,
                        tail_s, c_s, n_s, m_s):
    nb, L, _ = ub_ref.shape
    c = pl.program_id(1)

    @pl.when(c == 0)
    def _():
        tail_s[...] = jnp.zeros_like(tail_s)
        c_s[...] = jnp.zeros_like(c_s)
        n_s[...] = jnp.zeros_like(n_s)
        m_s[...] = jnp.zeros_like(m_s)

    r_i = lax.broadcasted_iota(jnp.int32, (L, L), 0)
    c_i = lax.broadcasted_iota(jnp.int32, (L, L), 1)
    causal = c_i <= r_i
    tri = causal.astype(BF16)
    tri_u = (r_i <= c_i).astype(BF16)
    heads = range(B_HEADS)
    sls = [slice(hd * B_HD, (hd + 1) * B_HD) for hd in heads]

    seqs = range(nb)
    ub_l, q_l, k_l, gc_l, gr_l, bcols_l, brows_l = [], [], [], [], [], [], []
    for bi in seqs:
        ub = ub_ref[bi]
        raw = ub[:, :2 * B_W]
        ext = jnp.concatenate([tail_s[bi], raw], axis=0)
        tail_s[bi] = raw[L - SUBLANES:]
        acc = cb_ref[...] + raw * cw_ref[B_CONV - 1:B_CONV, :]
        for jj in range(B_CONV - 1):
            sh = B_CONV - 1 - jj
            acc = acc + pltpu.roll(ext, sh, axis=0)[SUBLANES:] * cw_ref[jj:jj + 1, :]
        act = _silu(acc)
        ub_l.append(ub)
        q_l.append(act[:, :B_W])
        k_l.append(act[:, B_W:] * (B_HD ** -0.5))
        gc = g_ref[bi] + biasr_ref[...]
        gr = gt_ref[bi] + biasc_ref[...]
        gc_l.append(gc)
        gr_l.append(gr)
        bcols_l.append(_dot_sel_left(tri, _log_sigmoid(gc)))
        brows_l.append(_dot_sel(_log_sigmoid(gr), tri_u))
    m_all = [m_s[bi] for bi in seqs]
    n_all = [n_s[bi] for bi in seqs]

    probs = [(bi, hd) for bi in seqs for hd in heads]
    qb = [q_l[bi][:, sls[hd]].astype(BF16) for bi, hd in probs]
    vb = [ub_l[bi][:, 2 * B_W + hd * B_HD:2 * B_W + (hd + 1) * B_HD].astype(BF16) for bi, hd in probs]
    qk = [_dot_nt(qb[e], k_l[bi][:, sls[hd]].astype(BF16)) for e, (bi, hd) in enumerate(probs)]
    qc = [_dot(qb[e], c_s[bi * B_HEADS + hd].astype(BF16)) for e, (bi, hd) in enumerate(probs)]
    m_prev = [m_all[bi][hd:hd + 1, 0:1] for bi, hd in probs]
    n_prev = [n_all[bi][hd:hd + 1, :] for bi, hd in probs]
    bc = [_lane_col(bcols_l[bi], B_HEADS + hd) for bi, hd in probs]
    ic = [_lane_col(gc_l[bi], hd) for bi, hd in probs]
    dlog, inter, b_last, gi = [], [], [], []
    for e, (bi, hd) in enumerate(probs):
        br = brows_l[bi][B_HEADS + hd:B_HEADS + hd + 1, :]
        ir = gr_l[bi][hd:hd + 1, :]
        dlog.append(jnp.where(causal, bc[e] - br + ir, -jnp.inf))
        inter.append(bc[e] + m_prev[e])
        b_last.append(_row_pick(bc[e], L - 1))
        gi.append(b_last[e] - bc[e] + ic[e])
    row_max = [jnp.max(dlog[e], axis=1, keepdims=True) for e in range(len(probs))]
    gi_max = [jnp.max(gi[e], axis=0, keepdims=True) for e in range(len(probs))]
    s_l, iw, m_t, wc, kw, m_new = [], [], [], [], [], []
    for e, (bi, hd) in enumerate(probs):
        m_t.append(jnp.maximum(inter[e], row_max[e]))
        s_l.append(qk[e] * jnp.exp(dlog[e] - m_t[e]))
        iw.append(jnp.exp(inter[e] - m_t[e]))
        m_new.append(jnp.maximum(b_last[e] + m_prev[e], gi_max[e]))
        wc.append(jnp.exp(b_last[e] + m_prev[e] - m_new[e]))
        kw.append(k_l[bi][:, sls[hd]] * jnp.exp(gi[e] - m_new[e]))
    sv = [_dot(s_l[e].astype(BF16), vb[e]) for e in range(len(probs))]
    kv = [_dot_tn(kw[e].astype(BF16), vb[e]) for e in range(len(probs))]
    s_sum = [jnp.sum(s_l[e], axis=1, keepdims=True) for e in range(len(probs))]
    qn = [jnp.sum(q_l[bi][:, sls[hd]] * n_prev[e], axis=1, keepdims=True) for e, (bi, hd) in enumerate(probs)]
    k_sum = [jnp.sum(kw[e], axis=0, keepdims=True) for e in range(len(probs))]
    hb = []
    for e, (bi, hd) in enumerate(probs):
        den = s_sum[e] + iw[e] * qn[e]
        hh = (sv[e] + iw[e] * qc[e]) / jnp.maximum(jnp.abs(den), jnp.exp(-m_t[e]))
        hb.append(_sigmoid(ub_l[bi][:, 3 * B_W + hd * B_HD:3 * B_W + (hd + 1) * B_HD]) * hh)
        c_s[bi * B_HEADS + hd] = wc[e] * c_s[bi * B_HEADS + hd] + kv[e]
    mean = [jnp.sum(hb[e], axis=-1, keepdims=True) * (1.0 / B_HD) for e in range(len(probs))]
    dev = [hb[e] - mean[e] for e in range(len(probs))]
    var = [jnp.sum(dev[e] * dev[e], axis=-1, keepdims=True) * (1.0 / B_HD) for e in range(len(probs))]
    for e, (bi, hd) in enumerate(probs):
        y_ref[bi, :, sls[hd]] = (dev[e] * lax.rsqrt(var[e] + GN_EPS) * gn_ref[:, sls[hd]]).astype(y_ref.dtype)
    pad_rows = jnp.zeros((SUBLANES - B_HEADS, LANES), F32)
    for bi in seqs:
        es = [e for e, (bj, _) in enumerate(probs) if bj == bi]
        n_s[bi] = jnp.concatenate([wc[e] * n_prev[e] + k_sum[e] for e in es] + [pad_rows], axis=0)
        m_s[bi] = jnp.concatenate([jnp.broadcast_to(m_new[e], (1, LANES)) for e in es] + [pad_rows], axis=0)

    @pl.when(c == pl.num_programs(1) - 1)
    def _():
        for bi in range(nb):
            for hd in heads:
                co_ref[bi, hd] = c_s[bi * B_HEADS + hd]
        no_ref[...] = n_s[...]
        mo_ref[...] = m_s[...]


def _mlstm_chunk(ub, gates, gates_t, cw, cb, bias_r, bias_c, gn, bsz, t_len, nb):
    L = CHUNK
    seq = lambda w: pl.BlockSpec((nb, L, w), lambda b, c: (b, c, 0))
    per_b = lambda shp: pl.BlockSpec((nb,) + shp, lambda b, c: (b,) + (0,) * len(shp))
    return pl.pallas_call(
        _mlstm_chunk_kernel,
        grid=(bsz // nb, t_len // L),
        in_specs=[seq(4 * B_W), seq(LANES),
                  pl.BlockSpec((nb, SUBLANES, L), lambda b, c: (b, 0, c)),
                  _const_spec(cw.shape), _const_spec(cb.shape), _const_spec(bias_r.shape),
                  _const_spec(bias_c.shape), _const_spec(gn.shape)],
        out_specs=[seq(B_W), per_b((B_HEADS, B_HD, B_HD)), per_b((SUBLANES, B_HD)),
                   per_b((SUBLANES, LANES))],
        out_shape=[jax.ShapeDtypeStruct((bsz, t_len, B_W), BF16),
                   jax.ShapeDtypeStruct((bsz, B_HEADS, B_HD, B_HD), F32),
                   jax.ShapeDtypeStruct((bsz, SUBLANES, B_HD), F32),
                   jax.ShapeDtypeStruct((bsz, SUBLANES, LANES), F32)],
        scratch_shapes=[pltpu.VMEM((nb, SUBLANES, 2 * B_W), F32),
                        pltpu.VMEM((nb * B_HEADS, B_HD, B_HD), F32),
                        pltpu.VMEM((nb, SUBLANES, B_HD), F32),
                        pltpu.VMEM((nb, SUBLANES, LANES), F32)],
        compiler_params=_cparams("parallel", "arbitrary"),
        name="mlstm_chunk",
    )(ub, gates, gates_t, cw, cb, bias_r, bias_c, gn)


def _mlstm_step_kernel(ub_ref, g_ref, conv_ref, cw_ref, cb_ref, biasr_ref, gn_ref, c_ref, n_ref, m_ref,
                       y_ref, co_ref, no_ref, mo_ref):
    bb = ub_ref.shape[0]
    ub = ub_ref[...]
    raw = ub[:, :2 * B_W]
    conv = conv_ref[...]
    acc = cb_ref[...] + raw * cw_ref[B_CONV - 1:B_CONV, :]
    for jj in range(B_CONV - 1):
        acc = acc + conv[:, jj * 2 * B_W:(jj + 1) * 2 * B_W] * cw_ref[jj:jj + 1, :]
    qk = _silu(acc)
    q_all = qk[:, :B_W]
    k_all = qk[:, B_W:] * (B_HD ** -0.5)
    gc = g_ref[...] + biasr_ref[...]
    lf = _log_sigmoid(gc)
    m_all = m_ref[...]
    row_pad = lax.broadcasted_iota(jnp.int32, (LANES, LANES), 0)
    pad = jnp.zeros((LANES - bb, B_HD), F32)
    m_out = jnp.zeros((bb, LANES), F32)
    lane_m = lax.broadcasted_iota(jnp.int32, (bb, LANES), 1)
    for hd in range(B_HEADS):
        sl = slice(hd * B_HD, (hd + 1) * B_HD)
        q = q_all[:, sl]
        k = k_all[:, sl]
        v = ub[:, 2 * B_W + hd * B_HD:2 * B_W + (hd + 1) * B_HD]
        o = ub[:, 3 * B_W + hd * B_HD:3 * B_W + (hd + 1) * B_HD]
        ic = _lane_col(gc, hd)
        f = _lane_col(lf, B_HEADS + hd)
        m_prev = _lane_col(m_all, hd)
        n_prev = n_ref[:, sl]
        m_t = jnp.maximum(f + m_prev, ic)
        s = jnp.sum(q * k, axis=1, keepdims=True) * jnp.exp(ic - m_t)
        iw = jnp.exp(f + m_prev - m_t)
        wi = jnp.exp(ic - m_t)
        qb = q.astype(BF16)
        kw_t = jnp.concatenate([k * wi, pad], axis=0).T
        v_pad = jnp.concatenate([v, pad], axis=0)
        qc = jnp.zeros((bb, B_HD), F32)
        row_b = lax.broadcasted_iota(jnp.int32, (bb, B_HD), 0)
        for bi in range(bb):
            c_prev = c_ref[bi, hd]
            qc = qc + jnp.where(row_b == bi, _dot(qb, c_prev.astype(BF16)), 0.0)
            v_one = jnp.where(row_pad == bi, v_pad, 0.0)
            co_ref[bi, hd] = _row_pick(iw, bi) * c_prev + _dot3(kw_t, v_one)
        num = s * v + iw * qc
        den = s + iw * jnp.sum(q * n_prev, axis=1, keepdims=True)
        hh = num / jnp.maximum(jnp.abs(den), jnp.exp(-m_t))
        no_ref[:, sl] = iw * n_prev + wi * k
        m_out = jnp.where(lane_m == hd, m_t, m_out)
        hb = _sigmoid(o) * hh
        y_ref[:, sl] = _head_norm_lanes(hb, gn_ref[:, sl], GN_EPS).astype(y_ref.dtype)
    mo_ref[...] = m_out


def _mlstm_step(ub, gates, conv, cw, cb, bias_r, gn, c0, n0, m0, bb):
    bsz = ub.shape[0]
    row = lambda w: pl.BlockSpec((bb, w), lambda i: (i, 0))
    c_spec = pl.BlockSpec((bb, B_HEADS, B_HD, B_HD), lambda i: (i, 0, 0, 0))
    return pl.pallas_call(
        _mlstm_step_kernel,
        grid=(bsz // bb,),
        in_specs=[row(4 * B_W), row(LANES), row((B_CONV - 1) * 2 * B_W),
                  _const_spec(cw.shape), _const_spec(cb.shape), _const_spec(bias_r.shape),
                  _const_spec(gn.shape), c_spec, row(B_W), row(LANES)],
        out_specs=[row(B_W), c_spec, row(B_W), row(LANES)],
        out_shape=[jax.ShapeDtypeStruct((bsz, B_W), BF16),
                   jax.ShapeDtypeStruct(c0.shape, F32),
                   jax.ShapeDtypeStruct((bsz, B_W), F32),
                   jax.ShapeDtypeStruct((bsz, LANES), F32)],
        compiler_params=_cparams("parallel"),
        name="mlstm_step",
    )(ub, gates, conv, cw, cb, bias_r, gn, c0, n0, m0)


def _ret_log_gamma(hd):
    return math.log1p(-(2.0 ** (-5.0 - hd)))


def _ret_chunk_kernel(q_ref, qd_ref, k_ref, kd_ref, v_ref, g_ref, y_ref, so_ref, s_s, mask_s):
    nb, L, _ = q_ref.shape
    c = pl.program_id(1)

    @pl.when(c == 0)
    def _():
        s_s[...] = jnp.zeros_like(s_s)
        r_i = lax.broadcasted_iota(jnp.int32, (L, L), 0)
        c_i = lax.broadcasted_iota(jnp.int32, (L, L), 1)
        diff = (r_i - c_i).astype(F32)
        for hd in range(C_HEADS):
            mask_s[hd] = jnp.where(diff >= 0, jnp.exp(jnp.maximum(diff, 0.0) * _ret_log_gamma(hd)), 0.0)

    heads = range(C_HEADS)
    kls = [slice(hd * C_DK, (hd + 1) * C_DK) for hd in heads]
    sls = [slice(hd * C_DV, (hd + 1) * C_DV) for hd in heads]
    probs = [(bi, hd) for bi in range(nb) for hd in heads]
    n = range(len(probs))

    sc = [(_dot_nt(q_ref[bi, :, kls[hd]], k_ref[bi, :, kls[hd]]) * mask_s[hd]).astype(BF16) for bi, hd in probs]
    o_l = []
    for e, (bi, hd) in enumerate(probs):
        sl = sls[hd]
        c_dec = math.exp(L * _ret_log_gamma(hd))
        s_prev = s_s[e]
        o_l.append(_dot(sc[e], v_ref[bi, :, sl]) + _dot(qd_ref[bi, :, kls[hd]], s_prev.astype(BF16)))
        s_s[e] = c_dec * s_prev + _dot_tn(kd_ref[bi, :, kls[hd]], v_ref[bi, :, sl])
    mean = [jnp.sum(o_l[e], axis=-1, keepdims=True) * (1.0 / C_DV) for e in n]
    dev = [o_l[e] - mean[e] for e in n]
    var = [jnp.sum(dev[e] * dev[e], axis=-1, keepdims=True) * (1.0 / C_DV) for e in n]
    for e, (bi, hd) in enumerate(probs):
        sl = sls[hd]
        y_ref[bi, :, sl] = (dev[e] * lax.rsqrt(var[e] + GN_EPS) * g_ref[bi, :, sl]).astype(y_ref.dtype)

    @pl.when(c == pl.num_programs(1) - 1)
    def _():
        for e, (bi, hd) in enumerate(probs):
            so_ref[bi, hd] = s_s[e]


def _ret_chunk(q, qd, k, kd, v, g, bsz, t_len, nb):
    L = RET_CHUNK
    seq = lambda w: pl.BlockSpec((nb, L, w), lambda b, c: (b, c, 0))
    st = pl.BlockSpec((nb, C_HEADS, C_DK, C_DV), lambda b, c: (b, 0, 0, 0))
    return pl.pallas_call(
        _ret_chunk_kernel,
        grid=(bsz // nb, t_len // L),
        in_specs=[seq(C_HEADS * C_DK)] * 4 + [seq(C_OUT_W), seq(C_OUT_W)],
        out_specs=[seq(C_OUT_W), st],
        out_shape=[jax.ShapeDtypeStruct((bsz, t_len, C_OUT_W), BF16),
                   jax.ShapeDtypeStruct((bsz, C_HEADS, C_DK, C_DV), F32)],
        scratch_shapes=[pltpu.VMEM((nb * C_HEADS, C_DK, C_DV), F32), pltpu.VMEM((C_HEADS, L, L), F32)],
        compiler_params=_cparams("parallel", "arbitrary"),
        name="ret_chunk",
    )(q, qd, k, kd, v, g)


def _ret_step_kernel(q_ref, k_ref, v_ref, g_ref, s_ref, y_ref, so_ref):
    bb = q_ref.shape[0]
    hd = pl.program_id(1)
    gam = 1.0 - jnp.exp2(-5.0 - jnp.full((1, 1), hd).astype(F32))
    q = q_ref[...]
    k = k_ref[...]
    v = v_ref[...]
    s = jnp.sum(q * k, axis=1, keepdims=True)
    pad = jnp.zeros((LANES - bb, C_DK), F32)
    k_t = jnp.concatenate([k, pad], axis=0).T
    q_t = jnp.concatenate([q * gam, pad], axis=0).T
    row_b = lax.broadcasted_iota(jnp.int32, (bb, C_DV), 0)
    qs = jnp.zeros((bb, C_DV), F32)
    for bi in range(bb):
        s_prev = s_ref[bi, 0]
        qs = jnp.where(row_b == bi, jnp.sum(_lane_col(q_t, bi) * s_prev, axis=0, keepdims=True), qs)
        so_ref[bi, 0] = gam * s_prev + _lane_col(k_t, bi) * v[bi:bi + 1, :]
    o = s * v + qs
    y = _head_norm_lanes(o, g_ref[...], GN_EPS)
    y_ref[...] = y.astype(y_ref.dtype)


def _ret_step(q, k, v, g, s0, bb):
    bsz = q.shape[0]
    st = pl.BlockSpec((bb, 1, C_DK, C_DV), lambda i, h: (i, h, 0, 0))
    return pl.pallas_call(
        _ret_step_kernel,
        grid=(bsz // bb, C_HEADS),
        in_specs=[pl.BlockSpec((bb, C_DK), lambda i, h: (i, h)),
                  pl.BlockSpec((bb, C_DK), lambda i, h: (i, h)),
                  pl.BlockSpec((bb, C_DV), lambda i, h: (i, h)),
                  pl.BlockSpec((bb, C_DV), lambda i, h: (i, h)), st],
        out_specs=[pl.BlockSpec((bb, C_DV), lambda i, h: (i, h)), st],
        out_shape=[jax.ShapeDtypeStruct((bsz, C_OUT_W), BF16),
                   jax.ShapeDtypeStruct(s0.shape, F32)],
        compiler_params=_cparams("parallel", "parallel"),
        name="ret_step",
    )(q, k, v, g, s0)


def _post_kernel(*refs, n_mix):
    h_ref = refs[0]
    mix_refs = refs[1:1 + n_mix]
    wout_refs = refs[1 + n_mix:1 + 2 * n_mix]
    (p_ref, g_ref, up_ref, down_ref, gate_ref, proj_ref, o_ref, h1_s, xn_s, acc_s) = refs[1 + 2 * n_mix:]
    f = pl.program_id(1)

    tm = h_ref.shape[0]
    halves = [slice(0, tm // 2), slice(tm // 2, tm)] if tm >= 2 * LANES else [slice(0, tm)]

    @pl.when(f == 0)
    def _():
        for rows in halves:
            mix = _dot(mix_refs[0][rows, :], wout_refs[0][...])
            for m_ref, w_ref in zip(mix_refs[1:], wout_refs[1:]):
                mix = mix + _dot(m_ref[rows, :], w_ref[...])
            h1 = h_ref[rows, :] + _rms(mix, g_ref[1:2, :])
            h1_s[rows, :] = h1
            xn_s[rows, :] = _rms(h1, g_ref[2:3, :]).astype(BF16)
        acc_s[...] = jnp.zeros_like(acc_s)

    hid = jnp.square(jnp.maximum(_dot(xn_s[...], up_ref[...]), 0.0))
    acc_s[...] += _dot(hid.astype(BF16), down_ref[...])

    @pl.when(f == pl.num_programs(1) - 1)
    def _():
        for rows in halves:
            h2 = h1_s[rows, :] + _rms(acc_s[rows, :], g_ref[3:4, :])
            gate = _sigmoid(_dot(h2.astype(BF16), gate_ref[...]))
            emb = _dot(p_ref[rows, :].astype(BF16), proj_ref[...])
            o_ref[rows, :] = h2 + gate * emb


def _post(h, mixes, wouts, p, g, up, down, gate, proj, li, tm, tf):
    n = h.shape[0]
    n_mix = len(mixes)
    row = lambda w: pl.BlockSpec((tm, w), lambda i, f: (i, 0))
    layer = lambda a: pl.BlockSpec((None,) + a.shape[1:], lambda i, f: (li,) + (0,) * (a.ndim - 1),
                                   pipeline_mode=pl.Buffered(1))
    ff_mode = pl.Buffered(1) if tf == D_FF else None
    return pl.pallas_call(
        functools.partial(_post_kernel, n_mix=n_mix),
        grid=(n // tm, D_FF // tf),
        in_specs=[row(D_MODEL)] + [row(m.shape[1]) for m in mixes]
        + [_const_spec(w.shape) for w in wouts]
        + [pl.BlockSpec((None, tm, D_PLE), lambda i, f: (li, i, 0)), layer(g),
           pl.BlockSpec((None, D_MODEL, tf), lambda i, f: (li, 0, f), pipeline_mode=ff_mode),
           pl.BlockSpec((None, tf, D_MODEL), lambda i, f: (li, f, 0), pipeline_mode=ff_mode),
           layer(gate), layer(proj)],
        out_specs=row(D_MODEL),
        out_shape=jax.ShapeDtypeStruct((n, D_MODEL), F32),
        scratch_shapes=[pltpu.VMEM((tm, D_MODEL), F32), pltpu.VMEM((tm, D_MODEL), BF16),
                        pltpu.VMEM((tm, D_MODEL), F32)],
        compiler_params=_cparams("parallel", "arbitrary"),
        name="post",
    )(h, *mixes, *wouts, p, g, up, down, gate, proj)


def _prep_weights(W):
    f = {}
    w_in = W['mix0_w_in'][0]
    w_in_t = w_in.T
    f['w_a'] = w_in_t[:A_SHIFT_W].astype(BF16)
    f['w_b'] = w_in_t[A_SHIFT_W:A_SHIFT_W + 4 * B_W].astype(BF16)
    f['w_g'] = jnp.pad(w_in_t[A_SHIFT_W + 4 * B_W:], ((0, LANES - 2 * B_HEADS), (0, 0))).astype(BF16)
    wl = jnp.zeros((A_LORA_W, 3 * A_W), F32)
    wl = wl.at[:A_DECAY_LORA, :A_W].set(W['rw_w2'][0])
    wl = wl.at[A_DECAY_LORA:A_DECAY_LORA + A_AAA_LORA, A_W:2 * A_W].set(W['rw_a2'][0])
    wl = wl.at[A_DECAY_LORA + A_AAA_LORA:, 2 * A_W:].set(W['rw_g2'][0])
    f['w_lora'] = wl.astype(BF16)
    f['rw_vec'] = jnp.concatenate(
        [W['rw_w0'], W['rw_a0'], W['rw_kk'], W['rw_ka'], W['rw_rk'][0].reshape(1, A_W),
         jnp.zeros((3, A_W), F32)], axis=0)
    f['rw_mu'] = W['rw_mu']
    f['rw_ln'] = W['rw_ln']
    f['seg'] = (jnp.arange(A_W)[:, None] // A_HD == jnp.arange(A_W)[None, :] // A_HD).astype(BF16)
    f['cw'] = W['ml_conv_w'][0]
    f['cb'] = W['ml_conv_b']
    gate_bias = jnp.concatenate([W['ml_i_bias'][0], W['ml_f_bias'][0]])
    f['bias_r'] = jnp.pad(gate_bias, (0, LANES - 2 * B_HEADS)).reshape(1, LANES)
    f['bias_c'] = gate_bias.reshape(2 * B_HEADS, 1)
    f['ml_gn'] = W['ml_gn']
    w_out0 = W['mix0_w_out'][0].astype(BF16)
    f['w_out_a'] = w_out0[:A_W]
    f['w_out_b'] = w_out0[A_W:]
    w_ret = W['ret_w_in'][0]
    nq = C_HEADS * C_DK
    f['w_rq'] = w_ret[:, :nq].astype(BF16)
    f['w_rk'] = w_ret[:, nq:2 * nq].astype(BF16)
    f['w_rv'] = w_ret[:, 2 * nq:2 * nq + C_OUT_W].astype(BF16)
    f['w_rg'] = w_ret[:, 2 * nq + C_OUT_W:].astype(BF16)
    f['ret_gn'] = W['ret_gn']
    f['w_ret_out'] = W['ret_w_out'][0].astype(BF16)
    for name in ('ffn_up', 'ffn_down', 'ple_gate', 'ple_proj'):
        f[name] = W[name].astype(BF16)
    f['norm_g'] = W['norm_g']
    return f


def _rope_tables(pos):
    half = C_DK // 2
    inv = ROPE_BASE ** (-jnp.arange(half, dtype=F32) / half)
    ang = pos.astype(F32)[:, None] * inv[None, :]
    return jnp.cos(ang), jnp.sin(ang)


def _even_layer(h, p, st, f, bsz, t_len):
    n = bsz * t_len
    seq = st is None
    tm = 512 if seq else n
    ua, ub, gates = _inproj(h, f['norm_g'][0], [f['w_a'], f['w_b'], f['w_g']], [F32, F32, F32], tm)
    qk_raw = ub.reshape(bsz, t_len, 4 * B_W)[:, :, :2 * B_W]

    if seq:
        ya, s_a_new = _rw_chunk(ua, f['rw_mu'], f['w_lora'], f['rw_vec'], f['rw_ln'], bsz, t_len, rt=256)
        conv_new = qk_raw[:, -(B_CONV - 1):]
    else:
        shift, s_a, conv, c0, n0, m0 = st
        conv_new = jnp.concatenate([conv, qk_raw], axis=1)[:, -(B_CONV - 1):]
        r, w, k, v, a, b, g, bonus = _rw_prep(ua, shift, f['rw_mu'], f['w_lora'], f['rw_vec'], f['seg'])
        y_t, s_t = _rw_step(r, w, k, v, a, b, jnp.transpose(s_a, (1, 2, 3, 0)))
        s_a_new = jnp.transpose(s_t, (3, 0, 1, 2))
        ya = _rw_post(y_t, bonus, g, f['rw_ln'], f['seg'])

    if seq:
        gates3 = gates.reshape(bsz, t_len, LANES)
        gates_t = gates3[:, :, :SUBLANES].transpose(0, 2, 1)
        yb, c_new, n_new, m_new = _mlstm_chunk(ub.reshape(bsz, t_len, 4 * B_W), gates3, gates_t, f['cw'],
                                               f['cb'], f['bias_r'], f['bias_c'], f['ml_gn'], bsz, t_len,
                                               nb=4)
        yb = yb.reshape(n, B_W)
        n_new = n_new[:, :B_HEADS]
        m_new = m_new[:, :B_HEADS, 0]
    else:
        m0p = jnp.pad(m0, ((0, 0), (0, LANES - B_HEADS)))
        yb, c_new, n_new, m_new = _mlstm_step(ub, gates, conv.reshape(bsz, -1), f['cw'], f['cb'],
                                              f['bias_r'], f['ml_gn'], c0, n0.reshape(bsz, B_W), m0p,
                                              bb=SUBLANES)
        n_new = n_new.reshape(bsz, B_HEADS, B_HD)
        m_new = m_new[:, :B_HEADS]

    h_next = _post(h, [ya, yb], [f['w_out_a'], f['w_out_b']], p, f['norm_g'], f['ffn_up'],
                   f['ffn_down'], f['ple_gate'], f['ple_proj'], li=0, tm=tm, tf=POST_TF)

    shift_new = ua.reshape(bsz, t_len, A_SHIFT_W)[:, -1]
    return h_next, (shift_new, s_a_new, conv_new, c_new, n_new, m_new)


def _odd_layer(h, p, s0, pos, f, bsz, t_len):
    n = bsz * t_len
    seq = s0 is None
    cos, sin = _rope_tables(pos)
    tm = 512 if seq else n
    ws = [f['w_rq'], f['w_rk'], f['w_rv'], f['w_rg']]
    if seq:
        q, qd, k, kd, v, g = _inproj_ret(h, f['norm_g'][1], f['ret_gn'], cos, sin, ws, tm, t_len // tm,
                                         [BF16] * 5 + [F32], decayed=True)
        as3 = lambda a: a.reshape(bsz, t_len, a.shape[-1])
        y, s_new = _ret_chunk(as3(q), as3(qd), as3(k), as3(kd), as3(v), as3(g), bsz, t_len, nb=2)
        y = y.reshape(n, C_OUT_W)
    else:
        q, k, v, g = _inproj_ret(h, f['norm_g'][1], f['ret_gn'], cos, sin, ws, tm, 1, [F32] * 4,
                                 decayed=False)
        y, s_new = _ret_step(q, k, v, g, s0, bb=2 * SUBLANES)
    h_next = _post(h, [y], [f['w_ret_out']], p, f['norm_g'], f['ffn_up'], f['ffn_down'],
                   f['ple_gate'], f['ple_proj'], li=1, tm=512 if seq else n, tf=POST_TF)
    return h_next, s_new


def _trunk(x, p, pos, ev_states, od_state, f):
    bsz, t_len, _ = x.shape
    n = bsz * t_len
    h = x.reshape(n, D_MODEL)
    ev_in = None if ev_states is None else tuple(s[0] for s in ev_states)
    od_in = None if od_state is None else od_state[0]
    p = p.reshape(p.shape[0], n, D_PLE)
    h, ev_new = _even_layer(h, p, ev_in, f, bsz, t_len)
    h, od_new = _odd_layer(h, p, od_in, pos, f, bsz, t_len)
    return h.reshape(bsz, t_len, D_MODEL), [s[None] for s in ev_new], od_new[None]


def kernel(x_prompt, x_sample, state_rwkv_shift, state_rwkv_S, state_mlstm_conv, state_mlstm_C,
           state_mlstm_n, state_mlstm_m, state_ret_S, p_prompt, p_sample, norm_g, ffn_up, ffn_down,
           ple_gate, ple_proj, mix0_w_in, rw_mu, rw_w0, rw_w2, rw_a0, rw_a2, rw_g2, rw_kk, rw_ka,
           rw_rk, rw_ln, ml_conv_w, ml_conv_b, ml_i_bias, ml_f_bias, ml_gn, mix0_w_out,
           ret_w_in, ret_gn, ret_w_out):
    W = dict(norm_g=norm_g, ffn_up=ffn_up, ffn_down=ffn_down, ple_gate=ple_gate, ple_proj=ple_proj,
             mix0_w_in=mix0_w_in, rw_mu=rw_mu, rw_w0=rw_w0, rw_w2=rw_w2, rw_a0=rw_a0, rw_a2=rw_a2,
             rw_g2=rw_g2, rw_kk=rw_kk, rw_ka=rw_ka, rw_rk=rw_rk, rw_ln=rw_ln, ml_conv_w=ml_conv_w,
             ml_conv_b=ml_conv_b, ml_i_bias=ml_i_bias, ml_f_bias=ml_f_bias, ml_gn=ml_gn,
             mix0_w_out=mix0_w_out, ret_w_in=ret_w_in, ret_gn=ret_gn, ret_w_out=ret_w_out)
    f = _prep_weights(W)
    bp, tp = x_prompt.shape[:2]
    ts = x_sample.shape[1]
    y_prompt, ev_p, ret_p = _trunk(x_prompt, p_prompt, jnp.arange(tp), None, None, f)
    ev_s_in = (state_rwkv_shift, state_rwkv_S, state_mlstm_conv, state_mlstm_C, state_mlstm_n,
               state_mlstm_m)
    y_sample, ev_s, ret_s = _trunk(x_sample, p_sample, PAST_LEN + jnp.arange(ts), ev_s_in, state_ret_S, f)
    return (y_prompt, y_sample, *ev_p, ret_p, *ev_s, ret_s)
```
